```python
import jax, jax.numpy as jnp
from jax import lax
import numpy as np

D_MODEL = 1024
BATCH = 8
SEQ = 8192
DEPTH = 4

N_MIXERS = 3
EXPAND = 2
D_INNER = EXPAND * D_MODEL
CHUNK = 128
A_GROUPS = 8
A_GROUP_DIM = D_INNER // A_GROUPS
POOL_WINDOWS = (2, 4, 8, 16)
B_GROUPS = len(POOL_WINDOWS)
B_GROUP_DIM = D_INNER // B_GROUPS
CONV_WIDTH = 3
LN_EPS = 1e-5
ALPHA = (2.0 * DEPTH) ** 0.25
BETA = (8.0 * DEPTH) ** -0.25

kernel_name = "hybrid_gmlp_pool_shortconv_deepnorm"


def layer_norm(x, g, b):
    xf = x.astype(jnp.float32)
    mu = jnp.mean(xf, axis=-1, keepdims=True)
    var = jnp.mean(jnp.square(xf - mu), axis=-1, keepdims=True)
    y = (xf - mu) * lax.rsqrt(var + LN_EPS)
    return (y * g.astype(jnp.float32) + b.astype(jnp.float32)).astype(x.dtype)


def mixer_a(h, w_in, v_gain, v_bias, w_s, b_s, w_out):
    bsz, s, _ = h.shape
    u, v, z = jnp.split(h @ w_in, 3, axis=-1)
    u = jax.nn.gelu(u)
    v = layer_norm(jax.nn.gelu(v), v_gain, v_bias)
    v = v.reshape(bsz, s // CHUNK, CHUNK, A_GROUPS, A_GROUP_DIM)
    w_causal = w_s * jnp.tril(jnp.ones((CHUNK, CHUNK), w_s.dtype))
    sv = jnp.einsum('gts,bcsgd->bctgd', w_causal, v) + b_s.T[:, :, None]
    sv = sv.reshape(bsz, s, D_INNER)
    return (u * sv * jax.nn.silu(z)) @ w_out


def trailing_mean_minus_self(v, window):
    s = v.shape[1]
    vf = v.astype(jnp.float32)
    csum = jnp.cumsum(vf, axis=1)
    lag = jnp.pad(csum, ((0, 0), (window, 0), (0, 0)))[:, :s]
    count = jnp.minimum(jnp.arange(1, s + 1), window).astype(jnp.float32)[None, :, None]
    return ((csum - lag) / count - vf).astype(v.dtype)


def mixer_b(h, w_in, w_grp, scale, w_out):
    bsz, s, _ = h.shape
    v, z = jnp.split(h @ w_in, 2, axis=-1)
    groups = jnp.split(v, B_GROUPS, axis=-1)
    pooled = jnp.stack([trailing_mean_minus_self(g, w) for g, w in zip(groups, POOL_WINDOWS)], axis=2)
    mixed = jnp.einsum('bsgd,gde->bsge', pooled, w_grp).reshape(bsz, s, D_INNER)
    return (mixed * scale * jax.nn.silu(z)) @ w_out


def causal_depthwise_conv(x, w):
    k = w.shape[0]
    s = x.shape[1]
    xp = jnp.pad(x, ((0, 0), (k - 1, 0), (0, 0)))
    out = xp[:, 0:s] * w[0]
    for j in range(1, k):
        out = out + xp[:, j:j + s] * w[j]
    return out


def mixer_c(h, w_in, conv_w, w_out):
    b_gate, c_gate, hv, z = jnp.split(h @ w_in, 4, axis=-1)
    y = b_gate * causal_depthwise_conv(c_gate * hv, conv_w)
    return (y * jax.nn.silu(z)) @ w_out


def _dense(key, fan_in, fan_out, scale=1.0):
    return jax.random.normal(key, (fan_in, fan_out), jnp.float32) * (scale * fan_in ** -0.5)


def _gain(key, n):
    return 1.0 + 0.1 * jax.random.normal(key, (n,), jnp.float32)


def _bias(key, n):
    return 0.02 * jax.random.normal(key, (n,), jnp.float32)


def _fwd_setup_inputs(seed: int = 0) -> dict:
    key = jax.random.key(seed)
    keys = jax.random.split(key, 1 + DEPTH)
    params = {"x": jax.random.normal(keys[0], (BATCH, SEQ, D_MODEL), jnp.float32)}
    tags = ("a", "b", "c")
    for i in range(DEPTH):
        kind = i % N_MIXERS
        p = tags[kind] + str(i)
        k = jax.random.split(keys[1 + i], 8)
        if kind == 0:
            params[p + "_w_in"] = _dense(k[0], D_MODEL, 3 * D_INNER)
            params[p + "_v_gain"] = _gain(k[1], D_INNER)
            params[p + "_v_bias"] = _bias(k[2], D_INNER)
            params[p + "_w_s"] = jax.random.normal(k[3], (A_GROUPS, CHUNK, CHUNK), jnp.float32) * CHUNK ** -0.5
            params[p + "_b_s"] = 1.0 + 0.1 * jax.random.normal(k[4], (A_GROUPS, CHUNK), jnp.float32)
            params[p + "_w_out"] = _dense(k[5], D_INNER, D_MODEL, BETA)
        elif kind == 1:
            params[p + "_w_in"] = _dense(k[0], D_MODEL, 2 * D_INNER)
            params[p + "_w_grp"] = jax.random.normal(k[1], (B_GROUPS, B_GROUP_DIM, B_GROUP_DIM), jnp.float32) * B_GROUP_DIM ** -0.5
            params[p + "_scale"] = _gain(k[2], D_INNER)
            params[p + "_w_out"] = _dense(k[3], D_INNER, D_MODEL, BETA)
        else:
            params[p + "_w_in"] = _dense(k[0], D_MODEL, 4 * D_INNER)
            params[p + "_conv_w"] = jax.random.normal(k[1], (CONV_WIDTH, D_INNER), jnp.float32) * CONV_WIDTH ** -0.5
            params[p + "_w_out"] = _dense(k[2], D_INNER, D_MODEL, BETA)
        params["ln" + str(i) + "_gain"] = _gain(k[6], D_MODEL)
        params["ln" + str(i) + "_bias"] = _bias(k[7], D_MODEL)
    return params


def _fwd_reference(x,
              a0_w_in, a0_v_gain, a0_v_bias, a0_w_s, a0_b_s, a0_w_out, ln0_gain, ln0_bias,
              b1_w_in, b1_w_grp, b1_scale, b1_w_out, ln1_gain, ln1_bias,
              c2_w_in, c2_conv_w, c2_w_out, ln2_gain, ln2_bias,
              a3_w_in, a3_v_gain, a3_v_bias, a3_w_s, a3_b_s, a3_w_out, ln3_gain, ln3_bias):
    mixers = (mixer_a, mixer_b, mixer_c)
    layer_params = [
        (a0_w_in, a0_v_gain, a0_v_bias, a0_w_s, a0_b_s, a0_w_out),
        (b1_w_in, b1_w_grp, b1_scale, b1_w_out),
        (c2_w_in, c2_conv_w, c2_w_out),
        (a3_w_in, a3_v_gain, a3_v_bias, a3_w_s, a3_b_s, a3_w_out),
    ]
    norm_params = [(ln0_gain, ln0_bias), (ln1_gain, ln1_bias), (ln2_gain, ln2_bias), (ln3_gain, ln3_bias)]
    for i in range(DEPTH):
        mixer = mixers[i % N_MIXERS]
        g, b = norm_params[i]
        x = layer_norm(ALPHA * x + mixer(x, *layer_params[i]), g, b)
    return x


import jax as _jax
import jax.numpy as _jnp

TWIN_FORMAT = 'train_step'
FWD_PARAMS = ['x', 'a0_w_in', 'a0_v_gain', 'a0_v_bias', 'a0_w_s', 'a0_b_s', 'a0_w_out', 'ln0_gain', 'ln0_bias', 'b1_w_in', 'b1_w_grp', 'b1_scale', 'b1_w_out', 'ln1_gain', 'ln1_bias', 'c2_w_in', 'c2_conv_w', 'c2_w_out', 'ln2_gain', 'ln2_bias', 'a3_w_in', 'a3_v_gain', 'a3_v_bias', 'a3_w_s', 'a3_b_s', 'a3_w_out', 'ln3_gain', 'ln3_bias']
TWIN_WEIGHTS = ['a0_w_in', 'a0_v_gain', 'a0_v_bias', 'a0_w_s', 'a0_b_s', 'a0_w_out', 'ln0_gain', 'ln0_bias', 'b1_w_in', 'b1_w_grp', 'b1_scale', 'b1_w_out', 'ln1_gain', 'ln1_bias', 'c2_w_in', 'c2_conv_w', 'c2_w_out', 'ln2_gain', 'ln2_bias', 'a3_w_in', 'a3_v_gain', 'a3_v_bias', 'a3_w_s', 'a3_b_s', 'a3_w_out', 'ln3_gain', 'ln3_bias']
TWIN_DIFF_INPUT = 'x'
TWIN_INPUTS = ['x', 'a0_w_in', 'a0_v_gain', 'a0_v_bias', 'a0_w_s', 'a0_b_s', 'a0_w_out', 'ln0_gain', 'ln0_bias', 'b1_w_in', 'b1_w_grp', 'b1_scale', 'b1_w_out', 'ln1_gain', 'ln1_bias', 'c2_w_in', 'c2_conv_w', 'c2_w_out', 'ln2_gain', 'ln2_bias', 'a3_w_in', 'a3_v_gain', 'a3_v_bias', 'a3_w_s', 'a3_b_s', 'a3_w_out', 'ln3_gain', 'ln3_bias', 'loss_target', 'm_a0_w_in', 'm_a0_v_gain', 'm_a0_v_bias', 'm_a0_w_s', 'm_a0_b_s', 'm_a0_w_out', 'm_ln0_gain', 'm_ln0_bias', 'm_b1_w_in', 'm_b1_w_grp', 'm_b1_scale', 'm_b1_w_out', 'm_ln1_gain', 'm_ln1_bias', 'm_c2_w_in', 'm_c2_conv_w', 'm_c2_w_out', 'm_ln2_gain', 'm_ln2_bias', 'm_a3_w_in', 'm_a3_v_gain', 'm_a3_v_bias', 'm_a3_w_s', 'm_a3_b_s', 'm_a3_w_out', 'm_ln3_gain', 'm_ln3_bias', 'v_a0_w_in', 'v_a0_v_gain', 'v_a0_v_bias', 'v_a0_w_s', 'v_a0_b_s', 'v_a0_w_out', 'v_ln0_gain', 'v_ln0_bias', 'v_b1_w_in', 'v_b1_w_grp', 'v_b1_scale', 'v_b1_w_out', 'v_ln1_gain', 'v_ln1_bias', 'v_c2_w_in', 'v_c2_conv_w', 'v_c2_w_out', 'v_ln2_gain', 'v_ln2_bias', 'v_a3_w_in', 'v_a3_v_gain', 'v_a3_v_bias', 'v_a3_w_s', 'v_a3_b_s', 'v_a3_w_out', 'v_ln3_gain', 'v_ln3_bias']
TWIN_OUTPUTS = ['loss', 'grad_x', 'grad_a0_w_in', 'grad_a0_v_gain', 'grad_a0_v_bias', 'grad_a0_w_s', 'grad_a0_b_s', 'grad_a0_w_out', 'grad_ln0_gain', 'grad_ln0_bias', 'grad_b1_w_in', 'grad_b1_w_grp', 'grad_b1_scale', 'grad_b1_w_out', 'grad_ln1_gain', 'grad_ln1_bias', 'grad_c2_w_in', 'grad_c2_conv_w', 'grad_c2_w_out', 'grad_ln2_gain', 'grad_ln2_bias', 'grad_a3_w_in', 'grad_a3_v_gain', 'grad_a3_v_bias', 'grad_a3_w_s', 'grad_a3_b_s', 'grad_a3_w_out', 'grad_ln3_gain', 'grad_ln3_bias', 'delta_a0_w_in', 'delta_a0_v_gain', 'delta_a0_v_bias', 'delta_a0_w_s', 'delta_a0_b_s', 'delta_a0_w_out', 'delta_ln0_gain', 'delta_ln0_bias', 'delta_b1_w_in', 'delta_b1_w_grp', 'delta_b1_scale', 'delta_b1_w_out', 'delta_ln1_gain', 'delta_ln1_bias', 'delta_c2_w_in', 'delta_c2_conv_w', 'delta_c2_w_out', 'delta_ln2_gain', 'delta_ln2_bias', 'delta_a3_w_in', 'delta_a3_v_gain', 'delta_a3_v_bias', 'delta_a3_w_s', 'delta_a3_b_s', 'delta_a3_w_out', 'delta_ln3_gain', 'delta_ln3_bias', 'new_m_a0_w_in', 'new_m_a0_v_gain', 'new_m_a0_v_bias', 'new_m_a0_w_s', 'new_m_a0_b_s', 'new_m_a0_w_out', 'new_m_ln0_gain', 'new_m_ln0_bias', 'new_m_b1_w_in', 'new_m_b1_w_grp', 'new_m_b1_scale', 'new_m_b1_w_out', 'new_m_ln1_gain', 'new_m_ln1_bias', 'new_m_c2_w_in', 'new_m_c2_conv_w', 'new_m_c2_w_out', 'new_m_ln2_gain', 'new_m_ln2_bias', 'new_m_a3_w_in', 'new_m_a3_v_gain', 'new_m_a3_v_bias', 'new_m_a3_w_s', 'new_m_a3_b_s', 'new_m_a3_w_out', 'new_m_ln3_gain', 'new_m_ln3_bias', 'new_v_a0_w_in', 'new_v_a0_v_gain', 'new_v_a0_v_bias', 'new_v_a0_w_s', 'new_v_a0_b_s', 'new_v_a0_w_out', 'new_v_ln0_gain', 'new_v_ln0_bias', 'new_v_b1_w_in', 'new_v_b1_w_grp', 'new_v_b1_scale', 'new_v_b1_w_out', 'new_v_ln1_gain', 'new_v_ln1_bias', 'new_v_c2_w_in', 'new_v_c2_conv_w', 'new_v_c2_w_out', 'new_v_ln2_gain', 'new_v_ln2_bias', 'new_v_a3_w_in', 'new_v_a3_v_gain', 'new_v_a3_v_bias', 'new_v_a3_w_s', 'new_v_a3_b_s', 'new_v_a3_w_out', 'new_v_ln3_gain', 'new_v_ln3_bias']
TWIN_LEAF_KINDS = {'loss': 'loss', 'grad_x': 'grad_x', 'grad_a0_w_in': 'grad_w', 'grad_a0_v_gain': 'grad_w', 'grad_a0_v_bias': 'grad_w', 'grad_a0_w_s': 'grad_w', 'grad_a0_b_s': 'grad_w', 'grad_a0_w_out': 'grad_w', 'grad_ln0_gain': 'grad_w', 'grad_ln0_bias': 'grad_w', 'grad_b1_w_in': 'grad_w', 'grad_b1_w_grp': 'grad_w', 'grad_b1_scale': 'grad_w', 'grad_b1_w_out': 'grad_w', 'grad_ln1_gain': 'grad_w', 'grad_ln1_bias': 'grad_w', 'grad_c2_w_in': 'grad_w', 'grad_c2_conv_w': 'grad_w', 'grad_c2_w_out': 'grad_w', 'grad_ln2_gain': 'grad_w', 'grad_ln2_bias': 'grad_w', 'grad_a3_w_in': 'grad_w', 'grad_a3_v_gain': 'grad_w', 'grad_a3_v_bias': 'grad_w', 'grad_a3_w_s': 'grad_w', 'grad_a3_b_s': 'grad_w', 'grad_a3_w_out': 'grad_w', 'grad_ln3_gain': 'grad_w', 'grad_ln3_bias': 'grad_w', 'delta_a0_w_in': 'delta_w', 'delta_a0_v_gain': 'delta_w', 'delta_a0_v_bias': 'delta_w', 'delta_a0_w_s': 'delta_w', 'delta_a0_b_s': 'delta_w', 'delta_a0_w_out': 'delta_w', 'delta_ln0_gain': 'delta_w', 'delta_ln0_bias': 'delta_w', 'delta_b1_w_in': 'delta_w', 'delta_b1_w_grp': 'delta_w', 'delta_b1_scale': 'delta_w', 'delta_b1_w_out': 'delta_w', 'delta_ln1_gain': 'delta_w', 'delta_ln1_bias': 'delta_w', 'delta_c2_w_in': 'delta_w', 'delta_c2_conv_w': 'delta_w', 'delta_c2_w_out': 'delta_w', 'delta_ln2_gain': 'delta_w', 'delta_ln2_bias': 'delta_w', 'delta_a3_w_in': 'delta_w', 'delta_a3_v_gain': 'delta_w', 'delta_a3_v_bias': 'delta_w', 'delta_a3_w_s': 'delta_w', 'delta_a3_b_s': 'delta_w', 'delta_a3_w_out': 'delta_w', 'delta_ln3_gain': 'delta_w', 'delta_ln3_bias': 'delta_w', 'new_m_a0_w_in': 'new_m', 'new_m_a0_v_gain': 'new_m', 'new_m_a0_v_bias': 'new_m', 'new_m_a0_w_s': 'new_m', 'new_m_a0_b_s': 'new_m', 'new_m_a0_w_out': 'new_m', 'new_m_ln0_gain': 'new_m', 'new_m_ln0_bias': 'new_m', 'new_m_b1_w_in': 'new_m', 'new_m_b1_w_grp': 'new_m', 'new_m_b1_scale': 'new_m', 'new_m_b1_w_out': 'new_m', 'new_m_ln1_gain': 'new_m', 'new_m_ln1_bias': 'new_m', 'new_m_c2_w_in': 'new_m', 'new_m_c2_conv_w': 'new_m', 'new_m_c2_w_out': 'new_m', 'new_m_ln2_gain': 'new_m', 'new_m_ln2_bias': 'new_m', 'new_m_a3_w_in': 'new_m', 'new_m_a3_v_gain': 'new_m', 'new_m_a3_v_bias': 'new_m', 'new_m_a3_w_s': 'new_m', 'new_m_a3_b_s': 'new_m', 'new_m_a3_w_out': 'new_m', 'new_m_ln3_gain': 'new_m', 'new_m_ln3_bias': 'new_m', 'new_v_a0_w_in': 'new_v', 'new_v_a0_v_gain': 'new_v', 'new_v_a0_v_bias': 'new_v', 'new_v_a0_w_s': 'new_v', 'new_v_a0_b_s': 'new_v', 'new_v_a0_w_out': 'new_v', 'new_v_ln0_gain': 'new_v', 'new_v_ln0_bias': 'new_v', 'new_v_b1_w_in': 'new_v', 'new_v_b1_w_grp': 'new_v', 'new_v_b1_scale': 'new_v', 'new_v_b1_w_out': 'new_v', 'new_v_ln1_gain': 'new_v', 'new_v_ln1_bias': 'new_v', 'new_v_c2_w_in': 'new_v', 'new_v_c2_conv_w': 'new_v', 'new_v_c2_w_out': 'new_v', 'new_v_ln2_gain': 'new_v', 'new_v_ln2_bias': 'new_v', 'new_v_a3_w_in': 'new_v', 'new_v_a3_v_gain': 'new_v', 'new_v_a3_v_bias': 'new_v', 'new_v_a3_w_s': 'new_v', 'new_v_a3_b_s': 'new_v', 'new_v_a3_w_out': 'new_v', 'new_v_ln3_gain': 'new_v', 'new_v_ln3_bias': 'new_v'}


def _forward(args):
    return _fwd_reference(*[args[k] for k in FWD_PARAMS])


def _output_shape():
    def fwd():
        inp = _fwd_setup_inputs(0)
        return _fwd_reference(*[inp[k] for k in FWD_PARAMS])
    out = _jax.eval_shape(fwd)
    return out.shape, out.dtype

N_MICROBATCH = 1
ADAM_LR = 0.001
ADAM_B1 = 0.9
ADAM_B2 = 0.999
ADAM_EPS = 1e-08
ADAM_WD = 0.01
ADAM_STEP = 10
PER_EXAMPLE_BATCH_AXIS = {'x': 0, 'loss_target': 0}
SHARED_INPUTS = []
_WEIGHT_DTYPES = {'a0_w_in': _jnp.float32, 'a0_v_gain': _jnp.float32, 'a0_v_bias': _jnp.float32, 'a0_w_s': _jnp.float32, 'a0_b_s': _jnp.float32, 'a0_w_out': _jnp.float32, 'ln0_gain': _jnp.float32, 'ln0_bias': _jnp.float32, 'b1_w_in': _jnp.float32, 'b1_w_grp': _jnp.float32, 'b1_scale': _jnp.float32, 'b1_w_out': _jnp.float32, 'ln1_gain': _jnp.float32, 'ln1_bias': _jnp.float32, 'c2_w_in': _jnp.float32, 'c2_conv_w': _jnp.float32, 'c2_w_out': _jnp.float32, 'ln2_gain': _jnp.float32, 'ln2_bias': _jnp.float32, 'a3_w_in': _jnp.float32, 'a3_v_gain': _jnp.float32, 'a3_v_bias': _jnp.float32, 'a3_w_s': _jnp.float32, 'a3_b_s': _jnp.float32, 'a3_w_out': _jnp.float32, 'ln3_gain': _jnp.float32, 'ln3_bias': _jnp.float32}
MOMENT_SCALE = {'a0_w_in': 2.138567e-02, 'a0_v_gain': 1.356173e-02, 'a0_v_bias': 1.219469e-02, 'a0_w_s': 1.872556e-02, 'a0_b_s': 2.665283e-02, 'a0_w_out': 8.315164e-02, 'ln0_gain': 1.240172e+01, 'ln0_bias': 1.007338e+00, 'b1_w_in': 2.552044e-02, 'b1_w_grp': 2.538520e-02, 'b1_scale': 2.522857e-02, 'b1_w_out': 8.564217e-02, 'ln1_gain': 1.267233e+01, 'ln1_bias': 1.019765e+00, 'c2_w_in': 2.766127e-02, 'c2_conv_w': 2.685300e-02, 'c2_w_out': 9.332119e-02, 'ln2_gain': 1.274054e+01, 'ln2_bias': 1.058669e+00, 'a3_w_in': 2.064087e-02, 'a3_v_gain': 1.316107e-02, 'a3_v_bias': 1.302789e-02, 'a3_w_s': 1.808092e-02, 'a3_b_s': 2.656951e-02, 'a3_w_out': 8.100787e-02, 'ln3_gain': 6.746682e+01, 'ln3_bias': 2.395166e+00}


def _to_microbatches(a, axis):
    t = _jnp.moveaxis(a, axis, 0)
    t = t.reshape((N_MICROBATCH, t.shape[0] // N_MICROBATCH) + t.shape[1:])
    return _jnp.moveaxis(t, 1, axis + 1)


def setup_inputs(seed: int = 0) -> dict:
    inp = _fwd_setup_inputs(seed)
    key = _jax.random.fold_in(_jax.random.key(seed), 7919)
    shape, _ = _output_shape()
    out = dict(inp)
    out["loss_target"] = _jax.random.normal(_jax.random.fold_in(key, 0), shape, _jnp.float32)
    for i, name in enumerate(TWIN_WEIGHTS):
        w = inp[name].astype(_jnp.float32)
        if MOMENT_SCALE is None:
            s = _jnp.sqrt(_jnp.mean(_jnp.square(w)) + 1e-30)
        else:
            s = MOMENT_SCALE[name]
        km, kv = _jax.random.split(_jax.random.fold_in(key, i + 1))
        out[name] = w
        out["m_" + name] = s * _jax.random.normal(km, w.shape, _jnp.float32)
        out["v_" + name] = (s * s) * _jax.random.uniform(kv, w.shape, _jnp.float32, 0.5, 1.5)
    if N_MICROBATCH > 1:
        for name, axis in PER_EXAMPLE_BATCH_AXIS.items():
            out[name] = _to_microbatches(out[name], axis)
    return {'x': out['x'], 'a0_w_in': out['a0_w_in'], 'a0_v_gain': out['a0_v_gain'], 'a0_v_bias': out['a0_v_bias'], 'a0_w_s': out['a0_w_s'], 'a0_b_s': out['a0_b_s'], 'a0_w_out': out['a0_w_out'], 'ln0_gain': out['ln0_gain'], 'ln0_bias': out['ln0_bias'], 'b1_w_in': out['b1_w_in'], 'b1_w_grp': out['b1_w_grp'], 'b1_scale': out['b1_scale'], 'b1_w_out': out['b1_w_out'], 'ln1_gain': out['ln1_gain'], 'ln1_bias': out['ln1_bias'], 'c2_w_in': out['c2_w_in'], 'c2_conv_w': out['c2_conv_w'], 'c2_w_out': out['c2_w_out'], 'ln2_gain': out['ln2_gain'], 'ln2_bias': out['ln2_bias'], 'a3_w_in': out['a3_w_in'], 'a3_v_gain': out['a3_v_gain'], 'a3_v_bias': out['a3_v_bias'], 'a3_w_s': out['a3_w_s'], 'a3_b_s': out['a3_b_s'], 'a3_w_out': out['a3_w_out'], 'ln3_gain': out['ln3_gain'], 'ln3_bias': out['ln3_bias'], 'loss_target': out['loss_target'], 'm_a0_w_in': out['m_a0_w_in'], 'm_a0_v_gain': out['m_a0_v_gain'], 'm_a0_v_bias': out['m_a0_v_bias'], 'm_a0_w_s': out['m_a0_w_s'], 'm_a0_b_s': out['m_a0_b_s'], 'm_a0_w_out': out['m_a0_w_out'], 'm_ln0_gain': out['m_ln0_gain'], 'm_ln0_bias': out['m_ln0_bias'], 'm_b1_w_in': out['m_b1_w_in'], 'm_b1_w_grp': out['m_b1_w_grp'], 'm_b1_scale': out['m_b1_scale'], 'm_b1_w_out': out['m_b1_w_out'], 'm_ln1_gain': out['m_ln1_gain'], 'm_ln1_bias': out['m_ln1_bias'], 'm_c2_w_in': out['m_c2_w_in'], 'm_c2_conv_w': out['m_c2_conv_w'], 'm_c2_w_out': out['m_c2_w_out'], 'm_ln2_gain': out['m_ln2_gain'], 'm_ln2_bias': out['m_ln2_bias'], 'm_a3_w_in': out['m_a3_w_in'], 'm_a3_v_gain': out['m_a3_v_gain'], 'm_a3_v_bias': out['m_a3_v_bias'], 'm_a3_w_s': out['m_a3_w_s'], 'm_a3_b_s': out['m_a3_b_s'], 'm_a3_w_out': out['m_a3_w_out'], 'm_ln3_gain': out['m_ln3_gain'], 'm_ln3_bias': out['m_ln3_bias'], 'v_a0_w_in': out['v_a0_w_in'], 'v_a0_v_gain': out['v_a0_v_gain'], 'v_a0_v_bias': out['v_a0_v_bias'], 'v_a0_w_s': out['v_a0_w_s'], 'v_a0_b_s': out['v_a0_b_s'], 'v_a0_w_out': out['v_a0_w_out'], 'v_ln0_gain': out['v_ln0_gain'], 'v_ln0_bias': out['v_ln0_bias'], 'v_b1_w_in': out['v_b1_w_in'], 'v_b1_w_grp': out['v_b1_w_grp'], 'v_b1_scale': out['v_b1_scale'], 'v_b1_w_out': out['v_b1_w_out'], 'v_ln1_gain': out['v_ln1_gain'], 'v_ln1_bias': out['v_ln1_bias'], 'v_c2_w_in': out['v_c2_w_in'], 'v_c2_conv_w': out['v_c2_conv_w'], 'v_c2_w_out': out['v_c2_w_out'], 'v_ln2_gain': out['v_ln2_gain'], 'v_ln2_bias': out['v_ln2_bias'], 'v_a3_w_in': out['v_a3_w_in'], 'v_a3_v_gain': out['v_a3_v_gain'], 'v_a3_v_bias': out['v_a3_v_bias'], 'v_a3_w_s': out['v_a3_w_s'], 'v_a3_b_s': out['v_a3_b_s'], 'v_a3_w_out': out['v_a3_w_out'], 'v_ln3_gain': out['v_ln3_gain'], 'v_ln3_bias': out['v_ln3_bias']}


def _loss(weights, diff, rest, loss_target):
    with _jax.named_scope("forward"):
        args = {**rest, TWIN_DIFF_INPUT: diff, **{k: w.astype(_WEIGHT_DTYPES[k]) for k, w in weights.items()}}
        y = _forward(args)
    with _jax.named_scope("loss_head"):
        err = _jnp.square(y.astype(_jnp.float32) - loss_target)
        return 0.5 * _jnp.sum(_jnp.mean(err, axis=-1)) if err.ndim else 0.5 * err


def _adamw(w, g, m, v):
    m = ADAM_B1 * m + (1.0 - ADAM_B1) * g
    v = ADAM_B2 * v + (1.0 - ADAM_B2) * _jnp.square(g)
    m_hat = m / (1.0 - ADAM_B1 ** ADAM_STEP)
    v_hat = v / (1.0 - ADAM_B2 ** ADAM_STEP)
    delta = -ADAM_LR * (m_hat / (_jnp.sqrt(v_hat) + ADAM_EPS) + ADAM_WD * w)
    return delta, m, v


def reference(x, a0_w_in, a0_v_gain, a0_v_bias, a0_w_s, a0_b_s, a0_w_out, ln0_gain, ln0_bias, b1_w_in, b1_w_grp, b1_scale, b1_w_out, ln1_gain, ln1_bias, c2_w_in, c2_conv_w, c2_w_out, ln2_gain, ln2_bias, a3_w_in, a3_v_gain, a3_v_bias, a3_w_s, a3_b_s, a3_w_out, ln3_gain, ln3_bias, loss_target, m_a0_w_in, m_a0_v_gain, m_a0_v_bias, m_a0_w_s, m_a0_b_s, m_a0_w_out, m_ln0_gain, m_ln0_bias, m_b1_w_in, m_b1_w_grp, m_b1_scale, m_b1_w_out, m_ln1_gain, m_ln1_bias, m_c2_w_in, m_c2_conv_w, m_c2_w_out, m_ln2_gain, m_ln2_bias, m_a3_w_in, m_a3_v_gain, m_a3_v_bias, m_a3_w_s, m_a3_b_s, m_a3_w_out, m_ln3_gain, m_ln3_bias, v_a0_w_in, v_a0_v_gain, v_a0_v_bias, v_a0_w_s, v_a0_b_s, v_a0_w_out, v_ln0_gain, v_ln0_bias, v_b1_w_in, v_b1_w_grp, v_b1_scale, v_b1_w_out, v_ln1_gain, v_ln1_bias, v_c2_w_in, v_c2_conv_w, v_c2_w_out, v_ln2_gain, v_ln2_bias, v_a3_w_in, v_a3_v_gain, v_a3_v_bias, v_a3_w_s, v_a3_b_s, v_a3_w_out, v_ln3_gain, v_ln3_bias):
    given = dict(x=x, a0_w_in=a0_w_in, a0_v_gain=a0_v_gain, a0_v_bias=a0_v_bias, a0_w_s=a0_w_s, a0_b_s=a0_b_s, a0_w_out=a0_w_out, ln0_gain=ln0_gain, ln0_bias=ln0_bias, b1_w_in=b1_w_in, b1_w_grp=b1_w_grp, b1_scale=b1_scale, b1_w_out=b1_w_out, ln1_gain=ln1_gain, ln1_bias=ln1_bias, c2_w_in=c2_w_in, c2_conv_w=c2_conv_w, c2_w_out=c2_w_out, ln2_gain=ln2_gain, ln2_bias=ln2_bias, a3_w_in=a3_w_in, a3_v_gain=a3_v_gain, a3_v_bias=a3_v_bias, a3_w_s=a3_w_s, a3_b_s=a3_b_s, a3_w_out=a3_w_out, ln3_gain=ln3_gain, ln3_bias=ln3_bias, loss_target=loss_target, m_a0_w_in=m_a0_w_in, m_a0_v_gain=m_a0_v_gain, m_a0_v_bias=m_a0_v_bias, m_a0_w_s=m_a0_w_s, m_a0_b_s=m_a0_b_s, m_a0_w_out=m_a0_w_out, m_ln0_gain=m_ln0_gain, m_ln0_bias=m_ln0_bias, m_b1_w_in=m_b1_w_in, m_b1_w_grp=m_b1_w_grp, m_b1_scale=m_b1_scale, m_b1_w_out=m_b1_w_out, m_ln1_gain=m_ln1_gain, m_ln1_bias=m_ln1_bias, m_c2_w_in=m_c2_w_in, m_c2_conv_w=m_c2_conv_w, m_c2_w_out=m_c2_w_out, m_ln2_gain=m_ln2_gain, m_ln2_bias=m_ln2_bias, m_a3_w_in=m_a3_w_in, m_a3_v_gain=m_a3_v_gain, m_a3_v_bias=m_a3_v_bias, m_a3_w_s=m_a3_w_s, m_a3_b_s=m_a3_b_s, m_a3_w_out=m_a3_w_out, m_ln3_gain=m_ln3_gain, m_ln3_bias=m_ln3_bias, v_a0_w_in=v_a0_w_in, v_a0_v_gain=v_a0_v_gain, v_a0_v_bias=v_a0_v_bias, v_a0_w_s=v_a0_w_s, v_a0_b_s=v_a0_b_s, v_a0_w_out=v_a0_w_out, v_ln0_gain=v_ln0_gain, v_ln0_bias=v_ln0_bias, v_b1_w_in=v_b1_w_in, v_b1_w_grp=v_b1_w_grp, v_b1_scale=v_b1_scale, v_b1_w_out=v_b1_w_out, v_ln1_gain=v_ln1_gain, v_ln1_bias=v_ln1_bias, v_c2_w_in=v_c2_w_in, v_c2_conv_w=v_c2_conv_w, v_c2_w_out=v_c2_w_out, v_ln2_gain=v_ln2_gain, v_ln2_bias=v_ln2_bias, v_a3_w_in=v_a3_w_in, v_a3_v_gain=v_a3_v_gain, v_a3_v_bias=v_a3_v_bias, v_a3_w_s=v_a3_w_s, v_a3_b_s=v_a3_b_s, v_a3_w_out=v_a3_w_out, v_ln3_gain=v_ln3_gain, v_ln3_bias=v_ln3_bias)
    weights = {n: given[n] for n in TWIN_WEIGHTS}
    shared = {n: given[n] for n in SHARED_INPUTS}
    per_example = {n: given[n] for n in ['x']}
    grad_fn = _jax.value_and_grad(_loss, argnums=(0, 1))

    def one_microbatch(ex, loss_target):
        ex = dict(ex)
        diff = ex.pop(TWIN_DIFF_INPUT)
        return grad_fn(weights, diff, {**shared, **ex}, loss_target)

    if N_MICROBATCH == 1:
        loss, (grad_w, grad_x) = one_microbatch(per_example, given["loss_target"])
    else:
        def body(carry, xs):
            loss_sum, grad_sum = carry
            l_k, (gw_k, gx_k) = one_microbatch(xs[0], xs[1])
            with _jax.named_scope("update"):
                return (loss_sum + l_k, _jax.tree.map(_jnp.add, grad_sum, gw_k)), gx_k

        init = (_jnp.zeros((), _jnp.float32), _jax.tree.map(_jnp.zeros_like, weights))
        (loss, grad_w), grad_x = _jax.lax.scan(body, init, (per_example, given["loss_target"]))
    with _jax.named_scope("update"):
        delta_w, new_m, new_v = {}, {}, {}
        for n in TWIN_WEIGHTS:
            delta_w[n], new_m[n], new_v[n] = _adamw(weights[n], grad_w[n], given["m_" + n], given["v_" + n])
    return (loss, grad_x, *[grad_w[n] for n in TWIN_WEIGHTS], *[delta_w[n] for n in TWIN_WEIGHTS],
            *[new_m[n] for n in TWIN_WEIGHTS], *[new_v[n] for n in TWIN_WEIGHTS])
```

```python
import functools
import math

import jax
import jax.numpy as jnp
from jax import lax
from jax.experimental import pallas as pl
from jax.experimental.pallas import tpu as pltpu

F32 = jnp.float32
BF16 = jnp.bfloat16

N_DEV = 8
DEPTH = 4
CHUNK = 128
A_GROUPS = 8
POOL_WINDOWS = (2, 4, 8, 16)
LN_EPS = 1e-5
ALPHA = (2.0 * DEPTH) ** 0.25
ADAM_LR = 0.001
ADAM_B1 = 0.9
ADAM_B2 = 0.999
ADAM_EPS = 1e-08
ADAM_WD = 0.01
ADAM_STEP = 10

TM = 256
HALO = 16
CHALO = 8
CW = 512
VMEM_LIMIT_BYTES = 58 * 1024 * 1024

_NT = (((1,), (1,)), ((), ()))
_SQRT_2_OVER_PI = math.sqrt(2.0 / math.pi)
_MESH = pl.DeviceIdType.MESH


def _vmem():
    return pl.BlockSpec(memory_space=pltpu.VMEM)


def _params(sem=None):
    return pltpu.CompilerParams(dimension_semantics=sem, vmem_limit_bytes=VMEM_LIMIT_BYTES)


def _gelu(x):
    t = jnp.tanh(_SQRT_2_OVER_PI * (x + 0.044715 * (x * x * x)))
    return x * (0.5 * (1.0 + t))


def _gelu_and_grad(x):
    x2 = x * x
    t = jnp.tanh(_SQRT_2_OVER_PI * (x + 0.044715 * (x * x2)))
    cdf = 0.5 * (1.0 + t)
    grad = cdf + 0.5 * x * (1.0 - t * t) * (_SQRT_2_OVER_PI * (1.0 + 3.0 * 0.044715 * x2))
    return x * cdf, grad


def _sigmoid(z):
    return 1.0 / (1.0 + jnp.exp(-z))


def _fold8(a):
    return a.reshape(a.shape[0] // 8, 8, a.shape[1]).sum(axis=0)


def _row_mean(a):
    return jnp.mean(a, axis=-1, keepdims=True)


def _ln_stats(x):
    mu = _row_mean(x)
    xc = x - mu
    rstd = lax.rsqrt(_row_mean(xc * xc) + LN_EPS)
    return xc * rstd, rstd


def _post_norm(x, out, lng_ref, lnb_ref, pre_ref, xn_ref):
    pre = ALPHA * x + out
    pre_ref[...] = pre
    xhat, _ = _ln_stats(pre)
    xn_ref[...] = xhat * lng_ref[...] + lnb_ref[...]


def _post_norm_bwd(g_ref, pre_ref, lng_ref, dpre_ref, dlng_ref, dlnb_ref):
    go = g_ref[...]
    xhat, rstd = _ln_stats(pre_ref[...])
    dlng_ref[...] += _fold8(go * xhat)
    dlnb_ref[...] += _fold8(go)
    dxh = go * lng_ref[...]
    dpre = rstd * (dxh - _row_mean(dxh) - xhat * _row_mean(dxh * xhat))
    dpre_ref[...] = dpre
    return dpre


def _in_proj(xb, win_ref, p_ref):
    cs = win_ref.shape[2]
    for j in range(N_DEV):
        p_ref[:, j * cs:(j + 1) * cs] = jnp.dot(xb, win_ref[j], preferred_element_type=F32)


def _zero_at_first_step(*refs):
    @pl.when(pl.program_id(0) == 0)
    def _():
        for r in refs:
            r[...] = jnp.zeros(r.shape, r.dtype)


def _a_v_path(p_ref, gain_ref, bias_ref, vh_scr, vn_scr, di):
    tm = p_ref.shape[0]
    s1 = jnp.zeros((tm, 1), F32)
    for c in range(di // CW):
        sl = slice(c * CW, (c + 1) * CW)
        vg = _gelu(p_ref[:, di + c * CW:di + (c + 1) * CW])
        vh_scr[:, sl] = vg
        s1 += jnp.sum(vg, axis=1, keepdims=True)
    mu = s1 * (1.0 / di)
    s2 = jnp.zeros((tm, 1), F32)
    for c in range(di // CW):
        sl = slice(c * CW, (c + 1) * CW)
        d = vh_scr[:, sl] - mu
        s2 += jnp.sum(d * d, axis=1, keepdims=True)
    rstd = lax.rsqrt(s2 * (1.0 / di) + LN_EPS)
    for c in range(di // CW):
        sl = slice(c * CW, (c + 1) * CW)
        vh = (vh_scr[:, sl] - mu) * rstd
        vh_scr[:, sl] = vh
        vn_scr[:, sl] = (vh * gain_ref[:, sl] + bias_ref[:, sl]).astype(BF16)
    return rstd


def _fwd_a(x, win, wout, gain, bias, wc2, bs2, lng, lnb):
    t, d = x.shape
    cs = win.shape[2]
    di = wout.shape[0]
    gd = di // A_GROUPS
    n = N_DEV * cs

    def body(x_ref, win_ref, wout_ref, gain_ref, bias_ref, wc_ref, bs_ref, lng_ref, lnb_ref,
             p_ref, xt_ref, pre_ref, xn_ref, vh_scr, vn_scr, y_scr):
        xv = x_ref[...]
        xt_ref[...] = xv.T.astype(BF16)
        _in_proj(xv.astype(BF16), win_ref, p_ref)
        _a_v_path(p_ref, gain_ref, bias_ref, vh_scr, vn_scr, di)
        for g in range(A_GROUPS):
            sl = slice(g * gd, (g + 1) * gd)
            sv = jnp.dot(wc_ref[g], vn_scr[:, sl], preferred_element_type=F32) + bs_ref[g]
            z = p_ref[:, 2 * di + g * gd:2 * di + (g + 1) * gd]
            y_scr[:, sl] = (_gelu(p_ref[:, sl]) * sv * (z * _sigmoid(z))).astype(BF16)
        out = jnp.dot(y_scr[...], wout_ref[...], preferred_element_type=F32)
        _post_norm(xv, out, lng_ref, lnb_ref, pre_ref, xn_ref)

    row = lambda i: (i, 0)
    return pl.pallas_call(
        body, name="fwd_a", grid=(t // TM,),
        in_specs=[pl.BlockSpec((TM, d), row), _vmem(), _vmem(), _vmem(), _vmem(), _vmem(), _vmem(), _vmem(), _vmem()],
        out_specs=[pl.BlockSpec((TM, n), row), pl.BlockSpec((d, TM), lambda i: (0, i)),
                   pl.BlockSpec((TM, d), row), pl.BlockSpec((TM, d), row)],
        out_shape=[jax.ShapeDtypeStruct((t, n), F32), jax.ShapeDtypeStruct((d, t), BF16),
                   jax.ShapeDtypeStruct((t, d), F32), jax.ShapeDtypeStruct((t, d), F32)],
        scratch_shapes=[pltpu.VMEM((TM, di), F32), pltpu.VMEM((TM, di), BF16), pltpu.VMEM((TM, di), BF16)],
        compiler_params=_params(("arbitrary",)),
    )(x, win, wout, gain, bias, wc2, bs2, lng, lnb)


def _bwd_a(g, pre, p, wout, gain, bias, wc2, wc2t, bs2, lng):
    t, d = g.shape
    di = wout.shape[0]
    gd = di // A_GROUPS
    n = p.shape[1]

    def body(g_ref, pre_ref, p_ref, wout_ref, gain_ref, bias_ref, wc_ref, wct_ref, bs_ref, lng_ref,
             dpre_ref, dp_ref, yt_ref, dlng_ref, dlnb_ref, dgain_ref, dbias_ref, dbs_ref, dwc_ref,
             dy_scr, vh_scr, vn_scr, dv_scr, y_scr):
        _zero_at_first_step(dlng_ref, dlnb_ref, dgain_ref, dbias_ref, dbs_ref, dwc_ref)
        dpre = _post_norm_bwd(g_ref, pre_ref, lng_ref, dpre_ref, dlng_ref, dlnb_ref)
        dy_scr[...] = lax.dot_general(dpre.astype(BF16), wout_ref[...], _NT, preferred_element_type=F32)
        rstd_v = _a_v_path(p_ref, gain_ref, bias_ref, vh_scr, vn_scr, di)
        for grp in range(A_GROUPS):
            sl = slice(grp * gd, (grp + 1) * gd)
            vn = vn_scr[:, sl]
            sv = jnp.dot(wc_ref[grp], vn, preferred_element_type=F32) + bs_ref[grp]
            u, du = _gelu_and_grad(p_ref[:, sl])
            z = p_ref[:, 2 * di + grp * gd:2 * di + (grp + 1) * gd]
            sg = _sigmoid(z)
            s = z * sg
            ds = sg * (1.0 + z * (1.0 - sg))
            dy = dy_scr[:, sl]
            us = u * s
            y_scr[:, sl] = us * sv
            dys = dy * sv
            dp_ref[:, sl] = (dys * s * du).astype(BF16)
            dp_ref[:, 2 * di + grp * gd:2 * di + (grp + 1) * gd] = (dys * u * ds).astype(BF16)
            dsv = dy * us
            dbs_ref[grp] += jnp.sum(dsv, axis=1, keepdims=True)
            dsvb = dsv.astype(BF16)
            dwc_ref[grp] += lax.dot_general(dsvb, vn, _NT, preferred_element_type=F32)
            dv_scr[:, sl] = jnp.dot(wct_ref[grp], dsvb, preferred_element_type=F32)
        yt_ref[...] = y_scr[...].T.astype(BF16)
        tm = dv_scr.shape[0]
        a1 = jnp.zeros((tm, 1), F32)
        a2 = jnp.zeros((tm, 1), F32)
        for c in range(di // CW):
            sl = slice(c * CW, (c + 1) * CW)
            dv = dv_scr[:, sl]
            vh = vh_scr[:, sl]
            dgain_ref[:, sl] += _fold8(dv * vh)
            dbias_ref[:, sl] += _fold8(dv)
            dvh = dv * gain_ref[:, sl]
            a1 += jnp.sum(dvh, axis=1, keepdims=True)
            a2 += jnp.sum(dvh * vh, axis=1, keepdims=True)
        m1 = a1 * (1.0 / di)
        m2 = a2 * (1.0 / di)
        for c in range(di // CW):
            sl = slice(c * CW, (c + 1) * CW)
            dvg = rstd_v * (dv_scr[:, sl] * gain_ref[:, sl] - m1 - vh_scr[:, sl] * m2)
            _, dgel = _gelu_and_grad(p_ref[:, di + c * CW:di + (c + 1) * CW])
            dp_ref[:, di + c * CW:di + (c + 1) * CW] = (dvg * dgel).astype(BF16)

    row = lambda i: (i, 0)
    const2 = lambda i: (0, 0)
    const3 = lambda i: (0, 0, 0)
    return pl.pallas_call(
        body, name="bwd_a", grid=(t // TM,),
        in_specs=[pl.BlockSpec((TM, d), row), pl.BlockSpec((TM, d), row), pl.BlockSpec((TM, n), row),
                  _vmem(), _vmem(), _vmem(), _vmem(), _vmem(), _vmem(), _vmem()],
        out_specs=[pl.BlockSpec((TM, d), row), pl.BlockSpec((TM, n), row), pl.BlockSpec((di, TM), lambda i: (0, i)),
                   pl.BlockSpec((8, d), const2), pl.BlockSpec((8, d), const2),
                   pl.BlockSpec((8, di), const2), pl.BlockSpec((8, di), const2),
                   pl.BlockSpec((A_GROUPS, TM, 1), const3), pl.BlockSpec((A_GROUPS, TM, TM), const3)],
        out_shape=[jax.ShapeDtypeStruct((t, d), F32), jax.ShapeDtypeStruct((t, n), BF16),
                   jax.ShapeDtypeStruct((di, t), BF16),
                   jax.ShapeDtypeStruct((8, d), F32), jax.ShapeDtypeStruct((8, d), F32),
                   jax.ShapeDtypeStruct((8, di), F32), jax.ShapeDtypeStruct((8, di), F32),
                   jax.ShapeDtypeStruct((A_GROUPS, TM, 1), F32), jax.ShapeDtypeStruct((A_GROUPS, TM, TM), F32)],
        scratch_shapes=[pltpu.VMEM((TM, di), F32), pltpu.VMEM((TM, di), F32), pltpu.VMEM((TM, di), BF16),
                        pltpu.VMEM((TM, di), F32), pltpu.VMEM((TM, di), F32)],
        compiler_params=_params(("arbitrary",)),
    )(g, pre, p, wout, gain, bias, wc2, wc2t, bs2, lng)


def _inv_count(tile, window):
    pos = tile * TM + lax.broadcasted_iota(jnp.int32, (TM, 1), 0)
    return 1.0 / jnp.minimum(pos + 1, window).astype(F32)


def _window_sum(ext, window, down):
    rows = ext.shape[0]
    k = 1
    while k < window:
        ext = ext + pltpu.roll(ext, k if down else rows - k, 0)
        k *= 2
    return ext


def _fwd_b(x, win, wgrp, scale, wout, lng, lnb):
    t, d = x.shape
    cs = win.shape[2]
    di = wout.shape[0]
    gd = di // len(POOL_WINDOWS)

    def body(x_ref, win_ref, wgrp_ref, scale_ref, wout_ref, lng_ref, lnb_ref,
             z_ref, mixed_ref, poolt_ref, xt_ref, pre_ref, xn_ref, p_scr, ext_scr, y_scr):
        i = pl.program_id(0)

        @pl.when(i == 0)
        def _():
            ext_scr[0:HALO, :] = jnp.zeros((HALO, di), F32)

        xv = x_ref[...]
        xt_ref[...] = xv.T.astype(BF16)
        _in_proj(xv.astype(BF16), win_ref, p_scr)
        z_ref[...] = p_scr[:, di:]
        ext_scr[HALO:, :] = p_scr[:, :di]
        for grp, window in enumerate(POOL_WINDOWS):
            sl = slice(grp * gd, (grp + 1) * gd)
            ext = ext_scr[:, sl]
            v = ext[HALO:]
            pooled = (_window_sum(ext, window, True)[HALO:] * _inv_count(i, window) - v).astype(BF16)
            poolt_ref[sl, :] = pooled.astype(F32).T.astype(BF16)
            mixed = jnp.dot(pooled, wgrp_ref[grp], preferred_element_type=F32)
            mixed_ref[:, sl] = mixed
            z = p_scr[:, di + grp * gd:di + (grp + 1) * gd]
            y_scr[:, sl] = (mixed * scale_ref[:, sl] * (z * _sigmoid(z))).astype(BF16)
        ext_scr[0:HALO, :] = ext_scr[TM:TM + HALO, :]
        out = jnp.dot(y_scr[...], wout_ref[...], preferred_element_type=F32)
        _post_norm(xv, out, lng_ref, lnb_ref, pre_ref, xn_ref)

    row = lambda i: (i, 0)
    col = lambda i: (0, i)
    return pl.pallas_call(
        body, name="fwd_b", grid=(t // TM,),
        in_specs=[pl.BlockSpec((TM, d), row), _vmem(), _vmem(), _vmem(), _vmem(), _vmem(), _vmem()],
        out_specs=[pl.BlockSpec((TM, di), row), pl.BlockSpec((TM, di), row), pl.BlockSpec((di, TM), col),
                   pl.BlockSpec((d, TM), col), pl.BlockSpec((TM, d), row), pl.BlockSpec((TM, d), row)],
        out_shape=[jax.ShapeDtypeStruct((t, di), F32), jax.ShapeDtypeStruct((t, di), F32),
                   jax.ShapeDtypeStruct((di, t), BF16), jax.ShapeDtypeStruct((d, t), BF16),
                   jax.ShapeDtypeStruct((t, d), F32), jax.ShapeDtypeStruct((t, d), F32)],
        scratch_shapes=[pltpu.VMEM((TM, 2 * di), F32), pltpu.VMEM((TM + HALO, di), F32), pltpu.VMEM((TM, di), BF16)],
        compiler_params=_params(("arbitrary",)),
    )(x, win, wgrp, scale, wout, lng, lnb)


def _bwd_b(g, pre, z, mixed, wout, wgrp, scale, lng):
    t, d = g.shape
    di = wout.shape[0]
    gd = di // len(POOL_WINDOWS)
    nt = t // TM

    def body(g_ref, pre_ref, z_ref, mixed_ref, wout_ref, wgrp_ref, scale_ref, lng_ref,
             dpre_ref, dp_ref, yt_ref, dmix_ref, dlng_ref, dlnb_ref, dscale_ref,
             dy_scr, ext_scr, y_scr):
        i = pl.program_id(0)
        tile = nt - 1 - i
        _zero_at_first_step(dlng_ref, dlnb_ref, dscale_ref)

        @pl.when(i == 0)
        def _():
            ext_scr[TM:, :] = jnp.zeros((HALO, di), F32)

        dpre = _post_norm_bwd(g_ref, pre_ref, lng_ref, dpre_ref, dlng_ref, dlnb_ref)
        dy_scr[...] = lax.dot_general(dpre.astype(BF16), wout_ref[...], _NT, preferred_element_type=F32)
        for grp, window in enumerate(POOL_WINDOWS):
            sl = slice(grp * gd, (grp + 1) * gd)
            zz = z_ref[:, sl]
            sg = _sigmoid(zz)
            s = zz * sg
            ds = sg * (1.0 + zz * (1.0 - sg))
            mixed_v = mixed_ref[:, sl]
            dy = dy_scr[:, sl]
            sc = scale_ref[:, sl]
            y_scr[:, sl] = mixed_v * sc * s
            dym = dy * mixed_v
            dp_ref[:, di + grp * gd:di + (grp + 1) * gd] = (dym * sc * ds).astype(BF16)
            dscale_ref[:, sl] += _fold8(dym * s)
            dmixed = (dy * sc * s).astype(BF16)
            dmix_ref[:, sl] = dmixed
            dpooled = lax.dot_general(dmixed, wgrp_ref[grp], _NT, preferred_element_type=F32)
            ext_scr[0:TM, sl] = dpooled * _inv_count(tile, window)
            dv = _window_sum(ext_scr[:, sl], window, False)[0:TM] - dpooled
            dp_ref[:, sl] = dv.astype(BF16)
        yt_ref[...] = y_scr[...].T.astype(BF16)
        ext_scr[TM:, :] = ext_scr[0:HALO, :]

    rrow = lambda i: (nt - 1 - i, 0)
    rcol = lambda i: (0, nt - 1 - i)
    const2 = lambda i: (0, 0)
    return pl.pallas_call(
        body, name="bwd_b", grid=(nt,),
        in_specs=[pl.BlockSpec((TM, d), rrow), pl.BlockSpec((TM, d), rrow), pl.BlockSpec((TM, di), rrow),
                  pl.BlockSpec((TM, di), rrow), _vmem(), _vmem(), _vmem(), _vmem()],
        out_specs=[pl.BlockSpec((TM, d), rrow), pl.BlockSpec((TM, 2 * di), rrow), pl.BlockSpec((di, TM), rcol),
                   pl.BlockSpec((TM, di), rrow), pl.BlockSpec((8, d), const2), pl.BlockSpec((8, d), const2),
                   pl.BlockSpec((8, di), const2)],
        out_shape=[jax.ShapeDtypeStruct((t, d), F32), jax.ShapeDtypeStruct((t, 2 * di), BF16),
                   jax.ShapeDtypeStruct((di, t), BF16), jax.ShapeDtypeStruct((t, di), BF16),
                   jax.ShapeDtypeStruct((8, d), F32), jax.ShapeDtypeStruct((8, d), F32),
                   jax.ShapeDtypeStruct((8, di), F32)],
        scratch_shapes=[pltpu.VMEM((TM, di), F32), pltpu.VMEM((TM + HALO, di), F32), pltpu.VMEM((TM, di), F32)],
        compiler_params=_params(("arbitrary",)),
    )(g, pre, z, mixed, wout, wgrp, scale, lng)


def _conv3(ext_ref, sl, cw_ref):
    ext = ext_ref[:, sl]
    return (pltpu.roll(ext, 2, 0)[CHALO:] * cw_ref[0:1, sl] + pltpu.roll(ext, 1, 0)[CHALO:] * cw_ref[1:2, sl]
            + ext[CHALO:] * cw_ref[2:3, sl])


def _fwd_c(x, win, convw, wout, lng, lnb):
    t, d = x.shape
    cs = win.shape[2]
    di = wout.shape[0]
    n = N_DEV * cs

    def body(x_ref, win_ref, cw_ref, wout_ref, lng_ref, lnb_ref,
             p_ref, xt_ref, pre_ref, xn_ref, ext_scr, y_scr):
        i = pl.program_id(0)

        @pl.when(i == 0)
        def _():
            ext_scr[0:CHALO, :] = jnp.zeros((CHALO, di), F32)

        xv = x_ref[...]
        xt_ref[...] = xv.T.astype(BF16)
        _in_proj(xv.astype(BF16), win_ref, p_ref)
        for c in range(di // CW):
            sl = slice(c * CW, (c + 1) * CW)
            ext_scr[CHALO:, sl] = p_ref[:, di + c * CW:di + (c + 1) * CW] * p_ref[:, 2 * di + c * CW:2 * di + (c + 1) * CW]
            z = p_ref[:, 3 * di + c * CW:3 * di + (c + 1) * CW]
            y_scr[:, sl] = (p_ref[:, sl] * _conv3(ext_scr, sl, cw_ref) * (z * _sigmoid(z))).astype(BF16)
        ext_scr[0:CHALO, :] = ext_scr[TM:TM + CHALO, :]
        out = jnp.dot(y_scr[...], wout_ref[...], preferred_element_type=F32)
        _post_norm(xv, out, lng_ref, lnb_ref, pre_ref, xn_ref)

    row = lambda i: (i, 0)
    return pl.pallas_call(
        body, name="fwd_c", grid=(t // TM,),
        in_specs=[pl.BlockSpec((TM, d), row), _vmem(), _vmem(), _vmem(), _vmem(), _vmem()],
        out_specs=[pl.BlockSpec((TM, n), row), pl.BlockSpec((d, TM), lambda i: (0, i)),
                   pl.BlockSpec((TM, d), row), pl.BlockSpec((TM, d), row)],
        out_shape=[jax.ShapeDtypeStruct((t, n), F32), jax.ShapeDtypeStruct((d, t), BF16),
                   jax.ShapeDtypeStruct((t, d), F32), jax.ShapeDtypeStruct((t, d), F32)],
        scratch_shapes=[pltpu.VMEM((TM + CHALO, di), F32), pltpu.VMEM((TM, di), BF16)],
        compiler_params=_params(("arbitrary",)),
    )(x, win, convw, wout, lng, lnb)


def _bwd_c(g, pre, p, wout, convw, lng):
    t, d = g.shape
    di = wout.shape[0]
    n = p.shape[1]
    nt = t // TM
    halo_blocks = TM // CHALO

    def body(g_ref, pre_ref, p_ref, pprev_ref, wout_ref, cw_ref, lng_ref,
             dpre_ref, dp_ref, yt_ref, dlng_ref, dlnb_ref, dcw_ref,
             dy_scr, extq_scr, extd_scr, y_scr):
        i = pl.program_id(0)
        tile = nt - 1 - i
        _zero_at_first_step(dlng_ref, dlnb_ref, dcw_ref)

        @pl.when(i == 0)
        def _():
            extd_scr[TM:, :] = jnp.zeros((CHALO, di), F32)

        dpre = _post_norm_bwd(g_ref, pre_ref, lng_ref, dpre_ref, dlng_ref, dlnb_ref)
        dy_scr[...] = lax.dot_general(dpre.astype(BF16), wout_ref[...], _NT, preferred_element_type=F32)
        has_prev = (tile > 0).astype(F32)
        for c in range(di // CW):
            sl = slice(c * CW, (c + 1) * CW)
            bb = p_ref[:, sl]
            cc = p_ref[:, di + c * CW:di + (c + 1) * CW]
            hh = p_ref[:, 2 * di + c * CW:2 * di + (c + 1) * CW]
            zz = p_ref[:, 3 * di + c * CW:3 * di + (c + 1) * CW]
            q = cc * hh
            extq_scr[0:CHALO, sl] = (pprev_ref[:, di + c * CW:di + (c + 1) * CW]
                                     * pprev_ref[:, 2 * di + c * CW:2 * di + (c + 1) * CW]) * has_prev
            extq_scr[CHALO:, sl] = q
            conv = _conv3(extq_scr, sl, cw_ref)
            sg = _sigmoid(zz)
            s = zz * sg
            ds = sg * (1.0 + zz * (1.0 - sg))
            dy = dy_scr[:, sl]
            bs = bb * s
            y_scr[:, sl] = bs * conv
            dyc = dy * conv
            dp_ref[:, sl] = (dyc * s).astype(BF16)
            dp_ref[:, 3 * di + c * CW:3 * di + (c + 1) * CW] = (dyc * bb * ds).astype(BF16)
            dconv = dy * bs
            extd_scr[0:TM, sl] = dconv
            ext = extd_scr[:, sl]
            rows = TM + CHALO
            d1 = pltpu.roll(ext, rows - 1, 0)[0:TM]
            d2 = pltpu.roll(ext, rows - 2, 0)[0:TM]
            dq = dconv * cw_ref[2:3, sl] + d1 * cw_ref[1:2, sl] + d2 * cw_ref[0:1, sl]
            dcw_ref[0, :, sl] += _fold8(q * d2)
            dcw_ref[1, :, sl] += _fold8(q * d1)
            dcw_ref[2, :, sl] += _fold8(q * dconv)
            dp_ref[:, di + c * CW:di + (c + 1) * CW] = (dq * hh).astype(BF16)
            dp_ref[:, 2 * di + c * CW:2 * di + (c + 1) * CW] = (dq * cc).astype(BF16)
        yt_ref[...] = y_scr[...].T.astype(BF16)
        extd_scr[TM:, :] = extd_scr[0:CHALO, :]

    rrow = lambda i: (nt - 1 - i, 0)
    rcol = lambda i: (0, nt - 1 - i)
    prev = lambda i: (jnp.maximum((nt - 1 - i) * halo_blocks - 1, 0), 0)
    const2 = lambda i: (0, 0)
    return pl.pallas_call(
        body, name="bwd_c", grid=(nt,),
        in_specs=[pl.BlockSpec((TM, d), rrow), pl.BlockSpec((TM, d), rrow), pl.BlockSpec((TM, n), rrow),
                  pl.BlockSpec((CHALO, n), prev), _vmem(), _vmem(), _vmem()],
        out_specs=[pl.BlockSpec((TM, d), rrow), pl.BlockSpec((TM, n), rrow), pl.BlockSpec((di, TM), rcol),
                   pl.BlockSpec((8, d), const2), pl.BlockSpec((8, d), const2),
                   pl.BlockSpec((3, 8, di), lambda i: (0, 0, 0))],
        out_shape=[jax.ShapeDtypeStruct((t, d), F32), jax.ShapeDtypeStruct((t, n), BF16),
                   jax.ShapeDtypeStruct((di, t), BF16),
                   jax.ShapeDtypeStruct((8, d), F32), jax.ShapeDtypeStruct((8, d), F32),
                   jax.ShapeDtypeStruct((3, 8, di), F32)],
        scratch_shapes=[pltpu.VMEM((TM, di), F32), pltpu.VMEM((TM + CHALO, di), F32),
                        pltpu.VMEM((TM + CHALO, di), F32), pltpu.VMEM((TM, di), F32)],
        compiler_params=_params(("arbitrary",)),
    )(g, pre, p, p, wout, convw, lng)


def _dx(dp, dpre, win):
    t, d = dpre.shape
    n = dp.shape[1]
    cs = win.shape[2]

    def body(dp_ref, dpre_ref, win_ref, dx_ref):
        acc = ALPHA * dpre_ref[...]
        for j in range(N_DEV):
            acc += lax.dot_general(dp_ref[:, j * cs:(j + 1) * cs], win_ref[j], _NT, preferred_element_type=F32)
        dx_ref[...] = acc

    row = lambda i: (i, 0)
    return pl.pallas_call(
        body, name="dx", grid=(t // TM,),
        in_specs=[pl.BlockSpec((TM, n), row), pl.BlockSpec((TM, d), row), _vmem()],
        out_specs=pl.BlockSpec((TM, d), row),
        out_shape=jax.ShapeDtypeStruct((t, d), F32),
        compiler_params=_params(("arbitrary",)),
    )(dp, dpre, win)


def _wgrad(at, b, nb, per_block_rows, name):
    m_all, t = at.shape
    tn = b.shape[1] // nb
    m = m_all // nb if per_block_rows else m_all
    tk = min(512, t)
    nk = t // tk

    def body(at_ref, b_ref, out_ref, acc):
        k = pl.program_id(1)

        @pl.when(k == 0)
        def _():
            acc[...] = jnp.zeros(acc.shape, F32)

        acc[...] += jnp.dot(at_ref[...], b_ref[...].astype(BF16), preferred_element_type=F32)

        @pl.when(k == nk - 1)
        def _():
            out_ref[0] = acc[...].astype(BF16)

    at_map = (lambda j, k: (j, k)) if per_block_rows else (lambda j, k: (0, k))
    return pl.pallas_call(
        body, name=name, grid=(nb, nk),
        in_specs=[pl.BlockSpec((m, tk), at_map), pl.BlockSpec((tk, tn), lambda j, k: (k, j))],
        out_specs=pl.BlockSpec((1, m, tn), lambda j, k: (j, 0, 0)),
        out_shape=jax.ShapeDtypeStruct((nb, m, tn), BF16),
        scratch_shapes=[pltpu.VMEM((m, tn), F32)],
        compiler_params=_params(("arbitrary", "arbitrary")),
    )(at, b)


def _loss_head(y, target):
    t, d = y.shape

    def body(y_ref, t_ref, dy_ref, sq_ref):
        _zero_at_first_step(sq_ref)
        diff = y_ref[...] - t_ref[...]
        dy_ref[...] = diff * (1.0 / d)
        sq_ref[...] += _fold8(diff * diff)

    row = lambda i: (i, 0)
    return pl.pallas_call(
        body, name="loss_head", grid=(t // TM,),
        in_specs=[pl.BlockSpec((TM, d), row), pl.BlockSpec((TM, d), row)],
        out_specs=[pl.BlockSpec((TM, d), row), pl.BlockSpec((8, d), lambda i: (0, 0))],
        out_shape=[jax.ShapeDtypeStruct((t, d), F32), jax.ShapeDtypeStruct((8, d), F32)],
        compiler_params=_params(("arbitrary",)),
    )(y, target)


def _my_position():
    return lax.axis_index("x"), lax.axis_index("y"), lax.axis_index("c")


def _peer(k):
    x, y, c = _my_position()
    peer = (x ^ (k >> 2), y ^ ((k >> 1) & 1), c ^ (k & 1))
    return peer, 4 * peer[0] + 2 * peer[1] + peer[2]


def _all_gather(shards, name):
    nw = len(shards)

    def body(*refs):
        ins, outs = refs[:nw], refs[nw:2 * nw]
        send_sems, recv_sems, local_sems = refs[2 * nw:]
        x, y, c = _my_position()
        me = 4 * x + 2 * y + c
        local = [pltpu.make_async_copy(ins[w], outs[w].at[me], local_sems.at[w]) for w in range(nw)]
        for cp in local:
            cp.start()
        sends = []
        for k in range(1, N_DEV):
            peer, _ = _peer(k)
            for w in range(nw):
                cp = pltpu.make_async_remote_copy(
                    src_ref=ins[w], dst_ref=outs[w].at[me], send_sem=send_sems.at[w, k], recv_sem=recv_sems.at[w, k],
                    device_id=peer, device_id_type=_MESH)
                cp.start()
                sends.append(cp)
        for k in range(1, N_DEV):
            peer, peer_pos = _peer(k)
            for w in range(nw):
                pltpu.make_async_remote_copy(
                    src_ref=ins[w], dst_ref=outs[w].at[peer_pos], send_sem=send_sems.at[w, k],
                    recv_sem=recv_sems.at[w, k], device_id=peer, device_id_type=_MESH).wait_recv()
        for cp in sends:
            cp.wait_send()
        for cp in local:
            cp.wait()

    any_spec = pl.BlockSpec(memory_space=pl.ANY)
    return pl.pallas_call(
        body, name=name,
        in_specs=[any_spec] * nw, out_specs=[any_spec] * nw,
        out_shape=[jax.ShapeDtypeStruct((N_DEV,) + s.shape, s.dtype) for s in shards],
        scratch_shapes=[pltpu.SemaphoreType.DMA((nw, N_DEV)), pltpu.SemaphoreType.DMA((nw, N_DEV)),
                        pltpu.SemaphoreType.DMA((nw,))],
    )(*shards)


def _scatter_blocks(fulls, name):
    nw = len(fulls)

    def body(*refs):
        ins, outs = refs[:nw], refs[nw:2 * nw]
        send_sems, recv_sems, local_sems = refs[2 * nw:]
        x, y, c = _my_position()
        me = 4 * x + 2 * y + c
        local = [pltpu.make_async_copy(ins[w].at[me], outs[w].at[me], local_sems.at[w]) for w in range(nw)]
        for cp in local:
            cp.start()
        sends = []
        for k in range(1, N_DEV):
            peer, peer_pos = _peer(k)
            for w in range(nw):
                cp = pltpu.make_async_remote_copy(
                    src_ref=ins[w].at[peer_pos], dst_ref=outs[w].at[me], send_sem=send_sems.at[w, k],
                    recv_sem=recv_sems.at[w, k], device_id=peer, device_id_type=_MESH)
                cp.start()
                sends.append(cp)
        for k in range(1, N_DEV):
            peer, peer_pos = _peer(k)
            for w in range(nw):
                pltpu.make_async_remote_copy(
                    src_ref=ins[w].at[me], dst_ref=outs[w].at[peer_pos], send_sem=send_sems.at[w, k],
                    recv_sem=recv_sems.at[w, k], device_id=peer, device_id_type=_MESH).wait_recv()
        for cp in sends:
            cp.wait_send()
        for cp in local:
            cp.wait()

    any_spec = pl.BlockSpec(memory_space=pl.ANY)
    return pl.pallas_call(
        body, name=name,
        in_specs=[any_spec] * nw, out_specs=[any_spec] * nw,
        out_shape=[jax.ShapeDtypeStruct(f.shape, f.dtype) for f in fulls],
        scratch_shapes=[pltpu.SemaphoreType.DMA((nw, N_DEV)), pltpu.SemaphoreType.DMA((nw, N_DEV)),
                        pltpu.SemaphoreType.DMA((nw,))],
    )(*fulls)


def _sum_parts(parts_ref):
    g = parts_ref[0].astype(F32)
    for s in range(1, parts_ref.shape[0]):
        g = g + parts_ref[s].astype(F32)
    return g


def _row_tile(rows):
    for cand in (256, 128, 64, 32, 16, 8):
        if rows % cand == 0:
            return cand
    return rows


def _sum_devices(parts, name):
    s, r, c = parts.shape
    tr = _row_tile(r)

    def body(parts_ref, out_ref):
        out_ref[...] = _sum_parts(parts_ref)

    return pl.pallas_call(
        body, name=name, grid=(r // tr,),
        in_specs=[pl.BlockSpec((s, tr, c), lambda i: (0, i, 0))],
        out_specs=pl.BlockSpec((tr, c), lambda i: (i, 0)),
        out_shape=jax.ShapeDtypeStruct((r, c), F32),
        compiler_params=_params(("arbitrary",)),
    )(parts)


def _adamw(parts, w, m, v, name):
    s, r, c = parts.shape
    tr = _row_tile(r)
    bc1 = 1.0 - ADAM_B1 ** ADAM_STEP
    bc2 = 1.0 - ADAM_B2 ** ADAM_STEP

    def body(parts_ref, w_ref, m_ref, v_ref, g_ref, d_ref, nm_ref, nv_ref):
        g = _sum_parts(parts_ref)
        g_ref[...] = g
        nm = ADAM_B1 * m_ref[...] + (1.0 - ADAM_B1) * g
        nv = ADAM_B2 * v_ref[...] + (1.0 - ADAM_B2) * (g * g)
        nm_ref[...] = nm
        nv_ref[...] = nv
        d_ref[...] = -ADAM_LR * ((nm / bc1) / (jnp.sqrt(nv / bc2) + ADAM_EPS) + ADAM_WD * w_ref[...])

    blk = pl.BlockSpec((tr, c), lambda i: (i, 0))
    return pl.pallas_call(
        body, name=name, grid=(r // tr,),
        in_specs=[pl.BlockSpec((s, tr, c), lambda i: (0, i, 0)), blk, blk, blk],
        out_specs=[blk, blk, blk, blk],
        out_shape=[jax.ShapeDtypeStruct((r, c), F32)] * 4,
        compiler_params=_params(("arbitrary",)),
    )(parts, w, m, v)


_LANES = 128


def _pack(arrays):
    flat = jnp.concatenate([a.reshape(-1) for a in arrays])
    pad = (-flat.shape[0]) % (8 * _LANES)
    return jnp.pad(flat, (0, pad)).reshape(-1, _LANES)


def _unpack(packed, shapes):
    flat = packed.reshape(-1)
    out, off = [], 0
    for shp in shapes:
        size = math.prod(shp)
        out.append(flat[off:off + size].reshape(shp))
        off += size
    return out


def _spatial_weights(w_s, b_s):
    reps = TM // CHUNK
    tril = jnp.tril(jnp.ones((CHUNK, CHUNK), F32))
    wc = w_s * tril
    eye = jnp.eye(reps, dtype=F32)
    wc2 = jnp.einsum("ab,gts->gatbs", eye, wc).reshape(A_GROUPS, TM, TM)
    bs2 = jnp.tile(b_s, (1, reps)).reshape(A_GROUPS, TM, 1)
    return wc2.astype(BF16), jnp.swapaxes(wc2, 1, 2).astype(BF16), bs2


def _spatial_weight_grad(dwc2, dbs2):
    reps = TM // CHUNK
    tril = jnp.tril(jnp.ones((CHUNK, CHUNK), F32))
    blocks = dwc2.reshape(A_GROUPS, reps, CHUNK, reps, CHUNK)
    dws = sum(blocks[:, a, :, a, :] for a in range(reps)) * tril
    dbs = dbs2.reshape(A_GROUPS, reps, CHUNK).sum(axis=1)
    return dws, dbs


def _row2(a):
    return a.reshape(1, -1)


def kernel(x, a0_w_in, a0_v_gain, a0_v_bias, a0_w_s, a0_b_s, a0_w_out, ln0_gain, ln0_bias, b1_w_in, b1_w_grp, b1_scale, b1_w_out, ln1_gain, ln1_bias, c2_w_in, c2_conv_w, c2_w_out, ln2_gain, ln2_bias, a3_w_in, a3_v_gain, a3_v_bias, a3_w_s, a3_b_s, a3_w_out, ln3_gain, ln3_bias, loss_target, m_a0_w_in, m_a0_v_gain, m_a0_v_bias, m_a0_w_s, m_a0_b_s, m_a0_w_out, m_ln0_gain, m_ln0_bias, m_b1_w_in, m_b1_w_grp, m_b1_scale, m_b1_w_out, m_ln1_gain, m_ln1_bias, m_c2_w_in, m_c2_conv_w, m_c2_w_out, m_ln2_gain, m_ln2_bias, m_a3_w_in, m_a3_v_gain, m_a3_v_bias, m_a3_w_s, m_a3_b_s, m_a3_w_out, m_ln3_gain, m_ln3_bias, v_a0_w_in, v_a0_v_gain, v_a0_v_bias, v_a0_w_s, v_a0_b_s, v_a0_w_out, v_ln0_gain, v_ln0_bias, v_b1_w_in, v_b1_w_grp, v_b1_scale, v_b1_w_out, v_ln1_gain, v_ln1_bias, v_c2_w_in, v_c2_conv_w, v_c2_w_out, v_ln2_gain, v_ln2_bias, v_a3_w_in, v_a3_v_gain, v_a3_v_bias, v_a3_w_s, v_a3_b_s, v_a3_w_out, v_ln3_gain, v_ln3_bias):
    names = ["a0_w_in", "a0_v_gain", "a0_v_bias", "a0_w_s", "a0_b_s", "a0_w_out", "ln0_gain", "ln0_bias",
             "b1_w_in", "b1_w_grp", "b1_scale", "b1_w_out", "ln1_gain", "ln1_bias",
             "c2_w_in", "c2_conv_w", "c2_w_out", "ln2_gain", "ln2_bias",
             "a3_w_in", "a3_v_gain", "a3_v_bias", "a3_w_s", "a3_b_s", "a3_w_out", "ln3_gain", "ln3_bias"]
    env = dict(locals())
    w = {nm: env[nm] for nm in names}
    mom = {nm: env["m_" + nm] for nm in names}
    var = {nm: env["v_" + nm] for nm in names}

    x0 = x[0]
    target = loss_target[0]
    d_model = x0.shape[1]
    di = N_DEV * a0_w_out.shape[0]
    n_grp = len(POOL_WINDOWS)
    gd_b = di // n_grp

    big = ["a0_w_in", "a0_w_out", "b1_w_in", "b1_w_grp", "b1_w_out", "c2_w_in", "c2_w_out", "a3_w_in", "a3_w_out"]
    small = [nm for nm in names if nm not in big]
    layers = ("a0", "b1", "c2", "a3")
    conv_shape = c2_conv_w.shape
    gathered = _all_gather([w[nm].astype(BF16) for nm in big] + [_pack([c2_conv_w])], "gather_weights")
    conv_all = jnp.stack([_unpack(gathered[-1][j], [conv_shape])[0] for j in range(N_DEV)], axis=1)
    conv_full = conv_all.reshape(conv_shape[0], di)
    gathered = dict(zip(big, gathered))
    w_in = {nm: gathered[nm] for nm in big if nm.endswith("w_in")}
    w_out = {nm: gathered[nm].reshape(di, d_model) for nm in big if nm.endswith("w_out")}
    wgrp = jnp.swapaxes(gathered["b1_w_grp"], 0, 1).reshape(n_grp, gd_b, gd_b)
    spatial = {p: _spatial_weights(w[p + "_w_s"], w[p + "_b_s"]) for p in ("a0", "a3")}

    saved = {}
    h = x0
    for i, p in enumerate(layers):
        lng, lnb = _row2(w[f"ln{i}_gain"]), _row2(w[f"ln{i}_bias"])
        if p[0] == "a":
            wc2, _, bs2 = spatial[p]
            pp, xt, pre, h = _fwd_a(h, w_in[p + "_w_in"], w_out[p + "_w_out"], _row2(w[p + "_v_gain"]),
                                    _row2(w[p + "_v_bias"]), wc2, bs2, lng, lnb)
            saved[p] = (pp, xt, pre)
        elif p[0] == "b":
            z, mixed, poolt, xt, pre, h = _fwd_b(h, w_in[p + "_w_in"], wgrp, _row2(w[p + "_scale"]),
                                                 w_out[p + "_w_out"], lng, lnb)
            saved[p] = (z, mixed, poolt, xt, pre)
        else:
            pp, xt, pre, h = _fwd_c(h, w_in[p + "_w_in"], conv_full, w_out[p + "_w_out"], lng, lnb)
            saved[p] = (pp, xt, pre)

    gcur, sq = _loss_head(h, target)
    loss = lax.psum(jnp.sum(sq) * (0.5 / d_model), ("x", "y", "c"))

    part = {}
    full = {}
    for i, p in reversed(list(enumerate(layers))):
        lng = _row2(w[f"ln{i}_gain"])
        if p[0] == "a":
            pp, xt, pre = saved[p]
            wc2, wc2t, bs2 = spatial[p]
            dpre, dp, yt, dlng, dlnb, dgain, dbias, dbs2, dwc2 = _bwd_a(
                gcur, pre, pp, w_out[p + "_w_out"], _row2(w[p + "_v_gain"]), _row2(w[p + "_v_bias"]), wc2, wc2t, bs2, lng)
            part[p + "_v_gain"], part[p + "_v_bias"] = dgain.sum(axis=0), dbias.sum(axis=0)
            part[p + "_w_s"], part[p + "_b_s"] = _spatial_weight_grad(dwc2, dbs2)
        elif p[0] == "b":
            z, mixed, poolt, xt, pre = saved[p]
            dpre, dp, yt, dmixed, dlng, dlnb, dscale = _bwd_b(
                gcur, pre, z, mixed, w_out[p + "_w_out"], wgrp, _row2(w[p + "_scale"]), lng)
            part[p + "_scale"] = dscale.sum(axis=0)
            dwg = _wgrad(poolt, dmixed, n_grp, True, "wgrad_grp")
            full[p + "_w_grp"] = jnp.swapaxes(dwg.reshape(n_grp, N_DEV, gd_b // N_DEV, gd_b), 0, 1)
        else:
            pp, xt, pre = saved[p]
            dpre, dp, yt, dlng, dlnb, dcw = _bwd_c(gcur, pre, pp, w_out[p + "_w_out"], conv_full, lng)
            part[p + "_conv_w"] = dcw.sum(axis=1)
        part[f"ln{i}_gain"], part[f"ln{i}_bias"] = dlng.sum(axis=0), dlnb.sum(axis=0)
        full[p + "_w_out"] = _wgrad(yt, dpre, 1, False, "wgrad_out_" + p).reshape(N_DEV, di // N_DEV, d_model)
        full[p + "_w_in"] = _wgrad(xt, dp, N_DEV, False, "wgrad_in_" + p)
        gcur = _dx(dp, dpre, w_in[p + "_w_in"])
    grad_x = gcur[None]

    landed = dict(zip(big, _scatter_blocks([full[nm] for nm in big], "scatter_grads")))
    part_shapes = [part[nm].shape for nm in small]
    small_all = _all_gather([_pack([part[nm] for nm in small])], "gather_small_grads")[0]
    small_sum = dict(zip(small, _unpack(_sum_devices(small_all, "sum_small_grads"), part_shapes)))
    me = 4 * lax.axis_index("x") + 2 * lax.axis_index("y") + lax.axis_index("c")
    small_sum["c2_conv_w"] = lax.dynamic_slice_in_dim(small_sum["c2_conv_w"], me * conv_shape[1], conv_shape[1], axis=1)

    grads, deltas, new_m, new_v = {}, {}, {}, {}
    for nm in big:
        shp = w[nm].shape
        r2 = (math.prod(shp[:-1]), shp[-1])
        outs = _adamw(landed[nm].reshape((N_DEV,) + r2), w[nm].reshape(r2), mom[nm].reshape(r2), var[nm].reshape(r2),
                      "adamw_" + nm)
        grads[nm], deltas[nm], new_m[nm], new_v[nm] = (o.reshape(shp) for o in outs)
    small_shapes = [w[nm].shape for nm in small]
    outs = _adamw(_pack([small_sum[nm] for nm in small])[None], _pack([w[nm] for nm in small]),
                  _pack([mom[nm] for nm in small]), _pack([var[nm] for nm in small]), "adamw_small")
    for tgt, o in zip((grads, deltas, new_m, new_v), outs):
        tgt.update(zip(small, _unpack(o, small_shapes)))

    return (loss, grad_x, *[grads[nm] for nm in names], *[deltas[nm] for nm in names],
            *[new_m[nm] for nm in names], *[new_v[nm] for nm in names])
```

```python
import functools
import math

import jax
import jax.numpy as jnp
from jax import lax
from jax.experimental import pallas as pl
from jax.experimental.pallas import tpu as pltpu

F32 = jnp.float32
BF16 = jnp.bfloat16

N_DEV = 8
DEPTH = 4
CHUNK = 128
A_GROUPS = 8
POOL_WINDOWS = (2, 4, 8, 16)
LN_EPS = 1e-5
ALPHA = (2.0 * DEPTH) ** 0.25
ADAM_LR = 0.001
ADAM_B1 = 0.9
ADAM_B2 = 0.999
ADAM_EPS = 1e-08
ADAM_WD = 0.01
ADAM_STEP = 10

TM = 256
HALO = 16
CHALO = 8
CW = 512
VMEM_LIMIT_BYTES = 58 * 1024 * 1024

_NT = (((1,), (1,)), ((), ()))
_SQRT_2_OVER_PI = math.sqrt(2.0 / math.pi)
_MESH = pl.DeviceIdType.MESH


def _vmem():
    return pl.BlockSpec(memory_space=pltpu.VMEM)


def _params(sem=None):
    return pltpu.CompilerParams(dimension_semantics=sem, vmem_limit_bytes=VMEM_LIMIT_BYTES)


def _gelu(x):
    t = jnp.tanh(_SQRT_2_OVER_PI * (x + 0.044715 * (x * x * x)))
    return x * (0.5 * (1.0 + t))


def _gelu_and_grad(x):
    x2 = x * x
    t = jnp.tanh(_SQRT_2_OVER_PI * (x + 0.044715 * (x * x2)))
    cdf = 0.5 * (1.0 + t)
    grad = cdf + 0.5 * x * (1.0 - t * t) * (_SQRT_2_OVER_PI * (1.0 + 3.0 * 0.044715 * x2))
    return x * cdf, grad


def _sigmoid(z):
    return 1.0 / (1.0 + jnp.exp(-z))


def _fold8(a):
    return a.reshape(a.shape[0] // 8, 8, a.shape[1]).sum(axis=0)


def _row_mean(a):
    return jnp.mean(a, axis=-1, keepdims=True)


def _ln_stats(x):
    mu = _row_mean(x)
    xc = x - mu
    rstd = lax.rsqrt(_row_mean(xc * xc) + LN_EPS)
    return xc * rstd, rstd


def _post_norm(x, out, lng_ref, lnb_ref, pre_ref, xn_ref):
    pre = ALPHA * x + out
    pre_ref[...] = pre
    xhat, _ = _ln_stats(pre)
    xn_ref[...] = xhat * lng_ref[...] + lnb_ref[...]


def _post_norm_bwd(g_ref, pre_ref, lng_ref, dpre_ref, dlng_ref, dlnb_ref):
    go = g_ref[...]
    xhat, rstd = _ln_stats(pre_ref[...])
    dlng_ref[...] += _fold8(go * xhat)
    dlnb_ref[...] += _fold8(go)
    dxh = go * lng_ref[...]
    dpre = rstd * (dxh - _row_mean(dxh) - xhat * _row_mean(dxh * xhat))
    dpre_ref[...] = dpre
    return dpre


def _in_proj(xb, win_ref, p_ref):
    cs = win_ref.shape[2]
    for j in range(N_DEV):
        p_ref[:, j * cs:(j + 1) * cs] = jnp.dot(xb, win_ref[j], preferred_element_type=F32)


def _zero_at_first_step(*refs):
    @pl.when(pl.program_id(0) == 0)
    def _():
        for r in refs:
            r[...] = jnp.zeros(r.shape, r.dtype)


def _my_position():
    x, y, c = lax.axis_index("x"), lax.axis_index("y"), lax.axis_index("c")
    return (x, y, c), 4 * x + 2 * y + c


def _peer(k):
    (x, y, c), _ = _my_position()
    peer = (x ^ (k >> 2), y ^ ((k >> 1) & 1), c ^ (k & 1))
    return peer, 4 * peer[0] + 2 * peer[1] + peer[2]


class _Exchange:
    def __init__(self, gathers=(), scatters=()):
        self.args = list(gathers) + list(scatters)
        self.n_gather = len(gathers)
        self.out_shape = ([jax.ShapeDtypeStruct((N_DEV,) + a.shape, a.dtype) for a in gathers]
                          + [jax.ShapeDtypeStruct(a.shape, a.dtype) for a in scatters])
        n = len(self.args)
        self.scratch = [pltpu.SemaphoreType.DMA((n, N_DEV)), pltpu.SemaphoreType.DMA((n, N_DEV)),
                        pltpu.SemaphoreType.DMA((n,))]

    def _src(self, ins, w, pos):
        return ins[w] if w < self.n_gather else ins[w].at[pos]

    def _copies(self, ins, outs, sems, arrivals):
        send_sems, recv_sems, local_sems = sems
        _, me = _my_position()
        n = len(self.args)
        copies = []
        if not arrivals:
            copies = [pltpu.make_async_copy(self._src(ins, w, me), outs[w].at[me], local_sems.at[w]) for w in range(n)]
        for k in range(1, N_DEV):
            peer, peer_pos = _peer(k)
            for w in range(n):
                copies.append(pltpu.make_async_remote_copy(
                    src_ref=self._src(ins, w, me if arrivals else peer_pos),
                    dst_ref=outs[w].at[peer_pos if arrivals else me],
                    send_sem=send_sems.at[w, k], recv_sem=recv_sems.at[w, k], device_id=peer, device_id_type=_MESH))
        return copies

    def start(self, ins, outs, sems):
        for cp in self._copies(ins, outs, sems, False):
            cp.start()

    def wait(self, ins, outs, sems):
        n = len(self.args)
        for cp in self._copies(ins, outs, sems, True):
            cp.wait_recv()
        own = self._copies(ins, outs, sems, False)
        for cp in own[n:]:
            cp.wait_send()
        for cp in own[:n]:
            cp.wait()


def _call(body, name, grid, args, in_specs, out_shape, out_specs, scratch=(), exchange=None):
    sem = ("arbitrary",) * len(grid)
    if exchange is None:
        outs = pl.pallas_call(body, name=name, grid=grid, in_specs=in_specs, out_specs=out_specs, out_shape=out_shape,
                              scratch_shapes=list(scratch), compiler_params=_params(sem))(*args)
        return outs, []
    n_in, n_out, n_scr, n_ex = len(args), len(out_shape), len(scratch), len(exchange.args)

    def hosted(*refs):
        ex_in = refs[n_in:n_in + n_ex]
        o0 = n_in + n_ex
        ex_out = refs[o0 + n_out:o0 + n_out + n_ex]
        s0 = o0 + n_out + n_ex
        sems = refs[s0 + n_scr:]
        first = functools.reduce(jnp.logical_and, [pl.program_id(a) == 0 for a in range(len(grid))])
        last = functools.reduce(jnp.logical_and, [pl.program_id(a) == grid[a] - 1 for a in range(len(grid))])

        @pl.when(first)
        def _():
            exchange.start(ex_in, ex_out, sems)

        body(*refs[:n_in], *refs[o0:o0 + n_out], *refs[s0:s0 + n_scr])

        @pl.when(last)
        def _():
            exchange.wait(ex_in, ex_out, sems)

    any_spec = pl.BlockSpec(memory_space=pl.ANY)
    outs = pl.pallas_call(
        hosted, name=name, grid=grid, in_specs=list(in_specs) + [any_spec] * n_ex,
        out_specs=list(out_specs) + [any_spec] * n_ex, out_shape=list(out_shape) + exchange.out_shape,
        scratch_shapes=list(scratch) + exchange.scratch, compiler_params=_params(sem))(*args, *exchange.args)
    return outs[:n_out], outs[n_out:]


def _exchange_only(exchange, name):
    n = len(exchange.args)

    def body(*refs):
        exchange.start(refs[:n], refs[n:2 * n], refs[2 * n:])
        exchange.wait(refs[:n], refs[n:2 * n], refs[2 * n:])

    any_spec = pl.BlockSpec(memory_space=pl.ANY)
    return pl.pallas_call(body, name=name, in_specs=[any_spec] * n, out_specs=[any_spec] * n,
                          out_shape=exchange.out_shape, scratch_shapes=exchange.scratch)(*exchange.args)


def _a_v_path(p_ref, gain_ref, bias_ref, vh_scr, vn_scr, di):
    tm = p_ref.shape[0]
    s1 = jnp.zeros((tm, 1), F32)
    for c in range(di // CW):
        sl = slice(c * CW, (c + 1) * CW)
        vg = _gelu(p_ref[:, di + c * CW:di + (c + 1) * CW])
        vh_scr[:, sl] = vg
        s1 += jnp.sum(vg, axis=1, keepdims=True)
    mu = s1 * (1.0 / di)
    s2 = jnp.zeros((tm, 1), F32)
    for c in range(di // CW):
        sl = slice(c * CW, (c + 1) * CW)
        d = vh_scr[:, sl] - mu
        s2 += jnp.sum(d * d, axis=1, keepdims=True)
    rstd = lax.rsqrt(s2 * (1.0 / di) + LN_EPS)
    for c in range(di // CW):
        sl = slice(c * CW, (c + 1) * CW)
        vh = (vh_scr[:, sl] - mu) * rstd
        vh_scr[:, sl] = vh
        vn_scr[:, sl] = (vh * gain_ref[:, sl] + bias_ref[:, sl]).astype(BF16)
    return rstd


def _fwd_a(x, win, wout, gain, bias, wc2, bs2, lng, lnb, exchange=None):
    t, d = x.shape
    cs = win.shape[2]
    di = wout.shape[0]
    gd = di // A_GROUPS
    n = N_DEV * cs

    def body(x_ref, win_ref, wout_ref, gain_ref, bias_ref, wc_ref, bs_ref, lng_ref, lnb_ref,
             p_ref, xt_ref, pre_ref, xn_ref, vh_scr, vn_scr, y_scr):
        xv = x_ref[...]
        xt_ref[...] = xv.T.astype(BF16)
        _in_proj(xv.astype(BF16), win_ref, p_ref)
        _a_v_path(p_ref, gain_ref, bias_ref, vh_scr, vn_scr, di)
        for g in range(A_GROUPS):
            sl = slice(g * gd, (g + 1) * gd)
            sv = jnp.dot(wc_ref[g], vn_scr[:, sl], preferred_element_type=F32) + bs_ref[g]
            z = p_ref[:, 2 * di + g * gd:2 * di + (g + 1) * gd]
            y_scr[:, sl] = (_gelu(p_ref[:, sl]) * sv * (z * _sigmoid(z))).astype(BF16)
        out = jnp.dot(y_scr[...], wout_ref[...], preferred_element_type=F32)
        _post_norm(xv, out, lng_ref, lnb_ref, pre_ref, xn_ref)

    row = lambda i: (i, 0)
    return _call(
        body, "fwd_a", (t // TM,), (x, win, wout, gain, bias, wc2, bs2, lng, lnb),
        in_specs=[pl.BlockSpec((TM, d), row), _vmem(), _vmem(), _vmem(), _vmem(), _vmem(), _vmem(), _vmem(), _vmem()],
        out_specs=[pl.BlockSpec((TM, n), row), pl.BlockSpec((d, TM), lambda i: (0, i)),
                   pl.BlockSpec((TM, d), row), pl.BlockSpec((TM, d), row)],
        out_shape=[jax.ShapeDtypeStruct((t, n), F32), jax.ShapeDtypeStruct((d, t), BF16),
                   jax.ShapeDtypeStruct((t, d), F32), jax.ShapeDtypeStruct((t, d), F32)],
        scratch=[pltpu.VMEM((TM, di), F32), pltpu.VMEM((TM, di), BF16), pltpu.VMEM((TM, di), BF16)],
        exchange=exchange)


def _bwd_a(g, pre, p, wout, gain, bias, wc2, wc2t, bs2, lng, exchange=None):
    t, d = g.shape
    di = wout.shape[0]
    gd = di // A_GROUPS
    n = p.shape[1]

    def body(g_ref, pre_ref, p_ref, wout_ref, gain_ref, bias_ref, wc_ref, wct_ref, bs_ref, lng_ref,
             dpre_ref, dp_ref, yt_ref, dlng_ref, dlnb_ref, dgain_ref, dbias_ref, dbs_ref, dwc_ref,
             dy_scr, vh_scr, vn_scr, dv_scr, y_scr):
        _zero_at_first_step(dlng_ref, dlnb_ref, dgain_ref, dbias_ref, dbs_ref, dwc_ref)
        dpre = _post_norm_bwd(g_ref, pre_ref, lng_ref, dpre_ref, dlng_ref, dlnb_ref)
        dy_scr[...] = lax.dot_general(dpre.astype(BF16), wout_ref[...], _NT, preferred_element_type=F32)
        rstd_v = _a_v_path(p_ref, gain_ref, bias_ref, vh_scr, vn_scr, di)
        for grp in range(A_GROUPS):
            sl = slice(grp * gd, (grp + 1) * gd)
            vn = vn_scr[:, sl]
            sv = jnp.dot(wc_ref[grp], vn, preferred_element_type=F32) + bs_ref[grp]
            u, du = _gelu_and_grad(p_ref[:, sl])
            z = p_ref[:, 2 * di + grp * gd:2 * di + (grp + 1) * gd]
            sg = _sigmoid(z)
            s = z * sg
            ds = sg * (1.0 + z * (1.0 - sg))
            dy = dy_scr[:, sl]
            us = u * s
            y_scr[:, sl] = us * sv
            dys = dy * sv
            dp_ref[:, sl] = (dys * s * du).astype(BF16)
            dp_ref[:, 2 * di + grp * gd:2 * di + (grp + 1) * gd] = (dys * u * ds).astype(BF16)
            dsv = dy * us
            dbs_ref[grp] += jnp.sum(dsv, axis=1, keepdims=True)
            dsvb = dsv.astype(BF16)
            dwc_ref[grp] += lax.dot_general(dsvb, vn, _NT, preferred_element_type=F32)
            dv_scr[:, sl] = jnp.dot(wct_ref[grp], dsvb, preferred_element_type=F32)
        yt_ref[...] = y_scr[...].T.astype(BF16)
        tm = dv_scr.shape[0]
        a1 = jnp.zeros((tm, 1), F32)
        a2 = jnp.zeros((tm, 1), F32)
        for c in range(di // CW):
            sl = slice(c * CW, (c + 1) * CW)
            dv = dv_scr[:, sl]
            vh = vh_scr[:, sl]
            dgain_ref[:, sl] += _fold8(dv * vh)
            dbias_ref[:, sl] += _fold8(dv)
            dvh = dv * gain_ref[:, sl]
            a1 += jnp.sum(dvh, axis=1, keepdims=True)
            a2 += jnp.sum(dvh * vh, axis=1, keepdims=True)
        m1 = a1 * (1.0 / di)
        m2 = a2 * (1.0 / di)
        for c in range(di // CW):
            sl = slice(c * CW, (c + 1) * CW)
            dvg = rstd_v * (dv_scr[:, sl] * gain_ref[:, sl] - m1 - vh_scr[:, sl] * m2)
            _, dgel = _gelu_and_grad(p_ref[:, di + c * CW:di + (c + 1) * CW])
            dp_ref[:, di + c * CW:di + (c + 1) * CW] = (dvg * dgel).astype(BF16)

    row = lambda i: (i, 0)
    const2 = lambda i: (0, 0)
    const3 = lambda i: (0, 0, 0)
    return _call(
        body, "bwd_a", (t // TM,), (g, pre, p, wout, gain, bias, wc2, wc2t, bs2, lng),
        in_specs=[pl.BlockSpec((TM, d), row), pl.BlockSpec((TM, d), row), pl.BlockSpec((TM, n), row),
                  _vmem(), _vmem(), _vmem(), _vmem(), _vmem(), _vmem(), _vmem()],
        out_specs=[pl.BlockSpec((TM, d), row), pl.BlockSpec((TM, n), row), pl.BlockSpec((di, TM), lambda i: (0, i)),
                   pl.BlockSpec((8, d), const2), pl.BlockSpec((8, d), const2),
                   pl.BlockSpec((8, di), const2), pl.BlockSpec((8, di), const2),
                   pl.BlockSpec((A_GROUPS, TM, 1), const3), pl.BlockSpec((A_GROUPS, TM, TM), const3)],
        out_shape=[jax.ShapeDtypeStruct((t, d), F32), jax.ShapeDtypeStruct((t, n), BF16),
                   jax.ShapeDtypeStruct((di, t), BF16),
                   jax.ShapeDtypeStruct((8, d), F32), jax.ShapeDtypeStruct((8, d), F32),
                   jax.ShapeDtypeStruct((8, di), F32), jax.ShapeDtypeStruct((8, di), F32),
                   jax.ShapeDtypeStruct((A_GROUPS, TM, 1), F32), jax.ShapeDtypeStruct((A_GROUPS, TM, TM), F32)],
        scratch=[pltpu.VMEM((TM, di), F32), pltpu.VMEM((TM, di), F32), pltpu.VMEM((TM, di), BF16),
                 pltpu.VMEM((TM, di), F32), pltpu.VMEM((TM, di), F32)],
        exchange=exchange)


def _inv_count(tile, window):
    pos = tile * TM + lax.broadcasted_iota(jnp.int32, (TM, 1), 0)
    return 1.0 / jnp.minimum(pos + 1, window).astype(F32)


def _window_sum(ext, window, down):
    rows = ext.shape[0]
    k = 1
    while k < window:
        ext = ext + pltpu.roll(ext, k if down else rows - k, 0)
        k *= 2
    return ext


def _fwd_b(x, win, wgrp, scale, wout, lng, lnb, exchange=None):
    t, d = x.shape
    cs = win.shape[2]
    di = wout.shape[0]
    gd = di // len(POOL_WINDOWS)

    def body(x_ref, win_ref, wgrp_ref, scale_ref, wout_ref, lng_ref, lnb_ref,
             z_ref, mixed_ref, poolt_ref, xt_ref, pre_ref, xn_ref, p_scr, ext_scr, y_scr):
        i = pl.program_id(0)

        @pl.when(i == 0)
        def _():
            ext_scr[0:HALO, :] = jnp.zeros((HALO, di), F32)

        xv = x_ref[...]
        xt_ref[...] = xv.T.astype(BF16)
        _in_proj(xv.astype(BF16), win_ref, p_scr)
        z_ref[...] = p_scr[:, di:]
        ext_scr[HALO:, :] = p_scr[:, :di]
        for grp, window in enumerate(POOL_WINDOWS):
            sl = slice(grp * gd, (grp + 1) * gd)
            ext = ext_scr[:, sl]
            v = ext[HALO:]
            pooled = (_window_sum(ext, window, True)[HALO:] * _inv_count(i, window) - v).astype(BF16)
            poolt_ref[sl, :] = pooled.astype(F32).T.astype(BF16)
            mixed = jnp.dot(pooled, wgrp_ref[grp], preferred_element_type=F32)
            mixed_ref[:, sl] = mixed
            z = p_scr[:, di + grp * gd:di + (grp + 1) * gd]
            y_scr[:, sl] = (mixed * scale_ref[:, sl] * (z * _sigmoid(z))).astype(BF16)
        ext_scr[0:HALO, :] = ext_scr[TM:TM + HALO, :]
        out = jnp.dot(y_scr[...], wout_ref[...], preferred_element_type=F32)
        _post_norm(xv, out, lng_ref, lnb_ref, pre_ref, xn_ref)

    row = lambda i: (i, 0)
    col = lambda i: (0, i)
    return _call(
        body, "fwd_b", (t // TM,), (x, win, wgrp, scale, wout, lng, lnb),
        in_specs=[pl.BlockSpec((TM, d), row), _vmem(), _vmem(), _vmem(), _vmem(), _vmem(), _vmem()],
        out_specs=[pl.BlockSpec((TM, di), row), pl.BlockSpec((TM, di), row), pl.BlockSpec((di, TM), col),
                   pl.BlockSpec((d, TM), col), pl.BlockSpec((TM, d), row), pl.BlockSpec((TM, d), row)],
        out_shape=[jax.ShapeDtypeStruct((t, di), F32), jax.ShapeDtypeStruct((t, di), F32),
                   jax.ShapeDtypeStruct((di, t), BF16), jax.ShapeDtypeStruct((d, t), BF16),
                   jax.ShapeDtypeStruct((t, d), F32), jax.ShapeDtypeStruct((t, d), F32)],
        scratch=[pltpu.VMEM((TM, 2 * di), F32), pltpu.VMEM((TM + HALO, di), F32), pltpu.VMEM((TM, di), BF16)],
        exchange=exchange)


def _bwd_b(g, pre, z, mixed, wout, wgrp, scale, lng, exchange=None):
    t, d = g.shape
    di = wout.shape[0]
    gd = di // len(POOL_WINDOWS)
    nt = t // TM

    def body(g_ref, pre_ref, z_ref, mixed_ref, wout_ref, wgrp_ref, scale_ref, lng_ref,
             dpre_ref, dp_ref, yt_ref, dmix_ref, dlng_ref, dlnb_ref, dscale_ref,
             dy_scr, ext_scr, y_scr):
        i = pl.program_id(0)
        tile = nt - 1 - i
        _zero_at_first_step(dlng_ref, dlnb_ref, dscale_ref)

        @pl.when(i == 0)
        def _():
            ext_scr[TM:, :] = jnp.zeros((HALO, di), F32)

        dpre = _post_norm_bwd(g_ref, pre_ref, lng_ref, dpre_ref, dlng_ref, dlnb_ref)
        dy_scr[...] = lax.dot_general(dpre.astype(BF16), wout_ref[...], _NT, preferred_element_type=F32)
        for grp, window in enumerate(POOL_WINDOWS):
            sl = slice(grp * gd, (grp + 1) * gd)
            zz = z_ref[:, sl]
            sg = _sigmoid(zz)
            s = zz * sg
            ds = sg * (1.0 + zz * (1.0 - sg))
            mixed_v = mixed_ref[:, sl]
            dy = dy_scr[:, sl]
            sc = scale_ref[:, sl]
            y_scr[:, sl] = mixed_v * sc * s
            dym = dy * mixed_v
            dp_ref[:, di + grp * gd:di + (grp + 1) * gd] = (dym * sc * ds).astype(BF16)
            dscale_ref[:, sl] += _fold8(dym * s)
            dmixed = (dy * sc * s).astype(BF16)
            dmix_ref[:, sl] = dmixed
            dpooled = lax.dot_general(dmixed, wgrp_ref[grp], _NT, preferred_element_type=F32)
            ext_scr[0:TM, sl] = dpooled * _inv_count(tile, window)
            dv = _window_sum(ext_scr[:, sl], window, False)[0:TM] - dpooled
            dp_ref[:, sl] = dv.astype(BF16)
        yt_ref[...] = y_scr[...].T.astype(BF16)
        ext_scr[TM:, :] = ext_scr[0:HALO, :]

    rrow = lambda i: (nt - 1 - i, 0)
    rcol = lambda i: (0, nt - 1 - i)
    const2 = lambda i: (0, 0)
    return _call(
        body, "bwd_b", (nt,), (g, pre, z, mixed, wout, wgrp, scale, lng),
        in_specs=[pl.BlockSpec((TM, d), rrow), pl.BlockSpec((TM, d), rrow), pl.BlockSpec((TM, di), rrow),
                  pl.BlockSpec((TM, di), rrow), _vmem(), _vmem(), _vmem(), _vmem()],
        out_specs=[pl.BlockSpec((TM, d), rrow), pl.BlockSpec((TM, 2 * di), rrow), pl.BlockSpec((di, TM), rcol),
                   pl.BlockSpec((TM, di), rrow), pl.BlockSpec((8, d), const2), pl.BlockSpec((8, d), const2),
                   pl.BlockSpec((8, di), const2)],
        out_shape=[jax.ShapeDtypeStruct((t, d), F32), jax.ShapeDtypeStruct((t, 2 * di), BF16),
                   jax.ShapeDtypeStruct((di, t), BF16), jax.ShapeDtypeStruct((t, di), BF16),
                   jax.ShapeDtypeStruct((8, d), F32), jax.ShapeDtypeStruct((8, d), F32),
                   jax.ShapeDtypeStruct((8, di), F32)],
        scratch=[pltpu.VMEM((TM, di), F32), pltpu.VMEM((TM + HALO, di), F32), pltpu.VMEM((TM, di), F32)],
        exchange=exchange)


def _conv3(ext_ref, sl, cw_ref):
    ext = ext_ref[:, sl]
    return (pltpu.roll(ext, 2, 0)[CHALO:] * cw_ref[0:1, sl] + pltpu.roll(ext, 1, 0)[CHALO:] * cw_ref[1:2, sl]
            + ext[CHALO:] * cw_ref[2:3, sl])


def _fwd_c(x, win, convw, wout, lng, lnb, exchange=None):
    t, d = x.shape
    cs = win.shape[2]
    di = wout.shape[0]
    n = N_DEV * cs

    def body(x_ref, win_ref, cw_ref, wout_ref, lng_ref, lnb_ref,
             p_ref, xt_ref, pre_ref, xn_ref, ext_scr, y_scr):
        i = pl.program_id(0)

        @pl.when(i == 0)
        def _():
            ext_scr[0:CHALO, :] = jnp.zeros((CHALO, di), F32)

        xv = x_ref[...]
        xt_ref[...] = xv.T.astype(BF16)
        _in_proj(xv.astype(BF16), win_ref, p_ref)
        for c in range(di // CW):
            sl = slice(c * CW, (c + 1) * CW)
            ext_scr[CHALO:, sl] = p_ref[:, di + c * CW:di + (c + 1) * CW] * p_ref[:, 2 * di + c * CW:2 * di + (c + 1) * CW]
            z = p_ref[:, 3 * di + c * CW:3 * di + (c + 1) * CW]
            y_scr[:, sl] = (p_ref[:, sl] * _conv3(ext_scr, sl, cw_ref) * (z * _sigmoid(z))).astype(BF16)
        ext_scr[0:CHALO, :] = ext_scr[TM:TM + CHALO, :]
        out = jnp.dot(y_scr[...], wout_ref[...], preferred_element_type=F32)
        _post_norm(xv, out, lng_ref, lnb_ref, pre_ref, xn_ref)

    row = lambda i: (i, 0)
    return _call(
        body, "fwd_c", (t // TM,), (x, win, convw, wout, lng, lnb),
        in_specs=[pl.BlockSpec((TM, d), row), _vmem(), _vmem(), _vmem(), _vmem(), _vmem()],
        out_specs=[pl.BlockSpec((TM, n), row), pl.BlockSpec((d, TM), lambda i: (0, i)),
                   pl.BlockSpec((TM, d), row), pl.BlockSpec((TM, d), row)],
        out_shape=[jax.ShapeDtypeStruct((t, n), F32), jax.ShapeDtypeStruct((d, t), BF16),
                   jax.ShapeDtypeStruct((t, d), F32), jax.ShapeDtypeStruct((t, d), F32)],
        scratch=[pltpu.VMEM((TM + CHALO, di), F32), pltpu.VMEM((TM, di), BF16)],
        exchange=exchange)


def _bwd_c(g, pre, p, wout, convw, lng, exchange=None):
    t, d = g.shape
    di = wout.shape[0]
    n = p.shape[1]
    nt = t // TM
    halo_blocks = TM // CHALO

    def body(g_ref, pre_ref, p_ref, pprev_ref, wout_ref, cw_ref, lng_ref,
             dpre_ref, dp_ref, yt_ref, dlng_ref, dlnb_ref, dcw_ref,
             dy_scr, extq_scr, extd_scr, y_scr):
        i = pl.program_id(0)
        tile = nt - 1 - i
        _zero_at_first_step(dlng_ref, dlnb_ref, dcw_ref)

        @pl.when(i == 0)
        def _():
            extd_scr[TM:, :] = jnp.zeros((CHALO, di), F32)

        dpre = _post_norm_bwd(g_ref, pre_ref, lng_ref, dpre_ref, dlng_ref, dlnb_ref)
        dy_scr[...] = lax.dot_general(dpre.astype(BF16), wout_ref[...], _NT, preferred_element_type=F32)
        has_prev = (tile > 0).astype(F32)
        for c in range(di // CW):
            sl = slice(c * CW, (c + 1) * CW)
            bb = p_ref[:, sl]
            cc = p_ref[:, di + c * CW:di + (c + 1) * CW]
            hh = p_ref[:, 2 * di + c * CW:2 * di + (c + 1) * CW]
            zz = p_ref[:, 3 * di + c * CW:3 * di + (c + 1) * CW]
            q = cc * hh
            extq_scr[0:CHALO, sl] = (pprev_ref[:, di + c * CW:di + (c + 1) * CW]
                                     * pprev_ref[:, 2 * di + c * CW:2 * di + (c + 1) * CW]) * has_prev
            extq_scr[CHALO:, sl] = q
            conv = _conv3(extq_scr, sl, cw_ref)
            sg = _sigmoid(zz)
            s = zz * sg
            ds = sg * (1.0 + zz * (1.0 - sg))
            dy = dy_scr[:, sl]
            bs = bb * s
            y_scr[:, sl] = bs * conv
            dyc = dy * conv
            dp_ref[:, sl] = (dyc * s).astype(BF16)
            dp_ref[:, 3 * di + c * CW:3 * di + (c + 1) * CW] = (dyc * bb * ds).astype(BF16)
            dconv = dy * bs
            extd_scr[0:TM, sl] = dconv
            ext = extd_scr[:, sl]
            rows = TM + CHALO
            d1 = pltpu.roll(ext, rows - 1, 0)[0:TM]
            d2 = pltpu.roll(ext, rows - 2, 0)[0:TM]
            dq = dconv * cw_ref[2:3, sl] + d1 * cw_ref[1:2, sl] + d2 * cw_ref[0:1, sl]
            dcw_ref[0, :, sl] += _fold8(q * d2)
            dcw_ref[1, :, sl] += _fold8(q * d1)
            dcw_ref[2, :, sl] += _fold8(q * dconv)
            dp_ref[:, di + c * CW:di + (c + 1) * CW] = (dq * hh).astype(BF16)
            dp_ref[:, 2 * di + c * CW:2 * di + (c + 1) * CW] = (dq * cc).astype(BF16)
        yt_ref[...] = y_scr[...].T.astype(BF16)
        extd_scr[TM:, :] = extd_scr[0:CHALO, :]

    rrow = lambda i: (nt - 1 - i, 0)
    rcol = lambda i: (0, nt - 1 - i)
    prev = lambda i: (jnp.maximum((nt - 1 - i) * halo_blocks - 1, 0), 0)
    const2 = lambda i: (0, 0)
    return _call(
        body, "bwd_c", (nt,), (g, pre, p, p, wout, convw, lng),
        in_specs=[pl.BlockSpec((TM, d), rrow), pl.BlockSpec((TM, d), rrow), pl.BlockSpec((TM, n), rrow),
                  pl.BlockSpec((CHALO, n), prev), _vmem(), _vmem(), _vmem()],
        out_specs=[pl.BlockSpec((TM, d), rrow), pl.BlockSpec((TM, n), rrow), pl.BlockSpec((di, TM), rcol),
                   pl.BlockSpec((8, d), const2), pl.BlockSpec((8, d), const2),
                   pl.BlockSpec((3, 8, di), lambda i: (0, 0, 0))],
        out_shape=[jax.ShapeDtypeStruct((t, d), F32), jax.ShapeDtypeStruct((t, n), BF16),
                   jax.ShapeDtypeStruct((di, t), BF16),
                   jax.ShapeDtypeStruct((8, d), F32), jax.ShapeDtypeStruct((8, d), F32),
                   jax.ShapeDtypeStruct((3, 8, di), F32)],
        scratch=[pltpu.VMEM((TM, di), F32), pltpu.VMEM((TM + CHALO, di), F32),
                 pltpu.VMEM((TM + CHALO, di), F32), pltpu.VMEM((TM, di), F32)],
        exchange=exchange)


def _dx(dp, dpre, win, exchange=None):
    t, d = dpre.shape
    n = dp.shape[1]
    cs = win.shape[2]

    def body(dp_ref, dpre_ref, win_ref, dx_ref):
        acc = ALPHA * dpre_ref[...]
        for j in range(N_DEV):
            acc += lax.dot_general(dp_ref[:, j * cs:(j + 1) * cs], win_ref[j], _NT, preferred_element_type=F32)
        dx_ref[...] = acc

    row = lambda i: (i, 0)
    (dx,), ex = _call(
        body, "dx", (t // TM,), (dp, dpre, win),
        in_specs=[pl.BlockSpec((TM, n), row), pl.BlockSpec((TM, d), row), _vmem()],
        out_specs=[pl.BlockSpec((TM, d), row)],
        out_shape=[jax.ShapeDtypeStruct((t, d), F32)],
        exchange=exchange)
    return dx, ex


def _wgrad(at, b, nb, per_block_rows, name, exchange=None):
    m_all, t = at.shape
    tn = b.shape[1] // nb
    m = m_all // nb if per_block_rows else m_all
    tk = min(512, t)
    nk = t // tk

    def body(at_ref, b_ref, out_ref, acc):
        k = pl.program_id(1)

        @pl.when(k == 0)
        def _():
            acc[...] = jnp.zeros(acc.shape, F32)

        acc[...] += jnp.dot(at_ref[...], b_ref[...].astype(BF16), preferred_element_type=F32)

        @pl.when(k == nk - 1)
        def _():
            out_ref[0] = acc[...].astype(BF16)

    at_map = (lambda j, k: (j, k)) if per_block_rows else (lambda j, k: (0, k))
    (out,), ex = _call(
        body, name, (nb, nk), (at, b),
        in_specs=[pl.BlockSpec((m, tk), at_map), pl.BlockSpec((tk, tn), lambda j, k: (k, j))],
        out_specs=[pl.BlockSpec((1, m, tn), lambda j, k: (j, 0, 0))],
        out_shape=[jax.ShapeDtypeStruct((nb, m, tn), BF16)],
        scratch=[pltpu.VMEM((m, tn), F32)],
        exchange=exchange)
    return out, ex


def _loss_head(y, target):
    t, d = y.shape

    def body(y_ref, t_ref, dy_ref, sq_ref):
        _zero_at_first_step(sq_ref)
        diff = y_ref[...] - t_ref[...]
        dy_ref[...] = diff * (1.0 / d)
        sq_ref[...] += _fold8(diff * diff)

    row = lambda i: (i, 0)
    return pl.pallas_call(
        body, name="loss_head", grid=(t // TM,),
        in_specs=[pl.BlockSpec((TM, d), row), pl.BlockSpec((TM, d), row)],
        out_specs=[pl.BlockSpec((TM, d), row), pl.BlockSpec((8, d), lambda i: (0, 0))],
        out_shape=[jax.ShapeDtypeStruct((t, d), F32), jax.ShapeDtypeStruct((8, d), F32)],
        compiler_params=_params(("arbitrary",)),
    )(y, target)


ADAMW_BLOCK_BYTES = 6 * 1024 * 1024


def _sum_parts(parts_ref):
    g = parts_ref[0].astype(F32)
    for s in range(1, parts_ref.shape[0]):
        g = g + parts_ref[s].astype(F32)
    return g


def _row_tile(rows, bytes_per_row):
    if rows * bytes_per_row <= ADAMW_BLOCK_BYTES:
        return rows
    best = 8
    for cand in range(8, rows, 8):
        if rows % cand == 0 and cand * bytes_per_row <= ADAMW_BLOCK_BYTES:
            best = cand
    return best


def _adamw(parts, w, m, v, name):
    s, r, c = parts.shape
    tr = _row_tile(r, c * (s * parts.dtype.itemsize + 7 * 4))
    bc1 = 1.0 - ADAM_B1 ** ADAM_STEP
    bc2 = 1.0 - ADAM_B2 ** ADAM_STEP

    def body(parts_ref, w_ref, m_ref, v_ref, g_ref, d_ref, nm_ref, nv_ref):
        g = _sum_parts(parts_ref)
        g_ref[...] = g
        nm = ADAM_B1 * m_ref[...] + (1.0 - ADAM_B1) * g
        nv = ADAM_B2 * v_ref[...] + (1.0 - ADAM_B2) * (g * g)
        nm_ref[...] = nm
        nv_ref[...] = nv
        d_ref[...] = -ADAM_LR * ((nm / bc1) / (jnp.sqrt(nv / bc2) + ADAM_EPS) + ADAM_WD * w_ref[...])

    blk = pl.BlockSpec((tr, c), lambda i: (i, 0))
    return pl.pallas_call(
        body, name=name, grid=(r // tr,),
        in_specs=[pl.BlockSpec((s, tr, c), lambda i: (0, i, 0)), blk, blk, blk],
        out_specs=[blk, blk, blk, blk],
        out_shape=[jax.ShapeDtypeStruct((r, c), F32)] * 4,
        compiler_params=_params(("arbitrary",)),
    )(parts, w, m, v)


_LANES = 128


def _pack(arrays):
    flat = jnp.concatenate([a.reshape(-1) for a in arrays])
    pad = (-flat.shape[0]) % (8 * _LANES)
    return jnp.pad(flat, (0, pad)).reshape(-1, _LANES)


def _unpack(packed, shapes):
    flat = packed.reshape(-1)
    out, off = [], 0
    for shp in shapes:
        size = math.prod(shp)
        out.append(flat[off:off + size].reshape(shp))
        off += size
    return out


def _spatial_weights(w_s, b_s):
    reps = TM // CHUNK
    tril = jnp.tril(jnp.ones((CHUNK, CHUNK), F32))
    wc = w_s * tril
    eye = jnp.eye(reps, dtype=F32)
    wc2 = jnp.einsum("ab,gts->gatbs", eye, wc).reshape(A_GROUPS, TM, TM)
    bs2 = jnp.tile(b_s, (1, reps)).reshape(A_GROUPS, TM, 1)
    return wc2.astype(BF16), jnp.swapaxes(wc2, 1, 2).astype(BF16), bs2


def _spatial_weight_grad(dwc2, dbs2):
    reps = TM // CHUNK
    tril = jnp.tril(jnp.ones((CHUNK, CHUNK), F32))
    blocks = dwc2.reshape(A_GROUPS, reps, CHUNK, reps, CHUNK)
    dws = sum(blocks[:, a, :, a, :] for a in range(reps)) * tril
    dbs = dbs2.reshape(A_GROUPS, reps, CHUNK).sum(axis=1)
    return dws, dbs


def _row2(a):
    return a.reshape(1, -1)


def kernel(x, a0_w_in, a0_v_gain, a0_v_bias, a0_w_s, a0_b_s, a0_w_out, ln0_gain, ln0_bias, b1_w_in, b1_w_grp, b1_scale, b1_w_out, ln1_gain, ln1_bias, c2_w_in, c2_conv_w, c2_w_out, ln2_gain, ln2_bias, a3_w_in, a3_v_gain, a3_v_bias, a3_w_s, a3_b_s, a3_w_out, ln3_gain, ln3_bias, loss_target, m_a0_w_in, m_a0_v_gain, m_a0_v_bias, m_a0_w_s, m_a0_b_s, m_a0_w_out, m_ln0_gain, m_ln0_bias, m_b1_w_in, m_b1_w_grp, m_b1_scale, m_b1_w_out, m_ln1_gain, m_ln1_bias, m_c2_w_in, m_c2_conv_w, m_c2_w_out, m_ln2_gain, m_ln2_bias, m_a3_w_in, m_a3_v_gain, m_a3_v_bias, m_a3_w_s, m_a3_b_s, m_a3_w_out, m_ln3_gain, m_ln3_bias, v_a0_w_in, v_a0_v_gain, v_a0_v_bias, v_a0_w_s, v_a0_b_s, v_a0_w_out, v_ln0_gain, v_ln0_bias, v_b1_w_in, v_b1_w_grp, v_b1_scale, v_b1_w_out, v_ln1_gain, v_ln1_bias, v_c2_w_in, v_c2_conv_w, v_c2_w_out, v_ln2_gain, v_ln2_bias, v_a3_w_in, v_a3_v_gain, v_a3_v_bias, v_a3_w_s, v_a3_b_s, v_a3_w_out, v_ln3_gain, v_ln3_bias):
    names = ["a0_w_in", "a0_v_gain", "a0_v_bias", "a0_w_s", "a0_b_s", "a0_w_out", "ln0_gain", "ln0_bias",
             "b1_w_in", "b1_w_grp", "b1_scale", "b1_w_out", "ln1_gain", "ln1_bias",
             "c2_w_in", "c2_conv_w", "c2_w_out", "ln2_gain", "ln2_bias",
             "a3_w_in", "a3_v_gain", "a3_v_bias", "a3_w_s", "a3_b_s", "a3_w_out", "ln3_gain", "ln3_bias"]
    env = dict(locals())
    w = {nm: env[nm] for nm in names}
    mom = {nm: env["m_" + nm] for nm in names}
    var = {nm: env["v_" + nm] for nm in names}

    x0 = x[0]
    target = loss_target[0]
    d_model = x0.shape[1]
    di = N_DEV * a0_w_out.shape[0]
    n_grp = len(POOL_WINDOWS)
    gd_b = di // n_grp

    layers = ("a0", "b1", "c2", "a3")
    big_of = {"a0": ["a0_w_in", "a0_w_out"], "b1": ["b1_w_in", "b1_w_grp", "b1_w_out"],
              "c2": ["c2_w_in", "c2_w_out"], "a3": ["a3_w_in", "a3_w_out"]}
    big = [nm for p in layers for nm in big_of[p]]
    bucket_of = {"a0": ["a0_v_gain", "a0_v_bias", "a0_w_s", "a0_b_s", "ln0_gain", "ln0_bias"],
                 "b1": ["b1_scale", "ln1_gain", "ln1_bias"], "c2": ["ln2_gain", "ln2_bias"],
                 "a3": ["a3_v_gain", "a3_v_bias", "a3_w_s", "a3_b_s", "ln3_gain", "ln3_bias"]}
    conv_shape = c2_conv_w.shape
    spatial = {p: _spatial_weights(w[p + "_w_s"], w[p + "_b_s"]) for p in ("a0", "a3")}

    def weight_gather(p):
        return _Exchange(gathers=[w[nm].astype(BF16) for nm in big_of[p]])

    first = _exchange_only(_Exchange(gathers=[w[nm].astype(BF16) for nm in big_of["a0"]] + [_pack([c2_conv_w])]),
                           "gather_first")
    gathered = dict(zip(big_of["a0"], first))
    conv_all = jnp.stack([_unpack(first[-1][j], [conv_shape])[0] for j in range(N_DEV)], axis=1)
    conv_full = conv_all.reshape(conv_shape[0], di)
    w_in = lambda p: gathered[p + "_w_in"]
    w_out = lambda p: gathered[p + "_w_out"].reshape(di, d_model)
    saved = {}
    h = x0
    for i, p in enumerate(layers):
        lng, lnb = _row2(w[f"ln{i}_gain"]), _row2(w[f"ln{i}_bias"])
        nxt = layers[i + 1] if i + 1 < len(layers) else None
        ex = weight_gather(nxt) if nxt else None
        if p[0] == "a":
            wc2, _, bs2 = spatial[p]
            (pp, xt, pre, h), got = _fwd_a(h, w_in(p), w_out(p), _row2(w[p + "_v_gain"]),
                                           _row2(w[p + "_v_bias"]), wc2, bs2, lng, lnb, exchange=ex)
            saved[p] = (pp, xt, pre)
        elif p[0] == "b":
            wgrp = jnp.swapaxes(gathered["b1_w_grp"], 0, 1).reshape(n_grp, gd_b, gd_b)
            (z, mixed, poolt, xt, pre, h), got = _fwd_b(h, w_in(p), wgrp, _row2(w[p + "_scale"]),
                                                        w_out(p), lng, lnb, exchange=ex)
            saved[p] = (z, mixed, poolt, xt, pre)
        else:
            (pp, xt, pre, h), got = _fwd_c(h, w_in(p), conv_full, w_out(p), lng, lnb, exchange=ex)
            saved[p] = (pp, xt, pre)
        if nxt:
            gathered.update(zip(big_of[nxt], got))

    gcur, sq = _loss_head(h, target)
    loss = lax.psum(jnp.sum(sq) * (0.5 / d_model), ("x", "y", "c"))

    part, full, landed, small_all = {}, {}, {}, {}

    def grad_exchange(scatter_names, bucket):
        gathers = [_pack([part[nm] for nm in bucket_of[bucket]])] if bucket else []
        return _Exchange(gathers=gathers, scatters=[full[nm] for nm in scatter_names])

    def collect(scatter_names, bucket, got):
        got = list(got)
        if bucket:
            small_all[bucket] = got.pop(0)
        landed.update(zip(scatter_names, got))

    pending = None
    for i, p in reversed(list(enumerate(layers))):
        lng = _row2(w[f"ln{i}_gain"])
        ex = grad_exchange(*pending) if pending else None
        if p[0] == "a":
            pp, xt, pre = saved[p]
            wc2, wc2t, bs2 = spatial[p]
            (dpre, dp, yt, dlng, dlnb, dgain, dbias, dbs2, dwc2), got = _bwd_a(
                gcur, pre, pp, w_out(p), _row2(w[p + "_v_gain"]), _row2(w[p + "_v_bias"]), wc2, wc2t, bs2, lng,
                exchange=ex)
            part[p + "_v_gain"], part[p + "_v_bias"] = dgain.sum(axis=0), dbias.sum(axis=0)
            part[p + "_w_s"], part[p + "_b_s"] = _spatial_weight_grad(dwc2, dbs2)
        elif p[0] == "b":
            z, mixed, poolt, xt, pre = saved[p]
            (dpre, dp, yt, dmixed, dlng, dlnb, dscale), got = _bwd_b(
                gcur, pre, z, mixed, w_out(p), wgrp, _row2(w[p + "_scale"]), lng, exchange=ex)
            part[p + "_scale"] = dscale.sum(axis=0)
            dwg, _ = _wgrad(poolt, dmixed, n_grp, True, "wgrad_grp")
            full[p + "_w_grp"] = jnp.swapaxes(dwg.reshape(n_grp, N_DEV, gd_b // N_DEV, gd_b), 0, 1)
        else:
            pp, xt, pre = saved[p]
            (dpre, dp, yt, dlng, dlnb, dcw), got = _bwd_c(gcur, pre, pp, w_out(p), conv_full, lng, exchange=ex)
            dconv = dcw.sum(axis=1).reshape(conv_shape[0], N_DEV, conv_shape[1])
            full["c2_conv_w"] = jnp.stack([_pack([dconv[:, j]]) for j in range(N_DEV)])
        if pending:
            collect(*pending, got)
        part[f"ln{i}_gain"], part[f"ln{i}_bias"] = dlng.sum(axis=0), dlnb.sum(axis=0)
        dwo, _ = _wgrad(yt, dpre, 1, False, "wgrad_out_" + p)
        full[p + "_w_out"] = dwo.reshape(N_DEV, di // N_DEV, d_model)
        if i > 0:
            full[p + "_w_in"], _ = _wgrad(xt, dp, N_DEV, False, "wgrad_in_" + p)
            gcur, _ = _dx(dp, dpre, w_in(p))
            pending = (big_of[p] + (["c2_conv_w"] if p == "c2" else []), p)
        else:
            full[p + "_w_in"], got = _wgrad(xt, dp, N_DEV, False, "wgrad_in_" + p,
                                            exchange=grad_exchange([p + "_w_out"], None))
            collect([p + "_w_out"], None, got)
            gcur, got = _dx(dp, dpre, w_in(p), exchange=grad_exchange([p + "_w_in"], p))
            collect([p + "_w_in"], p, got)
    grad_x = gcur[None]

    grads, deltas, new_m, new_v = {}, {}, {}, {}
    for nm in big:
        shp = w[nm].shape
        r2 = (math.prod(shp[:-1]), shp[-1])
        outs = _adamw(landed[nm].reshape((N_DEV,) + r2), w[nm].reshape(r2), mom[nm].reshape(r2), var[nm].reshape(r2),
                      "adamw_" + nm)
        grads[nm], deltas[nm], new_m[nm], new_v[nm] = (o.reshape(shp) for o in outs)
    buckets = dict(bucket_of, conv=["c2_conv_w"])
    small_all["conv"] = landed["c2_conv_w"]
    for key, members in buckets.items():
        shapes = [w[nm].shape for nm in members]
        outs = _adamw(small_all[key], _pack([w[nm] for nm in members]), _pack([mom[nm] for nm in members]),
                      _pack([var[nm] for nm in members]), "adamw_small_" + key)
        for tgt, o in zip((grads, deltas, new_m, new_v), outs):
            tgt.update(zip(members, _unpack(o, shapes)))

    return (loss, grad_x, *[grads[nm] for nm in names], *[deltas[nm] for nm in names],
            *[new_m[nm] for nm in names], *[new_v[nm] for nm in names])
```

```python
import functools
import math

import jax
import jax.numpy as jnp
from jax import lax
from jax.experimental import pallas as pl
from jax.experimental.pallas import tpu as pltpu

F32 = jnp.float32
BF16 = jnp.bfloat16

N_DEV = 8
DEPTH = 4
CHUNK = 128
A_GROUPS = 8
POOL_WINDOWS = (2, 4, 8, 16)
LN_EPS = 1e-5
ALPHA = (2.0 * DEPTH) ** 0.25
ADAM_LR = 0.001
ADAM_B1 = 0.9
ADAM_B2 = 0.999
ADAM_EPS = 1e-08
ADAM_WD = 0.01
ADAM_STEP = 10

TM = 256
HALO = 16
CHALO = 8
CW = 512
VMEM_LIMIT_BYTES = 58 * 1024 * 1024

_NT = (((1,), (1,)), ((), ()))
_SQRT_2_OVER_PI = math.sqrt(2.0 / math.pi)
_MESH = pl.DeviceIdType.MESH


def _vmem():
    return pl.BlockSpec(memory_space=pltpu.VMEM)


def _params(sem=None):
    return pltpu.CompilerParams(dimension_semantics=sem, vmem_limit_bytes=VMEM_LIMIT_BYTES)


def _gelu(x):
    t = jnp.tanh(_SQRT_2_OVER_PI * (x + 0.044715 * (x * x * x)))
    return x * (0.5 * (1.0 + t))


def _gelu_and_grad(x):
    x2 = x * x
    t = jnp.tanh(_SQRT_2_OVER_PI * (x + 0.044715 * (x * x2)))
    cdf = 0.5 * (1.0 + t)
    grad = cdf + 0.5 * x * (1.0 - t * t) * (_SQRT_2_OVER_PI * (1.0 + 3.0 * 0.044715 * x2))
    return x * cdf, grad


def _silu_and_grad(z):
    sg = 1.0 / (1.0 + jnp.exp(-z))
    return z * sg, sg * (1.0 + z * (1.0 - sg))


def _fold8(a):
    return a.reshape(a.shape[0] // 8, 8, a.shape[1]).sum(axis=0)


def _row_mean(a):
    return jnp.mean(a, axis=-1, keepdims=True)


def _ln_stats(x):
    mu = _row_mean(x)
    xc = x - mu
    rstd = lax.rsqrt(_row_mean(xc * xc) + LN_EPS)
    return xc * rstd, rstd


def _post_norm(x, out, lng_ref, lnb_ref, pre_ref, xn_ref):
    pre = ALPHA * x + out
    pre_ref[...] = pre
    xhat, _ = _ln_stats(pre)
    xn_ref[...] = xhat * lng_ref[...] + lnb_ref[...]


def _post_norm_bwd(g_ref, pre_ref, lng_ref, dpre_ref, dlng_ref, dlnb_ref):
    go = g_ref[...]
    xhat, rstd = _ln_stats(pre_ref[...])
    dlng_ref[...] += _fold8(go * xhat)
    dlnb_ref[...] += _fold8(go)
    dxh = go * lng_ref[...]
    dpre = rstd * (dxh - _row_mean(dxh) - xhat * _row_mean(dxh * xhat))
    dpre_ref[...] = dpre
    return dpre


def _in_proj(xb, win_ref, p_ref):
    cs = win_ref.shape[2]
    for j in range(N_DEV):
        p_ref[:, j * cs:(j + 1) * cs] = jnp.dot(xb, win_ref[j], preferred_element_type=F32)


def _zero_at_first_step(*refs):
    @pl.when(pl.program_id(0) == 0)
    def _():
        for r in refs:
            r[...] = jnp.zeros(r.shape, r.dtype)


def _f32(ref, sl):
    return ref[:, sl].astype(F32)


def _my_position():
    x, y, c = lax.axis_index("x"), lax.axis_index("y"), lax.axis_index("c")
    return (x, y, c), 4 * x + 2 * y + c


def _peer(k):
    (x, y, c), _ = _my_position()
    peer = (x ^ (k >> 2), y ^ ((k >> 1) & 1), c ^ (k & 1))
    return peer, 4 * peer[0] + 2 * peer[1] + peer[2]


class _Exchange:
    def __init__(self, gathers=(), scatters=()):
        self.args = list(gathers) + list(scatters)
        self.n_gather = len(gathers)
        self.out_shape = ([jax.ShapeDtypeStruct((N_DEV,) + a.shape, a.dtype) for a in gathers]
                          + [jax.ShapeDtypeStruct(a.shape, a.dtype) for a in scatters])
        n = len(self.args)
        self.scratch = [pltpu.SemaphoreType.DMA((n, N_DEV)), pltpu.SemaphoreType.DMA((n, N_DEV)),
                        pltpu.SemaphoreType.DMA((n,))]

    def _src(self, ins, w, pos):
        return ins[w] if w < self.n_gather else ins[w].at[pos]

    def _copies(self, ins, outs, sems, arrivals):
        send_sems, recv_sems, local_sems = sems
        _, me = _my_position()
        n = len(self.args)
        copies = []
        if not arrivals:
            copies = [pltpu.make_async_copy(self._src(ins, w, me), outs[w].at[me], local_sems.at[w]) for w in range(n)]
        for k in range(1, N_DEV):
            peer, peer_pos = _peer(k)
            for w in range(n):
                copies.append(pltpu.make_async_remote_copy(
                    src_ref=self._src(ins, w, me if arrivals else peer_pos),
                    dst_ref=outs[w].at[peer_pos if arrivals else me],
                    send_sem=send_sems.at[w, k], recv_sem=recv_sems.at[w, k], device_id=peer, device_id_type=_MESH))
        return copies

    def start(self, ins, outs, sems):
        for cp in self._copies(ins, outs, sems, False):
            cp.start()

    def wait(self, ins, outs, sems):
        n = len(self.args)
        for cp in self._copies(ins, outs, sems, True):
            cp.wait_recv()
        own = self._copies(ins, outs, sems, False)
        for cp in own[n:]:
            cp.wait_send()
        for cp in own[:n]:
            cp.wait()


def _call(body, name, grid, args, in_specs, out_shape, out_specs, scratch=(), exchange=None):
    sem = ("arbitrary",) * len(grid)
    if exchange is None:
        outs = pl.pallas_call(body, name=name, grid=grid, in_specs=in_specs, out_specs=out_specs, out_shape=out_shape,
                              scratch_shapes=list(scratch), compiler_params=_params(sem))(*args)
        return outs, []
    n_in, n_out, n_scr, n_ex = len(args), len(out_shape), len(scratch), len(exchange.args)

    def hosted(*refs):
        ex_in = refs[n_in:n_in + n_ex]
        o0 = n_in + n_ex
        ex_out = refs[o0 + n_out:o0 + n_out + n_ex]
        s0 = o0 + n_out + n_ex
        sems = refs[s0 + n_scr:]
        first = functools.reduce(jnp.logical_and, [pl.program_id(a) == 0 for a in range(len(grid))])
        last = functools.reduce(jnp.logical_and, [pl.program_id(a) == grid[a] - 1 for a in range(len(grid))])

        @pl.when(first)
        def _():
            exchange.start(ex_in, ex_out, sems)

        body(*refs[:n_in], *refs[o0:o0 + n_out], *refs[s0:s0 + n_scr])

        @pl.when(last)
        def _():
            exchange.wait(ex_in, ex_out, sems)

    any_spec = pl.BlockSpec(memory_space=pl.ANY)
    outs = pl.pallas_call(
        hosted, name=name, grid=grid, in_specs=list(in_specs) + [any_spec] * n_ex,
        out_specs=list(out_specs) + [any_spec] * n_ex, out_shape=list(out_shape) + exchange.out_shape,
        scratch_shapes=list(scratch) + exchange.scratch, compiler_params=_params(sem))(*args, *exchange.args)
    return outs[:n_out], outs[n_out:]


def _exchange_only(exchange, name):
    n = len(exchange.args)

    def body(*refs):
        exchange.start(refs[:n], refs[n:2 * n], refs[2 * n:])
        exchange.wait(refs[:n], refs[n:2 * n], refs[2 * n:])

    any_spec = pl.BlockSpec(memory_space=pl.ANY)
    return pl.pallas_call(body, name=name, in_specs=[any_spec] * n, out_specs=[any_spec] * n,
                          out_shape=exchange.out_shape, scratch_shapes=exchange.scratch)(*exchange.args)


def _tile_specs(t, d, di):
    row = lambda i: (i, 0)
    col = lambda i: (0, i)
    return dict(
        xd=pl.BlockSpec((TM, d), row), xi=pl.BlockSpec((TM, di), row),
        td=pl.BlockSpec((d, TM), col), ti=pl.BlockSpec((di, TM), col),
        s_xd=jax.ShapeDtypeStruct((t, d), F32), s_xi=jax.ShapeDtypeStruct((t, di), BF16),
        s_td=jax.ShapeDtypeStruct((d, t), BF16), s_ti=jax.ShapeDtypeStruct((di, t), BF16))


def _fwd_a(x, win, wout, gain, bias, wc2, bs2, lng, lnb, exchange=None):
    t, d = x.shape
    di = wout.shape[0]
    gd = di // A_GROUPS

    def body(x_ref, win_ref, wout_ref, gain_ref, bias_ref, wc_ref, bs_ref, lng_ref, lnb_ref,
             a1_ref, a2_ref, a3_ref, vn_ref, vh_ref, rg_ref, yt_ref, xt_ref, pre_ref, xn_ref,
             p_scr, vg_scr, y_scr):
        xv = x_ref[...]
        xt_ref[...] = xv.T.astype(BF16)
        _in_proj(xv.astype(BF16), win_ref, p_scr)
        s1 = jnp.zeros((TM, 1), F32)
        for c in range(di // CW):
            sl = slice(c * CW, (c + 1) * CW)
            pv = slice(di + c * CW, di + (c + 1) * CW)
            vg, dvg = _gelu_and_grad(p_scr[:, pv])
            vg_scr[:, sl] = vg
            p_scr[:, pv] = dvg
            s1 += jnp.sum(vg, axis=1, keepdims=True)
        mu = s1 * (1.0 / di)
        s2 = jnp.zeros((TM, 1), F32)
        for c in range(di // CW):
            dlt = vg_scr[:, c * CW:(c + 1) * CW] - mu
            s2 += jnp.sum(dlt * dlt, axis=1, keepdims=True)
        rstd = lax.rsqrt(s2 * (1.0 / di) + LN_EPS)
        for c in range(di // CW):
            sl = slice(c * CW, (c + 1) * CW)
            vh = (vg_scr[:, sl] - mu) * rstd
            vh_ref[:, sl] = vh.astype(BF16)
            vn_ref[:, sl] = (vh * gain_ref[:, sl] + bias_ref[:, sl]).astype(BF16)
            rg_ref[:, sl] = (p_scr[:, di + c * CW:di + (c + 1) * CW] * rstd).astype(BF16)
        for g in range(A_GROUPS):
            sl = slice(g * gd, (g + 1) * gd)
            sv = jnp.dot(wc_ref[g], vn_ref[:, sl], preferred_element_type=F32) + bs_ref[g]
            u, du = _gelu_and_grad(p_scr[:, sl])
            s, ds = _silu_and_grad(p_scr[:, 2 * di + g * gd:2 * di + (g + 1) * gd])
            us = u * s
            a1_ref[:, sl] = (s * du).astype(BF16)
            a2_ref[:, sl] = (u * ds).astype(BF16)
            a3_ref[:, sl] = us.astype(BF16)
            y = us * sv
            y_scr[:, sl] = y.astype(BF16)
            yt_ref[sl, :] = y.T.astype(BF16)
        out = jnp.dot(y_scr[...], wout_ref[...], preferred_element_type=F32)
        _post_norm(xv, out, lng_ref, lnb_ref, pre_ref, xn_ref)

    sp = _tile_specs(t, d, di)
    return _call(
        body, "fwd_a", (t // TM,), (x, win, wout, gain, bias, wc2, bs2, lng, lnb),
        in_specs=[sp["xd"]] + [_vmem()] * 8,
        out_specs=[sp["xi"]] * 6 + [sp["ti"], sp["td"], sp["xd"], sp["xd"]],
        out_shape=[sp["s_xi"]] * 6 + [sp["s_ti"], sp["s_td"], sp["s_xd"], sp["s_xd"]],
        scratch=[pltpu.VMEM((TM, 3 * di), F32), pltpu.VMEM((TM, di), F32), pltpu.VMEM((TM, di), BF16)],
        exchange=exchange)


def _bwd_a(g, pre, a1, a2, a3, vn, vh, rg, wout, gain, wc2, wc2t, bs2, lng, exchange=None):
    t, d = g.shape
    di = wout.shape[0]
    gd = di // A_GROUPS

    def body(g_ref, pre_ref, a1_ref, a2_ref, a3_ref, vn_ref, vh_ref, rg_ref,
             wout_ref, gain_ref, wc_ref, wct_ref, bs_ref, lng_ref,
             dpre_ref, dp_ref, dlng_ref, dlnb_ref, dgain_ref, dbias_ref, dbs_ref, dwc_ref,
             dy_scr, dv_scr):
        _zero_at_first_step(dlng_ref, dlnb_ref, dgain_ref, dbias_ref, dbs_ref, dwc_ref)
        dpre = _post_norm_bwd(g_ref, pre_ref, lng_ref, dpre_ref, dlng_ref, dlnb_ref)
        dy_scr[...] = lax.dot_general(dpre.astype(BF16), wout_ref[...], _NT, preferred_element_type=F32)
        for grp in range(A_GROUPS):
            sl = slice(grp * gd, (grp + 1) * gd)
            vn_g = vn_ref[:, sl]
            sv = jnp.dot(wc_ref[grp], vn_g, preferred_element_type=F32) + bs_ref[grp]
            dy = dy_scr[:, sl]
            dys = dy * sv
            dp_ref[:, sl] = (dys * _f32(a1_ref, sl)).astype(BF16)
            dp_ref[:, 2 * di + grp * gd:2 * di + (grp + 1) * gd] = (dys * _f32(a2_ref, sl)).astype(BF16)
            dsv = dy * _f32(a3_ref, sl)
            dbs_ref[grp] += jnp.sum(dsv, axis=1, keepdims=True)
            dsvb = dsv.astype(BF16)
            dwc_ref[grp] += lax.dot_general(dsvb, vn_g, _NT, preferred_element_type=F32)
            dv_scr[:, sl] = jnp.dot(wct_ref[grp], dsvb, preferred_element_type=F32)
        r1 = jnp.zeros((TM, 1), F32)
        r2 = jnp.zeros((TM, 1), F32)
        for c in range(di // CW):
            sl = slice(c * CW, (c + 1) * CW)
            dv = dv_scr[:, sl]
            vhat = _f32(vh_ref, sl)
            dgain_ref[:, sl] += _fold8(dv * vhat)
            dbias_ref[:, sl] += _fold8(dv)
            dvh = dv * gain_ref[:, sl]
            dv_scr[:, sl] = dvh
            r1 += jnp.sum(dvh, axis=1, keepdims=True)
            r2 += jnp.sum(dvh * vhat, axis=1, keepdims=True)
        m1 = r1 * (1.0 / di)
        m2 = r2 * (1.0 / di)
        for c in range(di // CW):
            sl = slice(c * CW, (c + 1) * CW)
            dp_ref[:, di + c * CW:di + (c + 1) * CW] = (
                (dv_scr[:, sl] - m1 - _f32(vh_ref, sl) * m2) * _f32(rg_ref, sl)).astype(BF16)

    sp = _tile_specs(t, d, di)
    const2 = lambda i: (0, 0)
    const3 = lambda i: (0, 0, 0)
    return _call(
        body, "bwd_a", (t // TM,), (g, pre, a1, a2, a3, vn, vh, rg, wout, gain, wc2, wc2t, bs2, lng),
        in_specs=[sp["xd"], sp["xd"]] + [sp["xi"]] * 6 + [_vmem()] * 6,
        out_specs=[sp["xd"], pl.BlockSpec((TM, 3 * di), lambda i: (i, 0)),
                   pl.BlockSpec((8, d), const2), pl.BlockSpec((8, d), const2),
                   pl.BlockSpec((8, di), const2), pl.BlockSpec((8, di), const2),
                   pl.BlockSpec((A_GROUPS, TM, 1), const3), pl.BlockSpec((A_GROUPS, TM, TM), const3)],
        out_shape=[sp["s_xd"], jax.ShapeDtypeStruct((t, 3 * di), BF16),
                   jax.ShapeDtypeStruct((8, d), F32), jax.ShapeDtypeStruct((8, d), F32),
                   jax.ShapeDtypeStruct((8, di), F32), jax.ShapeDtypeStruct((8, di), F32),
                   jax.ShapeDtypeStruct((A_GROUPS, TM, 1), F32), jax.ShapeDtypeStruct((A_GROUPS, TM, TM), F32)],
        scratch=[pltpu.VMEM((TM, di), F32), pltpu.VMEM((TM, di), F32)],
        exchange=exchange)


def _inv_count(tile, window):
    pos = tile * TM + lax.broadcasted_iota(jnp.int32, (TM, 1), 0)
    return 1.0 / jnp.minimum(pos + 1, window).astype(F32)


def _window_sum(ext, window, down):
    rows = ext.shape[0]
    k = 1
    while k < window:
        ext = ext + pltpu.roll(ext, k if down else rows - k, 0)
        k *= 2
    return ext


def _fwd_b(x, win, wgrp, scale, wout, lng, lnb, exchange=None):
    t, d = x.shape
    di = wout.shape[0]
    gd = di // len(POOL_WINDOWS)

    def body(x_ref, win_ref, wgrp_ref, scale_ref, wout_ref, lng_ref, lnb_ref,
             b1_ref, b2_ref, b3_ref, poolt_ref, yt_ref, xt_ref, pre_ref, xn_ref, p_scr, ext_scr, y_scr):
        i = pl.program_id(0)

        @pl.when(i == 0)
        def _():
            ext_scr[0:HALO, :] = jnp.zeros((HALO, di), F32)

        xv = x_ref[...]
        xt_ref[...] = xv.T.astype(BF16)
        _in_proj(xv.astype(BF16), win_ref, p_scr)
        ext_scr[HALO:, :] = p_scr[:, :di]
        for grp, window in enumerate(POOL_WINDOWS):
            sl = slice(grp * gd, (grp + 1) * gd)
            ext = ext_scr[:, sl]
            pooled = (_window_sum(ext, window, True)[HALO:] * _inv_count(i, window) - ext[HALO:]).astype(BF16)
            poolt_ref[sl, :] = pooled.astype(F32).T.astype(BF16)
            mixed = jnp.dot(pooled, wgrp_ref[grp], preferred_element_type=F32)
            s, ds = _silu_and_grad(p_scr[:, di + grp * gd:di + (grp + 1) * gd])
            sc = scale_ref[:, sl]
            ms = mixed * s
            b1_ref[:, sl] = (mixed * sc * ds).astype(BF16)
            b2_ref[:, sl] = ms.astype(BF16)
            b3_ref[:, sl] = (sc * s).astype(BF16)
            y = ms * sc
            y_scr[:, sl] = y.astype(BF16)
            yt_ref[sl, :] = y.T.astype(BF16)
        ext_scr[0:HALO, :] = ext_scr[TM:TM + HALO, :]
        out = jnp.dot(y_scr[...], wout_ref[...], preferred_element_type=F32)
        _post_norm(xv, out, lng_ref, lnb_ref, pre_ref, xn_ref)

    sp = _tile_specs(t, d, di)
    return _call(
        body, "fwd_b", (t // TM,), (x, win, wgrp, scale, wout, lng, lnb),
        in_specs=[sp["xd"]] + [_vmem()] * 6,
        out_specs=[sp["xi"]] * 3 + [sp["ti"], sp["ti"], sp["td"], sp["xd"], sp["xd"]],
        out_shape=[sp["s_xi"]] * 3 + [sp["s_ti"], sp["s_ti"], sp["s_td"], sp["s_xd"], sp["s_xd"]],
        scratch=[pltpu.VMEM((TM, 2 * di), F32), pltpu.VMEM((TM + HALO, di), F32), pltpu.VMEM((TM, di), BF16)],
        exchange=exchange)


def _bwd_b(g, pre, b1, b2, b3, wout, wgrp, lng, exchange=None):
    t, d = g.shape
    di = wout.shape[0]
    gd = di // len(POOL_WINDOWS)
    nt = t // TM

    def body(g_ref, pre_ref, b1_ref, b2_ref, b3_ref, wout_ref, wgrp_ref, lng_ref,
             dpre_ref, dp_ref, dmix_ref, dlng_ref, dlnb_ref, dscale_ref, dy_scr, ext_scr):
        i = pl.program_id(0)
        tile = nt - 1 - i
        _zero_at_first_step(dlng_ref, dlnb_ref, dscale_ref)

        @pl.when(i == 0)
        def _():
            ext_scr[TM:, :] = jnp.zeros((HALO, di), F32)

        dpre = _post_norm_bwd(g_ref, pre_ref, lng_ref, dpre_ref, dlng_ref, dlnb_ref)
        dy_scr[...] = lax.dot_general(dpre.astype(BF16), wout_ref[...], _NT, preferred_element_type=F32)
        for grp, window in enumerate(POOL_WINDOWS):
            sl = slice(grp * gd, (grp + 1) * gd)
            dy = dy_scr[:, sl]
            dp_ref[:, di + grp * gd:di + (grp + 1) * gd] = (dy * _f32(b1_ref, sl)).astype(BF16)
            dscale_ref[:, sl] += _fold8(dy * _f32(b2_ref, sl))
            dmixed = (dy * _f32(b3_ref, sl)).astype(BF16)
            dmix_ref[:, sl] = dmixed
            dpooled = lax.dot_general(dmixed, wgrp_ref[grp], _NT, preferred_element_type=F32)
            ext_scr[0:TM, sl] = dpooled * _inv_count(tile, window)
            dv = _window_sum(ext_scr[:, sl], window, False)[0:TM] - dpooled
            dp_ref[:, sl] = dv.astype(BF16)
        ext_scr[TM:, :] = ext_scr[0:HALO, :]

    rrow = lambda i: (nt - 1 - i, 0)
    const2 = lambda i: (0, 0)
    xd, xi = pl.BlockSpec((TM, d), rrow), pl.BlockSpec((TM, di), rrow)
    return _call(
        body, "bwd_b", (nt,), (g, pre, b1, b2, b3, wout, wgrp, lng),
        in_specs=[xd, xd, xi, xi, xi, _vmem(), _vmem(), _vmem()],
        out_specs=[xd, pl.BlockSpec((TM, 2 * di), rrow), xi,
                   pl.BlockSpec((8, d), const2), pl.BlockSpec((8, d), const2), pl.BlockSpec((8, di), const2)],
        out_shape=[jax.ShapeDtypeStruct((t, d), F32), jax.ShapeDtypeStruct((t, 2 * di), BF16),
                   jax.ShapeDtypeStruct((t, di), BF16),
                   jax.ShapeDtypeStruct((8, d), F32), jax.ShapeDtypeStruct((8, d), F32),
                   jax.ShapeDtypeStruct((8, di), F32)],
        scratch=[pltpu.VMEM((TM, di), F32), pltpu.VMEM((TM + HALO, di), F32)],
        exchange=exchange)


def _fwd_c(x, win, convw, wout, lng, lnb, exchange=None):
    t, d = x.shape
    di = wout.shape[0]

    def body(x_ref, win_ref, cw_ref, wout_ref, lng_ref, lnb_ref,
             c1_ref, c2_ref, c3_ref, cg_ref, hg_ref, yt_ref, xt_ref, pre_ref, xn_ref, p_scr, ext_scr, y_scr):
        i = pl.program_id(0)

        @pl.when(i == 0)
        def _():
            ext_scr[0:CHALO, :] = jnp.zeros((CHALO, di), F32)

        xv = x_ref[...]
        xt_ref[...] = xv.T.astype(BF16)
        _in_proj(xv.astype(BF16), win_ref, p_scr)
        for c in range(di // CW):
            sl = slice(c * CW, (c + 1) * CW)
            bb = p_scr[:, sl]
            cc = p_scr[:, di + c * CW:di + (c + 1) * CW]
            hh = p_scr[:, 2 * di + c * CW:2 * di + (c + 1) * CW]
            s, ds = _silu_and_grad(p_scr[:, 3 * di + c * CW:3 * di + (c + 1) * CW])
            ext_scr[CHALO:, sl] = cc * hh
            ext = ext_scr[:, sl]
            conv = (pltpu.roll(ext, 2, 0)[CHALO:] * cw_ref[0:1, sl] + pltpu.roll(ext, 1, 0)[CHALO:] * cw_ref[1:2, sl]
                    + ext[CHALO:] * cw_ref[2:3, sl])
            cs = conv * s
            c1_ref[:, sl] = cs.astype(BF16)
            c2_ref[:, sl] = (bb * conv * ds).astype(BF16)
            c3_ref[:, sl] = (bb * s).astype(BF16)
            cg_ref[:, sl] = cc.astype(BF16)
            hg_ref[:, sl] = hh.astype(BF16)
            y = bb * cs
            y_scr[:, sl] = y.astype(BF16)
            yt_ref[sl, :] = y.T.astype(BF16)
        ext_scr[0:CHALO, :] = ext_scr[TM:TM + CHALO, :]
        out = jnp.dot(y_scr[...], wout_ref[...], preferred_element_type=F32)
        _post_norm(xv, out, lng_ref, lnb_ref, pre_ref, xn_ref)

    sp = _tile_specs(t, d, di)
    return _call(
        body, "fwd_c", (t // TM,), (x, win, convw, wout, lng, lnb),
        in_specs=[sp["xd"]] + [_vmem()] * 5,
        out_specs=[sp["xi"]] * 5 + [sp["ti"], sp["td"], sp["xd"], sp["xd"]],
        out_shape=[sp["s_xi"]] * 5 + [sp["s_ti"], sp["s_td"], sp["s_xd"], sp["s_xd"]],
        scratch=[pltpu.VMEM((TM, 4 * di), F32), pltpu.VMEM((TM + CHALO, di), F32), pltpu.VMEM((TM, di), BF16)],
        exchange=exchange)


def _bwd_c(g, pre, c1, c2, c3, cg, hg, wout, convw, lng, exchange=None):
    t, d = g.shape
    di = wout.shape[0]
    nt = t // TM

    def body(g_ref, pre_ref, c1_ref, c2_ref, c3_ref, cg_ref, hg_ref, wout_ref, cw_ref, lng_ref,
             dpre_ref, dp_ref, dlng_ref, dlnb_ref, dcw_ref, dy_scr, ext_scr):
        i = pl.program_id(0)
        _zero_at_first_step(dlng_ref, dlnb_ref, dcw_ref)

        @pl.when(i == 0)
        def _():
            ext_scr[TM:, :] = jnp.zeros((CHALO, di), F32)

        dpre = _post_norm_bwd(g_ref, pre_ref, lng_ref, dpre_ref, dlng_ref, dlnb_ref)
        dy_scr[...] = lax.dot_general(dpre.astype(BF16), wout_ref[...], _NT, preferred_element_type=F32)
        rows = TM + CHALO
        for c in range(di // CW):
            sl = slice(c * CW, (c + 1) * CW)
            dy = dy_scr[:, sl]
            cc = _f32(cg_ref, sl)
            hh = _f32(hg_ref, sl)
            dp_ref[:, sl] = (dy * _f32(c1_ref, sl)).astype(BF16)
            dp_ref[:, 3 * di + c * CW:3 * di + (c + 1) * CW] = (dy * _f32(c2_ref, sl)).astype(BF16)
            dconv = dy * _f32(c3_ref, sl)
            ext_scr[0:TM, sl] = dconv
            ext = ext_scr[:, sl]
            d1 = pltpu.roll(ext, rows - 1, 0)[0:TM]
            d2 = pltpu.roll(ext, rows - 2, 0)[0:TM]
            dq = dconv * cw_ref[2:3, sl] + d1 * cw_ref[1:2, sl] + d2 * cw_ref[0:1, sl]
            q = cc * hh
            dcw_ref[0, :, sl] += _fold8(q * d2)
            dcw_ref[1, :, sl] += _fold8(q * d1)
            dcw_ref[2, :, sl] += _fold8(q * dconv)
            dp_ref[:, di + c * CW:di + (c + 1) * CW] = (dq * hh).astype(BF16)
            dp_ref[:, 2 * di + c * CW:2 * di + (c + 1) * CW] = (dq * cc).astype(BF16)
        ext_scr[TM:, :] = ext_scr[0:CHALO, :]

    rrow = lambda i: (nt - 1 - i, 0)
    const2 = lambda i: (0, 0)
    xd, xi = pl.BlockSpec((TM, d), rrow), pl.BlockSpec((TM, di), rrow)
    return _call(
        body, "bwd_c", (nt,), (g, pre, c1, c2, c3, cg, hg, wout, convw, lng),
        in_specs=[xd, xd] + [xi] * 5 + [_vmem()] * 3,
        out_specs=[xd, pl.BlockSpec((TM, 4 * di), rrow),
                   pl.BlockSpec((8, d), const2), pl.BlockSpec((8, d), const2),
                   pl.BlockSpec((3, 8, di), lambda i: (0, 0, 0))],
        out_shape=[jax.ShapeDtypeStruct((t, d), F32), jax.ShapeDtypeStruct((t, 4 * di), BF16),
                   jax.ShapeDtypeStruct((8, d), F32), jax.ShapeDtypeStruct((8, d), F32),
                   jax.ShapeDtypeStruct((3, 8, di), F32)],
        scratch=[pltpu.VMEM((TM, di), F32), pltpu.VMEM((TM + CHALO, di), F32)],
        exchange=exchange)


def _dx(dp, dpre, win, exchange=None):
    t, d = dpre.shape
    n = dp.shape[1]
    cs = win.shape[2]

    def body(dp_ref, dpre_ref, win_ref, dx_ref):
        acc = ALPHA * dpre_ref[...]
        for j in range(N_DEV):
            acc += lax.dot_general(dp_ref[:, j * cs:(j + 1) * cs], win_ref[j], _NT, preferred_element_type=F32)
        dx_ref[...] = acc

    row = lambda i: (i, 0)
    (dx,), ex = _call(
        body, "dx", (t // TM,), (dp, dpre, win),
        in_specs=[pl.BlockSpec((TM, n), row), pl.BlockSpec((TM, d), row), _vmem()],
        out_specs=[pl.BlockSpec((TM, d), row)],
        out_shape=[jax.ShapeDtypeStruct((t, d), F32)],
        exchange=exchange)
    return dx, ex


def _wgrad(at, b, nb, per_block_rows, name, exchange=None):
    m_all, t = at.shape
    tn = b.shape[1] // nb
    m = m_all // nb if per_block_rows else m_all
    tk = min(512, t)
    nk = t // tk

    def body(at_ref, b_ref, out_ref, acc):
        k = pl.program_id(1)

        @pl.when(k == 0)
        def _():
            acc[...] = jnp.zeros(acc.shape, F32)

        acc[...] += jnp.dot(at_ref[...], b_ref[...].astype(BF16), preferred_element_type=F32)

        @pl.when(k == nk - 1)
        def _():
            out_ref[0] = acc[...].astype(BF16)

    at_map = (lambda j, k: (j, k)) if per_block_rows else (lambda j, k: (0, k))
    (out,), ex = _call(
        body, name, (nb, nk), (at, b),
        in_specs=[pl.BlockSpec((m, tk), at_map), pl.BlockSpec((tk, tn), lambda j, k: (k, j))],
        out_specs=[pl.BlockSpec((1, m, tn), lambda j, k: (j, 0, 0))],
        out_shape=[jax.ShapeDtypeStruct((nb, m, tn), BF16)],
        scratch=[pltpu.VMEM((m, tn), F32)],
        exchange=exchange)
    return out, ex


def _loss_head(y, target):
    t, d = y.shape

    def body(y_ref, t_ref, dy_ref, sq_ref):
        _zero_at_first_step(sq_ref)
        diff = y_ref[...] - t_ref[...]
        dy_ref[...] = diff * (1.0 / d)
        sq_ref[...] += _fold8(diff * diff)

    row = lambda i: (i, 0)
    return pl.pallas_call(
        body, name="loss_head", grid=(t // TM,),
        in_specs=[pl.BlockSpec((TM, d), row), pl.BlockSpec((TM, d), row)],
        out_specs=[pl.BlockSpec((TM, d), row), pl.BlockSpec((8, d), lambda i: (0, 0))],
        out_shape=[jax.ShapeDtypeStruct((t, d), F32), jax.ShapeDtypeStruct((8, d), F32)],
        compiler_params=_params(("arbitrary",)),
    )(y, target)


ADAMW_BLOCK_BYTES = 6 * 1024 * 1024


def _sum_parts(parts_ref):
    g = parts_ref[0].astype(F32)
    for s in range(1, parts_ref.shape[0]):
        g = g + parts_ref[s].astype(F32)
    return g


def _row_tile(rows, bytes_per_row):
    if rows * bytes_per_row <= ADAMW_BLOCK_BYTES:
        return rows
    best = 8
    for cand in range(8, rows, 8):
        if rows % cand == 0 and cand * bytes_per_row <= ADAMW_BLOCK_BYTES:
            best = cand
    return best


def _adamw(parts, w, m, v, name):
    s, r, c = parts.shape
    tr = _row_tile(r, c * (s * parts.dtype.itemsize + 7 * 4))
    bc1 = 1.0 - ADAM_B1 ** ADAM_STEP
    bc2 = 1.0 - ADAM_B2 ** ADAM_STEP

    def body(parts_ref, w_ref, m_ref, v_ref, g_ref, d_ref, nm_ref, nv_ref):
        g = _sum_parts(parts_ref)
        g_ref[...] = g
        nm = ADAM_B1 * m_ref[...] + (1.0 - ADAM_B1) * g
        nv = ADAM_B2 * v_ref[...] + (1.0 - ADAM_B2) * (g * g)
        nm_ref[...] = nm
        nv_ref[...] = nv
        d_ref[...] = -ADAM_LR * ((nm / bc1) / (jnp.sqrt(nv / bc2) + ADAM_EPS) + ADAM_WD * w_ref[...])

    blk = pl.BlockSpec((tr, c), lambda i: (i, 0))
    return pl.pallas_call(
        body, name=name, grid=(r // tr,),
        in_specs=[pl.BlockSpec((s, tr, c), lambda i: (0, i, 0)), blk, blk, blk],
        out_specs=[blk, blk, blk, blk],
        out_shape=[jax.ShapeDtypeStruct((r, c), F32)] * 4,
        compiler_params=_params(("arbitrary",)),
    )(parts, w, m, v)


_LANES = 128


def _pack(arrays):
    flat = jnp.concatenate([a.reshape(-1) for a in arrays])
    pad = (-flat.shape[0]) % (8 * _LANES)
    return jnp.pad(flat, (0, pad)).reshape(-1, _LANES)


def _unpack(packed, shapes):
    flat = packed.reshape(-1)
    out, off = [], 0
    for shp in shapes:
        size = math.prod(shp)
        out.append(flat[off:off + size].reshape(shp))
        off += size
    return out


def _spatial_weights(w_s, b_s):
    reps = TM // CHUNK
    tril = jnp.tril(jnp.ones((CHUNK, CHUNK), F32))
    wc = w_s * tril
    eye = jnp.eye(reps, dtype=F32)
    wc2 = jnp.einsum("ab,gts->gatbs", eye, wc).reshape(A_GROUPS, TM, TM)
    bs2 = jnp.tile(b_s, (1, reps)).reshape(A_GROUPS, TM, 1)
    return wc2.astype(BF16), jnp.swapaxes(wc2, 1, 2).astype(BF16), bs2


def _spatial_weight_grad(dwc2, dbs2):
    reps = TM // CHUNK
    tril = jnp.tril(jnp.ones((CHUNK, CHUNK), F32))
    blocks = dwc2.reshape(A_GROUPS, reps, CHUNK, reps, CHUNK)
    dws = sum(blocks[:, a, :, a, :] for a in range(reps)) * tril
    dbs = dbs2.reshape(A_GROUPS, reps, CHUNK).sum(axis=1)
    return dws, dbs


def _row2(a):
    return a.reshape(1, -1)


def kernel(x, a0_w_in, a0_v_gain, a0_v_bias, a0_w_s, a0_b_s, a0_w_out, ln0_gain, ln0_bias, b1_w_in, b1_w_grp, b1_scale, b1_w_out, ln1_gain, ln1_bias, c2_w_in, c2_conv_w, c2_w_out, ln2_gain, ln2_bias, a3_w_in, a3_v_gain, a3_v_bias, a3_w_s, a3_b_s, a3_w_out, ln3_gain, ln3_bias, loss_target, m_a0_w_in, m_a0_v_gain, m_a0_v_bias, m_a0_w_s, m_a0_b_s, m_a0_w_out, m_ln0_gain, m_ln0_bias, m_b1_w_in, m_b1_w_grp, m_b1_scale, m_b1_w_out, m_ln1_gain, m_ln1_bias, m_c2_w_in, m_c2_conv_w, m_c2_w_out, m_ln2_gain, m_ln2_bias, m_a3_w_in, m_a3_v_gain, m_a3_v_bias, m_a3_w_s, m_a3_b_s, m_a3_w_out, m_ln3_gain, m_ln3_bias, v_a0_w_in, v_a0_v_gain, v_a0_v_bias, v_a0_w_s, v_a0_b_s, v_a0_w_out, v_ln0_gain, v_ln0_bias, v_b1_w_in, v_b1_w_grp, v_b1_scale, v_b1_w_out, v_ln1_gain, v_ln1_bias, v_c2_w_in, v_c2_conv_w, v_c2_w_out, v_ln2_gain, v_ln2_bias, v_a3_w_in, v_a3_v_gain, v_a3_v_bias, v_a3_w_s, v_a3_b_s, v_a3_w_out, v_ln3_gain, v_ln3_bias):
    names = ["a0_w_in", "a0_v_gain", "a0_v_bias", "a0_w_s", "a0_b_s", "a0_w_out", "ln0_gain", "ln0_bias",
             "b1_w_in", "b1_w_grp", "b1_scale", "b1_w_out", "ln1_gain", "ln1_bias",
             "c2_w_in", "c2_conv_w", "c2_w_out", "ln2_gain", "ln2_bias",
             "a3_w_in", "a3_v_gain", "a3_v_bias", "a3_w_s", "a3_b_s", "a3_w_out", "ln3_gain", "ln3_bias"]
    env = dict(locals())
    w = {nm: env[nm] for nm in names}
    mom = {nm: env["m_" + nm] for nm in names}
    var = {nm: env["v_" + nm] for nm in names}

    x0 = x[0]
    target = loss_target[0]
    d_model = x0.shape[1]
    di = N_DEV * a0_w_out.shape[0]
    n_grp = len(POOL_WINDOWS)
    gd_b = di // n_grp

    layers = ("a0", "b1", "c2", "a3")
    big_of = {"a0": ["a0_w_in", "a0_w_out"], "b1": ["b1_w_in", "b1_w_grp", "b1_w_out"],
              "c2": ["c2_w_in", "c2_w_out"], "a3": ["a3_w_in", "a3_w_out"]}
    big = [nm for p in layers for nm in big_of[p]]
    bucket_of = {"a0": ["a0_v_gain", "a0_v_bias", "a0_w_s", "a0_b_s", "ln0_gain", "ln0_bias"],
                 "b1": ["b1_scale", "ln1_gain", "ln1_bias"], "c2": ["ln2_gain", "ln2_bias"],
                 "a3": ["a3_v_gain", "a3_v_bias", "a3_w_s", "a3_b_s", "ln3_gain", "ln3_bias"]}
    conv_shape = c2_conv_w.shape
    spatial = {p: _spatial_weights(w[p + "_w_s"], w[p + "_b_s"]) for p in ("a0", "a3")}

    def weight_gather(p):
        return _Exchange(gathers=[w[nm].astype(BF16) for nm in big_of[p]])

    first = _exchange_only(_Exchange(gathers=[w[nm].astype(BF16) for nm in big_of["a0"]] + [_pack([c2_conv_w])]),
                           "gather_first")
    gathered = dict(zip(big_of["a0"], first))
    conv_all = jnp.stack([_unpack(first[-1][j], [conv_shape])[0] for j in range(N_DEV)], axis=1)
    conv_full = conv_all.reshape(conv_shape[0], di)
    w_in = lambda p: gathered[p + "_w_in"]
    w_out = lambda p: gathered[p + "_w_out"].reshape(di, d_model)
    saved = {}
    h = x0
    for i, p in enumerate(layers):
        lng, lnb = _row2(w[f"ln{i}_gain"]), _row2(w[f"ln{i}_bias"])
        nxt = layers[i + 1] if i + 1 < len(layers) else None
        ex = weight_gather(nxt) if nxt else None
        if p[0] == "a":
            wc2, _, bs2 = spatial[p]
            outs, got = _fwd_a(h, w_in(p), w_out(p), _row2(w[p + "_v_gain"]), _row2(w[p + "_v_bias"]),
                               wc2, bs2, lng, lnb, exchange=ex)
        elif p[0] == "b":
            wgrp = jnp.swapaxes(gathered["b1_w_grp"], 0, 1).reshape(n_grp, gd_b, gd_b)
            outs, got = _fwd_b(h, w_in(p), wgrp, _row2(w[p + "_scale"]), w_out(p), lng, lnb, exchange=ex)
        else:
            outs, got = _fwd_c(h, w_in(p), conv_full, w_out(p), lng, lnb, exchange=ex)
        saved[p], h = outs[:-1], outs[-1]
        if nxt:
            gathered.update(zip(big_of[nxt], got))

    gcur, sq = _loss_head(h, target)
    loss = lax.psum(jnp.sum(sq) * (0.5 / d_model), ("x", "y", "c"))

    part, full, landed, small_all = {}, {}, {}, {}

    def grad_exchange(scatter_names, bucket):
        gathers = [_pack([part[nm] for nm in bucket_of[bucket]])] if bucket else []
        return _Exchange(gathers=gathers, scatters=[full[nm] for nm in scatter_names])

    def collect(scatter_names, bucket, got):
        got = list(got)
        if bucket:
            small_all[bucket] = got.pop(0)
        landed.update(zip(scatter_names, got))

    pending = None
    for i, p in reversed(list(enumerate(layers))):
        lng = _row2(w[f"ln{i}_gain"])
        ex = grad_exchange(*pending) if pending else None
        *factors, yt, xt, pre = saved[p]
        if p[0] == "a":
            wc2, wc2t, bs2 = spatial[p]
            (dpre, dp, dlng, dlnb, dgain, dbias, dbs2, dwc2), got = _bwd_a(
                gcur, pre, *factors, w_out(p), _row2(w[p + "_v_gain"]), wc2, wc2t, bs2, lng, exchange=ex)
            part[p + "_v_gain"], part[p + "_v_bias"] = dgain.sum(axis=0), dbias.sum(axis=0)
            part[p + "_w_s"], part[p + "_b_s"] = _spatial_weight_grad(dwc2, dbs2)
        elif p[0] == "b":
            b1f, b2f, b3f, poolt = factors
            (dpre, dp, dmixed, dlng, dlnb, dscale), got = _bwd_b(
                gcur, pre, b1f, b2f, b3f, w_out(p), wgrp, lng, exchange=ex)
            part[p + "_scale"] = dscale.sum(axis=0)
            dwg, _ = _wgrad(poolt, dmixed, n_grp, True, "wgrad_grp")
            full[p + "_w_grp"] = jnp.swapaxes(dwg.reshape(n_grp, N_DEV, gd_b // N_DEV, gd_b), 0, 1)
        else:
            (dpre, dp, dlng, dlnb, dcw), got = _bwd_c(gcur, pre, *factors, w_out(p), conv_full, lng, exchange=ex)
            dconv = dcw.sum(axis=1).reshape(conv_shape[0], N_DEV, conv_shape[1])
            full["c2_conv_w"] = jnp.stack([_pack([dconv[:, j]]) for j in range(N_DEV)])
        if pending:
            collect(*pending, got)
        part[f"ln{i}_gain"], part[f"ln{i}_bias"] = dlng.sum(axis=0), dlnb.sum(axis=0)
        dwo, _ = _wgrad(yt, dpre, 1, False, "wgrad_out_" + p)
        full[p + "_w_out"] = dwo.reshape(N_DEV, di // N_DEV, d_model)
        if i > 0:
            full[p + "_w_in"], _ = _wgrad(xt, dp, N_DEV, False, "wgrad_in_" + p)
            gcur, _ = _dx(dp, dpre, w_in(p))
            pending = (big_of[p] + (["c2_conv_w"] if p == "c2" else []), p)
        else:
            full[p + "_w_in"], got = _wgrad(xt, dp, N_DEV, False, "wgrad_in_" + p,
                                            exchange=grad_exchange([p + "_w_out"], None))
            collect([p + "_w_out"], None, got)
            gcur, got = _dx(dp, dpre, w_in(p), exchange=grad_exchange([p + "_w_in"], p))
            collect([p + "_w_in"], p, got)
    grad_x = gcur[None]

    grads, deltas, new_m, new_v = {}, {}, {}, {}
    for nm in big:
        shp = w[nm].shape
        r2 = (math.prod(shp[:-1]), shp[-1])
        outs = _adamw(landed[nm].reshape((N_DEV,) + r2), w[nm].reshape(r2), mom[nm].reshape(r2), var[nm].reshape(r2),
                      "adamw_" + nm)
        grads[nm], deltas[nm], new_m[nm], new_v[nm] = (o.reshape(shp) for o in outs)
    buckets = dict(bucket_of, conv=["c2_conv_w"])
    small_all["conv"] = landed["c2_conv_w"]
    for key, members in buckets.items():
        shapes = [w[nm].shape for nm in members]
        outs = _adamw(small_all[key], _pack([w[nm] for nm in members]), _pack([mom[nm] for nm in members]),
                      _pack([var[nm] for nm in members]), "adamw_small_" + key)
        for tgt, o in zip((grads, deltas, new_m, new_v), outs):
            tgt.update(zip(members, _unpack(o, shapes)))

    return (loss, grad_x, *[grads[nm] for nm in names], *[deltas[nm] for nm in names],
            *[new_m[nm] for nm in names], *[new_v[nm] for nm in names])
```

```python
import functools
import math

import jax
import jax.numpy as jnp
from jax import lax
from jax.experimental import pallas as pl
from jax.experimental.pallas import tpu as pltpu

F32 = jnp.float32
BF16 = jnp.bfloat16

N_DEV = 8
DEPTH = 4
CHUNK = 128
A_GROUPS = 8
POOL_WINDOWS = (2, 4, 8, 16)
LN_EPS = 1e-5
ALPHA = (2.0 * DEPTH) ** 0.25
ADAM_LR = 0.001
ADAM_B1 = 0.9
ADAM_B2 = 0.999
ADAM_EPS = 1e-08
ADAM_WD = 0.01
ADAM_STEP = 10

TM = 256
HALO = 16
CHALO = 8
CW = 512
WGRAD_TK = 2048
VMEM_LIMIT_BYTES = 58 * 1024 * 1024

_NT = (((1,), (1,)), ((), ()))
_SQRT_2_OVER_PI = math.sqrt(2.0 / math.pi)
_MESH = pl.DeviceIdType.MESH


def _vmem():
    return pl.BlockSpec(memory_space=pltpu.VMEM)


def _params(sem=None):
    return pltpu.CompilerParams(dimension_semantics=sem, vmem_limit_bytes=VMEM_LIMIT_BYTES)


def _gelu(x):
    t = jnp.tanh(_SQRT_2_OVER_PI * (x + 0.044715 * (x * x * x)))
    return x * (0.5 * (1.0 + t))


def _gelu_and_grad(x):
    x2 = x * x
    t = jnp.tanh(_SQRT_2_OVER_PI * (x + 0.044715 * (x * x2)))
    cdf = 0.5 * (1.0 + t)
    grad = cdf + 0.5 * x * (1.0 - t * t) * (_SQRT_2_OVER_PI * (1.0 + 3.0 * 0.044715 * x2))
    return x * cdf, grad


def _silu_and_grad(z):
    sg = 1.0 / (1.0 + jnp.exp(-z))
    return z * sg, sg * (1.0 + z * (1.0 - sg))


def _fold8(a):
    return a.reshape(a.shape[0] // 8, 8, a.shape[1]).sum(axis=0)


def _row_mean(a):
    return jnp.mean(a, axis=-1, keepdims=True)


def _ln_stats(x):
    mu = _row_mean(x)
    xc = x - mu
    rstd = lax.rsqrt(_row_mean(xc * xc) + LN_EPS)
    return xc * rstd, rstd


def _post_norm(x, out, lng_ref, lnb_ref, pre_ref, xn_ref):
    pre = ALPHA * x + out
    pre_ref[...] = pre
    xhat, _ = _ln_stats(pre)
    xn_ref[...] = xhat * lng_ref[...] + lnb_ref[...]


def _post_norm_bwd(g_ref, pre_ref, lng_ref, dpre_ref, dlng_ref, dlnb_ref):
    go = g_ref[...]
    xhat, rstd = _ln_stats(pre_ref[...])
    dlng_ref[...] += _fold8(go * xhat)
    dlnb_ref[...] += _fold8(go)
    dxh = go * lng_ref[...]
    dpre = rstd * (dxh - _row_mean(dxh) - xhat * _row_mean(dxh * xhat))
    dpre_ref[...] = dpre
    return dpre


def _in_proj(xb, win_ref, p_ref):
    cs = win_ref.shape[2]
    for j in range(N_DEV):
        p_ref[:, j * cs:(j + 1) * cs] = jnp.dot(xb, win_ref[j], preferred_element_type=F32)


def _zero_at_first_step(*refs):
    @pl.when(pl.program_id(0) == 0)
    def _():
        for r in refs:
            r[...] = jnp.zeros(r.shape, r.dtype)


def _f32(ref, sl):
    return ref[:, sl].astype(F32)


def _my_position():
    x, y, c = lax.axis_index("x"), lax.axis_index("y"), lax.axis_index("c")
    return (x, y, c), 4 * x + 2 * y + c


def _peer(k):
    (x, y, c), _ = _my_position()
    peer = (x ^ (k >> 2), y ^ ((k >> 1) & 1), c ^ (k & 1))
    return peer, 4 * peer[0] + 2 * peer[1] + peer[2]


class _Exchange:
    def __init__(self, gathers=(), scatters=()):
        self.args = list(gathers) + list(scatters)
        self.n_gather = len(gathers)
        self.out_shape = ([jax.ShapeDtypeStruct((N_DEV,) + a.shape, a.dtype) for a in gathers]
                          + [jax.ShapeDtypeStruct(a.shape, a.dtype) for a in scatters])
        n = len(self.args)
        self.scratch = [pltpu.SemaphoreType.DMA((n, N_DEV)), pltpu.SemaphoreType.DMA((n, N_DEV)),
                        pltpu.SemaphoreType.DMA((n,))]

    def _src(self, ins, w, pos):
        return ins[w] if w < self.n_gather else ins[w].at[pos]

    def _copies(self, ins, outs, sems, arrivals):
        send_sems, recv_sems, local_sems = sems
        _, me = _my_position()
        n = len(self.args)
        copies = []
        if not arrivals:
            copies = [pltpu.make_async_copy(self._src(ins, w, me), outs[w].at[me], local_sems.at[w]) for w in range(n)]
        for k in range(1, N_DEV):
            peer, peer_pos = _peer(k)
            for w in range(n):
                copies.append(pltpu.make_async_remote_copy(
                    src_ref=self._src(ins, w, me if arrivals else peer_pos),
                    dst_ref=outs[w].at[peer_pos if arrivals else me],
                    send_sem=send_sems.at[w, k], recv_sem=recv_sems.at[w, k], device_id=peer, device_id_type=_MESH))
        return copies

    def start(self, ins, outs, sems):
        for cp in self._copies(ins, outs, sems, False):
            cp.start()

    def mid(self, ins, outs, sems):
        pass

    def wait(self, ins, outs, sems):
        n = len(self.args)
        for cp in self._copies(ins, outs, sems, True):
            cp.wait_recv()
        own = self._copies(ins, outs, sems, False)
        for cp in own[n:]:
            cp.wait_send()
        for cp in own[:n]:
            cp.wait()


def _remote(src, dst, send_sem, recv_sem, peer):
    return pltpu.make_async_remote_copy(src_ref=src, dst_ref=dst, send_sem=send_sem, recv_sem=recv_sem,
                                        device_id=peer, device_id_type=_MESH)


class _Gather:
    def __init__(self, shards):
        self.args = list(shards)
        n = len(self.args)
        self.out_shape = [jax.ShapeDtypeStruct((N_DEV,) + a.shape, a.dtype) for a in shards]
        self.scratch = [pltpu.SemaphoreType.DMA((n, N_DEV)), pltpu.SemaphoreType.DMA((n, N_DEV)),
                        pltpu.SemaphoreType.DMA((n,))]

    def _own(self, ins, outs, sems):
        send, recv, loc = sems
        _, me = _my_position()
        local = [pltpu.make_async_copy(ins[w], outs[w].at[me], loc.at[w]) for w in range(len(ins))]
        first = [_remote(ins[w], outs[w].at[me], send.at[w, k], recv.at[w, k], _peer(k)[0])
                 for k in (1, 2, 4, 6) for w in range(len(ins))]
        return local, first

    def _passed_on(self, ins, outs, sems):
        send, recv, _ = sems
        sibling, _ = _peer(1)
        return [_remote(outs[w].at[_peer(k)[1]], outs[w].at[_peer(k)[1]], send.at[w, k + 1], recv.at[w, k + 1], sibling)
                for k in (2, 4, 6) for w in range(len(ins))]

    def _arrival(self, ins, outs, sems, k, w):
        send, recv, _ = sems
        peer, pos = _peer(k)
        return _remote(ins[w], outs[w].at[pos], send.at[w, k], recv.at[w, k], peer)

    def start(self, ins, outs, sems):
        local, first = self._own(ins, outs, sems)
        for cp in local + first:
            cp.start()

    def mid(self, ins, outs, sems):
        for k in (2, 4, 6):
            for w in range(len(ins)):
                self._arrival(ins, outs, sems, k, w).wait_recv()
        for cp in self._passed_on(ins, outs, sems):
            cp.start()

    def wait(self, ins, outs, sems):
        for k in (1, 3, 5, 7):
            for w in range(len(ins)):
                self._arrival(ins, outs, sems, k, w).wait_recv()
        local, first = self._own(ins, outs, sems)
        for cp in first + self._passed_on(ins, outs, sems):
            cp.wait_send()
        for cp in local:
            cp.wait()


class _PairExchange:
    def __init__(self, fulls):
        self.args = list(fulls)
        n = len(self.args)
        half = [jax.ShapeDtypeStruct((N_DEV // 2,) + a.shape[1:], a.dtype) for a in fulls]
        self.out_shape = half + half
        self.scratch = [pltpu.SemaphoreType.DMA((n, N_DEV // 2)), pltpu.SemaphoreType.DMA((n, N_DEV // 2)),
                        pltpu.SemaphoreType.DMA((n, N_DEV // 2))]

    def _copies(self, ins, outs, sems):
        send, recv, loc = sems
        n = len(ins)
        (x, y, c), _ = _my_position()
        sibling, _ = _peer(1)
        local, remote = [], []
        for q in range(N_DEV // 2):
            for w in range(n):
                local.append(pltpu.make_async_copy(ins[w].at[2 * q + c], outs[w].at[q], loc.at[w, q]))
                remote.append(_remote(ins[w].at[2 * q + 1 - c], outs[n + w].at[q], send.at[w, q], recv.at[w, q], sibling))
        return local, remote

    def start(self, ins, outs, sems):
        local, remote = self._copies(ins, outs, sems)
        for cp in local + remote:
            cp.start()

    def mid(self, ins, outs, sems):
        pass

    def wait(self, ins, outs, sems):
        local, remote = self._copies(ins, outs, sems)
        for cp in remote:
            cp.wait()
        for cp in local:
            cp.wait()


class _ChipScatter:
    def __init__(self, sums):
        self.args = list(sums)
        n = len(self.args)
        self.out_shape = [jax.ShapeDtypeStruct(a.shape, a.dtype) for a in sums]
        self.scratch = [pltpu.SemaphoreType.DMA((n, N_DEV // 2)), pltpu.SemaphoreType.DMA((n, N_DEV // 2)),
                        pltpu.SemaphoreType.DMA((n,))]

    def _copies(self, ins, outs, sems, arrivals):
        send, recv, loc = sems
        (x, y, c), _ = _my_position()
        my_chip = 2 * x + y
        copies = []
        if not arrivals:
            copies = [pltpu.make_async_copy(ins[w].at[my_chip], outs[w].at[my_chip], loc.at[w]) for w in range(len(ins))]
        for k in (1, 2, 3):
            peer = (x ^ (k >> 1), y ^ (k & 1), c)
            chip = my_chip ^ k
            for w in range(len(ins)):
                copies.append(_remote(ins[w].at[my_chip if arrivals else chip], outs[w].at[chip if arrivals else my_chip],
                                      send.at[w, k], recv.at[w, k], peer))
        return copies

    def start(self, ins, outs, sems):
        for cp in self._copies(ins, outs, sems, False):
            cp.start()

    def mid(self, ins, outs, sems):
        pass

    def wait(self, ins, outs, sems):
        n = len(ins)
        for cp in self._copies(ins, outs, sems, True):
            cp.wait_recv()
        own = self._copies(ins, outs, sems, False)
        for cp in own[n:]:
            cp.wait_send()
        for cp in own[:n]:
            cp.wait()


def _split(refs, sizes):
    out, off = [], 0
    for size in sizes:
        out.append(refs[off:off + size])
        off += size
    return out


def _call(body, name, grid, args, in_specs, out_shape, out_specs, scratch=(), exchanges=()):
    sem = ("arbitrary",) * len(grid)
    exchanges = [e for e in exchanges if e is not None]
    if not exchanges:
        outs = pl.pallas_call(body, name=name, grid=grid, in_specs=in_specs, out_specs=out_specs, out_shape=out_shape,
                              scratch_shapes=list(scratch), compiler_params=_params(sem))(*args)
        return outs, []
    n_in, n_out, n_scr = len(args), len(out_shape), len(scratch)
    ex_in = [len(e.args) for e in exchanges]
    ex_out = [len(e.out_shape) for e in exchanges]
    ex_scr = [len(e.scratch) for e in exchanges]
    steps = math.prod(grid)
    mid_step = min((3 * steps) // 4, steps - 1)

    def hosted(*refs):
        main_in, xin, main_out, xout, main_scr, xscr = _split(
            refs, [n_in, sum(ex_in), n_out, sum(ex_out), n_scr, sum(ex_scr)])
        parts = list(zip(exchanges, _split(xin, ex_in), _split(xout, ex_out), _split(xscr, ex_scr)))
        step = pl.program_id(0)
        for a in range(1, len(grid)):
            step = step * grid[a] + pl.program_id(a)

        @pl.when(step == 0)
        def _():
            for e, ins, outs, sems in parts:
                e.start(ins, outs, sems)

        body(*main_in, *main_out, *main_scr)

        @pl.when(step == mid_step)
        def _():
            for e, ins, outs, sems in parts:
                e.mid(ins, outs, sems)

        @pl.when(step == steps - 1)
        def _():
            for e, ins, outs, sems in parts:
                e.wait(ins, outs, sems)

    any_spec = pl.BlockSpec(memory_space=pl.ANY)
    outs = pl.pallas_call(
        hosted, name=name, grid=grid, in_specs=list(in_specs) + [any_spec] * sum(ex_in),
        out_specs=list(out_specs) + [any_spec] * sum(ex_out),
        out_shape=list(out_shape) + [s for e in exchanges for s in e.out_shape],
        scratch_shapes=list(scratch) + [s for e in exchanges for s in e.scratch],
        compiler_params=_params(sem))(*args, *[a for e in exchanges for a in e.args])
    return outs[:n_out], _split(outs[n_out:], ex_out)


def _exchange_only(exchanges, name):
    ex_in = [len(e.args) for e in exchanges]
    ex_out = [len(e.out_shape) for e in exchanges]
    ex_scr = [len(e.scratch) for e in exchanges]

    def body(*refs):
        xin, xout, xscr = _split(refs, [sum(ex_in), sum(ex_out), sum(ex_scr)])
        parts = list(zip(exchanges, _split(xin, ex_in), _split(xout, ex_out), _split(xscr, ex_scr)))
        for phase in ("start", "mid", "wait"):
            for e, ins, outs, sems in parts:
                getattr(e, phase)(ins, outs, sems)

    any_spec = pl.BlockSpec(memory_space=pl.ANY)
    outs = pl.pallas_call(
        body, name=name, in_specs=[any_spec] * sum(ex_in), out_specs=[any_spec] * sum(ex_out),
        out_shape=[s for e in exchanges for s in e.out_shape],
        scratch_shapes=[s for e in exchanges for s in e.scratch])(*[a for e in exchanges for a in e.args])
    return _split(outs, ex_out)


def _tile_specs(t, d, di):
    row = lambda i: (i, 0)
    col = lambda i: (0, i)
    return dict(
        xd=pl.BlockSpec((TM, d), row), xi=pl.BlockSpec((TM, di), row),
        td=pl.BlockSpec((d, TM), col), ti=pl.BlockSpec((di, TM), col),
        s_xd=jax.ShapeDtypeStruct((t, d), F32), s_xi=jax.ShapeDtypeStruct((t, di), BF16),
        s_td=jax.ShapeDtypeStruct((d, t), BF16), s_ti=jax.ShapeDtypeStruct((di, t), BF16))


def _fwd_a(x, win, wout, gain, bias, wc2, bs2, lng, lnb, exchanges=()):
    t, d = x.shape
    di = wout.shape[0]
    gd = di // A_GROUPS

    def body(x_ref, win_ref, wout_ref, gain_ref, bias_ref, wc_ref, bs_ref, lng_ref, lnb_ref,
             a1_ref, a2_ref, a3_ref, vn_ref, vh_ref, rg_ref, yt_ref, xt_ref, pre_ref, xn_ref,
             p_scr, vg_scr, y_scr):
        xv = x_ref[...]
        xt_ref[...] = xv.T.astype(BF16)
        _in_proj(xv.astype(BF16), win_ref, p_scr)
        s1 = jnp.zeros((TM, 1), F32)
        for c in range(di // CW):
            sl = slice(c * CW, (c + 1) * CW)
            pv = slice(di + c * CW, di + (c + 1) * CW)
            vg, dvg = _gelu_and_grad(p_scr[:, pv])
            vg_scr[:, sl] = vg
            p_scr[:, pv] = dvg
            s1 += jnp.sum(vg, axis=1, keepdims=True)
        mu = s1 * (1.0 / di)
        s2 = jnp.zeros((TM, 1), F32)
        for c in range(di // CW):
            dlt = vg_scr[:, c * CW:(c + 1) * CW] - mu
            s2 += jnp.sum(dlt * dlt, axis=1, keepdims=True)
        rstd = lax.rsqrt(s2 * (1.0 / di) + LN_EPS)
        for c in range(di // CW):
            sl = slice(c * CW, (c + 1) * CW)
            vh = (vg_scr[:, sl] - mu) * rstd
            vh_ref[:, sl] = vh.astype(BF16)
            vn_ref[:, sl] = (vh * gain_ref[:, sl] + bias_ref[:, sl]).astype(BF16)
            rg_ref[:, sl] = (p_scr[:, di + c * CW:di + (c + 1) * CW] * rstd).astype(BF16)
        for g in range(A_GROUPS):
            sl = slice(g * gd, (g + 1) * gd)
            sv = jnp.dot(wc_ref[g], vn_ref[:, sl], preferred_element_type=F32) + bs_ref[g]
            u, du = _gelu_and_grad(p_scr[:, sl])
            s, ds = _silu_and_grad(p_scr[:, 2 * di + g * gd:2 * di + (g + 1) * gd])
            us = u * s
            a1_ref[:, sl] = (s * du).astype(BF16)
            a2_ref[:, sl] = (u * ds).astype(BF16)
            a3_ref[:, sl] = us.astype(BF16)
            y = us * sv
            y_scr[:, sl] = y.astype(BF16)
            yt_ref[sl, :] = y.T.astype(BF16)
        out = jnp.dot(y_scr[...], wout_ref[...], preferred_element_type=F32)
        _post_norm(xv, out, lng_ref, lnb_ref, pre_ref, xn_ref)

    sp = _tile_specs(t, d, di)
    return _call(
        body, "fwd_a", (t // TM,), (x, win, wout, gain, bias, wc2, bs2, lng, lnb),
        in_specs=[sp["xd"]] + [_vmem()] * 8,
        out_specs=[sp["xi"]] * 6 + [sp["ti"], sp["td"], sp["xd"], sp["xd"]],
        out_shape=[sp["s_xi"]] * 6 + [sp["s_ti"], sp["s_td"], sp["s_xd"], sp["s_xd"]],
        scratch=[pltpu.VMEM((TM, 3 * di), F32), pltpu.VMEM((TM, di), F32), pltpu.VMEM((TM, di), BF16)],
        exchanges=exchanges)


def _bwd_a(g, pre, a1, a2, a3, vn, vh, rg, wout, gain, wc2, wc2t, bs2, lng, exchanges=()):
    t, d = g.shape
    di = wout.shape[0]
    gd = di // A_GROUPS

    def body(g_ref, pre_ref, a1_ref, a2_ref, a3_ref, vn_ref, vh_ref, rg_ref,
             wout_ref, gain_ref, wc_ref, wct_ref, bs_ref, lng_ref,
             dpre_ref, dp_ref, dlng_ref, dlnb_ref, dgain_ref, dbias_ref, dbs_ref, dwc_ref,
             dy_scr, dv_scr):
        _zero_at_first_step(dlng_ref, dlnb_ref, dgain_ref, dbias_ref, dbs_ref, dwc_ref)
        dpre = _post_norm_bwd(g_ref, pre_ref, lng_ref, dpre_ref, dlng_ref, dlnb_ref)
        dy_scr[...] = lax.dot_general(dpre.astype(BF16), wout_ref[...], _NT, preferred_element_type=F32)
        for grp in range(A_GROUPS):
            sl = slice(grp * gd, (grp + 1) * gd)
            vn_g = vn_ref[:, sl]
            sv = jnp.dot(wc_ref[grp], vn_g, preferred_element_type=F32) + bs_ref[grp]
            dy = dy_scr[:, sl]
            dys = dy * sv
            dp_ref[:, sl] = (dys * _f32(a1_ref, sl)).astype(BF16)
            dp_ref[:, 2 * di + grp * gd:2 * di + (grp + 1) * gd] = (dys * _f32(a2_ref, sl)).astype(BF16)
            dsv = dy * _f32(a3_ref, sl)
            dbs_ref[grp] += jnp.sum(dsv, axis=1, keepdims=True)
            dsvb = dsv.astype(BF16)
            dwc_ref[grp] += lax.dot_general(dsvb, vn_g, _NT, preferred_element_type=F32)
            dv_scr[:, sl] = jnp.dot(wct_ref[grp], dsvb, preferred_element_type=F32)
        r1 = jnp.zeros((TM, 1), F32)
        r2 = jnp.zeros((TM, 1), F32)
        for c in range(di // CW):
            sl = slice(c * CW, (c + 1) * CW)
            dv = dv_scr[:, sl]
            vhat = _f32(vh_ref, sl)
            dgain_ref[:, sl] += _fold8(dv * vhat)
            dbias_ref[:, sl] += _fold8(dv)
            dvh = dv * gain_ref[:, sl]
            dv_scr[:, sl] = dvh
            r1 += jnp.sum(dvh, axis=1, keepdims=True)
            r2 += jnp.sum(dvh * vhat, axis=1, keepdims=True)
        m1 = r1 * (1.0 / di)
        m2 = r2 * (1.0 / di)
        for c in range(di // CW):
            sl = slice(c * CW, (c + 1) * CW)
            dp_ref[:, di + c * CW:di + (c + 1) * CW] = (
                (dv_scr[:, sl] - m1 - _f32(vh_ref, sl) * m2) * _f32(rg_ref, sl)).astype(BF16)

    sp = _tile_specs(t, d, di)
    const2 = lambda i: (0, 0)
    const3 = lambda i: (0, 0, 0)
    return _call(
        body, "bwd_a", (t // TM,), (g, pre, a1, a2, a3, vn, vh, rg, wout, gain, wc2, wc2t, bs2, lng),
        in_specs=[sp["xd"], sp["xd"]] + [sp["xi"]] * 6 + [_vmem()] * 6,
        out_specs=[sp["xd"], pl.BlockSpec((TM, 3 * di), lambda i: (i, 0)),
                   pl.BlockSpec((8, d), const2), pl.BlockSpec((8, d), const2),
                   pl.BlockSpec((8, di), const2), pl.BlockSpec((8, di), const2),
                   pl.BlockSpec((A_GROUPS, TM, 1), const3), pl.BlockSpec((A_GROUPS, TM, TM), const3)],
        out_shape=[sp["s_xd"], jax.ShapeDtypeStruct((t, 3 * di), BF16),
                   jax.ShapeDtypeStruct((8, d), F32), jax.ShapeDtypeStruct((8, d), F32),
                   jax.ShapeDtypeStruct((8, di), F32), jax.ShapeDtypeStruct((8, di), F32),
                   jax.ShapeDtypeStruct((A_GROUPS, TM, 1), F32), jax.ShapeDtypeStruct((A_GROUPS, TM, TM), F32)],
        scratch=[pltpu.VMEM((TM, di), F32), pltpu.VMEM((TM, di), F32)],
        exchanges=exchanges)


def _inv_count(tile, window):
    pos = tile * TM + lax.broadcasted_iota(jnp.int32, (TM, 1), 0)
    return 1.0 / jnp.minimum(pos + 1, window).astype(F32)


def _window_sum(ext, window, down):
    rows = ext.shape[0]
    k = 1
    while k < window:
        ext = ext + pltpu.roll(ext, k if down else rows - k, 0)
        k *= 2
    return ext


def _fwd_b(x, win, wgrp, scale, wout, lng, lnb, exchanges=()):
    t, d = x.shape
    di = wout.shape[0]
    gd = di // len(POOL_WINDOWS)

    def body(x_ref, win_ref, wgrp_ref, scale_ref, wout_ref, lng_ref, lnb_ref,
             b1_ref, b2_ref, b3_ref, poolt_ref, yt_ref, xt_ref, pre_ref, xn_ref, p_scr, ext_scr, y_scr):
        i = pl.program_id(0)

        @pl.when(i == 0)
        def _():
            ext_scr[0:HALO, :] = jnp.zeros((HALO, di), F32)

        xv = x_ref[...]
        xt_ref[...] = xv.T.astype(BF16)
        _in_proj(xv.astype(BF16), win_ref, p_scr)
        ext_scr[HALO:, :] = p_scr[:, :di]
        for grp, window in enumerate(POOL_WINDOWS):
            sl = slice(grp * gd, (grp + 1) * gd)
            ext = ext_scr[:, sl]
            pooled = (_window_sum(ext, window, True)[HALO:] * _inv_count(i, window) - ext[HALO:]).astype(BF16)
            poolt_ref[sl, :] = pooled.astype(F32).T.astype(BF16)
            mixed = jnp.dot(pooled, wgrp_ref[grp], preferred_element_type=F32)
            s, ds = _silu_and_grad(p_scr[:, di + grp * gd:di + (grp + 1) * gd])
            sc = scale_ref[:, sl]
            ms = mixed * s
            b1_ref[:, sl] = (mixed * sc * ds).astype(BF16)
            b2_ref[:, sl] = ms.astype(BF16)
            b3_ref[:, sl] = (sc * s).astype(BF16)
            y = ms * sc
            y_scr[:, sl] = y.astype(BF16)
            yt_ref[sl, :] = y.T.astype(BF16)
        ext_scr[0:HALO, :] = ext_scr[TM:TM + HALO, :]
        out = jnp.dot(y_scr[...], wout_ref[...], preferred_element_type=F32)
        _post_norm(xv, out, lng_ref, lnb_ref, pre_ref, xn_ref)

    sp = _tile_specs(t, d, di)
    return _call(
        body, "fwd_b", (t // TM,), (x, win, wgrp, scale, wout, lng, lnb),
        in_specs=[sp["xd"]] + [_vmem()] * 6,
        out_specs=[sp["xi"]] * 3 + [sp["ti"], sp["ti"], sp["td"], sp["xd"], sp["xd"]],
        out_shape=[sp["s_xi"]] * 3 + [sp["s_ti"], sp["s_ti"], sp["s_td"], sp["s_xd"], sp["s_xd"]],
        scratch=[pltpu.VMEM((TM, 2 * di), F32), pltpu.VMEM((TM + HALO, di), F32), pltpu.VMEM((TM, di), BF16)],
        exchanges=exchanges)


def _bwd_b(g, pre, b1, b2, b3, wout, wgrp, lng, exchanges=()):
    t, d = g.shape
    di = wout.shape[0]
    gd = di // len(POOL_WINDOWS)
    nt = t // TM

    def body(g_ref, pre_ref, b1_ref, b2_ref, b3_ref, wout_ref, wgrp_ref, lng_ref,
             dpre_ref, dp_ref, dmix_ref, dlng_ref, dlnb_ref, dscale_ref, dy_scr, ext_scr):
        i = pl.program_id(0)
        tile = nt - 1 - i
        _zero_at_first_step(dlng_ref, dlnb_ref, dscale_ref)

        @pl.when(i == 0)
        def _():
            ext_scr[TM:, :] = jnp.zeros((HALO, di), F32)

        dpre = _post_norm_bwd(g_ref, pre_ref, lng_ref, dpre_ref, dlng_ref, dlnb_ref)
        dy_scr[...] = lax.dot_general(dpre.astype(BF16), wout_ref[...], _NT, preferred_element_type=F32)
        for grp, window in enumerate(POOL_WINDOWS):
            sl = slice(grp * gd, (grp + 1) * gd)
            dy = dy_scr[:, sl]
            dp_ref[:, di + grp * gd:di + (grp + 1) * gd] = (dy * _f32(b1_ref, sl)).astype(BF16)
            dscale_ref[:, sl] += _fold8(dy * _f32(b2_ref, sl))
            dmixed = (dy * _f32(b3_ref, sl)).astype(BF16)
            dmix_ref[:, sl] = dmixed
            dpooled = lax.dot_general(dmixed, wgrp_ref[grp], _NT, preferred_element_type=F32)
            ext_scr[0:TM, sl] = dpooled * _inv_count(tile, window)
            dv = _window_sum(ext_scr[:, sl], window, False)[0:TM] - dpooled
            dp_ref[:, sl] = dv.astype(BF16)
        ext_scr[TM:, :] = ext_scr[0:HALO, :]

    rrow = lambda i: (nt - 1 - i, 0)
    const2 = lambda i: (0, 0)
    xd, xi = pl.BlockSpec((TM, d), rrow), pl.BlockSpec((TM, di), rrow)
    return _call(
        body, "bwd_b", (nt,), (g, pre, b1, b2, b3, wout, wgrp, lng),
        in_specs=[xd, xd, xi, xi, xi, _vmem(), _vmem(), _vmem()],
        out_specs=[xd, pl.BlockSpec((TM, 2 * di), rrow), xi,
                   pl.BlockSpec((8, d), const2), pl.BlockSpec((8, d), const2), pl.BlockSpec((8, di), const2)],
        out_shape=[jax.ShapeDtypeStruct((t, d), F32), jax.ShapeDtypeStruct((t, 2 * di), BF16),
                   jax.ShapeDtypeStruct((t, di), BF16),
                   jax.ShapeDtypeStruct((8, d), F32), jax.ShapeDtypeStruct((8, d), F32),
                   jax.ShapeDtypeStruct((8, di), F32)],
        scratch=[pltpu.VMEM((TM, di), F32), pltpu.VMEM((TM + HALO, di), F32)],
        exchanges=exchanges)


def _fwd_c(x, win, convw, wout, lng, lnb, exchanges=()):
    t, d = x.shape
    di = wout.shape[0]

    def body(x_ref, win_ref, cw_ref, wout_ref, lng_ref, lnb_ref,
             c1_ref, c2_ref, c3_ref, cg_ref, hg_ref, yt_ref, xt_ref, pre_ref, xn_ref, p_scr, ext_scr, y_scr):
        i = pl.program_id(0)

        @pl.when(i == 0)
        def _():
            ext_scr[0:CHALO, :] = jnp.zeros((CHALO, di), F32)

        xv = x_ref[...]
        xt_ref[...] = xv.T.astype(BF16)
        _in_proj(xv.astype(BF16), win_ref, p_scr)
        for c in range(di // CW):
            sl = slice(c * CW, (c + 1) * CW)
            bb = p_scr[:, sl]
            cc = p_scr[:, di + c * CW:di + (c + 1) * CW]
            hh = p_scr[:, 2 * di + c * CW:2 * di + (c + 1) * CW]
            s, ds = _silu_and_grad(p_scr[:, 3 * di + c * CW:3 * di + (c + 1) * CW])
            ext_scr[CHALO:, sl] = cc * hh
            ext = ext_scr[:, sl]
            conv = (pltpu.roll(ext, 2, 0)[CHALO:] * cw_ref[0:1, sl] + pltpu.roll(ext, 1, 0)[CHALO:] * cw_ref[1:2, sl]
                    + ext[CHALO:] * cw_ref[2:3, sl])
            cs = conv * s
            c1_ref[:, sl] = cs.astype(BF16)
            c2_ref[:, sl] = (bb * conv * ds).astype(BF16)
            c3_ref[:, sl] = (bb * s).astype(BF16)
            cg_ref[:, sl] = cc.astype(BF16)
            hg_ref[:, sl] = hh.astype(BF16)
            y = bb * cs
            y_scr[:, sl] = y.astype(BF16)
            yt_ref[sl, :] = y.T.astype(BF16)
        ext_scr[0:CHALO, :] = ext_scr[TM:TM + CHALO, :]
        out = jnp.dot(y_scr[...], wout_ref[...], preferred_element_type=F32)
        _post_norm(xv, out, lng_ref, lnb_ref, pre_ref, xn_ref)

    sp = _tile_specs(t, d, di)
    return _call(
        body, "fwd_c", (t // TM,), (x, win, convw, wout, lng, lnb),
        in_specs=[sp["xd"]] + [_vmem()] * 5,
        out_specs=[sp["xi"]] * 5 + [sp["ti"], sp["td"], sp["xd"], sp["xd"]],
        out_shape=[sp["s_xi"]] * 5 + [sp["s_ti"], sp["s_td"], sp["s_xd"], sp["s_xd"]],
        scratch=[pltpu.VMEM((TM, 4 * di), F32), pltpu.VMEM((TM + CHALO, di), F32), pltpu.VMEM((TM, di), BF16)],
        exchanges=exchanges)


def _bwd_c(g, pre, c1, c2, c3, cg, hg, wout, convw, lng, exchanges=()):
    t, d = g.shape
    di = wout.shape[0]
    nt = t // TM

    def body(g_ref, pre_ref, c1_ref, c2_ref, c3_ref, cg_ref, hg_ref, wout_ref, cw_ref, lng_ref,
             dpre_ref, dp_ref, dlng_ref, dlnb_ref, dcw_ref, dy_scr, ext_scr):
        i = pl.program_id(0)
        _zero_at_first_step(dlng_ref, dlnb_ref, dcw_ref)

        @pl.when(i == 0)
        def _():
            ext_scr[TM:, :] = jnp.zeros((CHALO, di), F32)

        dpre = _post_norm_bwd(g_ref, pre_ref, lng_ref, dpre_ref, dlng_ref, dlnb_ref)
        dy_scr[...] = lax.dot_general(dpre.astype(BF16), wout_ref[...], _NT, preferred_element_type=F32)
        rows = TM + CHALO
        for c in range(di // CW):
            sl = slice(c * CW, (c + 1) * CW)
            dy = dy_scr[:, sl]
            cc = _f32(cg_ref, sl)
            hh = _f32(hg_ref, sl)
            dp_ref[:, sl] = (dy * _f32(c1_ref, sl)).astype(BF16)
            dp_ref[:, 3 * di + c * CW:3 * di + (c + 1) * CW] = (dy * _f32(c2_ref, sl)).astype(BF16)
            dconv = dy * _f32(c3_ref, sl)
            ext_scr[0:TM, sl] = dconv
            ext = ext_scr[:, sl]
            d1 = pltpu.roll(ext, rows - 1, 0)[0:TM]
            d2 = pltpu.roll(ext, rows - 2, 0)[0:TM]
            dq = dconv * cw_ref[2:3, sl] + d1 * cw_ref[1:2, sl] + d2 * cw_ref[0:1, sl]
            q = cc * hh
            dcw_ref[0, :, sl] += _fold8(q * d2)
            dcw_ref[1, :, sl] += _fold8(q * d1)
            dcw_ref[2, :, sl] += _fold8(q * dconv)
            dp_ref[:, di + c * CW:di + (c + 1) * CW] = (dq * hh).astype(BF16)
            dp_ref[:, 2 * di + c * CW:2 * di + (c + 1) * CW] = (dq * cc).astype(BF16)
        ext_scr[TM:, :] = ext_scr[0:CHALO, :]

    rrow = lambda i: (nt - 1 - i, 0)
    const2 = lambda i: (0, 0)
    xd, xi = pl.BlockSpec((TM, d), rrow), pl.BlockSpec((TM, di), rrow)
    return _call(
        body, "bwd_c", (nt,), (g, pre, c1, c2, c3, cg, hg, wout, convw, lng),
        in_specs=[xd, xd] + [xi] * 5 + [_vmem()] * 3,
        out_specs=[xd, pl.BlockSpec((TM, 4 * di), rrow),
                   pl.BlockSpec((8, d), const2), pl.BlockSpec((8, d), const2),
                   pl.BlockSpec((3, 8, di), lambda i: (0, 0, 0))],
        out_shape=[jax.ShapeDtypeStruct((t, d), F32), jax.ShapeDtypeStruct((t, 4 * di), BF16),
                   jax.ShapeDtypeStruct((8, d), F32), jax.ShapeDtypeStruct((8, d), F32),
                   jax.ShapeDtypeStruct((3, 8, di), F32)],
        scratch=[pltpu.VMEM((TM, di), F32), pltpu.VMEM((TM + CHALO, di), F32)],
        exchanges=exchanges)


def _dx(dp, dpre, win, exchanges=()):
    t, d = dpre.shape
    n = dp.shape[1]
    cs = win.shape[2]

    def body(dp_ref, dpre_ref, win_ref, dx_ref):
        acc = ALPHA * dpre_ref[...]
        for j in range(N_DEV):
            acc += lax.dot_general(dp_ref[:, j * cs:(j + 1) * cs], win_ref[j], _NT, preferred_element_type=F32)
        dx_ref[...] = acc

    row = lambda i: (i, 0)
    (dx,), ex = _call(
        body, "dx", (t // TM,), (dp, dpre, win),
        in_specs=[pl.BlockSpec((TM, n), row), pl.BlockSpec((TM, d), row), _vmem()],
        out_specs=[pl.BlockSpec((TM, d), row)],
        out_shape=[jax.ShapeDtypeStruct((t, d), F32)],
        exchanges=exchanges)
    return dx, ex


def _wgrad(at, b, nb, per_block_rows, name, exchanges=()):
    m_all, t = at.shape
    tn = b.shape[1] // nb
    m = m_all // nb if per_block_rows else m_all
    tk = min(WGRAD_TK, t)
    nk = t // tk

    def body(at_ref, b_ref, out_ref, acc):
        k = pl.program_id(1)

        @pl.when(k == 0)
        def _():
            acc[...] = jnp.zeros(acc.shape, F32)

        acc[...] += jnp.dot(at_ref[...], b_ref[...].astype(BF16), preferred_element_type=F32)

        @pl.when(k == nk - 1)
        def _():
            out_ref[0] = acc[...].astype(BF16)

    at_map = (lambda j, k: (j, k)) if per_block_rows else (lambda j, k: (0, k))
    (out,), ex = _call(
        body, name, (nb, nk), (at, b),
        in_specs=[pl.BlockSpec((m, tk), at_map), pl.BlockSpec((tk, tn), lambda j, k: (k, j))],
        out_specs=[pl.BlockSpec((1, m, tn), lambda j, k: (j, 0, 0))],
        out_shape=[jax.ShapeDtypeStruct((nb, m, tn), BF16)],
        scratch=[pltpu.VMEM((m, tn), F32)],
        exchanges=exchanges)
    return out, ex


def _loss_head(y, target):
    t, d = y.shape

    def body(y_ref, t_ref, dy_ref, sq_ref):
        _zero_at_first_step(sq_ref)
        diff = y_ref[...] - t_ref[...]
        dy_ref[...] = diff * (1.0 / d)
        sq_ref[...] += _fold8(diff * diff)

    row = lambda i: (i, 0)
    return pl.pallas_call(
        body, name="loss_head", grid=(t // TM,),
        in_specs=[pl.BlockSpec((TM, d), row), pl.BlockSpec((TM, d), row)],
        out_specs=[pl.BlockSpec((TM, d), row), pl.BlockSpec((8, d), lambda i: (0, 0))],
        out_shape=[jax.ShapeDtypeStruct((t, d), F32), jax.ShapeDtypeStruct((8, d), F32)],
        compiler_params=_params(("arbitrary",)),
    )(y, target)


ADAMW_BLOCK_BYTES = 6 * 1024 * 1024


def _sum_parts(parts_ref):
    g = parts_ref[0].astype(F32)
    for s in range(1, parts_ref.shape[0]):
        g = g + parts_ref[s].astype(F32)
    return g


def _row_tile(rows, bytes_per_row):
    if rows * bytes_per_row <= ADAMW_BLOCK_BYTES:
        return rows
    best = 8
    for cand in range(8, rows, 8):
        if rows % cand == 0 and cand * bytes_per_row <= ADAMW_BLOCK_BYTES:
            best = cand
    return best


def _pair_sum(mine, theirs, name):
    shape = mine.shape
    r, c = math.prod(shape[:-1]), shape[-1]
    tr = _row_tile(r, c * 3 * mine.dtype.itemsize)

    def body(a_ref, b_ref, out_ref):
        out_ref[...] = (a_ref[...].astype(F32) + b_ref[...].astype(F32)).astype(out_ref.dtype)

    blk = pl.BlockSpec((tr, c), lambda i: (i, 0))
    return pl.pallas_call(
        body, name=name, grid=(r // tr,), in_specs=[blk, blk], out_specs=blk,
        out_shape=jax.ShapeDtypeStruct((r, c), mine.dtype), compiler_params=_params(("arbitrary",)),
    )(mine.reshape(r, c), theirs.reshape(r, c)).reshape(shape)


def _adamw(parts, w, m, v, name):
    s, r, c = parts.shape
    tr = _row_tile(r, c * (s * parts.dtype.itemsize + 7 * 4))
    bc1 = 1.0 - ADAM_B1 ** ADAM_STEP
    bc2 = 1.0 - ADAM_B2 ** ADAM_STEP

    def body(parts_ref, w_ref, m_ref, v_ref, g_ref, d_ref, nm_ref, nv_ref):
        g = _sum_parts(parts_ref)
        g_ref[...] = g
        nm = ADAM_B1 * m_ref[...] + (1.0 - ADAM_B1) * g
        nv = ADAM_B2 * v_ref[...] + (1.0 - ADAM_B2) * (g * g)
        nm_ref[...] = nm
        nv_ref[...] = nv
        d_ref[...] = -ADAM_LR * ((nm / bc1) / (jnp.sqrt(nv / bc2) + ADAM_EPS) + ADAM_WD * w_ref[...])

    blk = pl.BlockSpec((tr, c), lambda i: (i, 0))
    return pl.pallas_call(
        body, name=name, grid=(r // tr,),
        in_specs=[pl.BlockSpec((s, tr, c), lambda i: (0, i, 0)), blk, blk, blk],
        out_specs=[blk, blk, blk, blk],
        out_shape=[jax.ShapeDtypeStruct((r, c), F32)] * 4,
        compiler_params=_params(("arbitrary",)),
    )(parts, w, m, v)


_LANES = 128


def _pack(arrays):
    flat = jnp.concatenate([a.reshape(-1) for a in arrays])
    pad = (-flat.shape[0]) % (8 * _LANES)
    return jnp.pad(flat, (0, pad)).reshape(-1, _LANES)


def _unpack(packed, shapes):
    flat = packed.reshape(-1)
    out, off = [], 0
    for shp in shapes:
        size = math.prod(shp)
        out.append(flat[off:off + size].reshape(shp))
        off += size
    return out


def _spatial_weights(w_s, b_s):
    reps = TM // CHUNK
    tril = jnp.tril(jnp.ones((CHUNK, CHUNK), F32))
    wc = w_s * tril
    eye = jnp.eye(reps, dtype=F32)
    wc2 = jnp.einsum("ab,gts->gatbs", eye, wc).reshape(A_GROUPS, TM, TM)
    bs2 = jnp.tile(b_s, (1, reps)).reshape(A_GROUPS, TM, 1)
    return wc2.astype(BF16), jnp.swapaxes(wc2, 1, 2).astype(BF16), bs2


def _spatial_weight_grad(dwc2, dbs2):
    reps = TM // CHUNK
    tril = jnp.tril(jnp.ones((CHUNK, CHUNK), F32))
    blocks = dwc2.reshape(A_GROUPS, reps, CHUNK, reps, CHUNK)
    dws = sum(blocks[:, a, :, a, :] for a in range(reps)) * tril
    dbs = dbs2.reshape(A_GROUPS, reps, CHUNK).sum(axis=1)
    return dws, dbs


def _row2(a):
    return a.reshape(1, -1)


def kernel(x, a0_w_in, a0_v_gain, a0_v_bias, a0_w_s, a0_b_s, a0_w_out, ln0_gain, ln0_bias, b1_w_in, b1_w_grp, b1_scale, b1_w_out, ln1_gain, ln1_bias, c2_w_in, c2_conv_w, c2_w_out, ln2_gain, ln2_bias, a3_w_in, a3_v_gain, a3_v_bias, a3_w_s, a3_b_s, a3_w_out, ln3_gain, ln3_bias, loss_target, m_a0_w_in, m_a0_v_gain, m_a0_v_bias, m_a0_w_s, m_a0_b_s, m_a0_w_out, m_ln0_gain, m_ln0_bias, m_b1_w_in, m_b1_w_grp, m_b1_scale, m_b1_w_out, m_ln1_gain, m_ln1_bias, m_c2_w_in, m_c2_conv_w, m_c2_w_out, m_ln2_gain, m_ln2_bias, m_a3_w_in, m_a3_v_gain, m_a3_v_bias, m_a3_w_s, m_a3_b_s, m_a3_w_out, m_ln3_gain, m_ln3_bias, v_a0_w_in, v_a0_v_gain, v_a0_v_bias, v_a0_w_s, v_a0_b_s, v_a0_w_out, v_ln0_gain, v_ln0_bias, v_b1_w_in, v_b1_w_grp, v_b1_scale, v_b1_w_out, v_ln1_gain, v_ln1_bias, v_c2_w_in, v_c2_conv_w, v_c2_w_out, v_ln2_gain, v_ln2_bias, v_a3_w_in, v_a3_v_gain, v_a3_v_bias, v_a3_w_s, v_a3_b_s, v_a3_w_out, v_ln3_gain, v_ln3_bias):
    names = ["a0_w_in", "a0_v_gain", "a0_v_bias", "a0_w_s", "a0_b_s", "a0_w_out", "ln0_gain", "ln0_bias",
             "b1_w_in", "b1_w_grp", "b1_scale", "b1_w_out", "ln1_gain", "ln1_bias",
             "c2_w_in", "c2_conv_w", "c2_w_out", "ln2_gain", "ln2_bias",
             "a3_w_in", "a3_v_gain", "a3_v_bias", "a3_w_s", "a3_b_s", "a3_w_out", "ln3_gain", "ln3_bias"]
    env = dict(locals())
    w = {nm: env[nm] for nm in names}
    mom = {nm: env["m_" + nm] for nm in names}
    var = {nm: env["v_" + nm] for nm in names}

    x0 = x[0]
    target = loss_target[0]
    d_model = x0.shape[1]
    di = N_DEV * a0_w_out.shape[0]
    n_grp = len(POOL_WINDOWS)
    gd_b = di // n_grp

    layers = ("a0", "b1", "c2", "a3")
    big_of = {"a0": ["a0_w_in", "a0_w_out"], "b1": ["b1_w_in", "b1_w_grp", "b1_w_out"],
              "c2": ["c2_w_in", "c2_w_out"], "a3": ["a3_w_in", "a3_w_out"]}
    big = [nm for p in layers for nm in big_of[p]]
    bucket_of = {"a0": ["a0_v_gain", "a0_v_bias", "a0_w_s", "a0_b_s", "ln0_gain", "ln0_bias"],
                 "b1": ["b1_scale", "ln1_gain", "ln1_bias"], "c2": ["ln2_gain", "ln2_bias"],
                 "a3": ["a3_v_gain", "a3_v_bias", "a3_w_s", "a3_b_s", "ln3_gain", "ln3_bias"]}
    conv_shape = c2_conv_w.shape
    spatial = {p: _spatial_weights(w[p + "_w_s"], w[p + "_b_s"]) for p in ("a0", "a3")}

    def weight_gather(p):
        return _Gather([w[nm].astype(BF16) for nm in big_of[p]])

    (first,) = _exchange_only([_Gather([w[nm].astype(BF16) for nm in big_of["a0"]] + [_pack([c2_conv_w])])],
                              "gather_first")
    gathered = dict(zip(big_of["a0"], first))
    conv_all = jnp.stack([_unpack(first[-1][j], [conv_shape])[0] for j in range(N_DEV)], axis=1)
    conv_full = conv_all.reshape(conv_shape[0], di)
    w_in = lambda p: gathered[p + "_w_in"]
    w_out = lambda p: gathered[p + "_w_out"].reshape(di, d_model)
    saved = {}
    h = x0
    for i, p in enumerate(layers):
        lng, lnb = _row2(w[f"ln{i}_gain"]), _row2(w[f"ln{i}_bias"])
        nxt = layers[i + 1] if i + 1 < len(layers) else None
        ex = [weight_gather(nxt)] if nxt else []
        if p[0] == "a":
            wc2, _, bs2 = spatial[p]
            outs, got = _fwd_a(h, w_in(p), w_out(p), _row2(w[p + "_v_gain"]), _row2(w[p + "_v_bias"]),
                               wc2, bs2, lng, lnb, exchanges=ex)
        elif p[0] == "b":
            wgrp = jnp.swapaxes(gathered["b1_w_grp"], 0, 1).reshape(n_grp, gd_b, gd_b)
            outs, got = _fwd_b(h, w_in(p), wgrp, _row2(w[p + "_scale"]), w_out(p), lng, lnb, exchanges=ex)
        else:
            outs, got = _fwd_c(h, w_in(p), conv_full, w_out(p), lng, lnb, exchanges=ex)
        saved[p], h = outs[:-1], outs[-1]
        if nxt:
            gathered.update(zip(big_of[nxt], got[0]))

    gcur, sq = _loss_head(h, target)
    loss = lax.psum(jnp.sum(sq) * (0.5 / d_model), ("x", "y", "c"))

    part, full, landed, small_all = {}, {}, {}, {}

    def bucket_gather(bucket):
        return _Gather([_pack([part[nm] for nm in bucket_of[bucket]])])

    def chip_sums(names, got):
        n = len(names)
        return [_pair_sum(got[k], got[n + k], "pair_sum_" + nm) for k, nm in enumerate(names)]

    pending = None
    for i, p in reversed(list(enumerate(layers))):
        lng = _row2(w[f"ln{i}_gain"])
        ex = []
        if pending:
            ex = [_PairExchange([full[nm] for nm in big_of[pending]]), bucket_gather(pending)]
            if pending == "c2":
                ex.append(_Exchange(scatters=[full["c2_conv_w"]]))
        *factors, yt, xt, pre = saved[p]
        if p[0] == "a":
            wc2, wc2t, bs2 = spatial[p]
            (dpre, dp, dlng, dlnb, dgain, dbias, dbs2, dwc2), got = _bwd_a(
                gcur, pre, *factors, w_out(p), _row2(w[p + "_v_gain"]), wc2, wc2t, bs2, lng, exchanges=ex)
            part[p + "_v_gain"], part[p + "_v_bias"] = dgain.sum(axis=0), dbias.sum(axis=0)
            part[p + "_w_s"], part[p + "_b_s"] = _spatial_weight_grad(dwc2, dbs2)
        elif p[0] == "b":
            b1f, b2f, b3f, poolt = factors
            (dpre, dp, dmixed, dlng, dlnb, dscale), got = _bwd_b(
                gcur, pre, b1f, b2f, b3f, w_out(p), wgrp, lng, exchanges=ex)
            part[p + "_scale"] = dscale.sum(axis=0)
            dwg, _ = _wgrad(poolt, dmixed, n_grp, True, "wgrad_grp")
            full[p + "_w_grp"] = jnp.swapaxes(dwg.reshape(n_grp, N_DEV, gd_b // N_DEV, gd_b), 0, 1)
        else:
            (dpre, dp, dlng, dlnb, dcw), got = _bwd_c(gcur, pre, *factors, w_out(p), conv_full, lng, exchanges=ex)
            dconv = dcw.sum(axis=1).reshape(conv_shape[0], N_DEV, conv_shape[1])
            full["c2_conv_w"] = jnp.stack([_pack([dconv[:, j]]) for j in range(N_DEV)])
        part[f"ln{i}_gain"], part[f"ln{i}_bias"] = dlng.sum(axis=0), dlnb.sum(axis=0)
        ex = []
        if pending:
            small_all[pending] = got[1][0]
            if pending == "c2":
                small_all["conv"] = got[2][0]
            ex = [_ChipScatter(chip_sums(big_of[pending], got[0]))]
        full[p + "_w_in"], got = _wgrad(xt, dp, N_DEV, False, "wgrad_in_" + p, exchanges=ex)
        if pending:
            landed.update(zip(big_of[pending], got[0]))
        if i > 0:
            dwo, _ = _wgrad(yt, dpre, 1, False, "wgrad_out_" + p)
            full[p + "_w_out"] = dwo.reshape(N_DEV, di // N_DEV, d_model)
            gcur, _ = _dx(dp, dpre, w_in(p))
            pending = p
        else:
            dwo, got = _wgrad(yt, dpre, 1, False, "wgrad_out_" + p, exchanges=[_PairExchange([full[p + "_w_in"]])])
            full[p + "_w_out"] = dwo.reshape(N_DEV, di // N_DEV, d_model)
            gcur, got = _dx(dp, dpre, w_in(p), exchanges=[
                _ChipScatter(chip_sums([p + "_w_in"], got[0])), _Exchange(scatters=[full[p + "_w_out"]]),
                bucket_gather(p)])
            landed[p + "_w_in"], landed[p + "_w_out"], small_all[p] = got[0][0], got[1][0], got[2][0]
    grad_x = gcur[None]

    grads, deltas, new_m, new_v = {}, {}, {}, {}
    for nm in big:
        shp = w[nm].shape
        r2 = (math.prod(shp[:-1]), shp[-1])
        outs = _adamw(landed[nm].reshape((-1,) + r2), w[nm].reshape(r2), mom[nm].reshape(r2), var[nm].reshape(r2),
                      "adamw_" + nm)
        grads[nm], deltas[nm], new_m[nm], new_v[nm] = (o.reshape(shp) for o in outs)
    buckets = dict(bucket_of, conv=["c2_conv_w"])
    for key, members in buckets.items():
        shapes = [w[nm].shape for nm in members]
        outs = _adamw(small_all[key], _pack([w[nm] for nm in members]), _pack([mom[nm] for nm in members]),
                      _pack([var[nm] for nm in members]), "adamw_small_" + key)
        for tgt, o in zip((grads, deltas, new_m, new_v), outs):
            tgt.update(zip(members, _unpack(o, shapes)))

    return (loss, grad_x, *[grads[nm] for nm in names], *[deltas[nm] for nm in names],
            *[new_m[nm] for nm in names], *[new_v[nm] for nm in names])
```

```python
import functools
import math

import jax
import jax.numpy as jnp
from jax import lax
from jax.experimental import pallas as pl
from jax.experimental.pallas import tpu as pltpu

F32 = jnp.float32
BF16 = jnp.bfloat16

N_DEV = 8
DEPTH = 4
CHUNK = 128
A_GROUPS = 8
POOL_WINDOWS = (2, 4, 8, 16)
LN_EPS = 1e-5
ALPHA = (2.0 * DEPTH) ** 0.25
ADAM_LR = 0.001
ADAM_B1 = 0.9
ADAM_B2 = 0.999
ADAM_EPS = 1e-08
ADAM_WD = 0.01
ADAM_STEP = 10

TM = 256
HALO = 16
CHALO = 8
CW = 512
WGRAD_TK = 2048
VMEM_LIMIT_BYTES = 58 * 1024 * 1024

_NT = (((1,), (1,)), ((), ()))
_SQRT_2_OVER_PI = math.sqrt(2.0 / math.pi)
_MESH = pl.DeviceIdType.MESH


def _vmem():
    return pl.BlockSpec(memory_space=pltpu.VMEM)


def _params(sem=None):
    return pltpu.CompilerParams(dimension_semantics=sem, vmem_limit_bytes=VMEM_LIMIT_BYTES)


def _gelu(x):
    t = jnp.tanh(_SQRT_2_OVER_PI * (x + 0.044715 * (x * x * x)))
    return x * (0.5 * (1.0 + t))


def _gelu_and_grad(x):
    x2 = x * x
    t = jnp.tanh(_SQRT_2_OVER_PI * (x + 0.044715 * (x * x2)))
    cdf = 0.5 * (1.0 + t)
    grad = cdf + 0.5 * x * (1.0 - t * t) * (_SQRT_2_OVER_PI * (1.0 + 3.0 * 0.044715 * x2))
    return x * cdf, grad


def _silu_and_grad(z):
    sg = 1.0 / (1.0 + jnp.exp(-z))
    return z * sg, sg * (1.0 + z * (1.0 - sg))


def _fold8(a):
    return a.reshape(a.shape[0] // 8, 8, a.shape[1]).sum(axis=0)


def _row_mean(a):
    return jnp.mean(a, axis=-1, keepdims=True)


def _ln_stats(x):
    mu = _row_mean(x)
    xc = x - mu
    rstd = lax.rsqrt(_row_mean(xc * xc) + LN_EPS)
    return xc * rstd, rstd


def _post_norm(x, out, lng_ref, lnb_ref, pre_ref, xn_ref):
    pre = ALPHA * x + out
    pre_ref[...] = pre
    xhat, _ = _ln_stats(pre)
    xn_ref[...] = xhat * lng_ref[...] + lnb_ref[...]


def _post_norm_bwd(g_ref, pre_ref, lng_ref, dpre_ref, dlng_ref, dlnb_ref):
    go = g_ref[...]
    xhat, rstd = _ln_stats(pre_ref[...])
    dlng_ref[...] += _fold8(go * xhat)
    dlnb_ref[...] += _fold8(go)
    dxh = go * lng_ref[...]
    dpre = rstd * (dxh - _row_mean(dxh) - xhat * _row_mean(dxh * xhat))
    dpre_ref[...] = dpre
    return dpre


def _in_proj(xb, win_ref, p_ref):
    cs = win_ref.shape[2]
    for j in range(N_DEV):
        p_ref[:, j * cs:(j + 1) * cs] = jnp.dot(xb, win_ref[j], preferred_element_type=F32)


def _zero_at_first_step(*refs):
    @pl.when(pl.program_id(0) == 0)
    def _():
        for r in refs:
            r[...] = jnp.zeros(r.shape, r.dtype)


def _f32(ref, sl):
    return ref[:, sl].astype(F32)


def _my_position():
    x, y, c = lax.axis_index("x"), lax.axis_index("y"), lax.axis_index("c")
    return (x, y, c), 4 * x + 2 * y + c


def _peer(k):
    (x, y, c), _ = _my_position()
    peer = (x ^ (k >> 2), y ^ ((k >> 1) & 1), c ^ (k & 1))
    return peer, 4 * peer[0] + 2 * peer[1] + peer[2]


class _Exchange:
    def __init__(self, gathers=(), scatters=()):
        self.args = list(gathers) + list(scatters)
        self.n_gather = len(gathers)
        self.out_shape = ([jax.ShapeDtypeStruct((N_DEV,) + a.shape, a.dtype) for a in gathers]
                          + [jax.ShapeDtypeStruct(a.shape, a.dtype) for a in scatters])
        n = len(self.args)
        self.scratch = [pltpu.SemaphoreType.DMA((n, N_DEV)), pltpu.SemaphoreType.DMA((n, N_DEV)),
                        pltpu.SemaphoreType.DMA((n,))]

    def _src(self, ins, w, pos):
        return ins[w] if w < self.n_gather else ins[w].at[pos]

    def _copies(self, ins, outs, sems, arrivals):
        send_sems, recv_sems, local_sems = sems
        _, me = _my_position()
        n = len(self.args)
        copies = []
        if not arrivals:
            copies = [pltpu.make_async_copy(self._src(ins, w, me), outs[w].at[me], local_sems.at[w]) for w in range(n)]
        for k in range(1, N_DEV):
            peer, peer_pos = _peer(k)
            for w in range(n):
                copies.append(pltpu.make_async_remote_copy(
                    src_ref=self._src(ins, w, me if arrivals else peer_pos),
                    dst_ref=outs[w].at[peer_pos if arrivals else me],
                    send_sem=send_sems.at[w, k], recv_sem=recv_sems.at[w, k], device_id=peer, device_id_type=_MESH))
        return copies

    def start(self, ins, outs, sems):
        for cp in self._copies(ins, outs, sems, False):
            cp.start()

    def mid(self, ins, outs, sems):
        pass

    def wait(self, ins, outs, sems):
        n = len(self.args)
        for cp in self._copies(ins, outs, sems, True):
            cp.wait_recv()
        own = self._copies(ins, outs, sems, False)
        for cp in own[n:]:
            cp.wait_send()
        for cp in own[:n]:
            cp.wait()


def _remote(src, dst, send_sem, recv_sem, peer):
    return pltpu.make_async_remote_copy(src_ref=src, dst_ref=dst, send_sem=send_sem, recv_sem=recv_sem,
                                        device_id=peer, device_id_type=_MESH)


class _Gather:
    def __init__(self, shards):
        self.args = list(shards)
        n = len(self.args)
        self.out_shape = [jax.ShapeDtypeStruct((N_DEV,) + a.shape, a.dtype) for a in shards]
        self.scratch = [pltpu.SemaphoreType.DMA((n, N_DEV)), pltpu.SemaphoreType.DMA((n, N_DEV)),
                        pltpu.SemaphoreType.DMA((n,))]

    def _own(self, ins, outs, sems):
        send, recv, loc = sems
        _, me = _my_position()
        local = [pltpu.make_async_copy(ins[w], outs[w].at[me], loc.at[w]) for w in range(len(ins))]
        first = [_remote(ins[w], outs[w].at[me], send.at[w, k], recv.at[w, k], _peer(k)[0])
                 for k in (1, 2, 4, 6) for w in range(len(ins))]
        return local, first

    def _passed_on(self, ins, outs, sems):
        send, recv, _ = sems
        sibling, _ = _peer(1)
        return [_remote(outs[w].at[_peer(k)[1]], outs[w].at[_peer(k)[1]], send.at[w, k + 1], recv.at[w, k + 1], sibling)
                for k in (2, 4, 6) for w in range(len(ins))]

    def _arrival(self, ins, outs, sems, k, w):
        send, recv, _ = sems
        peer, pos = _peer(k)
        return _remote(ins[w], outs[w].at[pos], send.at[w, k], recv.at[w, k], peer)

    def start(self, ins, outs, sems):
        local, first = self._own(ins, outs, sems)
        for cp in local + first:
            cp.start()

    def mid(self, ins, outs, sems):
        for k in (2, 4, 6):
            for w in range(len(ins)):
                self._arrival(ins, outs, sems, k, w).wait_recv()
        for cp in self._passed_on(ins, outs, sems):
            cp.start()

    def wait(self, ins, outs, sems):
        for k in (1, 3, 5, 7):
            for w in range(len(ins)):
                self._arrival(ins, outs, sems, k, w).wait_recv()
        local, first = self._own(ins, outs, sems)
        for cp in first + self._passed_on(ins, outs, sems):
            cp.wait_send()
        for cp in local:
            cp.wait()


class _PairExchange:
    def __init__(self, fulls):
        self.args = list(fulls)
        n = len(self.args)
        self.out_shape = [jax.ShapeDtypeStruct((N_DEV // 2,) + a.shape[1:], a.dtype) for a in fulls]
        self.scratch = [pltpu.SemaphoreType.DMA((n, N_DEV // 2)), pltpu.SemaphoreType.DMA((n, N_DEV // 2))]

    def _copies(self, ins, outs, sems):
        send, recv = sems
        (x, y, c), _ = _my_position()
        sibling, _ = _peer(1)
        return [_remote(ins[w].at[2 * q + 1 - c], outs[w].at[q], send.at[w, q], recv.at[w, q], sibling)
                for q in range(N_DEV // 2) for w in range(len(ins))]

    def start(self, ins, outs, sems):
        for cp in self._copies(ins, outs, sems):
            cp.start()

    def mid(self, ins, outs, sems):
        pass

    def wait(self, ins, outs, sems):
        for cp in self._copies(ins, outs, sems):
            cp.wait()


class _ChipScatter:
    def __init__(self, sums):
        self.args = list(sums)
        n = len(self.args)
        self.out_shape = [jax.ShapeDtypeStruct(a.shape, a.dtype) for a in sums]
        self.scratch = [pltpu.SemaphoreType.DMA((n, N_DEV // 2)), pltpu.SemaphoreType.DMA((n, N_DEV // 2)),
                        pltpu.SemaphoreType.DMA((n,))]

    def _copies(self, ins, outs, sems, arrivals):
        send, recv, loc = sems
        (x, y, c), _ = _my_position()
        my_chip = 2 * x + y
        copies = []
        if not arrivals:
            copies = [pltpu.make_async_copy(ins[w].at[my_chip], outs[w].at[my_chip], loc.at[w]) for w in range(len(ins))]
        for k in (1, 2, 3):
            peer = (x ^ (k >> 1), y ^ (k & 1), c)
            chip = my_chip ^ k
            for w in range(len(ins)):
                copies.append(_remote(ins[w].at[my_chip if arrivals else chip], outs[w].at[chip if arrivals else my_chip],
                                      send.at[w, k], recv.at[w, k], peer))
        return copies

    def start(self, ins, outs, sems):
        for cp in self._copies(ins, outs, sems, False):
            cp.start()

    def mid(self, ins, outs, sems):
        pass

    def wait(self, ins, outs, sems):
        n = len(ins)
        for cp in self._copies(ins, outs, sems, True):
            cp.wait_recv()
        own = self._copies(ins, outs, sems, False)
        for cp in own[n:]:
            cp.wait_send()
        for cp in own[:n]:
            cp.wait()


def _split(refs, sizes):
    out, off = [], 0
    for size in sizes:
        out.append(refs[off:off + size])
        off += size
    return out


def _call(body, name, grid, args, in_specs, out_shape, out_specs, scratch=(), exchanges=()):
    sem = ("arbitrary",) * len(grid)
    exchanges = [e for e in exchanges if e is not None]
    if not exchanges:
        outs = pl.pallas_call(body, name=name, grid=grid, in_specs=in_specs, out_specs=out_specs, out_shape=out_shape,
                              scratch_shapes=list(scratch), compiler_params=_params(sem))(*args)
        return outs, []
    n_in, n_out, n_scr = len(args), len(out_shape), len(scratch)
    ex_in = [len(e.args) for e in exchanges]
    ex_out = [len(e.out_shape) for e in exchanges]
    ex_scr = [len(e.scratch) for e in exchanges]
    steps = math.prod(grid)
    mid_step = min((3 * steps) // 4, steps - 1)

    def hosted(*refs):
        main_in, xin, main_out, xout, main_scr, xscr = _split(
            refs, [n_in, sum(ex_in), n_out, sum(ex_out), n_scr, sum(ex_scr)])
        parts = list(zip(exchanges, _split(xin, ex_in), _split(xout, ex_out), _split(xscr, ex_scr)))
        step = pl.program_id(0)
        for a in range(1, len(grid)):
            step = step * grid[a] + pl.program_id(a)

        @pl.when(step == 0)
        def _():
            for e, ins, outs, sems in parts:
                e.start(ins, outs, sems)

        body(*main_in, *main_out, *main_scr)

        @pl.when(step == mid_step)
        def _():
            for e, ins, outs, sems in parts:
                e.mid(ins, outs, sems)

        @pl.when(step == steps - 1)
        def _():
            for e, ins, outs, sems in parts:
                e.wait(ins, outs, sems)

    any_spec = pl.BlockSpec(memory_space=pl.ANY)
    outs = pl.pallas_call(
        hosted, name=name, grid=grid, in_specs=list(in_specs) + [any_spec] * sum(ex_in),
        out_specs=list(out_specs) + [any_spec] * sum(ex_out),
        out_shape=list(out_shape) + [s for e in exchanges for s in e.out_shape],
        scratch_shapes=list(scratch) + [s for e in exchanges for s in e.scratch],
        compiler_params=_params(sem))(*args, *[a for e in exchanges for a in e.args])
    return outs[:n_out], _split(outs[n_out:], ex_out)


def _exchange_only(exchanges, name):
    ex_in = [len(e.args) for e in exchanges]
    ex_out = [len(e.out_shape) for e in exchanges]
    ex_scr = [len(e.scratch) for e in exchanges]

    def body(*refs):
        xin, xout, xscr = _split(refs, [sum(ex_in), sum(ex_out), sum(ex_scr)])
        parts = list(zip(exchanges, _split(xin, ex_in), _split(xout, ex_out), _split(xscr, ex_scr)))
        for phase in ("start", "mid", "wait"):
            for e, ins, outs, sems in parts:
                getattr(e, phase)(ins, outs, sems)

    any_spec = pl.BlockSpec(memory_space=pl.ANY)
    outs = pl.pallas_call(
        body, name=name, in_specs=[any_spec] * sum(ex_in), out_specs=[any_spec] * sum(ex_out),
        out_shape=[s for e in exchanges for s in e.out_shape],
        scratch_shapes=[s for e in exchanges for s in e.scratch])(*[a for e in exchanges for a in e.args])
    return _split(outs, ex_out)


def _tile_specs(t, d, di):
    row = lambda i: (i, 0)
    col = lambda i: (0, i)
    return dict(
        xd=pl.BlockSpec((TM, d), row), xi=pl.BlockSpec((TM, di), row),
        td=pl.BlockSpec((d, TM), col), ti=pl.BlockSpec((di, TM), col),
        s_xd=jax.ShapeDtypeStruct((t, d), F32), s_xi=jax.ShapeDtypeStruct((t, di), BF16),
        s_td=jax.ShapeDtypeStruct((d, t), BF16), s_ti=jax.ShapeDtypeStruct((di, t), BF16))


def _fwd_a(x, win, wout, gain, bias, wc2, bs2, lng, lnb, exchanges=()):
    t, d = x.shape
    di = wout.shape[0]
    gd = di // A_GROUPS

    def body(x_ref, win_ref, wout_ref, gain_ref, bias_ref, wc_ref, bs_ref, lng_ref, lnb_ref,
             a1_ref, a2_ref, a3_ref, vn_ref, vh_ref, rg_ref, yt_ref, xt_ref, pre_ref, xn_ref,
             p_scr, vg_scr, y_scr):
        xv = x_ref[...]
        xt_ref[...] = xv.T.astype(BF16)
        _in_proj(xv.astype(BF16), win_ref, p_scr)
        s1 = jnp.zeros((TM, 1), F32)
        for c in range(di // CW):
            sl = slice(c * CW, (c + 1) * CW)
            pv = slice(di + c * CW, di + (c + 1) * CW)
            vg, dvg = _gelu_and_grad(p_scr[:, pv])
            vg_scr[:, sl] = vg
            p_scr[:, pv] = dvg
            s1 += jnp.sum(vg, axis=1, keepdims=True)
        mu = s1 * (1.0 / di)
        s2 = jnp.zeros((TM, 1), F32)
        for c in range(di // CW):
            dlt = vg_scr[:, c * CW:(c + 1) * CW] - mu
            s2 += jnp.sum(dlt * dlt, axis=1, keepdims=True)
        rstd = lax.rsqrt(s2 * (1.0 / di) + LN_EPS)
        for c in range(di // CW):
            sl = slice(c * CW, (c + 1) * CW)
            vh = (vg_scr[:, sl] - mu) * rstd
            vh_ref[:, sl] = vh.astype(BF16)
            vn_ref[:, sl] = (vh * gain_ref[:, sl] + bias_ref[:, sl]).astype(BF16)
            rg_ref[:, sl] = (p_scr[:, di + c * CW:di + (c + 1) * CW] * rstd).astype(BF16)
        for g in range(A_GROUPS):
            sl = slice(g * gd, (g + 1) * gd)
            sv = jnp.dot(wc_ref[g], vn_ref[:, sl], preferred_element_type=F32) + bs_ref[g]
            u, du = _gelu_and_grad(p_scr[:, sl])
            s, ds = _silu_and_grad(p_scr[:, 2 * di + g * gd:2 * di + (g + 1) * gd])
            us = u * s
            a1_ref[:, sl] = (s * du).astype(BF16)
            a2_ref[:, sl] = (u * ds).astype(BF16)
            a3_ref[:, sl] = us.astype(BF16)
            y = us * sv
            y_scr[:, sl] = y.astype(BF16)
            yt_ref[sl, :] = y.T.astype(BF16)
        out = jnp.dot(y_scr[...], wout_ref[...], preferred_element_type=F32)
        _post_norm(xv, out, lng_ref, lnb_ref, pre_ref, xn_ref)

    sp = _tile_specs(t, d, di)
    return _call(
        body, "fwd_a", (t // TM,), (x, win, wout, gain, bias, wc2, bs2, lng, lnb),
        in_specs=[sp["xd"]] + [_vmem()] * 8,
        out_specs=[sp["xi"]] * 6 + [sp["ti"], sp["td"], sp["xd"], sp["xd"]],
        out_shape=[sp["s_xi"]] * 6 + [sp["s_ti"], sp["s_td"], sp["s_xd"], sp["s_xd"]],
        scratch=[pltpu.VMEM((TM, 3 * di), F32), pltpu.VMEM((TM, di), F32), pltpu.VMEM((TM, di), BF16)],
        exchanges=exchanges)


def _bwd_a(g, pre, a1, a2, a3, vn, vh, rg, wout, gain, wc2, wc2t, bs2, lng, exchanges=()):
    t, d = g.shape
    di = wout.shape[0]
    gd = di // A_GROUPS

    def body(g_ref, pre_ref, a1_ref, a2_ref, a3_ref, vn_ref, vh_ref, rg_ref,
             wout_ref, gain_ref, wc_ref, wct_ref, bs_ref, lng_ref,
             dpre_ref, dp_ref, dlng_ref, dlnb_ref, dgain_ref, dbias_ref, dbs_ref, dwc_ref,
             dy_scr, dv_scr):
        _zero_at_first_step(dlng_ref, dlnb_ref, dgain_ref, dbias_ref, dbs_ref, dwc_ref)
        dpre = _post_norm_bwd(g_ref, pre_ref, lng_ref, dpre_ref, dlng_ref, dlnb_ref)
        dy_scr[...] = lax.dot_general(dpre.astype(BF16), wout_ref[...], _NT, preferred_element_type=F32)
        for grp in range(A_GROUPS):
            sl = slice(grp * gd, (grp + 1) * gd)
            vn_g = vn_ref[:, sl]
            sv = jnp.dot(wc_ref[grp], vn_g, preferred_element_type=F32) + bs_ref[grp]
            dy = dy_scr[:, sl]
            dys = dy * sv
            dp_ref[:, sl] = (dys * _f32(a1_ref, sl)).astype(BF16)
            dp_ref[:, 2 * di + grp * gd:2 * di + (grp + 1) * gd] = (dys * _f32(a2_ref, sl)).astype(BF16)
            dsv = dy * _f32(a3_ref, sl)
            dbs_ref[grp] += jnp.sum(dsv, axis=1, keepdims=True)
            dsvb = dsv.astype(BF16)
            dwc_ref[grp] += lax.dot_general(dsvb, vn_g, _NT, preferred_element_type=F32)
            dv_scr[:, sl] = jnp.dot(wct_ref[grp], dsvb, preferred_element_type=F32)
        r1 = jnp.zeros((TM, 1), F32)
        r2 = jnp.zeros((TM, 1), F32)
        for c in range(di // CW):
            sl = slice(c * CW, (c + 1) * CW)
            dv = dv_scr[:, sl]
            vhat = _f32(vh_ref, sl)
            dgain_ref[:, sl] += _fold8(dv * vhat)
            dbias_ref[:, sl] += _fold8(dv)
            dvh = dv * gain_ref[:, sl]
            dv_scr[:, sl] = dvh
            r1 += jnp.sum(dvh, axis=1, keepdims=True)
            r2 += jnp.sum(dvh * vhat, axis=1, keepdims=True)
        m1 = r1 * (1.0 / di)
        m2 = r2 * (1.0 / di)
        for c in range(di // CW):
            sl = slice(c * CW, (c + 1) * CW)
            dp_ref[:, di + c * CW:di + (c + 1) * CW] = (
                (dv_scr[:, sl] - m1 - _f32(vh_ref, sl) * m2) * _f32(rg_ref, sl)).astype(BF16)

    sp = _tile_specs(t, d, di)
    const2 = lambda i: (0, 0)
    const3 = lambda i: (0, 0, 0)
    return _call(
        body, "bwd_a", (t // TM,), (g, pre, a1, a2, a3, vn, vh, rg, wout, gain, wc2, wc2t, bs2, lng),
        in_specs=[sp["xd"], sp["xd"]] + [sp["xi"]] * 6 + [_vmem()] * 6,
        out_specs=[sp["xd"], pl.BlockSpec((TM, 3 * di), lambda i: (i, 0)),
                   pl.BlockSpec((8, d), const2), pl.BlockSpec((8, d), const2),
                   pl.BlockSpec((8, di), const2), pl.BlockSpec((8, di), const2),
                   pl.BlockSpec((A_GROUPS, TM, 1), const3), pl.BlockSpec((A_GROUPS, TM, TM), const3)],
        out_shape=[sp["s_xd"], jax.ShapeDtypeStruct((t, 3 * di), BF16),
                   jax.ShapeDtypeStruct((8, d), F32), jax.ShapeDtypeStruct((8, d), F32),
                   jax.ShapeDtypeStruct((8, di), F32), jax.ShapeDtypeStruct((8, di), F32),
                   jax.ShapeDtypeStruct((A_GROUPS, TM, 1), F32), jax.ShapeDtypeStruct((A_GROUPS, TM, TM), F32)],
        scratch=[pltpu.VMEM((TM, di), F32), pltpu.VMEM((TM, di), F32)],
        exchanges=exchanges)


def _inv_count(tile, window):
    pos = tile * TM + lax.broadcasted_iota(jnp.int32, (TM, 1), 0)
    return 1.0 / jnp.minimum(pos + 1, window).astype(F32)


def _window_sum(ext, window, down):
    rows = ext.shape[0]
    k = 1
    while k < window:
        ext = ext + pltpu.roll(ext, k if down else rows - k, 0)
        k *= 2
    return ext


def _fwd_b(x, win, wgrp, scale, wout, lng, lnb, exchanges=()):
    t, d = x.shape
    di = wout.shape[0]
    gd = di // len(POOL_WINDOWS)

    def body(x_ref, win_ref, wgrp_ref, scale_ref, wout_ref, lng_ref, lnb_ref,
             b1_ref, b2_ref, b3_ref, poolt_ref, yt_ref, xt_ref, pre_ref, xn_ref, p_scr, ext_scr, y_scr):
        i = pl.program_id(0)

        @pl.when(i == 0)
        def _():
            ext_scr[0:HALO, :] = jnp.zeros((HALO, di), F32)

        xv = x_ref[...]
        xt_ref[...] = xv.T.astype(BF16)
        _in_proj(xv.astype(BF16), win_ref, p_scr)
        ext_scr[HALO:, :] = p_scr[:, :di]
        for grp, window in enumerate(POOL_WINDOWS):
            sl = slice(grp * gd, (grp + 1) * gd)
            ext = ext_scr[:, sl]
            pooled = (_window_sum(ext, window, True)[HALO:] * _inv_count(i, window) - ext[HALO:]).astype(BF16)
            poolt_ref[sl, :] = pooled.astype(F32).T.astype(BF16)
            mixed = jnp.dot(pooled, wgrp_ref[grp], preferred_element_type=F32)
            s, ds = _silu_and_grad(p_scr[:, di + grp * gd:di + (grp + 1) * gd])
            sc = scale_ref[:, sl]
            ms = mixed * s
            b1_ref[:, sl] = (mixed * sc * ds).astype(BF16)
            b2_ref[:, sl] = ms.astype(BF16)
            b3_ref[:, sl] = (sc * s).astype(BF16)
            y = ms * sc
            y_scr[:, sl] = y.astype(BF16)
            yt_ref[sl, :] = y.T.astype(BF16)
        ext_scr[0:HALO, :] = ext_scr[TM:TM + HALO, :]
        out = jnp.dot(y_scr[...], wout_ref[...], preferred_element_type=F32)
        _post_norm(xv, out, lng_ref, lnb_ref, pre_ref, xn_ref)

    sp = _tile_specs(t, d, di)
    return _call(
        body, "fwd_b", (t // TM,), (x, win, wgrp, scale, wout, lng, lnb),
        in_specs=[sp["xd"]] + [_vmem()] * 6,
        out_specs=[sp["xi"]] * 3 + [sp["ti"], sp["ti"], sp["td"], sp["xd"], sp["xd"]],
        out_shape=[sp["s_xi"]] * 3 + [sp["s_ti"], sp["s_ti"], sp["s_td"], sp["s_xd"], sp["s_xd"]],
        scratch=[pltpu.VMEM((TM, 2 * di), F32), pltpu.VMEM((TM + HALO, di), F32), pltpu.VMEM((TM, di), BF16)],
        exchanges=exchanges)


def _bwd_b(g, pre, b1, b2, b3, wout, wgrp, lng, exchanges=()):
    t, d = g.shape
    di = wout.shape[0]
    gd = di // len(POOL_WINDOWS)
    nt = t // TM

    def body(g_ref, pre_ref, b1_ref, b2_ref, b3_ref, wout_ref, wgrp_ref, lng_ref,
             dpre_ref, dp_ref, dmix_ref, dlng_ref, dlnb_ref, dscale_ref, dy_scr, ext_scr):
        i = pl.program_id(0)
        tile = nt - 1 - i
        _zero_at_first_step(dlng_ref, dlnb_ref, dscale_ref)

        @pl.when(i == 0)
        def _():
            ext_scr[TM:, :] = jnp.zeros((HALO, di), F32)

        dpre = _post_norm_bwd(g_ref, pre_ref, lng_ref, dpre_ref, dlng_ref, dlnb_ref)
        dy_scr[...] = lax.dot_general(dpre.astype(BF16), wout_ref[...], _NT, preferred_element_type=F32)
        for grp, window in enumerate(POOL_WINDOWS):
            sl = slice(grp * gd, (grp + 1) * gd)
            dy = dy_scr[:, sl]
            dp_ref[:, di + grp * gd:di + (grp + 1) * gd] = (dy * _f32(b1_ref, sl)).astype(BF16)
            dscale_ref[:, sl] += _fold8(dy * _f32(b2_ref, sl))
            dmixed = (dy * _f32(b3_ref, sl)).astype(BF16)
            dmix_ref[:, sl] = dmixed
            dpooled = lax.dot_general(dmixed, wgrp_ref[grp], _NT, preferred_element_type=F32)
            ext_scr[0:TM, sl] = dpooled * _inv_count(tile, window)
            dv = _window_sum(ext_scr[:, sl], window, False)[0:TM] - dpooled
            dp_ref[:, sl] = dv.astype(BF16)
        ext_scr[TM:, :] = ext_scr[0:HALO, :]

    rrow = lambda i: (nt - 1 - i, 0)
    const2 = lambda i: (0, 0)
    xd, xi = pl.BlockSpec((TM, d), rrow), pl.BlockSpec((TM, di), rrow)
    return _call(
        body, "bwd_b", (nt,), (g, pre, b1, b2, b3, wout, wgrp, lng),
        in_specs=[xd, xd, xi, xi, xi, _vmem(), _vmem(), _vmem()],
        out_specs=[xd, pl.BlockSpec((TM, 2 * di), rrow), xi,
                   pl.BlockSpec((8, d), const2), pl.BlockSpec((8, d), const2), pl.BlockSpec((8, di), const2)],
        out_shape=[jax.ShapeDtypeStruct((t, d), F32), jax.ShapeDtypeStruct((t, 2 * di), BF16),
                   jax.ShapeDtypeStruct((t, di), BF16),
                   jax.ShapeDtypeStruct((8, d), F32), jax.ShapeDtypeStruct((8, d), F32),
                   jax.ShapeDtypeStruct((8, di), F32)],
        scratch=[pltpu.VMEM((TM, di), F32), pltpu.VMEM((TM + HALO, di), F32)],
        exchanges=exchanges)


def _fwd_c(x, win, convw, wout, lng, lnb, exchanges=()):
    t, d = x.shape
    di = wout.shape[0]

    def body(x_ref, win_ref, cw_ref, wout_ref, lng_ref, lnb_ref,
             c1_ref, c2_ref, c3_ref, cg_ref, hg_ref, yt_ref, xt_ref, pre_ref, xn_ref, p_scr, ext_scr, y_scr):
        i = pl.program_id(0)

        @pl.when(i == 0)
        def _():
            ext_scr[0:CHALO, :] = jnp.zeros((CHALO, di), F32)

        xv = x_ref[...]
        xt_ref[...] = xv.T.astype(BF16)
        _in_proj(xv.astype(BF16), win_ref, p_scr)
        for c in range(di // CW):
            sl = slice(c * CW, (c + 1) * CW)
            bb = p_scr[:, sl]
            cc = p_scr[:, di + c * CW:di + (c + 1) * CW]
            hh = p_scr[:, 2 * di + c * CW:2 * di + (c + 1) * CW]
            s, ds = _silu_and_grad(p_scr[:, 3 * di + c * CW:3 * di + (c + 1) * CW])
            ext_scr[CHALO:, sl] = cc * hh
            ext = ext_scr[:, sl]
            conv = (pltpu.roll(ext, 2, 0)[CHALO:] * cw_ref[0:1, sl] + pltpu.roll(ext, 1, 0)[CHALO:] * cw_ref[1:2, sl]
                    + ext[CHALO:] * cw_ref[2:3, sl])
            cs = conv * s
            c1_ref[:, sl] = cs.astype(BF16)
            c2_ref[:, sl] = (bb * conv * ds).astype(BF16)
            c3_ref[:, sl] = (bb * s).astype(BF16)
            cg_ref[:, sl] = cc.astype(BF16)
            hg_ref[:, sl] = hh.astype(BF16)
            y = bb * cs
            y_scr[:, sl] = y.astype(BF16)
            yt_ref[sl, :] = y.T.astype(BF16)
        ext_scr[0:CHALO, :] = ext_scr[TM:TM + CHALO, :]
        out = jnp.dot(y_scr[...], wout_ref[...], preferred_element_type=F32)
        _post_norm(xv, out, lng_ref, lnb_ref, pre_ref, xn_ref)

    sp = _tile_specs(t, d, di)
    return _call(
        body, "fwd_c", (t // TM,), (x, win, convw, wout, lng, lnb),
        in_specs=[sp["xd"]] + [_vmem()] * 5,
        out_specs=[sp["xi"]] * 5 + [sp["ti"], sp["td"], sp["xd"], sp["xd"]],
        out_shape=[sp["s_xi"]] * 5 + [sp["s_ti"], sp["s_td"], sp["s_xd"], sp["s_xd"]],
        scratch=[pltpu.VMEM((TM, 4 * di), F32), pltpu.VMEM((TM + CHALO, di), F32), pltpu.VMEM((TM, di), BF16)],
        exchanges=exchanges)


def _bwd_c(g, pre, c1, c2, c3, cg, hg, wout, convw, lng, exchanges=()):
    t, d = g.shape
    di = wout.shape[0]
    nt = t // TM

    def body(g_ref, pre_ref, c1_ref, c2_ref, c3_ref, cg_ref, hg_ref, wout_ref, cw_ref, lng_ref,
             dpre_ref, dp_ref, dlng_ref, dlnb_ref, dcw_ref, dy_scr, ext_scr):
        i = pl.program_id(0)
        _zero_at_first_step(dlng_ref, dlnb_ref, dcw_ref)

        @pl.when(i == 0)
        def _():
            ext_scr[TM:, :] = jnp.zeros((CHALO, di), F32)

        dpre = _post_norm_bwd(g_ref, pre_ref, lng_ref, dpre_ref, dlng_ref, dlnb_ref)
        dy_scr[...] = lax.dot_general(dpre.astype(BF16), wout_ref[...], _NT, preferred_element_type=F32)
        rows = TM + CHALO
        for c in range(di // CW):
            sl = slice(c * CW, (c + 1) * CW)
            dy = dy_scr[:, sl]
            cc = _f32(cg_ref, sl)
            hh = _f32(hg_ref, sl)
            dp_ref[:, sl] = (dy * _f32(c1_ref, sl)).astype(BF16)
            dp_ref[:, 3 * di + c * CW:3 * di + (c + 1) * CW] = (dy * _f32(c2_ref, sl)).astype(BF16)
            dconv = dy * _f32(c3_ref, sl)
            ext_scr[0:TM, sl] = dconv
            ext = ext_scr[:, sl]
            d1 = pltpu.roll(ext, rows - 1, 0)[0:TM]
            d2 = pltpu.roll(ext, rows - 2, 0)[0:TM]
            dq = dconv * cw_ref[2:3, sl] + d1 * cw_ref[1:2, sl] + d2 * cw_ref[0:1, sl]
            q = cc * hh
            dcw_ref[0, :, sl] += _fold8(q * d2)
            dcw_ref[1, :, sl] += _fold8(q * d1)
            dcw_ref[2, :, sl] += _fold8(q * dconv)
            dp_ref[:, di + c * CW:di + (c + 1) * CW] = (dq * hh).astype(BF16)
            dp_ref[:, 2 * di + c * CW:2 * di + (c + 1) * CW] = (dq * cc).astype(BF16)
        ext_scr[TM:, :] = ext_scr[0:CHALO, :]

    rrow = lambda i: (nt - 1 - i, 0)
    const2 = lambda i: (0, 0)
    xd, xi = pl.BlockSpec((TM, d), rrow), pl.BlockSpec((TM, di), rrow)
    return _call(
        body, "bwd_c", (nt,), (g, pre, c1, c2, c3, cg, hg, wout, convw, lng),
        in_specs=[xd, xd] + [xi] * 5 + [_vmem()] * 3,
        out_specs=[xd, pl.BlockSpec((TM, 4 * di), rrow),
                   pl.BlockSpec((8, d), const2), pl.BlockSpec((8, d), const2),
                   pl.BlockSpec((3, 8, di), lambda i: (0, 0, 0))],
        out_shape=[jax.ShapeDtypeStruct((t, d), F32), jax.ShapeDtypeStruct((t, 4 * di), BF16),
                   jax.ShapeDtypeStruct((8, d), F32), jax.ShapeDtypeStruct((8, d), F32),
                   jax.ShapeDtypeStruct((3, 8, di), F32)],
        scratch=[pltpu.VMEM((TM, di), F32), pltpu.VMEM((TM + CHALO, di), F32)],
        exchanges=exchanges)


def _dx(dp, dpre, win, exchanges=()):
    t, d = dpre.shape
    n = dp.shape[1]
    cs = win.shape[2]

    def body(dp_ref, dpre_ref, win_ref, dx_ref):
        acc = ALPHA * dpre_ref[...]
        for j in range(N_DEV):
            acc += lax.dot_general(dp_ref[:, j * cs:(j + 1) * cs], win_ref[j], _NT, preferred_element_type=F32)
        dx_ref[...] = acc

    row = lambda i: (i, 0)
    (dx,), ex = _call(
        body, "dx", (t // TM,), (dp, dpre, win),
        in_specs=[pl.BlockSpec((TM, n), row), pl.BlockSpec((TM, d), row), _vmem()],
        out_specs=[pl.BlockSpec((TM, d), row)],
        out_shape=[jax.ShapeDtypeStruct((t, d), F32)],
        exchanges=exchanges)
    return dx, ex


def _wgrad(at, b, nb, per_block_rows, name, exchanges=()):
    m_all, t = at.shape
    tn = b.shape[1] // nb
    m = m_all // nb if per_block_rows else m_all
    tk = min(WGRAD_TK, t)
    nk = t // tk

    def body(at_ref, b_ref, out_ref, acc):
        k = pl.program_id(1)

        @pl.when(k == 0)
        def _():
            acc[...] = jnp.zeros(acc.shape, F32)

        acc[...] += jnp.dot(at_ref[...], b_ref[...].astype(BF16), preferred_element_type=F32)

        @pl.when(k == nk - 1)
        def _():
            out_ref[0] = acc[...].astype(BF16)

    at_map = (lambda j, k: (j, k)) if per_block_rows else (lambda j, k: (0, k))
    (out,), ex = _call(
        body, name, (nb, nk), (at, b),
        in_specs=[pl.BlockSpec((m, tk), at_map), pl.BlockSpec((tk, tn), lambda j, k: (k, j))],
        out_specs=[pl.BlockSpec((1, m, tn), lambda j, k: (j, 0, 0))],
        out_shape=[jax.ShapeDtypeStruct((nb, m, tn), BF16)],
        scratch=[pltpu.VMEM((m, tn), F32)],
        exchanges=exchanges)
    return out, ex


def _loss_head(y, target):
    t, d = y.shape

    def body(y_ref, t_ref, dy_ref, sq_ref):
        _zero_at_first_step(sq_ref)
        diff = y_ref[...] - t_ref[...]
        dy_ref[...] = diff * (1.0 / d)
        sq_ref[...] += _fold8(diff * diff)

    row = lambda i: (i, 0)
    return pl.pallas_call(
        body, name="loss_head", grid=(t // TM,),
        in_specs=[pl.BlockSpec((TM, d), row), pl.BlockSpec((TM, d), row)],
        out_specs=[pl.BlockSpec((TM, d), row), pl.BlockSpec((8, d), lambda i: (0, 0))],
        out_shape=[jax.ShapeDtypeStruct((t, d), F32), jax.ShapeDtypeStruct((8, d), F32)],
        compiler_params=_params(("arbitrary",)),
    )(y, target)


ADAMW_BLOCK_BYTES = 6 * 1024 * 1024


def _sum_parts(parts_ref):
    g = parts_ref[0].astype(F32)
    for s in range(1, parts_ref.shape[0]):
        g = g + parts_ref[s].astype(F32)
    return g


def _row_tile(rows, bytes_per_row):
    if rows * bytes_per_row <= ADAMW_BLOCK_BYTES:
        return rows
    best = 8
    for cand in range(8, rows, 8):
        if rows % cand == 0 and cand * bytes_per_row <= ADAMW_BLOCK_BYTES:
            best = cand
    return best


def _pair_sum(full, theirs, core, name):
    shape = theirs.shape
    r, c = math.prod(shape[1:-1]), shape[-1]
    tr = _row_tile(r, c * 3 * full.dtype.itemsize)
    half = N_DEV // 2

    def body(core_ref, a_ref, b_ref, out_ref):
        out_ref[...] = (a_ref[...].astype(F32) + b_ref[...].astype(F32)).astype(out_ref.dtype)

    blk = pl.BlockSpec((None, tr, c), lambda q, i, core_ref: (q, i, 0))
    grid_spec = pltpu.PrefetchScalarGridSpec(
        num_scalar_prefetch=1, grid=(half, r // tr),
        in_specs=[pl.BlockSpec((None, None, tr, c), lambda q, i, core_ref: (q, core_ref[0], i, 0)), blk],
        out_specs=blk)
    return pl.pallas_call(
        body, name=name, grid_spec=grid_spec, out_shape=jax.ShapeDtypeStruct((half, r, c), full.dtype),
        compiler_params=_params(("arbitrary", "arbitrary")),
    )(core, full.reshape(half, 2, r, c), theirs.reshape(half, r, c)).reshape(shape)


def _adamw(parts, w, m, v, name):
    s, r, c = parts.shape
    tr = _row_tile(r, c * (s * parts.dtype.itemsize + 7 * 4))
    bc1 = 1.0 - ADAM_B1 ** ADAM_STEP
    bc2 = 1.0 - ADAM_B2 ** ADAM_STEP

    def body(parts_ref, w_ref, m_ref, v_ref, g_ref, d_ref, nm_ref, nv_ref):
        g = _sum_parts(parts_ref)
        g_ref[...] = g
        nm = ADAM_B1 * m_ref[...] + (1.0 - ADAM_B1) * g
        nv = ADAM_B2 * v_ref[...] + (1.0 - ADAM_B2) * (g * g)
        nm_ref[...] = nm
        nv_ref[...] = nv
        d_ref[...] = -ADAM_LR * ((nm / bc1) / (jnp.sqrt(nv / bc2) + ADAM_EPS) + ADAM_WD * w_ref[...])

    blk = pl.BlockSpec((tr, c), lambda i: (i, 0))
    return pl.pallas_call(
        body, name=name, grid=(r // tr,),
        in_specs=[pl.BlockSpec((s, tr, c), lambda i: (0, i, 0)), blk, blk, blk],
        out_specs=[blk, blk, blk, blk],
        out_shape=[jax.ShapeDtypeStruct((r, c), F32)] * 4,
        compiler_params=_params(("arbitrary",)),
    )(parts, w, m, v)


_LANES = 128


def _pack(arrays):
    flat = jnp.concatenate([a.reshape(-1) for a in arrays])
    pad = (-flat.shape[0]) % (8 * _LANES)
    return jnp.pad(flat, (0, pad)).reshape(-1, _LANES)


def _unpack(packed, shapes):
    flat = packed.reshape(-1)
    out, off = [], 0
    for shp in shapes:
        size = math.prod(shp)
        out.append(flat[off:off + size].reshape(shp))
        off += size
    return out


def _spatial_weights(w_s, b_s):
    reps = TM // CHUNK
    tril = jnp.tril(jnp.ones((CHUNK, CHUNK), F32))
    wc = w_s * tril
    eye = jnp.eye(reps, dtype=F32)
    wc2 = jnp.einsum("ab,gts->gatbs", eye, wc).reshape(A_GROUPS, TM, TM)
    bs2 = jnp.tile(b_s, (1, reps)).reshape(A_GROUPS, TM, 1)
    return wc2.astype(BF16), jnp.swapaxes(wc2, 1, 2).astype(BF16), bs2


def _spatial_weight_grad(dwc2, dbs2):
    reps = TM // CHUNK
    tril = jnp.tril(jnp.ones((CHUNK, CHUNK), F32))
    blocks = dwc2.reshape(A_GROUPS, reps, CHUNK, reps, CHUNK)
    dws = sum(blocks[:, a, :, a, :] for a in range(reps)) * tril
    dbs = dbs2.reshape(A_GROUPS, reps, CHUNK).sum(axis=1)
    return dws, dbs


def _row2(a):
    return a.reshape(1, -1)


def kernel(x, a0_w_in, a0_v_gain, a0_v_bias, a0_w_s, a0_b_s, a0_w_out, ln0_gain, ln0_bias, b1_w_in, b1_w_grp, b1_scale, b1_w_out, ln1_gain, ln1_bias, c2_w_in, c2_conv_w, c2_w_out, ln2_gain, ln2_bias, a3_w_in, a3_v_gain, a3_v_bias, a3_w_s, a3_b_s, a3_w_out, ln3_gain, ln3_bias, loss_target, m_a0_w_in, m_a0_v_gain, m_a0_v_bias, m_a0_w_s, m_a0_b_s, m_a0_w_out, m_ln0_gain, m_ln0_bias, m_b1_w_in, m_b1_w_grp, m_b1_scale, m_b1_w_out, m_ln1_gain, m_ln1_bias, m_c2_w_in, m_c2_conv_w, m_c2_w_out, m_ln2_gain, m_ln2_bias, m_a3_w_in, m_a3_v_gain, m_a3_v_bias, m_a3_w_s, m_a3_b_s, m_a3_w_out, m_ln3_gain, m_ln3_bias, v_a0_w_in, v_a0_v_gain, v_a0_v_bias, v_a0_w_s, v_a0_b_s, v_a0_w_out, v_ln0_gain, v_ln0_bias, v_b1_w_in, v_b1_w_grp, v_b1_scale, v_b1_w_out, v_ln1_gain, v_ln1_bias, v_c2_w_in, v_c2_conv_w, v_c2_w_out, v_ln2_gain, v_ln2_bias, v_a3_w_in, v_a3_v_gain, v_a3_v_bias, v_a3_w_s, v_a3_b_s, v_a3_w_out, v_ln3_gain, v_ln3_bias):
    names = ["a0_w_in", "a0_v_gain", "a0_v_bias", "a0_w_s", "a0_b_s", "a0_w_out", "ln0_gain", "ln0_bias",
             "b1_w_in", "b1_w_grp", "b1_scale", "b1_w_out", "ln1_gain", "ln1_bias",
             "c2_w_in", "c2_conv_w", "c2_w_out", "ln2_gain", "ln2_bias",
             "a3_w_in", "a3_v_gain", "a3_v_bias", "a3_w_s", "a3_b_s", "a3_w_out", "ln3_gain", "ln3_bias"]
    env = dict(locals())
    w = {nm: env[nm] for nm in names}
    mom = {nm: env["m_" + nm] for nm in names}
    var = {nm: env["v_" + nm] for nm in names}

    x0 = x[0]
    target = loss_target[0]
    d_model = x0.shape[1]
    di = N_DEV * a0_w_out.shape[0]
    n_grp = len(POOL_WINDOWS)
    gd_b = di // n_grp

    layers = ("a0", "b1", "c2", "a3")
    big_of = {"a0": ["a0_w_in", "a0_w_out"], "b1": ["b1_w_in", "b1_w_grp", "b1_w_out"],
              "c2": ["c2_w_in", "c2_w_out"], "a3": ["a3_w_in", "a3_w_out"]}
    big = [nm for p in layers for nm in big_of[p]]
    bucket_of = {"a0": ["a0_v_gain", "a0_v_bias", "a0_w_s", "a0_b_s", "ln0_gain", "ln0_bias"],
                 "b1": ["b1_scale", "ln1_gain", "ln1_bias"], "c2": ["ln2_gain", "ln2_bias"],
                 "a3": ["a3_v_gain", "a3_v_bias", "a3_w_s", "a3_b_s", "ln3_gain", "ln3_bias"]}
    conv_shape = c2_conv_w.shape
    spatial = {p: _spatial_weights(w[p + "_w_s"], w[p + "_b_s"]) for p in ("a0", "a3")}

    def weight_gather(p):
        return _Gather([w[nm].astype(BF16) for nm in big_of[p]])

    (first,) = _exchange_only([_Gather([w[nm].astype(BF16) for nm in big_of["a0"]] + [_pack([c2_conv_w])])],
                              "gather_first")
    gathered = dict(zip(big_of["a0"], first))
    conv_all = jnp.stack([_unpack(first[-1][j], [conv_shape])[0] for j in range(N_DEV)], axis=1)
    conv_full = conv_all.reshape(conv_shape[0], di)
    w_in = lambda p: gathered[p + "_w_in"]
    w_out = lambda p: gathered[p + "_w_out"].reshape(di, d_model)
    saved = {}
    h = x0
    for i, p in enumerate(layers):
        lng, lnb = _row2(w[f"ln{i}_gain"]), _row2(w[f"ln{i}_bias"])
        nxt = layers[i + 1] if i + 1 < len(layers) else None
        ex = [weight_gather(nxt)] if nxt else []
        if p[0] == "a":
            wc2, _, bs2 = spatial[p]
            outs, got = _fwd_a(h, w_in(p), w_out(p), _row2(w[p + "_v_gain"]), _row2(w[p + "_v_bias"]),
                               wc2, bs2, lng, lnb, exchanges=ex)
        elif p[0] == "b":
            wgrp = jnp.swapaxes(gathered["b1_w_grp"], 0, 1).reshape(n_grp, gd_b, gd_b)
            outs, got = _fwd_b(h, w_in(p), wgrp, _row2(w[p + "_scale"]), w_out(p), lng, lnb, exchanges=ex)
        else:
            outs, got = _fwd_c(h, w_in(p), conv_full, w_out(p), lng, lnb, exchanges=ex)
        saved[p], h = outs[:-1], outs[-1]
        if nxt:
            gathered.update(zip(big_of[nxt], got[0]))

    gcur, sq = _loss_head(h, target)
    loss = lax.psum(jnp.sum(sq) * (0.5 / d_model), ("x", "y", "c"))

    part, full, landed, small_all = {}, {}, {}, {}

    def bucket_gather(bucket):
        return _Gather([_pack([part[nm] for nm in bucket_of[bucket]])])

    core = lax.axis_index("c").astype(jnp.int32).reshape(1)

    def chip_sums(names, got):
        return [_pair_sum(full[nm], theirs, core, "pair_sum_" + nm) for nm, theirs in zip(names, got)]

    pending = None
    for i, p in reversed(list(enumerate(layers))):
        lng = _row2(w[f"ln{i}_gain"])
        ex = []
        if pending:
            ex = [_PairExchange([full[nm] for nm in big_of[pending]]), bucket_gather(pending)]
            if pending == "c2":
                ex.append(_Exchange(scatters=[full["c2_conv_w"]]))
        *factors, yt, xt, pre = saved[p]
        if p[0] == "a":
            wc2, wc2t, bs2 = spatial[p]
            (dpre, dp, dlng, dlnb, dgain, dbias, dbs2, dwc2), got = _bwd_a(
                gcur, pre, *factors, w_out(p), _row2(w[p + "_v_gain"]), wc2, wc2t, bs2, lng, exchanges=ex)
            part[p + "_v_gain"], part[p + "_v_bias"] = dgain.sum(axis=0), dbias.sum(axis=0)
            part[p + "_w_s"], part[p + "_b_s"] = _spatial_weight_grad(dwc2, dbs2)
        elif p[0] == "b":
            b1f, b2f, b3f, poolt = factors
            (dpre, dp, dmixed, dlng, dlnb, dscale), got = _bwd_b(
                gcur, pre, b1f, b2f, b3f, w_out(p), wgrp, lng, exchanges=ex)
            part[p + "_scale"] = dscale.sum(axis=0)
            dwg, _ = _wgrad(poolt, dmixed, n_grp, True, "wgrad_grp")
            full[p + "_w_grp"] = jnp.swapaxes(dwg.reshape(n_grp, N_DEV, gd_b // N_DEV, gd_b), 0, 1)
        else:
            (dpre, dp, dlng, dlnb, dcw), got = _bwd_c(gcur, pre, *factors, w_out(p), conv_full, lng, exchanges=ex)
            dconv = dcw.sum(axis=1).reshape(conv_shape[0], N_DEV, conv_shape[1])
            full["c2_conv_w"] = jnp.stack([_pack([dconv[:, j]]) for j in range(N_DEV)])
        part[f"ln{i}_gain"], part[f"ln{i}_bias"] = dlng.sum(axis=0), dlnb.sum(axis=0)
        ex = []
        if pending:
            small_all[pending] = got[1][0]
            if pending == "c2":
                small_all["conv"] = got[2][0]
            ex = [_ChipScatter(chip_sums(big_of[pending], got[0]))]
        full[p + "_w_in"], got = _wgrad(xt, dp, N_DEV, False, "wgrad_in_" + p, exchanges=ex)
        if pending:
            landed.update(zip(big_of[pending], got[0]))
        if i > 0:
            dwo, _ = _wgrad(yt, dpre, 1, False, "wgrad_out_" + p)
            full[p + "_w_out"] = dwo.reshape(N_DEV, di // N_DEV, d_model)
            gcur, _ = _dx(dp, dpre, w_in(p))
            pending = p
        else:
            dwo, got = _wgrad(yt, dpre, 1, False, "wgrad_out_" + p, exchanges=[_PairExchange([full[p + "_w_in"]])])
            full[p + "_w_out"] = dwo.reshape(N_DEV, di // N_DEV, d_model)
            gcur, got = _dx(dp, dpre, w_in(p), exchanges=[
                _ChipScatter(chip_sums([p + "_w_in"], got[0])), _Exchange(scatters=[full[p + "_w_out"]]),
                bucket_gather(p)])
            landed[p + "_w_in"], landed[p + "_w_out"], small_all[p] = got[0][0], got[1][0], got[2][0]
    grad_x = gcur[None]

    grads, deltas, new_m, new_v = {}, {}, {}, {}
    for nm in big:
        shp = w[nm].shape
        r2 = (math.prod(shp[:-1]), shp[-1])
        outs = _adamw(landed[nm].reshape((-1,) + r2), w[nm].reshape(r2), mom[nm].reshape(r2), var[nm].reshape(r2),
                      "adamw_" + nm)
        grads[nm], deltas[nm], new_m[nm], new_v[nm] = (o.reshape(shp) for o in outs)
    buckets = dict(bucket_of, conv=["c2_conv_w"])
    for key, members in buckets.items():
        shapes = [w[nm].shape for nm in members]
        outs = _adamw(small_all[key], _pack([w[nm] for nm in members]), _pack([mom[nm] for nm in members]),
                      _pack([var[nm] for nm in members]), "adamw_small_" + key)
        for tgt, o in zip((grads, deltas, new_m, new_v), outs):
            tgt.update(zip(members, _unpack(o, shapes)))

    return (loss, grad_x, *[grads[nm] for nm in names], *[deltas[nm] for nm in names],
            *[new_m[nm] for nm in names], *[new_v[nm] for nm in names])
```

```python
import functools
import math

import jax
import jax.numpy as jnp
from jax import lax
from jax.experimental import pallas as pl
from jax.experimental.pallas import tpu as pltpu

F32 = jnp.float32
BF16 = jnp.bfloat16

N_DEV = 8
DEPTH = 4
CHUNK = 128
A_GROUPS = 8
POOL_WINDOWS = (2, 4, 8, 16)
LN_EPS = 1e-5
ALPHA = (2.0 * DEPTH) ** 0.25
ADAM_LR = 0.001
ADAM_B1 = 0.9
ADAM_B2 = 0.999
ADAM_EPS = 1e-08
ADAM_WD = 0.01
ADAM_STEP = 10

TM = 256
HALO = 16
CHALO = 8
CW = 512
WGRAD_TK = 2048
VMEM_LIMIT_BYTES = 58 * 1024 * 1024

_NT = (((1,), (1,)), ((), ()))
_SQRT_2_OVER_PI = math.sqrt(2.0 / math.pi)
_MESH = pl.DeviceIdType.MESH


def _vmem():
    return pl.BlockSpec(memory_space=pltpu.VMEM)


def _params(sem=None):
    return pltpu.CompilerParams(dimension_semantics=sem, vmem_limit_bytes=VMEM_LIMIT_BYTES)


def _gelu(x):
    t = jnp.tanh(_SQRT_2_OVER_PI * (x + 0.044715 * (x * x * x)))
    return x * (0.5 * (1.0 + t))


def _gelu_and_grad(x):
    x2 = x * x
    t = jnp.tanh(_SQRT_2_OVER_PI * (x + 0.044715 * (x * x2)))
    cdf = 0.5 * (1.0 + t)
    grad = cdf + 0.5 * x * (1.0 - t * t) * (_SQRT_2_OVER_PI * (1.0 + 3.0 * 0.044715 * x2))
    return x * cdf, grad


def _silu_and_grad(z):
    sg = 1.0 / (1.0 + jnp.exp(-z))
    return z * sg, sg * (1.0 + z * (1.0 - sg))


def _fold8(a):
    return a.reshape(a.shape[0] // 8, 8, a.shape[1]).sum(axis=0)


def _row_mean(a):
    return jnp.mean(a, axis=-1, keepdims=True)


def _ln_stats(x):
    mu = _row_mean(x)
    xc = x - mu
    rstd = lax.rsqrt(_row_mean(xc * xc) + LN_EPS)
    return xc * rstd, rstd


def _post_norm(x, out, lng_ref, lnb_ref, pre_ref, xn_ref):
    pre = ALPHA * x + out
    pre_ref[...] = pre
    xhat, _ = _ln_stats(pre)
    xn_ref[...] = xhat * lng_ref[...] + lnb_ref[...]


def _post_norm_bwd(g_ref, pre_ref, lng_ref, dpre_ref, dlng_ref, dlnb_ref):
    go = g_ref[...]
    xhat, rstd = _ln_stats(pre_ref[...])
    dlng_ref[...] += _fold8(go * xhat)
    dlnb_ref[...] += _fold8(go)
    dxh = go * lng_ref[...]
    dpre = rstd * (dxh - _row_mean(dxh) - xhat * _row_mean(dxh * xhat))
    dpre_ref[...] = dpre
    return dpre


def _in_proj(xb, win_ref, p_ref):
    cs = win_ref.shape[2]
    for j in range(N_DEV):
        p_ref[:, j * cs:(j + 1) * cs] = jnp.dot(xb, win_ref[j], preferred_element_type=F32)


def _zero_at_first_step(*refs):
    @pl.when(pl.program_id(0) == 0)
    def _():
        for r in refs:
            r[...] = jnp.zeros(r.shape, r.dtype)


def _f32(ref, sl):
    return ref[:, sl].astype(F32)


def _my_position():
    x, y, c = lax.axis_index("x"), lax.axis_index("y"), lax.axis_index("c")
    return (x, y, c), 4 * x + 2 * y + c


def _peer(k):
    (x, y, c), _ = _my_position()
    peer = (x ^ (k >> 2), y ^ ((k >> 1) & 1), c ^ (k & 1))
    return peer, 4 * peer[0] + 2 * peer[1] + peer[2]


class _Exchange:
    def __init__(self, gathers=(), scatters=()):
        self.args = list(gathers) + list(scatters)
        self.n_gather = len(gathers)
        self.out_shape = ([jax.ShapeDtypeStruct((N_DEV,) + a.shape, a.dtype) for a in gathers]
                          + [jax.ShapeDtypeStruct(a.shape, a.dtype) for a in scatters])
        n = len(self.args)
        self.scratch = [pltpu.SemaphoreType.DMA((n, N_DEV)), pltpu.SemaphoreType.DMA((n, N_DEV)),
                        pltpu.SemaphoreType.DMA((n,))]

    def _src(self, ins, w, pos):
        return ins[w] if w < self.n_gather else ins[w].at[pos]

    def _copies(self, ins, outs, sems, arrivals):
        send_sems, recv_sems, local_sems = sems
        _, me = _my_position()
        n = len(self.args)
        copies = []
        if not arrivals:
            copies = [pltpu.make_async_copy(self._src(ins, w, me), outs[w].at[me], local_sems.at[w]) for w in range(n)]
        for k in range(1, N_DEV):
            peer, peer_pos = _peer(k)
            for w in range(n):
                copies.append(pltpu.make_async_remote_copy(
                    src_ref=self._src(ins, w, me if arrivals else peer_pos),
                    dst_ref=outs[w].at[peer_pos if arrivals else me],
                    send_sem=send_sems.at[w, k], recv_sem=recv_sems.at[w, k], device_id=peer, device_id_type=_MESH))
        return copies

    def start(self, ins, outs, sems):
        for cp in self._copies(ins, outs, sems, False):
            cp.start()

    def mid(self, ins, outs, sems):
        pass

    def wait(self, ins, outs, sems):
        n = len(self.args)
        for cp in self._copies(ins, outs, sems, True):
            cp.wait_recv()
        own = self._copies(ins, outs, sems, False)
        for cp in own[n:]:
            cp.wait_send()
        for cp in own[:n]:
            cp.wait()


def _remote(src, dst, send_sem, recv_sem, peer):
    return pltpu.make_async_remote_copy(src_ref=src, dst_ref=dst, send_sem=send_sem, recv_sem=recv_sem,
                                        device_id=peer, device_id_type=_MESH)


class _Gather:
    def __init__(self, shards):
        self.args = list(shards)
        n = len(self.args)
        self.out_shape = [jax.ShapeDtypeStruct((N_DEV,) + a.shape, a.dtype) for a in shards]
        self.scratch = [pltpu.SemaphoreType.DMA((n, N_DEV)), pltpu.SemaphoreType.DMA((n, N_DEV)),
                        pltpu.SemaphoreType.DMA((n,))]

    def _own(self, ins, outs, sems):
        send, recv, loc = sems
        _, me = _my_position()
        local = [pltpu.make_async_copy(ins[w], outs[w].at[me], loc.at[w]) for w in range(len(ins))]
        first = [_remote(ins[w], outs[w].at[me], send.at[w, k], recv.at[w, k], _peer(k)[0])
                 for k in (1, 2, 4, 6) for w in range(len(ins))]
        return local, first

    def _passed_on(self, ins, outs, sems):
        send, recv, _ = sems
        sibling, _ = _peer(1)
        return [_remote(outs[w].at[_peer(k)[1]], outs[w].at[_peer(k)[1]], send.at[w, k + 1], recv.at[w, k + 1], sibling)
                for k in (2, 4, 6) for w in range(len(ins))]

    def _arrival(self, ins, outs, sems, k, w):
        send, recv, _ = sems
        peer, pos = _peer(k)
        return _remote(ins[w], outs[w].at[pos], send.at[w, k], recv.at[w, k], peer)

    def start(self, ins, outs, sems):
        local, first = self._own(ins, outs, sems)
        for cp in local + first:
            cp.start()

    def mid(self, ins, outs, sems):
        for k in (2, 4, 6):
            for w in range(len(ins)):
                self._arrival(ins, outs, sems, k, w).wait_recv()
        for cp in self._passed_on(ins, outs, sems):
            cp.start()

    def wait(self, ins, outs, sems):
        for k in (1, 3, 5, 7):
            for w in range(len(ins)):
                self._arrival(ins, outs, sems, k, w).wait_recv()
        local, first = self._own(ins, outs, sems)
        for cp in first + self._passed_on(ins, outs, sems):
            cp.wait_send()
        for cp in local:
            cp.wait()


class _PairExchange:
    def __init__(self, fulls):
        self.args = list(fulls)
        n = len(self.args)
        self.out_shape = [jax.ShapeDtypeStruct((N_DEV // 2,) + a.shape[1:], a.dtype) for a in fulls]
        self.scratch = [pltpu.SemaphoreType.DMA((n, N_DEV // 2)), pltpu.SemaphoreType.DMA((n, N_DEV // 2))]

    def _copies(self, ins, outs, sems):
        send, recv = sems
        (x, y, c), _ = _my_position()
        sibling, _ = _peer(1)
        return [_remote(ins[w].at[2 * q + 1 - c], outs[w].at[q], send.at[w, q], recv.at[w, q], sibling)
                for q in range(N_DEV // 2) for w in range(len(ins))]

    def start(self, ins, outs, sems):
        for cp in self._copies(ins, outs, sems):
            cp.start()

    def mid(self, ins, outs, sems):
        pass

    def wait(self, ins, outs, sems):
        for cp in self._copies(ins, outs, sems):
            cp.wait()


class _ChipScatter:
    def __init__(self, sums):
        self.args = list(sums)
        n = len(self.args)
        self.out_shape = [jax.ShapeDtypeStruct(a.shape, a.dtype) for a in sums]
        self.scratch = [pltpu.SemaphoreType.DMA((n, N_DEV // 2)), pltpu.SemaphoreType.DMA((n, N_DEV // 2)),
                        pltpu.SemaphoreType.DMA((n,))]

    def _copies(self, ins, outs, sems, arrivals):
        send, recv, loc = sems
        (x, y, c), _ = _my_position()
        my_chip = 2 * x + y
        copies = []
        if not arrivals:
            copies = [pltpu.make_async_copy(ins[w].at[my_chip], outs[w].at[my_chip], loc.at[w]) for w in range(len(ins))]
        for k in (1, 2, 3):
            peer = (x ^ (k >> 1), y ^ (k & 1), c)
            chip = my_chip ^ k
            for w in range(len(ins)):
                copies.append(_remote(ins[w].at[my_chip if arrivals else chip], outs[w].at[chip if arrivals else my_chip],
                                      send.at[w, k], recv.at[w, k], peer))
        return copies

    def start(self, ins, outs, sems):
        for cp in self._copies(ins, outs, sems, False):
            cp.start()

    def mid(self, ins, outs, sems):
        pass

    def wait(self, ins, outs, sems):
        n = len(ins)
        for cp in self._copies(ins, outs, sems, True):
            cp.wait_recv()
        own = self._copies(ins, outs, sems, False)
        for cp in own[n:]:
            cp.wait_send()
        for cp in own[:n]:
            cp.wait()


def _split(refs, sizes):
    out, off = [], 0
    for size in sizes:
        out.append(refs[off:off + size])
        off += size
    return out


def _call(body, name, grid, args, in_specs, out_shape, out_specs, scratch=(), exchanges=()):
    sem = ("arbitrary",) * len(grid)
    exchanges = [e for e in exchanges if e is not None]
    if not exchanges:
        outs = pl.pallas_call(body, name=name, grid=grid, in_specs=in_specs, out_specs=out_specs, out_shape=out_shape,
                              scratch_shapes=list(scratch), compiler_params=_params(sem))(*args)
        return outs, []
    n_in, n_out, n_scr = len(args), len(out_shape), len(scratch)
    ex_in = [len(e.args) for e in exchanges]
    ex_out = [len(e.out_shape) for e in exchanges]
    ex_scr = [len(e.scratch) for e in exchanges]
    steps = math.prod(grid)
    mid_step = min((3 * steps) // 4, steps - 1)

    def hosted(*refs):
        main_in, xin, main_out, xout, main_scr, xscr = _split(
            refs, [n_in, sum(ex_in), n_out, sum(ex_out), n_scr, sum(ex_scr)])
        parts = list(zip(exchanges, _split(xin, ex_in), _split(xout, ex_out), _split(xscr, ex_scr)))
        step = pl.program_id(0)
        for a in range(1, len(grid)):
            step = step * grid[a] + pl.program_id(a)

        @pl.when(step == 0)
        def _():
            for e, ins, outs, sems in parts:
                e.start(ins, outs, sems)

        body(*main_in, *main_out, *main_scr)

        @pl.when(step == mid_step)
        def _():
            for e, ins, outs, sems in parts:
                e.mid(ins, outs, sems)

        @pl.when(step == steps - 1)
        def _():
            for e, ins, outs, sems in parts:
                e.wait(ins, outs, sems)

    any_spec = pl.BlockSpec(memory_space=pl.ANY)
    outs = pl.pallas_call(
        hosted, name=name, grid=grid, in_specs=list(in_specs) + [any_spec] * sum(ex_in),
        out_specs=list(out_specs) + [any_spec] * sum(ex_out),
        out_shape=list(out_shape) + [s for e in exchanges for s in e.out_shape],
        scratch_shapes=list(scratch) + [s for e in exchanges for s in e.scratch],
        compiler_params=_params(sem))(*args, *[a for e in exchanges for a in e.args])
    return outs[:n_out], _split(outs[n_out:], ex_out)


def _exchange_only(exchanges, name):
    ex_in = [len(e.args) for e in exchanges]
    ex_out = [len(e.out_shape) for e in exchanges]
    ex_scr = [len(e.scratch) for e in exchanges]

    def body(*refs):
        xin, xout, xscr = _split(refs, [sum(ex_in), sum(ex_out), sum(ex_scr)])
        parts = list(zip(exchanges, _split(xin, ex_in), _split(xout, ex_out), _split(xscr, ex_scr)))
        for phase in ("start", "mid", "wait"):
            for e, ins, outs, sems in parts:
                getattr(e, phase)(ins, outs, sems)

    any_spec = pl.BlockSpec(memory_space=pl.ANY)
    outs = pl.pallas_call(
        body, name=name, in_specs=[any_spec] * sum(ex_in), out_specs=[any_spec] * sum(ex_out),
        out_shape=[s for e in exchanges for s in e.out_shape],
        scratch_shapes=[s for e in exchanges for s in e.scratch])(*[a for e in exchanges for a in e.args])
    return _split(outs, ex_out)


def _tile_specs(t, d, di):
    row = lambda i: (i, 0)
    col = lambda i: (0, i)
    return dict(
        xd=pl.BlockSpec((TM, d), row), xi=pl.BlockSpec((TM, di), row),
        td=pl.BlockSpec((d, TM), col), ti=pl.BlockSpec((di, TM), col),
        s_xd=jax.ShapeDtypeStruct((t, d), F32), s_xi=jax.ShapeDtypeStruct((t, di), BF16),
        s_td=jax.ShapeDtypeStruct((d, t), BF16), s_ti=jax.ShapeDtypeStruct((di, t), BF16))


def _fwd_a(x, win, wout, gain, bias, wc2, bs2, lng, lnb, target=None, exchanges=()):
    t, d = x.shape
    di = wout.shape[0]
    gd = di // A_GROUPS
    n_loss = 0 if target is None else 1

    def body(*refs):
        x_ref, win_ref, wout_ref, gain_ref, bias_ref, wc_ref, bs_ref, lng_ref, lnb_ref = refs[:9]
        (a1_ref, a2_ref, a3_ref, vn_ref, vh_ref, rg_ref, yt_ref, xt_ref, pre_ref,
         xn_ref) = refs[9 + n_loss:19 + n_loss]
        p_scr, vg_scr, y_scr = refs[19 + 2 * n_loss:]
        xv = x_ref[...]
        xt_ref[...] = xv.T.astype(BF16)
        _in_proj(xv.astype(BF16), win_ref, p_scr)
        s1 = jnp.zeros((TM, 1), F32)
        for c in range(di // CW):
            sl = slice(c * CW, (c + 1) * CW)
            pv = slice(di + c * CW, di + (c + 1) * CW)
            vg, dvg = _gelu_and_grad(p_scr[:, pv])
            vg_scr[:, sl] = vg
            p_scr[:, pv] = dvg
            s1 += jnp.sum(vg, axis=1, keepdims=True)
        mu = s1 * (1.0 / di)
        s2 = jnp.zeros((TM, 1), F32)
        for c in range(di // CW):
            dlt = vg_scr[:, c * CW:(c + 1) * CW] - mu
            s2 += jnp.sum(dlt * dlt, axis=1, keepdims=True)
        rstd = lax.rsqrt(s2 * (1.0 / di) + LN_EPS)
        for c in range(di // CW):
            sl = slice(c * CW, (c + 1) * CW)
            vh = (vg_scr[:, sl] - mu) * rstd
            vh_ref[:, sl] = vh.astype(BF16)
            vn_ref[:, sl] = (vh * gain_ref[:, sl] + bias_ref[:, sl]).astype(BF16)
            rg_ref[:, sl] = (p_scr[:, di + c * CW:di + (c + 1) * CW] * rstd).astype(BF16)
        for g in range(A_GROUPS):
            sl = slice(g * gd, (g + 1) * gd)
            sv = jnp.dot(wc_ref[g], vn_ref[:, sl], preferred_element_type=F32) + bs_ref[g]
            u, du = _gelu_and_grad(p_scr[:, sl])
            s, ds = _silu_and_grad(p_scr[:, 2 * di + g * gd:2 * di + (g + 1) * gd])
            us = u * s
            a1_ref[:, sl] = (s * du).astype(BF16)
            a2_ref[:, sl] = (u * ds).astype(BF16)
            a3_ref[:, sl] = us.astype(BF16)
            y = us * sv
            y_scr[:, sl] = y.astype(BF16)
            yt_ref[sl, :] = y.T.astype(BF16)
        out = jnp.dot(y_scr[...], wout_ref[...], preferred_element_type=F32)
        _post_norm(xv, out, lng_ref, lnb_ref, pre_ref, xn_ref)
        if n_loss:
            t_ref, sq_ref = refs[9], refs[20]
            _zero_at_first_step(sq_ref)
            diff = xn_ref[...] - t_ref[...]
            xn_ref[...] = diff * (1.0 / d)
            sq_ref[...] += _fold8(diff * diff)

    sp = _tile_specs(t, d, di)
    loss_in = [] if target is None else [target]
    return _call(
        body, "fwd_a", (t // TM,), (x, win, wout, gain, bias, wc2, bs2, lng, lnb, *loss_in),
        in_specs=[sp["xd"]] + [_vmem()] * 8 + [sp["xd"]] * n_loss,
        out_specs=([sp["xi"]] * 6 + [sp["ti"], sp["td"], sp["xd"], sp["xd"]]
                   + [pl.BlockSpec((8, d), lambda i: (0, 0))] * n_loss),
        out_shape=([sp["s_xi"]] * 6 + [sp["s_ti"], sp["s_td"], sp["s_xd"], sp["s_xd"]]
                   + [jax.ShapeDtypeStruct((8, d), F32)] * n_loss),
        scratch=[pltpu.VMEM((TM, 3 * di), F32), pltpu.VMEM((TM, di), F32), pltpu.VMEM((TM, di), BF16)],
        exchanges=exchanges)


def _bwd_a(g, pre, a1, a2, a3, vn, vh, rg, wout, gain, wc2, wc2t, bs2, lng, exchanges=()):
    t, d = g.shape
    di = wout.shape[0]
    gd = di // A_GROUPS

    def body(g_ref, pre_ref, a1_ref, a2_ref, a3_ref, vn_ref, vh_ref, rg_ref,
             wout_ref, gain_ref, wc_ref, wct_ref, bs_ref, lng_ref,
             dpre_ref, dp_ref, dlng_ref, dlnb_ref, dgain_ref, dbias_ref, dbs_ref, dwc_ref,
             dy_scr, dv_scr):
        _zero_at_first_step(dlng_ref, dlnb_ref, dgain_ref, dbias_ref, dbs_ref, dwc_ref)
        dpre = _post_norm_bwd(g_ref, pre_ref, lng_ref, dpre_ref, dlng_ref, dlnb_ref)
        dy_scr[...] = lax.dot_general(dpre.astype(BF16), wout_ref[...], _NT, preferred_element_type=F32)
        for grp in range(A_GROUPS):
            sl = slice(grp * gd, (grp + 1) * gd)
            vn_g = vn_ref[:, sl]
            sv = jnp.dot(wc_ref[grp], vn_g, preferred_element_type=F32) + bs_ref[grp]
            dy = dy_scr[:, sl]
            dys = dy * sv
            dp_ref[:, sl] = (dys * _f32(a1_ref, sl)).astype(BF16)
            dp_ref[:, 2 * di + grp * gd:2 * di + (grp + 1) * gd] = (dys * _f32(a2_ref, sl)).astype(BF16)
            dsv = dy * _f32(a3_ref, sl)
            dbs_ref[grp] += jnp.sum(dsv, axis=1, keepdims=True)
            dsvb = dsv.astype(BF16)
            dwc_ref[grp] += lax.dot_general(dsvb, vn_g, _NT, preferred_element_type=F32)
            dv_scr[:, sl] = jnp.dot(wct_ref[grp], dsvb, preferred_element_type=F32)
        r1 = jnp.zeros((TM, 1), F32)
        r2 = jnp.zeros((TM, 1), F32)
        for c in range(di // CW):
            sl = slice(c * CW, (c + 1) * CW)
            dv = dv_scr[:, sl]
            vhat = _f32(vh_ref, sl)
            dgain_ref[:, sl] += _fold8(dv * vhat)
            dbias_ref[:, sl] += _fold8(dv)
            dvh = dv * gain_ref[:, sl]
            dv_scr[:, sl] = dvh
            r1 += jnp.sum(dvh, axis=1, keepdims=True)
            r2 += jnp.sum(dvh * vhat, axis=1, keepdims=True)
        m1 = r1 * (1.0 / di)
        m2 = r2 * (1.0 / di)
        for c in range(di // CW):
            sl = slice(c * CW, (c + 1) * CW)
            dp_ref[:, di + c * CW:di + (c + 1) * CW] = (
                (dv_scr[:, sl] - m1 - _f32(vh_ref, sl) * m2) * _f32(rg_ref, sl)).astype(BF16)

    sp = _tile_specs(t, d, di)
    const2 = lambda i: (0, 0)
    const3 = lambda i: (0, 0, 0)
    return _call(
        body, "bwd_a", (t // TM,), (g, pre, a1, a2, a3, vn, vh, rg, wout, gain, wc2, wc2t, bs2, lng),
        in_specs=[sp["xd"], sp["xd"]] + [sp["xi"]] * 6 + [_vmem()] * 6,
        out_specs=[sp["xd"], pl.BlockSpec((TM, 3 * di), lambda i: (i, 0)),
                   pl.BlockSpec((8, d), const2), pl.BlockSpec((8, d), const2),
                   pl.BlockSpec((8, di), const2), pl.BlockSpec((8, di), const2),
                   pl.BlockSpec((A_GROUPS, TM, 1), const3), pl.BlockSpec((A_GROUPS, TM, TM), const3)],
        out_shape=[sp["s_xd"], jax.ShapeDtypeStruct((t, 3 * di), BF16),
                   jax.ShapeDtypeStruct((8, d), F32), jax.ShapeDtypeStruct((8, d), F32),
                   jax.ShapeDtypeStruct((8, di), F32), jax.ShapeDtypeStruct((8, di), F32),
                   jax.ShapeDtypeStruct((A_GROUPS, TM, 1), F32), jax.ShapeDtypeStruct((A_GROUPS, TM, TM), F32)],
        scratch=[pltpu.VMEM((TM, di), F32), pltpu.VMEM((TM, di), F32)],
        exchanges=exchanges)


def _inv_count(tile, window, rows=TM):
    pos = tile * rows + lax.broadcasted_iota(jnp.int32, (rows, 1), 0)
    return 1.0 / jnp.minimum(pos + 1, window).astype(F32)


def _window_sum(ext, window, down):
    rows = ext.shape[0]
    k = 1
    while k < window:
        ext = ext + pltpu.roll(ext, k if down else rows - k, 0)
        k *= 2
    return ext


def _fwd_b(x, win, wgrp, scale, wout, lng, lnb, exchanges=()):
    t, d = x.shape
    di = wout.shape[0]
    gd = di // len(POOL_WINDOWS)

    def body(x_ref, win_ref, wgrp_ref, scale_ref, wout_ref, lng_ref, lnb_ref,
             b1_ref, b2_ref, b3_ref, poolt_ref, yt_ref, xt_ref, pre_ref, xn_ref, p_scr, ext_scr, y_scr):
        i = pl.program_id(0)

        @pl.when(i == 0)
        def _():
            ext_scr[0:HALO, :] = jnp.zeros((HALO, di), F32)

        xv = x_ref[...]
        xt_ref[...] = xv.T.astype(BF16)
        _in_proj(xv.astype(BF16), win_ref, p_scr)
        ext_scr[HALO:, :] = p_scr[:, :di]
        for grp, window in enumerate(POOL_WINDOWS):
            sl = slice(grp * gd, (grp + 1) * gd)
            ext = ext_scr[:, sl]
            pooled = (_window_sum(ext, window, True)[HALO:] * _inv_count(i, window) - ext[HALO:]).astype(BF16)
            poolt_ref[sl, :] = pooled.astype(F32).T.astype(BF16)
            mixed = jnp.dot(pooled, wgrp_ref[grp], preferred_element_type=F32)
            s, ds = _silu_and_grad(p_scr[:, di + grp * gd:di + (grp + 1) * gd])
            sc = scale_ref[:, sl]
            ms = mixed * s
            b1_ref[:, sl] = (mixed * sc * ds).astype(BF16)
            b2_ref[:, sl] = ms.astype(BF16)
            b3_ref[:, sl] = (sc * s).astype(BF16)
            y = ms * sc
            y_scr[:, sl] = y.astype(BF16)
            yt_ref[sl, :] = y.T.astype(BF16)
        ext_scr[0:HALO, :] = ext_scr[TM:TM + HALO, :]
        out = jnp.dot(y_scr[...], wout_ref[...], preferred_element_type=F32)
        _post_norm(xv, out, lng_ref, lnb_ref, pre_ref, xn_ref)

    sp = _tile_specs(t, d, di)
    return _call(
        body, "fwd_b", (t // TM,), (x, win, wgrp, scale, wout, lng, lnb),
        in_specs=[sp["xd"]] + [_vmem()] * 6,
        out_specs=[sp["xi"]] * 3 + [sp["ti"], sp["ti"], sp["td"], sp["xd"], sp["xd"]],
        out_shape=[sp["s_xi"]] * 3 + [sp["s_ti"], sp["s_ti"], sp["s_td"], sp["s_xd"], sp["s_xd"]],
        scratch=[pltpu.VMEM((TM, 2 * di), F32), pltpu.VMEM((TM + HALO, di), F32), pltpu.VMEM((TM, di), BF16)],
        exchanges=exchanges)


def _bwd_b(g, pre, b1, b2, b3, wout, wgrp, lng, exchanges=()):
    t, d = g.shape
    di = wout.shape[0]
    gd = di // len(POOL_WINDOWS)
    nt = t // TM

    def body(g_ref, pre_ref, b1_ref, b2_ref, b3_ref, wout_ref, wgrp_ref, lng_ref,
             dpre_ref, dp_ref, dmix_ref, dlng_ref, dlnb_ref, dscale_ref, dy_scr, ext_scr):
        i = pl.program_id(0)
        tile = nt - 1 - i
        _zero_at_first_step(dlng_ref, dlnb_ref, dscale_ref)

        @pl.when(i == 0)
        def _():
            ext_scr[TM:, :] = jnp.zeros((HALO, di), F32)

        dpre = _post_norm_bwd(g_ref, pre_ref, lng_ref, dpre_ref, dlng_ref, dlnb_ref)
        dy_scr[...] = lax.dot_general(dpre.astype(BF16), wout_ref[...], _NT, preferred_element_type=F32)
        for grp, window in enumerate(POOL_WINDOWS):
            sl = slice(grp * gd, (grp + 1) * gd)
            dy = dy_scr[:, sl]
            dp_ref[:, di + grp * gd:di + (grp + 1) * gd] = (dy * _f32(b1_ref, sl)).astype(BF16)
            dscale_ref[:, sl] += _fold8(dy * _f32(b2_ref, sl))
            dmixed = (dy * _f32(b3_ref, sl)).astype(BF16)
            dmix_ref[:, sl] = dmixed
            dpooled = lax.dot_general(dmixed, wgrp_ref[grp], _NT, preferred_element_type=F32)
            ext_scr[0:TM, sl] = dpooled * _inv_count(tile, window)
            dv = _window_sum(ext_scr[:, sl], window, False)[0:TM] - dpooled
            dp_ref[:, sl] = dv.astype(BF16)
        ext_scr[TM:, :] = ext_scr[0:HALO, :]

    rrow = lambda i: (nt - 1 - i, 0)
    const2 = lambda i: (0, 0)
    xd, xi = pl.BlockSpec((TM, d), rrow), pl.BlockSpec((TM, di), rrow)
    return _call(
        body, "bwd_b", (nt,), (g, pre, b1, b2, b3, wout, wgrp, lng),
        in_specs=[xd, xd, xi, xi, xi, _vmem(), _vmem(), _vmem()],
        out_specs=[xd, pl.BlockSpec((TM, 2 * di), rrow), xi,
                   pl.BlockSpec((8, d), const2), pl.BlockSpec((8, d), const2), pl.BlockSpec((8, di), const2)],
        out_shape=[jax.ShapeDtypeStruct((t, d), F32), jax.ShapeDtypeStruct((t, 2 * di), BF16),
                   jax.ShapeDtypeStruct((t, di), BF16),
                   jax.ShapeDtypeStruct((8, d), F32), jax.ShapeDtypeStruct((8, d), F32),
                   jax.ShapeDtypeStruct((8, di), F32)],
        scratch=[pltpu.VMEM((TM, di), F32), pltpu.VMEM((TM + HALO, di), F32)],
        exchanges=exchanges)


def _fwd_c(x, win, convw, wout, lng, lnb, exchanges=()):
    t, d = x.shape
    di = wout.shape[0]

    def body(x_ref, win_ref, cw_ref, wout_ref, lng_ref, lnb_ref,
             c1_ref, c2_ref, c3_ref, cg_ref, hg_ref, yt_ref, xt_ref, pre_ref, xn_ref, p_scr, ext_scr, y_scr):
        i = pl.program_id(0)

        @pl.when(i == 0)
        def _():
            ext_scr[0:CHALO, :] = jnp.zeros((CHALO, di), F32)

        xv = x_ref[...]
        xt_ref[...] = xv.T.astype(BF16)
        _in_proj(xv.astype(BF16), win_ref, p_scr)
        for c in range(di // CW):
            sl = slice(c * CW, (c + 1) * CW)
            bb = p_scr[:, sl]
            cc = p_scr[:, di + c * CW:di + (c + 1) * CW]
            hh = p_scr[:, 2 * di + c * CW:2 * di + (c + 1) * CW]
            s, ds = _silu_and_grad(p_scr[:, 3 * di + c * CW:3 * di + (c + 1) * CW])
            ext_scr[CHALO:, sl] = cc * hh
            ext = ext_scr[:, sl]
            conv = (pltpu.roll(ext, 2, 0)[CHALO:] * cw_ref[0:1, sl] + pltpu.roll(ext, 1, 0)[CHALO:] * cw_ref[1:2, sl]
                    + ext[CHALO:] * cw_ref[2:3, sl])
            cs = conv * s
            c1_ref[:, sl] = cs.astype(BF16)
            c2_ref[:, sl] = (bb * conv * ds).astype(BF16)
            c3_ref[:, sl] = (bb * s).astype(BF16)
            cg_ref[:, sl] = cc.astype(BF16)
            hg_ref[:, sl] = hh.astype(BF16)
            y = bb * cs
            y_scr[:, sl] = y.astype(BF16)
            yt_ref[sl, :] = y.T.astype(BF16)
        ext_scr[0:CHALO, :] = ext_scr[TM:TM + CHALO, :]
        out = jnp.dot(y_scr[...], wout_ref[...], preferred_element_type=F32)
        _post_norm(xv, out, lng_ref, lnb_ref, pre_ref, xn_ref)

    sp = _tile_specs(t, d, di)
    return _call(
        body, "fwd_c", (t // TM,), (x, win, convw, wout, lng, lnb),
        in_specs=[sp["xd"]] + [_vmem()] * 5,
        out_specs=[sp["xi"]] * 5 + [sp["ti"], sp["td"], sp["xd"], sp["xd"]],
        out_shape=[sp["s_xi"]] * 5 + [sp["s_ti"], sp["s_td"], sp["s_xd"], sp["s_xd"]],
        scratch=[pltpu.VMEM((TM, 4 * di), F32), pltpu.VMEM((TM + CHALO, di), F32), pltpu.VMEM((TM, di), BF16)],
        exchanges=exchanges)


def _bwd_c(g, pre, c1, c2, c3, cg, hg, wout, convw, lng, exchanges=()):
    t, d = g.shape
    di = wout.shape[0]
    nt = t // TM

    def body(g_ref, pre_ref, c1_ref, c2_ref, c3_ref, cg_ref, hg_ref, wout_ref, cw_ref, lng_ref,
             dpre_ref, dp_ref, dlng_ref, dlnb_ref, dcw_ref, dy_scr, ext_scr):
        i = pl.program_id(0)
        _zero_at_first_step(dlng_ref, dlnb_ref, dcw_ref)

        @pl.when(i == 0)
        def _():
            ext_scr[TM:, :] = jnp.zeros((CHALO, di), F32)

        dpre = _post_norm_bwd(g_ref, pre_ref, lng_ref, dpre_ref, dlng_ref, dlnb_ref)
        dy_scr[...] = lax.dot_general(dpre.astype(BF16), wout_ref[...], _NT, preferred_element_type=F32)
        rows = TM + CHALO
        for c in range(di // CW):
            sl = slice(c * CW, (c + 1) * CW)
            dy = dy_scr[:, sl]
            cc = _f32(cg_ref, sl)
            hh = _f32(hg_ref, sl)
            dp_ref[:, sl] = (dy * _f32(c1_ref, sl)).astype(BF16)
            dp_ref[:, 3 * di + c * CW:3 * di + (c + 1) * CW] = (dy * _f32(c2_ref, sl)).astype(BF16)
            dconv = dy * _f32(c3_ref, sl)
            ext_scr[0:TM, sl] = dconv
            ext = ext_scr[:, sl]
            d1 = pltpu.roll(ext, rows - 1, 0)[0:TM]
            d2 = pltpu.roll(ext, rows - 2, 0)[0:TM]
            dq = dconv * cw_ref[2:3, sl] + d1 * cw_ref[1:2, sl] + d2 * cw_ref[0:1, sl]
            q = cc * hh
            dcw_ref[0, :, sl] += _fold8(q * d2)
            dcw_ref[1, :, sl] += _fold8(q * d1)
            dcw_ref[2, :, sl] += _fold8(q * dconv)
            dp_ref[:, di + c * CW:di + (c + 1) * CW] = (dq * hh).astype(BF16)
            dp_ref[:, 2 * di + c * CW:2 * di + (c + 1) * CW] = (dq * cc).astype(BF16)
        ext_scr[TM:, :] = ext_scr[0:CHALO, :]

    rrow = lambda i: (nt - 1 - i, 0)
    const2 = lambda i: (0, 0)
    xd, xi = pl.BlockSpec((TM, d), rrow), pl.BlockSpec((TM, di), rrow)
    return _call(
        body, "bwd_c", (nt,), (g, pre, c1, c2, c3, cg, hg, wout, convw, lng),
        in_specs=[xd, xd] + [xi] * 5 + [_vmem()] * 3,
        out_specs=[xd, pl.BlockSpec((TM, 4 * di), rrow),
                   pl.BlockSpec((8, d), const2), pl.BlockSpec((8, d), const2),
                   pl.BlockSpec((3, 8, di), lambda i: (0, 0, 0))],
        out_shape=[jax.ShapeDtypeStruct((t, d), F32), jax.ShapeDtypeStruct((t, 4 * di), BF16),
                   jax.ShapeDtypeStruct((8, d), F32), jax.ShapeDtypeStruct((8, d), F32),
                   jax.ShapeDtypeStruct((3, 8, di), F32)],
        scratch=[pltpu.VMEM((TM, di), F32), pltpu.VMEM((TM + CHALO, di), F32)],
        exchanges=exchanges)


def _dx(dp, dpre, win, exchanges=()):
    t, d = dpre.shape
    n = dp.shape[1]
    cs = win.shape[2]

    def body(dp_ref, dpre_ref, win_ref, dx_ref):
        acc = ALPHA * dpre_ref[...]
        for j in range(N_DEV):
            acc += lax.dot_general(dp_ref[:, j * cs:(j + 1) * cs], win_ref[j], _NT, preferred_element_type=F32)
        dx_ref[...] = acc

    row = lambda i: (i, 0)
    (dx,), ex = _call(
        body, "dx", (t // TM,), (dp, dpre, win),
        in_specs=[pl.BlockSpec((TM, n), row), pl.BlockSpec((TM, d), row), _vmem()],
        out_specs=[pl.BlockSpec((TM, d), row)],
        out_shape=[jax.ShapeDtypeStruct((t, d), F32)],
        exchanges=exchanges)
    return dx, ex


def _wgrad(at, b, nb, per_block_rows, name, exchanges=()):
    m_all, t = at.shape
    tn = b.shape[1] // nb
    m = m_all // nb if per_block_rows else m_all
    tk = min(WGRAD_TK, t)
    nk = t // tk

    def body(at_ref, b_ref, out_ref, acc):
        k = pl.program_id(1)

        @pl.when(k == 0)
        def _():
            acc[...] = jnp.zeros(acc.shape, F32)

        acc[...] += jnp.dot(at_ref[...], b_ref[...].astype(BF16), preferred_element_type=F32)

        @pl.when(k == nk - 1)
        def _():
            out_ref[0] = acc[...].astype(BF16)

    at_map = (lambda j, k: (j, k)) if per_block_rows else (lambda j, k: (0, k))
    (out,), ex = _call(
        body, name, (nb, nk), (at, b),
        in_specs=[pl.BlockSpec((m, tk), at_map), pl.BlockSpec((tk, tn), lambda j, k: (k, j))],
        out_specs=[pl.BlockSpec((1, m, tn), lambda j, k: (j, 0, 0))],
        out_shape=[jax.ShapeDtypeStruct((nb, m, tn), BF16)],
        scratch=[pltpu.VMEM((m, tn), F32)],
        exchanges=exchanges)
    return out, ex


ADAMW_BLOCK_BYTES = 6 * 1024 * 1024


def _sum_parts(parts_ref):
    g = parts_ref[0].astype(F32)
    for s in range(1, parts_ref.shape[0]):
        g = g + parts_ref[s].astype(F32)
    return g


def _row_tile(rows, bytes_per_row):
    if rows * bytes_per_row <= ADAMW_BLOCK_BYTES:
        return rows
    best = 8
    for cand in range(8, rows, 8):
        if rows % cand == 0 and cand * bytes_per_row <= ADAMW_BLOCK_BYTES:
            best = cand
    return best


def _pair_sum(full, theirs, core, name):
    shape = theirs.shape
    r, c = math.prod(shape[1:-1]), shape[-1]
    tr = _row_tile(r, c * 3 * full.dtype.itemsize)
    half = N_DEV // 2

    def body(core_ref, a_ref, b_ref, out_ref):
        out_ref[...] = (a_ref[...].astype(F32) + b_ref[...].astype(F32)).astype(out_ref.dtype)

    blk = pl.BlockSpec((None, tr, c), lambda q, i, core_ref: (q, i, 0))
    grid_spec = pltpu.PrefetchScalarGridSpec(
        num_scalar_prefetch=1, grid=(half, r // tr),
        in_specs=[pl.BlockSpec((None, None, tr, c), lambda q, i, core_ref: (q, core_ref[0], i, 0)), blk],
        out_specs=blk)
    return pl.pallas_call(
        body, name=name, grid_spec=grid_spec, out_shape=jax.ShapeDtypeStruct((half, r, c), full.dtype),
        compiler_params=_params(("arbitrary", "arbitrary")),
    )(core, full.reshape(half, 2, r, c), theirs.reshape(half, r, c)).reshape(shape)


def _adamw(parts, w, m, v, name):
    s, r, c = parts.shape
    tr = _row_tile(r, c * (s * parts.dtype.itemsize + 7 * 4))
    bc1 = 1.0 - ADAM_B1 ** ADAM_STEP
    bc2 = 1.0 - ADAM_B2 ** ADAM_STEP

    def body(parts_ref, w_ref, m_ref, v_ref, g_ref, d_ref, nm_ref, nv_ref):
        g = _sum_parts(parts_ref)
        g_ref[...] = g
        nm = ADAM_B1 * m_ref[...] + (1.0 - ADAM_B1) * g
        nv = ADAM_B2 * v_ref[...] + (1.0 - ADAM_B2) * (g * g)
        nm_ref[...] = nm
        nv_ref[...] = nv
        d_ref[...] = -ADAM_LR * ((nm / bc1) / (jnp.sqrt(nv / bc2) + ADAM_EPS) + ADAM_WD * w_ref[...])

    blk = pl.BlockSpec((tr, c), lambda i: (i, 0))
    return pl.pallas_call(
        body, name=name, grid=(r // tr,),
        in_specs=[pl.BlockSpec((s, tr, c), lambda i: (0, i, 0)), blk, blk, blk],
        out_specs=[blk, blk, blk, blk],
        out_shape=[jax.ShapeDtypeStruct((r, c), F32)] * 4,
        compiler_params=_params(("arbitrary",)),
    )(parts, w, m, v)


_LANES = 128


def _pack(arrays):
    flat = jnp.concatenate([a.reshape(-1) for a in arrays])
    pad = (-flat.shape[0]) % (8 * _LANES)
    return jnp.pad(flat, (0, pad)).reshape(-1, _LANES)


def _unpack(packed, shapes):
    flat = packed.reshape(-1)
    out, off = [], 0
    for shp in shapes:
        size = math.prod(shp)
        out.append(flat[off:off + size].reshape(shp))
        off += size
    return out


def _spatial_weights(w_s, b_s, rows):
    reps = rows // CHUNK
    tril = jnp.tril(jnp.ones((CHUNK, CHUNK), F32))
    wc = w_s * tril
    eye = jnp.eye(reps, dtype=F32)
    wc2 = jnp.einsum("ab,gts->gatbs", eye, wc).reshape(A_GROUPS, rows, rows)
    bs2 = jnp.tile(b_s, (1, reps)).reshape(A_GROUPS, rows, 1)
    return wc2.astype(BF16), jnp.swapaxes(wc2, 1, 2).astype(BF16), bs2


def _spatial_weight_grad(dwc2, dbs2):
    reps = TM // CHUNK
    tril = jnp.tril(jnp.ones((CHUNK, CHUNK), F32))
    blocks = dwc2.reshape(A_GROUPS, reps, CHUNK, reps, CHUNK)
    dws = sum(blocks[:, a, :, a, :] for a in range(reps)) * tril
    dbs = dbs2.reshape(A_GROUPS, reps, CHUNK).sum(axis=1)
    return dws, dbs


def _row2(a):
    return a.reshape(1, -1)


def kernel(x, a0_w_in, a0_v_gain, a0_v_bias, a0_w_s, a0_b_s, a0_w_out, ln0_gain, ln0_bias, b1_w_in, b1_w_grp, b1_scale, b1_w_out, ln1_gain, ln1_bias, c2_w_in, c2_conv_w, c2_w_out, ln2_gain, ln2_bias, a3_w_in, a3_v_gain, a3_v_bias, a3_w_s, a3_b_s, a3_w_out, ln3_gain, ln3_bias, loss_target, m_a0_w_in, m_a0_v_gain, m_a0_v_bias, m_a0_w_s, m_a0_b_s, m_a0_w_out, m_ln0_gain, m_ln0_bias, m_b1_w_in, m_b1_w_grp, m_b1_scale, m_b1_w_out, m_ln1_gain, m_ln1_bias, m_c2_w_in, m_c2_conv_w, m_c2_w_out, m_ln2_gain, m_ln2_bias, m_a3_w_in, m_a3_v_gain, m_a3_v_bias, m_a3_w_s, m_a3_b_s, m_a3_w_out, m_ln3_gain, m_ln3_bias, v_a0_w_in, v_a0_v_gain, v_a0_v_bias, v_a0_w_s, v_a0_b_s, v_a0_w_out, v_ln0_gain, v_ln0_bias, v_b1_w_in, v_b1_w_grp, v_b1_scale, v_b1_w_out, v_ln1_gain, v_ln1_bias, v_c2_w_in, v_c2_conv_w, v_c2_w_out, v_ln2_gain, v_ln2_bias, v_a3_w_in, v_a3_v_gain, v_a3_v_bias, v_a3_w_s, v_a3_b_s, v_a3_w_out, v_ln3_gain, v_ln3_bias):
    names = ["a0_w_in", "a0_v_gain", "a0_v_bias", "a0_w_s", "a0_b_s", "a0_w_out", "ln0_gain", "ln0_bias",
             "b1_w_in", "b1_w_grp", "b1_scale", "b1_w_out", "ln1_gain", "ln1_bias",
             "c2_w_in", "c2_conv_w", "c2_w_out", "ln2_gain", "ln2_bias",
             "a3_w_in", "a3_v_gain", "a3_v_bias", "a3_w_s", "a3_b_s", "a3_w_out", "ln3_gain", "ln3_bias"]
    env = dict(locals())
    w = {nm: env[nm] for nm in names}
    mom = {nm: env["m_" + nm] for nm in names}
    var = {nm: env["v_" + nm] for nm in names}

    x0 = x[0]
    target = loss_target[0]
    d_model = x0.shape[1]
    di = N_DEV * a0_w_out.shape[0]
    n_grp = len(POOL_WINDOWS)
    gd_b = di // n_grp

    layers = ("a0", "b1", "c2", "a3")
    big_of = {"a0": ["a0_w_in", "a0_w_out"], "b1": ["b1_w_in", "b1_w_grp", "b1_w_out"],
              "c2": ["c2_w_in", "c2_w_out"], "a3": ["a3_w_in", "a3_w_out"]}
    big = [nm for p in layers for nm in big_of[p]]
    bucket_of = {"a0": ["a0_v_gain", "a0_v_bias", "a0_w_s", "a0_b_s", "ln0_gain", "ln0_bias"],
                 "b1": ["b1_scale", "ln1_gain", "ln1_bias"], "c2": ["ln2_gain", "ln2_bias"],
                 "a3": ["a3_v_gain", "a3_v_bias", "a3_w_s", "a3_b_s", "ln3_gain", "ln3_bias"]}
    conv_shape = c2_conv_w.shape
    spatial = {p: _spatial_weights(w[p + "_w_s"], w[p + "_b_s"], TM) for p in ("a0", "a3")}

    def weight_gather(p):
        return _Gather([w[nm].astype(BF16) for nm in big_of[p]])

    (first,) = _exchange_only([_Gather([w[nm].astype(BF16) for nm in big_of["a0"]] + [_pack([c2_conv_w])])],
                              "gather_first")
    gathered = dict(zip(big_of["a0"], first))
    conv_all = jnp.stack([_unpack(first[-1][j], [conv_shape])[0] for j in range(N_DEV)], axis=1)
    conv_full = conv_all.reshape(conv_shape[0], di)
    w_in = lambda p: gathered[p + "_w_in"]
    w_out = lambda p: gathered[p + "_w_out"].reshape(di, d_model)
    saved = {}
    h = x0
    for i, p in enumerate(layers):
        lng, lnb = _row2(w[f"ln{i}_gain"]), _row2(w[f"ln{i}_bias"])
        nxt = layers[i + 1] if i + 1 < len(layers) else None
        ex = [weight_gather(nxt)] if nxt else []
        if p[0] == "a":
            wc2, _, bs2 = spatial[p]
            outs, got = _fwd_a(h, w_in(p), w_out(p), _row2(w[p + "_v_gain"]), _row2(w[p + "_v_bias"]),
                               wc2, bs2, lng, lnb, target=None if nxt else target, exchanges=ex)
        elif p[0] == "b":
            wgrp = jnp.swapaxes(gathered["b1_w_grp"], 0, 1).reshape(n_grp, gd_b, gd_b)
            outs, got = _fwd_b(h, w_in(p), wgrp, _row2(w[p + "_scale"]), w_out(p), lng, lnb, exchanges=ex)
        else:
            outs, got = _fwd_c(h, w_in(p), conv_full, w_out(p), lng, lnb, exchanges=ex)
        if nxt:
            saved[p], h = outs[:-1], outs[-1]
            gathered.update(zip(big_of[nxt], got[0]))
        else:
            saved[p], gcur, sq = outs[:-2], outs[-2], outs[-1]

    loss = lax.psum(jnp.sum(sq) * (0.5 / d_model), ("x", "y", "c"))

    part, full, landed, small_all = {}, {}, {}, {}

    def bucket_gather(bucket):
        return _Gather([_pack([part[nm] for nm in bucket_of[bucket]])])

    core = lax.axis_index("c").astype(jnp.int32).reshape(1)

    def chip_sums(names, got):
        return [_pair_sum(full[nm], theirs, core, "pair_sum_" + nm) for nm, theirs in zip(names, got)]

    pending = None
    for i, p in reversed(list(enumerate(layers))):
        lng = _row2(w[f"ln{i}_gain"])
        ex = []
        if pending:
            ex = [_PairExchange([full[nm] for nm in big_of[pending]]), bucket_gather(pending)]
            if pending == "c2":
                ex.append(_Exchange(scatters=[full["c2_conv_w"]]))
        *factors, yt, xt, pre = saved[p]
        if p[0] == "a":
            wc2, wc2t, bs2 = spatial[p]
            (dpre, dp, dlng, dlnb, dgain, dbias, dbs2, dwc2), got = _bwd_a(
                gcur, pre, *factors, w_out(p), _row2(w[p + "_v_gain"]), wc2, wc2t, bs2, lng, exchanges=ex)
            part[p + "_v_gain"], part[p + "_v_bias"] = dgain.sum(axis=0), dbias.sum(axis=0)
            part[p + "_w_s"], part[p + "_b_s"] = _spatial_weight_grad(dwc2, dbs2)
        elif p[0] == "b":
            b1f, b2f, b3f, poolt = factors
            (dpre, dp, dmixed, dlng, dlnb, dscale), got = _bwd_b(
                gcur, pre, b1f, b2f, b3f, w_out(p), wgrp, lng, exchanges=ex)
            part[p + "_scale"] = dscale.sum(axis=0)
            dwg, _ = _wgrad(poolt, dmixed, n_grp, True, "wgrad_grp")
            full[p + "_w_grp"] = jnp.swapaxes(dwg.reshape(n_grp, N_DEV, gd_b // N_DEV, gd_b), 0, 1)
        else:
            (dpre, dp, dlng, dlnb, dcw), got = _bwd_c(gcur, pre, *factors, w_out(p), conv_full, lng, exchanges=ex)
            dconv = dcw.sum(axis=1).reshape(conv_shape[0], N_DEV, conv_shape[1])
            full["c2_conv_w"] = jnp.stack([_pack([dconv[:, j]]) for j in range(N_DEV)])
        part[f"ln{i}_gain"], part[f"ln{i}_bias"] = dlng.sum(axis=0), dlnb.sum(axis=0)
        if pending:
            small_all[pending] = got[1][0]
            if pending == "c2":
                small_all["conv"] = got[2][0]
        dwo, _ = _wgrad(yt, dpre, 1, False, "wgrad_out_" + p)
        full[p + "_w_out"] = dwo.reshape(N_DEV, di // N_DEV, d_model)
        ex = [_ChipScatter(chip_sums(big_of[pending], got[0]))] if pending else []
        if i == 0:
            ex += [_Exchange(scatters=[full[p + "_w_out"]]), bucket_gather(p)]
        full[p + "_w_in"], got = _wgrad(xt, dp, N_DEV, False, "wgrad_in_" + p, exchanges=ex)
        if pending:
            landed.update(zip(big_of[pending], got[0]))
        if i > 0:
            gcur, _ = _dx(dp, dpre, w_in(p))
            pending = p
        else:
            landed[p + "_w_out"], small_all[p] = got[-2][0], got[-1][0]
            (theirs,) = _exchange_only([_PairExchange([full[p + "_w_in"]])], "pair_exchange_last")
            gcur, got = _dx(dp, dpre, w_in(p), exchanges=[_ChipScatter(chip_sums([p + "_w_in"], theirs))])
            landed[p + "_w_in"] = got[0][0]
    grad_x = gcur[None]

    grads, deltas, new_m, new_v = {}, {}, {}, {}
    for nm in big:
        shp = w[nm].shape
        r2 = (math.prod(shp[:-1]), shp[-1])
        outs = _adamw(landed[nm].reshape((-1,) + r2), w[nm].reshape(r2), mom[nm].reshape(r2), var[nm].reshape(r2),
                      "adamw_" + nm)
        grads[nm], deltas[nm], new_m[nm], new_v[nm] = (o.reshape(shp) for o in outs)
    buckets = dict(bucket_of, conv=["c2_conv_w"])
    for key, members in buckets.items():
        shapes = [w[nm].shape for nm in members]
        outs = _adamw(small_all[key], _pack([w[nm] for nm in members]), _pack([mom[nm] for nm in members]),
                      _pack([var[nm] for nm in members]), "adamw_small_" + key)
        for tgt, o in zip((grads, deltas, new_m, new_v), outs):
            tgt.update(zip(members, _unpack(o, shapes)))

    return (loss, grad_x, *[grads[nm] for nm in names], *[deltas[nm] for nm in names],
            *[new_m[nm] for nm in names], *[new_v[nm] for nm in names])
```

```python
import functools
import math

import jax
import jax.numpy as jnp
from jax import lax
from jax.experimental import pallas as pl
from jax.experimental.pallas import tpu as pltpu

F32 = jnp.float32
BF16 = jnp.bfloat16

N_DEV = 8
DEPTH = 4
CHUNK = 128
A_GROUPS = 8
POOL_WINDOWS = (2, 4, 8, 16)
LN_EPS = 1e-5
ALPHA = (2.0 * DEPTH) ** 0.25
ADAM_LR = 0.001
ADAM_B1 = 0.9
ADAM_B2 = 0.999
ADAM_EPS = 1e-08
ADAM_WD = 0.01
ADAM_STEP = 10

TM = 256
HALO = 16
CHALO = 8
CW = 512
WGRAD_TK = 2048
VMEM_LIMIT_BYTES = 58 * 1024 * 1024

_NT = (((1,), (1,)), ((), ()))
_SQRT_2_OVER_PI = math.sqrt(2.0 / math.pi)
_MESH = pl.DeviceIdType.MESH


def _vmem():
    return pl.BlockSpec(memory_space=pltpu.VMEM)


def _params(sem=None):
    return pltpu.CompilerParams(dimension_semantics=sem, vmem_limit_bytes=VMEM_LIMIT_BYTES)


def _gelu(x):
    t = jnp.tanh(_SQRT_2_OVER_PI * (x + 0.044715 * (x * x * x)))
    return x * (0.5 * (1.0 + t))


def _gelu_and_grad(x):
    x2 = x * x
    t = jnp.tanh(_SQRT_2_OVER_PI * (x + 0.044715 * (x * x2)))
    cdf = 0.5 * (1.0 + t)
    grad = cdf + 0.5 * x * (1.0 - t * t) * (_SQRT_2_OVER_PI * (1.0 + 3.0 * 0.044715 * x2))
    return x * cdf, grad


def _silu_and_grad(z):
    sg = 1.0 / (1.0 + jnp.exp(-z))
    return z * sg, sg * (1.0 + z * (1.0 - sg))


def _fold8(a):
    return a.reshape(a.shape[0] // 8, 8, a.shape[1]).sum(axis=0)


def _row_mean(a):
    return jnp.mean(a, axis=-1, keepdims=True)


def _ln_stats(x):
    mu = _row_mean(x)
    xc = x - mu
    rstd = lax.rsqrt(_row_mean(xc * xc) + LN_EPS)
    return xc * rstd, rstd


def _post_norm(x, out, lng_ref, lnb_ref, pre_ref, xn_ref):
    pre = ALPHA * x + out
    pre_ref[...] = pre
    xhat, _ = _ln_stats(pre)
    xn_ref[...] = xhat * lng_ref[...] + lnb_ref[...]


def _post_norm_bwd(g_ref, pre_ref, lng_ref, dpre_ref, dlng_ref, dlnb_ref):
    go = g_ref[...]
    xhat, rstd = _ln_stats(pre_ref[...])
    dlng_ref[...] += _fold8(go * xhat)
    dlnb_ref[...] += _fold8(go)
    dxh = go * lng_ref[...]
    dpre = rstd * (dxh - _row_mean(dxh) - xhat * _row_mean(dxh * xhat))
    dpre_ref[...] = dpre
    return dpre


def _in_proj(xb, win_ref, p_ref):
    cs = win_ref.shape[2]
    for j in range(N_DEV):
        p_ref[:, j * cs:(j + 1) * cs] = jnp.dot(xb, win_ref[j], preferred_element_type=F32)


def _zero_at_first_step(*refs):
    @pl.when(pl.program_id(0) == 0)
    def _():
        for r in refs:
            r[...] = jnp.zeros(r.shape, r.dtype)


def _f32(ref, sl):
    return ref[:, sl].astype(F32)


def _my_position():
    x, y, c = lax.axis_index("x"), lax.axis_index("y"), lax.axis_index("c")
    return (x, y, c), 4 * x + 2 * y + c


def _peer(k):
    (x, y, c), _ = _my_position()
    peer = (x ^ (k >> 2), y ^ ((k >> 1) & 1), c ^ (k & 1))
    return peer, 4 * peer[0] + 2 * peer[1] + peer[2]


class _Exchange:
    def __init__(self, gathers=(), scatters=()):
        self.args = list(gathers) + list(scatters)
        self.n_gather = len(gathers)
        self.out_shape = ([jax.ShapeDtypeStruct((N_DEV,) + a.shape, a.dtype) for a in gathers]
                          + [jax.ShapeDtypeStruct(a.shape, a.dtype) for a in scatters])
        n = len(self.args)
        self.scratch = [pltpu.SemaphoreType.DMA((n, N_DEV)), pltpu.SemaphoreType.DMA((n, N_DEV)),
                        pltpu.SemaphoreType.DMA((n,))]

    def _src(self, ins, w, pos):
        return ins[w] if w < self.n_gather else ins[w].at[pos]

    def _copies(self, ins, outs, sems, arrivals):
        send_sems, recv_sems, local_sems = sems
        _, me = _my_position()
        n = len(self.args)
        copies = []
        if not arrivals:
            copies = [pltpu.make_async_copy(self._src(ins, w, me), outs[w].at[me], local_sems.at[w]) for w in range(n)]
        for k in range(1, N_DEV):
            peer, peer_pos = _peer(k)
            for w in range(n):
                copies.append(pltpu.make_async_remote_copy(
                    src_ref=self._src(ins, w, me if arrivals else peer_pos),
                    dst_ref=outs[w].at[peer_pos if arrivals else me],
                    send_sem=send_sems.at[w, k], recv_sem=recv_sems.at[w, k], device_id=peer, device_id_type=_MESH))
        return copies

    def start(self, ins, outs, sems):
        for cp in self._copies(ins, outs, sems, False):
            cp.start()

    def mid(self, ins, outs, sems):
        pass

    def wait(self, ins, outs, sems):
        n = len(self.args)
        for cp in self._copies(ins, outs, sems, True):
            cp.wait_recv()
        own = self._copies(ins, outs, sems, False)
        for cp in own[n:]:
            cp.wait_send()
        for cp in own[:n]:
            cp.wait()


def _remote(src, dst, send_sem, recv_sem, peer):
    return pltpu.make_async_remote_copy(src_ref=src, dst_ref=dst, send_sem=send_sem, recv_sem=recv_sem,
                                        device_id=peer, device_id_type=_MESH)


class _Gather:
    def __init__(self, shards):
        self.args = list(shards)
        n = len(self.args)
        self.out_shape = [jax.ShapeDtypeStruct((N_DEV,) + a.shape, a.dtype) for a in shards]
        self.scratch = [pltpu.SemaphoreType.DMA((n, N_DEV)), pltpu.SemaphoreType.DMA((n, N_DEV)),
                        pltpu.SemaphoreType.DMA((n,))]

    def _own(self, ins, outs, sems):
        send, recv, loc = sems
        _, me = _my_position()
        local = [pltpu.make_async_copy(ins[w], outs[w].at[me], loc.at[w]) for w in range(len(ins))]
        first = [_remote(ins[w], outs[w].at[me], send.at[w, k], recv.at[w, k], _peer(k)[0])
                 for k in (1, 2, 4, 6) for w in range(len(ins))]
        return local, first

    def _passed_on(self, ins, outs, sems):
        send, recv, _ = sems
        sibling, _ = _peer(1)
        return [_remote(outs[w].at[_peer(k)[1]], outs[w].at[_peer(k)[1]], send.at[w, k + 1], recv.at[w, k + 1], sibling)
                for k in (2, 4, 6) for w in range(len(ins))]

    def _arrival(self, ins, outs, sems, k, w):
        send, recv, _ = sems
        peer, pos = _peer(k)
        return _remote(ins[w], outs[w].at[pos], send.at[w, k], recv.at[w, k], peer)

    def start(self, ins, outs, sems):
        local, first = self._own(ins, outs, sems)
        for cp in local + first:
            cp.start()

    def mid(self, ins, outs, sems):
        for k in (2, 4, 6):
            for w in range(len(ins)):
                self._arrival(ins, outs, sems, k, w).wait_recv()
        for cp in self._passed_on(ins, outs, sems):
            cp.start()

    def wait(self, ins, outs, sems):
        for k in (1, 3, 5, 7):
            for w in range(len(ins)):
                self._arrival(ins, outs, sems, k, w).wait_recv()
        local, first = self._own(ins, outs, sems)
        for cp in first + self._passed_on(ins, outs, sems):
            cp.wait_send()
        for cp in local:
            cp.wait()


class _PairExchange:
    def __init__(self, fulls):
        self.args = list(fulls)
        n = len(self.args)
        self.out_shape = [jax.ShapeDtypeStruct((N_DEV // 2,) + a.shape[1:], a.dtype) for a in fulls]
        self.scratch = [pltpu.SemaphoreType.DMA((n, N_DEV // 2)), pltpu.SemaphoreType.DMA((n, N_DEV // 2))]

    def _copies(self, ins, outs, sems):
        send, recv = sems
        (x, y, c), _ = _my_position()
        sibling, _ = _peer(1)
        return [_remote(ins[w].at[2 * q + 1 - c], outs[w].at[q], send.at[w, q], recv.at[w, q], sibling)
                for q in range(N_DEV // 2) for w in range(len(ins))]

    def start(self, ins, outs, sems):
        for cp in self._copies(ins, outs, sems):
            cp.start()

    def mid(self, ins, outs, sems):
        pass

    def wait(self, ins, outs, sems):
        for cp in self._copies(ins, outs, sems):
            cp.wait()


class _ChipScatter:
    def __init__(self, sums):
        self.args = list(sums)
        n = len(self.args)
        self.out_shape = [jax.ShapeDtypeStruct(a.shape, a.dtype) for a in sums]
        self.scratch = [pltpu.SemaphoreType.DMA((n, N_DEV // 2)), pltpu.SemaphoreType.DMA((n, N_DEV // 2)),
                        pltpu.SemaphoreType.DMA((n,))]

    def _copies(self, ins, outs, sems, arrivals):
        send, recv, loc = sems
        (x, y, c), _ = _my_position()
        my_chip = 2 * x + y
        copies = []
        if not arrivals:
            copies = [pltpu.make_async_copy(ins[w].at[my_chip], outs[w].at[my_chip], loc.at[w]) for w in range(len(ins))]
        for k in (1, 2, 3):
            peer = (x ^ (k >> 1), y ^ (k & 1), c)
            chip = my_chip ^ k
            for w in range(len(ins)):
                copies.append(_remote(ins[w].at[my_chip if arrivals else chip], outs[w].at[chip if arrivals else my_chip],
                                      send.at[w, k], recv.at[w, k], peer))
        return copies

    def start(self, ins, outs, sems):
        for cp in self._copies(ins, outs, sems, False):
            cp.start()

    def mid(self, ins, outs, sems):
        pass

    def wait(self, ins, outs, sems):
        n = len(ins)
        for cp in self._copies(ins, outs, sems, True):
            cp.wait_recv()
        own = self._copies(ins, outs, sems, False)
        for cp in own[n:]:
            cp.wait_send()
        for cp in own[:n]:
            cp.wait()


def _split(refs, sizes):
    out, off = [], 0
    for size in sizes:
        out.append(refs[off:off + size])
        off += size
    return out


def _call(body, name, grid, args, in_specs, out_shape, out_specs, scratch=(), exchanges=()):
    sem = ("arbitrary",) * len(grid)
    exchanges = [e for e in exchanges if e is not None]
    if not exchanges:
        outs = pl.pallas_call(body, name=name, grid=grid, in_specs=in_specs, out_specs=out_specs, out_shape=out_shape,
                              scratch_shapes=list(scratch), compiler_params=_params(sem))(*args)
        return outs, []
    n_in, n_out, n_scr = len(args), len(out_shape), len(scratch)
    ex_in = [len(e.args) for e in exchanges]
    ex_out = [len(e.out_shape) for e in exchanges]
    ex_scr = [len(e.scratch) for e in exchanges]
    steps = math.prod(grid)
    mid_step = min((3 * steps) // 4, steps - 1)

    def hosted(*refs):
        main_in, xin, main_out, xout, main_scr, xscr = _split(
            refs, [n_in, sum(ex_in), n_out, sum(ex_out), n_scr, sum(ex_scr)])
        parts = list(zip(exchanges, _split(xin, ex_in), _split(xout, ex_out), _split(xscr, ex_scr)))
        step = pl.program_id(0)
        for a in range(1, len(grid)):
            step = step * grid[a] + pl.program_id(a)

        @pl.when(step == 0)
        def _():
            for e, ins, outs, sems in parts:
                e.start(ins, outs, sems)

        body(*main_in, *main_out, *main_scr)

        @pl.when(step == mid_step)
        def _():
            for e, ins, outs, sems in parts:
                e.mid(ins, outs, sems)

        @pl.when(step == steps - 1)
        def _():
            for e, ins, outs, sems in parts:
                e.wait(ins, outs, sems)

    any_spec = pl.BlockSpec(memory_space=pl.ANY)
    outs = pl.pallas_call(
        hosted, name=name, grid=grid, in_specs=list(in_specs) + [any_spec] * sum(ex_in),
        out_specs=list(out_specs) + [any_spec] * sum(ex_out),
        out_shape=list(out_shape) + [s for e in exchanges for s in e.out_shape],
        scratch_shapes=list(scratch) + [s for e in exchanges for s in e.scratch],
        compiler_params=_params(sem))(*args, *[a for e in exchanges for a in e.args])
    return outs[:n_out], _split(outs[n_out:], ex_out)


def _exchange_only(exchanges, name):
    ex_in = [len(e.args) for e in exchanges]
    ex_out = [len(e.out_shape) for e in exchanges]
    ex_scr = [len(e.scratch) for e in exchanges]

    def body(*refs):
        xin, xout, xscr = _split(refs, [sum(ex_in), sum(ex_out), sum(ex_scr)])
        parts = list(zip(exchanges, _split(xin, ex_in), _split(xout, ex_out), _split(xscr, ex_scr)))
        for phase in ("start", "mid", "wait"):
            for e, ins, outs, sems in parts:
                getattr(e, phase)(ins, outs, sems)

    any_spec = pl.BlockSpec(memory_space=pl.ANY)
    outs = pl.pallas_call(
        body, name=name, in_specs=[any_spec] * sum(ex_in), out_specs=[any_spec] * sum(ex_out),
        out_shape=[s for e in exchanges for s in e.out_shape],
        scratch_shapes=[s for e in exchanges for s in e.scratch])(*[a for e in exchanges for a in e.args])
    return _split(outs, ex_out)


def _tile_specs(t, d, di):
    row = lambda i: (i, 0)
    col = lambda i: (0, i)
    return dict(
        xd=pl.BlockSpec((TM, d), row), xi=pl.BlockSpec((TM, di), row),
        td=pl.BlockSpec((d, TM), col), ti=pl.BlockSpec((di, TM), col),
        s_xd=jax.ShapeDtypeStruct((t, d), F32), s_xi=jax.ShapeDtypeStruct((t, di), BF16),
        s_td=jax.ShapeDtypeStruct((d, t), BF16), s_ti=jax.ShapeDtypeStruct((di, t), BF16))


def _fwd_a(x, win, wout, gain, bias, wc2, bs2, lng, lnb, target=None, exchanges=()):
    t, d = x.shape
    di = wout.shape[0]
    gd = di // A_GROUPS
    n_loss = 0 if target is None else 1

    def body(*refs):
        x_ref, win_ref, wout_ref, gain_ref, bias_ref, wc_ref, bs_ref, lng_ref, lnb_ref = refs[:9]
        (a1_ref, a2_ref, a3_ref, vn_ref, vh_ref, rg_ref, yt_ref, xt_ref, pre_ref,
         xn_ref) = refs[9 + n_loss:19 + n_loss]
        p_scr, vg_scr, y_scr = refs[19 + 2 * n_loss:]
        xv = x_ref[...]
        xt_ref[...] = xv.T.astype(BF16)
        _in_proj(xv.astype(BF16), win_ref, p_scr)
        s1 = jnp.zeros((TM, 1), F32)
        for c in range(di // CW):
            sl = slice(c * CW, (c + 1) * CW)
            pv = slice(di + c * CW, di + (c + 1) * CW)
            vg, dvg = _gelu_and_grad(p_scr[:, pv])
            vg_scr[:, sl] = vg
            p_scr[:, pv] = dvg
            s1 += jnp.sum(vg, axis=1, keepdims=True)
        mu = s1 * (1.0 / di)
        s2 = jnp.zeros((TM, 1), F32)
        for c in range(di // CW):
            dlt = vg_scr[:, c * CW:(c + 1) * CW] - mu
            s2 += jnp.sum(dlt * dlt, axis=1, keepdims=True)
        rstd = lax.rsqrt(s2 * (1.0 / di) + LN_EPS)
        for c in range(di // CW):
            sl = slice(c * CW, (c + 1) * CW)
            vh = (vg_scr[:, sl] - mu) * rstd
            vh_ref[:, sl] = vh.astype(BF16)
            vn_ref[:, sl] = (vh * gain_ref[:, sl] + bias_ref[:, sl]).astype(BF16)
            rg_ref[:, sl] = (p_scr[:, di + c * CW:di + (c + 1) * CW] * rstd).astype(BF16)
        for g in range(A_GROUPS):
            sl = slice(g * gd, (g + 1) * gd)
            sv = jnp.dot(wc_ref[g], vn_ref[:, sl], preferred_element_type=F32) + bs_ref[g]
            u, du = _gelu_and_grad(p_scr[:, sl])
            s, ds = _silu_and_grad(p_scr[:, 2 * di + g * gd:2 * di + (g + 1) * gd])
            us = u * s
            a1_ref[:, sl] = (s * du).astype(BF16)
            a2_ref[:, sl] = (u * ds).astype(BF16)
            a3_ref[:, sl] = us.astype(BF16)
            y = us * sv
            y_scr[:, sl] = y.astype(BF16)
            yt_ref[sl, :] = y.T.astype(BF16)
        out = jnp.dot(y_scr[...], wout_ref[...], preferred_element_type=F32)
        _post_norm(xv, out, lng_ref, lnb_ref, pre_ref, xn_ref)
        if n_loss:
            t_ref, sq_ref = refs[9], refs[20]
            _zero_at_first_step(sq_ref)
            diff = xn_ref[...] - t_ref[...]
            xn_ref[...] = diff * (1.0 / d)
            sq_ref[...] += _fold8(diff * diff)

    sp = _tile_specs(t, d, di)
    loss_in = [] if target is None else [target]
    return _call(
        body, "fwd_a", (t // TM,), (x, win, wout, gain, bias, wc2, bs2, lng, lnb, *loss_in),
        in_specs=[sp["xd"]] + [_vmem()] * 8 + [sp["xd"]] * n_loss,
        out_specs=([sp["xi"]] * 6 + [sp["ti"], sp["td"], sp["xd"], sp["xd"]]
                   + [pl.BlockSpec((8, d), lambda i: (0, 0))] * n_loss),
        out_shape=([sp["s_xi"]] * 6 + [sp["s_ti"], sp["s_td"], sp["s_xd"], sp["s_xd"]]
                   + [jax.ShapeDtypeStruct((8, d), F32)] * n_loss),
        scratch=[pltpu.VMEM((TM, 3 * di), F32), pltpu.VMEM((TM, di), F32), pltpu.VMEM((TM, di), BF16)],
        exchanges=exchanges)


def _bwd_a(g, pre, a1, a2, a3, vn, vh, rg, wout, gain, wc2, wc2t, bs2, lng, exchanges=()):
    t, d = g.shape
    di = wout.shape[0]
    gd = di // A_GROUPS

    def body(g_ref, pre_ref, a1_ref, a2_ref, a3_ref, vn_ref, vh_ref, rg_ref,
             wout_ref, gain_ref, wc_ref, wct_ref, bs_ref, lng_ref,
             dpre_ref, dp_ref, dlng_ref, dlnb_ref, dgain_ref, dbias_ref, dbs_ref, dwc_ref,
             dy_scr, dv_scr):
        _zero_at_first_step(dlng_ref, dlnb_ref, dgain_ref, dbias_ref, dbs_ref, dwc_ref)
        dpre = _post_norm_bwd(g_ref, pre_ref, lng_ref, dpre_ref, dlng_ref, dlnb_ref)
        dy_scr[...] = lax.dot_general(dpre.astype(BF16), wout_ref[...], _NT, preferred_element_type=F32)
        for grp in range(A_GROUPS):
            sl = slice(grp * gd, (grp + 1) * gd)
            vn_g = vn_ref[:, sl]
            sv = jnp.dot(wc_ref[grp], vn_g, preferred_element_type=F32) + bs_ref[grp]
            dy = dy_scr[:, sl]
            dys = dy * sv
            dp_ref[:, sl] = (dys * _f32(a1_ref, sl)).astype(BF16)
            dp_ref[:, 2 * di + grp * gd:2 * di + (grp + 1) * gd] = (dys * _f32(a2_ref, sl)).astype(BF16)
            dsv = dy * _f32(a3_ref, sl)
            dbs_ref[grp] += jnp.sum(dsv, axis=1, keepdims=True)
            dsvb = dsv.astype(BF16)
            dwc_ref[grp] += lax.dot_general(dsvb, vn_g, _NT, preferred_element_type=F32)
            dv_scr[:, sl] = jnp.dot(wct_ref[grp], dsvb, preferred_element_type=F32)
        r1 = jnp.zeros((TM, 1), F32)
        r2 = jnp.zeros((TM, 1), F32)
        for c in range(di // CW):
            sl = slice(c * CW, (c + 1) * CW)
            dv = dv_scr[:, sl]
            vhat = _f32(vh_ref, sl)
            dgain_ref[:, sl] += _fold8(dv * vhat)
            dbias_ref[:, sl] += _fold8(dv)
            dvh = dv * gain_ref[:, sl]
            dv_scr[:, sl] = dvh
            r1 += jnp.sum(dvh, axis=1, keepdims=True)
            r2 += jnp.sum(dvh * vhat, axis=1, keepdims=True)
        m1 = r1 * (1.0 / di)
        m2 = r2 * (1.0 / di)
        for c in range(di // CW):
            sl = slice(c * CW, (c + 1) * CW)
            dp_ref[:, di + c * CW:di + (c + 1) * CW] = (
                (dv_scr[:, sl] - m1 - _f32(vh_ref, sl) * m2) * _f32(rg_ref, sl)).astype(BF16)

    sp = _tile_specs(t, d, di)
    const2 = lambda i: (0, 0)
    const3 = lambda i: (0, 0, 0)
    return _call(
        body, "bwd_a", (t // TM,), (g, pre, a1, a2, a3, vn, vh, rg, wout, gain, wc2, wc2t, bs2, lng),
        in_specs=[sp["xd"], sp["xd"]] + [sp["xi"]] * 6 + [_vmem()] * 6,
        out_specs=[sp["xd"], pl.BlockSpec((TM, 3 * di), lambda i: (i, 0)),
                   pl.BlockSpec((8, d), const2), pl.BlockSpec((8, d), const2),
                   pl.BlockSpec((8, di), const2), pl.BlockSpec((8, di), const2),
                   pl.BlockSpec((A_GROUPS, TM, 1), const3), pl.BlockSpec((A_GROUPS, TM, TM), const3)],
        out_shape=[sp["s_xd"], jax.ShapeDtypeStruct((t, 3 * di), BF16),
                   jax.ShapeDtypeStruct((8, d), F32), jax.ShapeDtypeStruct((8, d), F32),
                   jax.ShapeDtypeStruct((8, di), F32), jax.ShapeDtypeStruct((8, di), F32),
                   jax.ShapeDtypeStruct((A_GROUPS, TM, 1), F32), jax.ShapeDtypeStruct((A_GROUPS, TM, TM), F32)],
        scratch=[pltpu.VMEM((TM, di), F32), pltpu.VMEM((TM, di), F32)],
        exchanges=exchanges)


def _inv_count(tile, window, rows=TM):
    pos = tile * rows + lax.broadcasted_iota(jnp.int32, (rows, 1), 0)
    return 1.0 / jnp.minimum(pos + 1, window).astype(F32)


def _window_sum(ext, window, down):
    rows = ext.shape[0]
    k = 1
    while k < window:
        ext = ext + pltpu.roll(ext, k if down else rows - k, 0)
        k *= 2
    return ext


def _fwd_b(x, win, wgrp, scale, wout, lng, lnb, exchanges=()):
    t, d = x.shape
    di = wout.shape[0]
    gd = di // len(POOL_WINDOWS)

    def body(x_ref, win_ref, wgrp_ref, scale_ref, wout_ref, lng_ref, lnb_ref,
             b1_ref, b2_ref, b3_ref, poolt_ref, yt_ref, xt_ref, pre_ref, xn_ref, p_scr, ext_scr, y_scr):
        i = pl.program_id(0)

        @pl.when(i == 0)
        def _():
            ext_scr[0:HALO, :] = jnp.zeros((HALO, di), F32)

        xv = x_ref[...]
        xt_ref[...] = xv.T.astype(BF16)
        _in_proj(xv.astype(BF16), win_ref, p_scr)
        ext_scr[HALO:, :] = p_scr[:, :di]
        for grp, window in enumerate(POOL_WINDOWS):
            sl = slice(grp * gd, (grp + 1) * gd)
            ext = ext_scr[:, sl]
            pooled = (_window_sum(ext, window, True)[HALO:] * _inv_count(i, window) - ext[HALO:]).astype(BF16)
            poolt_ref[sl, :] = pooled.astype(F32).T.astype(BF16)
            mixed = jnp.dot(pooled, wgrp_ref[grp], preferred_element_type=F32)
            s, ds = _silu_and_grad(p_scr[:, di + grp * gd:di + (grp + 1) * gd])
            sc = scale_ref[:, sl]
            ms = mixed * s
            b1_ref[:, sl] = (mixed * sc * ds).astype(BF16)
            b2_ref[:, sl] = ms.astype(BF16)
            b3_ref[:, sl] = (sc * s).astype(BF16)
            y = ms * sc
            y_scr[:, sl] = y.astype(BF16)
            yt_ref[sl, :] = y.T.astype(BF16)
        ext_scr[0:HALO, :] = ext_scr[TM:TM + HALO, :]
        out = jnp.dot(y_scr[...], wout_ref[...], preferred_element_type=F32)
        _post_norm(xv, out, lng_ref, lnb_ref, pre_ref, xn_ref)

    sp = _tile_specs(t, d, di)
    return _call(
        body, "fwd_b", (t // TM,), (x, win, wgrp, scale, wout, lng, lnb),
        in_specs=[sp["xd"]] + [_vmem()] * 6,
        out_specs=[sp["xi"]] * 3 + [sp["ti"], sp["ti"], sp["td"], sp["xd"], sp["xd"]],
        out_shape=[sp["s_xi"]] * 3 + [sp["s_ti"], sp["s_ti"], sp["s_td"], sp["s_xd"], sp["s_xd"]],
        scratch=[pltpu.VMEM((TM, 2 * di), F32), pltpu.VMEM((TM + HALO, di), F32), pltpu.VMEM((TM, di), BF16)],
        exchanges=exchanges)


def _bwd_b(g, pre, b1, b2, b3, wout, wgrp, lng, exchanges=()):
    t, d = g.shape
    di = wout.shape[0]
    gd = di // len(POOL_WINDOWS)
    nt = t // TM

    def body(g_ref, pre_ref, b1_ref, b2_ref, b3_ref, wout_ref, wgrp_ref, lng_ref,
             dpre_ref, dp_ref, dmix_ref, dlng_ref, dlnb_ref, dscale_ref, dy_scr, ext_scr):
        i = pl.program_id(0)
        tile = nt - 1 - i
        _zero_at_first_step(dlng_ref, dlnb_ref, dscale_ref)

        @pl.when(i == 0)
        def _():
            ext_scr[TM:, :] = jnp.zeros((HALO, di), F32)

        dpre = _post_norm_bwd(g_ref, pre_ref, lng_ref, dpre_ref, dlng_ref, dlnb_ref)
        dy_scr[...] = lax.dot_general(dpre.astype(BF16), wout_ref[...], _NT, preferred_element_type=F32)
        for grp, window in enumerate(POOL_WINDOWS):
            sl = slice(grp * gd, (grp + 1) * gd)
            dy = dy_scr[:, sl]
            dp_ref[:, di + grp * gd:di + (grp + 1) * gd] = (dy * _f32(b1_ref, sl)).astype(BF16)
            dscale_ref[:, sl] += _fold8(dy * _f32(b2_ref, sl))
            dmixed = (dy * _f32(b3_ref, sl)).astype(BF16)
            dmix_ref[:, sl] = dmixed
            dpooled = lax.dot_general(dmixed, wgrp_ref[grp], _NT, preferred_element_type=F32)
            ext_scr[0:TM, sl] = dpooled * _inv_count(tile, window)
            dv = _window_sum(ext_scr[:, sl], window, False)[0:TM] - dpooled
            dp_ref[:, sl] = dv.astype(BF16)
        ext_scr[TM:, :] = ext_scr[0:HALO, :]

    rrow = lambda i: (nt - 1 - i, 0)
    const2 = lambda i: (0, 0)
    xd, xi = pl.BlockSpec((TM, d), rrow), pl.BlockSpec((TM, di), rrow)
    return _call(
        body, "bwd_b", (nt,), (g, pre, b1, b2, b3, wout, wgrp, lng),
        in_specs=[xd, xd, xi, xi, xi, _vmem(), _vmem(), _vmem()],
        out_specs=[xd, pl.BlockSpec((TM, 2 * di), rrow), xi,
                   pl.BlockSpec((8, d), const2), pl.BlockSpec((8, d), const2), pl.BlockSpec((8, di), const2)],
        out_shape=[jax.ShapeDtypeStruct((t, d), F32), jax.ShapeDtypeStruct((t, 2 * di), BF16),
                   jax.ShapeDtypeStruct((t, di), BF16),
                   jax.ShapeDtypeStruct((8, d), F32), jax.ShapeDtypeStruct((8, d), F32),
                   jax.ShapeDtypeStruct((8, di), F32)],
        scratch=[pltpu.VMEM((TM, di), F32), pltpu.VMEM((TM + HALO, di), F32)],
        exchanges=exchanges)


def _fwd_c(x, win, convw, wout, lng, lnb, exchanges=()):
    t, d = x.shape
    di = wout.shape[0]

    def body(x_ref, win_ref, cw_ref, wout_ref, lng_ref, lnb_ref,
             c1_ref, c2_ref, c3_ref, cg_ref, hg_ref, yt_ref, xt_ref, pre_ref, xn_ref, p_scr, ext_scr, y_scr):
        i = pl.program_id(0)

        @pl.when(i == 0)
        def _():
            ext_scr[0:CHALO, :] = jnp.zeros((CHALO, di), F32)

        xv = x_ref[...]
        xt_ref[...] = xv.T.astype(BF16)
        _in_proj(xv.astype(BF16), win_ref, p_scr)
        for c in range(di // CW):
            sl = slice(c * CW, (c + 1) * CW)
            bb = p_scr[:, sl]
            cc = p_scr[:, di + c * CW:di + (c + 1) * CW]
            hh = p_scr[:, 2 * di + c * CW:2 * di + (c + 1) * CW]
            s, ds = _silu_and_grad(p_scr[:, 3 * di + c * CW:3 * di + (c + 1) * CW])
            ext_scr[CHALO:, sl] = cc * hh
            ext = ext_scr[:, sl]
            conv = (pltpu.roll(ext, 2, 0)[CHALO:] * cw_ref[0:1, sl] + pltpu.roll(ext, 1, 0)[CHALO:] * cw_ref[1:2, sl]
                    + ext[CHALO:] * cw_ref[2:3, sl])
            cs = conv * s
            c1_ref[:, sl] = cs.astype(BF16)
            c2_ref[:, sl] = (bb * conv * ds).astype(BF16)
            c3_ref[:, sl] = (bb * s).astype(BF16)
            cg_ref[:, sl] = cc.astype(BF16)
            hg_ref[:, sl] = hh.astype(BF16)
            y = bb * cs
            y_scr[:, sl] = y.astype(BF16)
            yt_ref[sl, :] = y.T.astype(BF16)
        ext_scr[0:CHALO, :] = ext_scr[TM:TM + CHALO, :]
        out = jnp.dot(y_scr[...], wout_ref[...], preferred_element_type=F32)
        _post_norm(xv, out, lng_ref, lnb_ref, pre_ref, xn_ref)

    sp = _tile_specs(t, d, di)
    return _call(
        body, "fwd_c", (t // TM,), (x, win, convw, wout, lng, lnb),
        in_specs=[sp["xd"]] + [_vmem()] * 5,
        out_specs=[sp["xi"]] * 5 + [sp["ti"], sp["td"], sp["xd"], sp["xd"]],
        out_shape=[sp["s_xi"]] * 5 + [sp["s_ti"], sp["s_td"], sp["s_xd"], sp["s_xd"]],
        scratch=[pltpu.VMEM((TM, 4 * di), F32), pltpu.VMEM((TM + CHALO, di), F32), pltpu.VMEM((TM, di), BF16)],
        exchanges=exchanges)


def _bwd_c(g, pre, c1, c2, c3, cg, hg, wout, convw, lng, exchanges=()):
    t, d = g.shape
    di = wout.shape[0]
    nt = t // TM

    def body(g_ref, pre_ref, c1_ref, c2_ref, c3_ref, cg_ref, hg_ref, wout_ref, cw_ref, lng_ref,
             dpre_ref, dp_ref, dlng_ref, dlnb_ref, dcw_ref, dy_scr, ext_scr):
        i = pl.program_id(0)
        _zero_at_first_step(dlng_ref, dlnb_ref, dcw_ref)

        @pl.when(i == 0)
        def _():
            ext_scr[TM:, :] = jnp.zeros((CHALO, di), F32)

        dpre = _post_norm_bwd(g_ref, pre_ref, lng_ref, dpre_ref, dlng_ref, dlnb_ref)
        dy_scr[...] = lax.dot_general(dpre.astype(BF16), wout_ref[...], _NT, preferred_element_type=F32)
        rows = TM + CHALO
        for c in range(di // CW):
            sl = slice(c * CW, (c + 1) * CW)
            dy = dy_scr[:, sl]
            cc = _f32(cg_ref, sl)
            hh = _f32(hg_ref, sl)
            dp_ref[:, sl] = (dy * _f32(c1_ref, sl)).astype(BF16)
            dp_ref[:, 3 * di + c * CW:3 * di + (c + 1) * CW] = (dy * _f32(c2_ref, sl)).astype(BF16)
            dconv = dy * _f32(c3_ref, sl)
            ext_scr[0:TM, sl] = dconv
            ext = ext_scr[:, sl]
            d1 = pltpu.roll(ext, rows - 1, 0)[0:TM]
            d2 = pltpu.roll(ext, rows - 2, 0)[0:TM]
            dq = dconv * cw_ref[2:3, sl] + d1 * cw_ref[1:2, sl] + d2 * cw_ref[0:1, sl]
            q = cc * hh
            dcw_ref[0, :, sl] += _fold8(q * d2)
            dcw_ref[1, :, sl] += _fold8(q * d1)
            dcw_ref[2, :, sl] += _fold8(q * dconv)
            dp_ref[:, di + c * CW:di + (c + 1) * CW] = (dq * hh).astype(BF16)
            dp_ref[:, 2 * di + c * CW:2 * di + (c + 1) * CW] = (dq * cc).astype(BF16)
        ext_scr[TM:, :] = ext_scr[0:CHALO, :]

    rrow = lambda i: (nt - 1 - i, 0)
    const2 = lambda i: (0, 0)
    xd, xi = pl.BlockSpec((TM, d), rrow), pl.BlockSpec((TM, di), rrow)
    return _call(
        body, "bwd_c", (nt,), (g, pre, c1, c2, c3, cg, hg, wout, convw, lng),
        in_specs=[xd, xd] + [xi] * 5 + [_vmem()] * 3,
        out_specs=[xd, pl.BlockSpec((TM, 4 * di), rrow),
                   pl.BlockSpec((8, d), const2), pl.BlockSpec((8, d), const2),
                   pl.BlockSpec((3, 8, di), lambda i: (0, 0, 0))],
        out_shape=[jax.ShapeDtypeStruct((t, d), F32), jax.ShapeDtypeStruct((t, 4 * di), BF16),
                   jax.ShapeDtypeStruct((8, d), F32), jax.ShapeDtypeStruct((8, d), F32),
                   jax.ShapeDtypeStruct((3, 8, di), F32)],
        scratch=[pltpu.VMEM((TM, di), F32), pltpu.VMEM((TM + CHALO, di), F32)],
        exchanges=exchanges)


def _dx(dp, dpre, win, exchanges=()):
    t, d = dpre.shape
    n = dp.shape[1]
    cs = win.shape[2]

    def body(dp_ref, dpre_ref, win_ref, dx_ref):
        acc = ALPHA * dpre_ref[...]
        for j in range(N_DEV):
            acc += lax.dot_general(dp_ref[:, j * cs:(j + 1) * cs], win_ref[j], _NT, preferred_element_type=F32)
        dx_ref[...] = acc

    row = lambda i: (i, 0)
    (dx,), ex = _call(
        body, "dx", (t // TM,), (dp, dpre, win),
        in_specs=[pl.BlockSpec((TM, n), row), pl.BlockSpec((TM, d), row), _vmem()],
        out_specs=[pl.BlockSpec((TM, d), row)],
        out_shape=[jax.ShapeDtypeStruct((t, d), F32)],
        exchanges=exchanges)
    return dx, ex


def _wgrad(at, b, nb, per_block_rows, name, exchanges=()):
    m_all, t = at.shape
    tn = b.shape[1] // nb
    m = m_all // nb if per_block_rows else m_all
    tk = min(WGRAD_TK, t)
    nk = t // tk

    def body(at_ref, b_ref, out_ref, acc):
        k = pl.program_id(1)

        @pl.when(k == 0)
        def _():
            acc[...] = jnp.zeros(acc.shape, F32)

        acc[...] += jnp.dot(at_ref[...], b_ref[...].astype(BF16), preferred_element_type=F32)

        @pl.when(k == nk - 1)
        def _():
            out_ref[0] = acc[...].astype(BF16)

    at_map = (lambda j, k: (j, k)) if per_block_rows else (lambda j, k: (0, k))
    (out,), ex = _call(
        body, name, (nb, nk), (at, b),
        in_specs=[pl.BlockSpec((m, tk), at_map), pl.BlockSpec((tk, tn), lambda j, k: (k, j))],
        out_specs=[pl.BlockSpec((1, m, tn), lambda j, k: (j, 0, 0))],
        out_shape=[jax.ShapeDtypeStruct((nb, m, tn), BF16)],
        scratch=[pltpu.VMEM((m, tn), F32)],
        exchanges=exchanges)
    return out, ex


ADAMW_BLOCK_BYTES = 6 * 1024 * 1024


def _sum_parts(parts_ref):
    g = parts_ref[0].astype(F32)
    for s in range(1, parts_ref.shape[0]):
        g = g + parts_ref[s].astype(F32)
    return g


def _row_tile(rows, bytes_per_row):
    if rows * bytes_per_row <= ADAMW_BLOCK_BYTES:
        return rows
    best = 8
    for cand in range(8, rows, 8):
        if rows % cand == 0 and cand * bytes_per_row <= ADAMW_BLOCK_BYTES:
            best = cand
    return best


def _pair_sum(full, theirs, core, name):
    shape = theirs.shape
    r, c = math.prod(shape[1:-1]), shape[-1]
    tr = _row_tile(r, c * 3 * full.dtype.itemsize)
    half = N_DEV // 2

    def body(core_ref, a_ref, b_ref, out_ref):
        out_ref[...] = (a_ref[...].astype(F32) + b_ref[...].astype(F32)).astype(out_ref.dtype)

    blk = pl.BlockSpec((None, tr, c), lambda q, i, core_ref: (q, i, 0))
    grid_spec = pltpu.PrefetchScalarGridSpec(
        num_scalar_prefetch=1, grid=(half, r // tr),
        in_specs=[pl.BlockSpec((None, None, tr, c), lambda q, i, core_ref: (q, core_ref[0], i, 0)), blk],
        out_specs=blk)
    return pl.pallas_call(
        body, name=name, grid_spec=grid_spec, out_shape=jax.ShapeDtypeStruct((half, r, c), full.dtype),
        compiler_params=_params(("arbitrary", "arbitrary")),
    )(core, full.reshape(half, 2, r, c), theirs.reshape(half, r, c)).reshape(shape)


def _adamw(parts, w, m, v, name):
    s, r, c = parts.shape
    tr = _row_tile(r, c * (s * parts.dtype.itemsize + 7 * 4))
    bc1 = 1.0 - ADAM_B1 ** ADAM_STEP
    bc2 = 1.0 - ADAM_B2 ** ADAM_STEP

    def body(parts_ref, w_ref, m_ref, v_ref, g_ref, d_ref, nm_ref, nv_ref):
        g = _sum_parts(parts_ref)
        g_ref[...] = g
        nm = ADAM_B1 * m_ref[...] + (1.0 - ADAM_B1) * g
        nv = ADAM_B2 * v_ref[...] + (1.0 - ADAM_B2) * (g * g)
        nm_ref[...] = nm
        nv_ref[...] = nv
        d_ref[...] = -ADAM_LR * ((nm / bc1) / (jnp.sqrt(nv / bc2) + ADAM_EPS) + ADAM_WD * w_ref[...])

    blk = pl.BlockSpec((tr, c), lambda i: (i, 0))
    return pl.pallas_call(
        body, name=name, grid=(r // tr,),
        in_specs=[pl.BlockSpec((s, tr, c), lambda i: (0, i, 0)), blk, blk, blk],
        out_specs=[blk, blk, blk, blk],
        out_shape=[jax.ShapeDtypeStruct((r, c), F32)] * 4,
        compiler_params=_params(("arbitrary",)),
    )(parts, w, m, v)


_LANES = 128


def _pack(arrays):
    flat = jnp.concatenate([a.reshape(-1) for a in arrays])
    pad = (-flat.shape[0]) % (8 * _LANES)
    return jnp.pad(flat, (0, pad)).reshape(-1, _LANES)


def _unpack(packed, shapes):
    flat = packed.reshape(-1)
    out, off = [], 0
    for shp in shapes:
        size = math.prod(shp)
        out.append(flat[off:off + size].reshape(shp))
        off += size
    return out


def _spatial_weights(w_s, b_s, rows):
    reps = rows // CHUNK
    tril = jnp.tril(jnp.ones((CHUNK, CHUNK), F32))
    wc = w_s * tril
    eye = jnp.eye(reps, dtype=F32)
    wc2 = jnp.einsum("ab,gts->gatbs", eye, wc).reshape(A_GROUPS, rows, rows)
    bs2 = jnp.tile(b_s, (1, reps)).reshape(A_GROUPS, rows, 1)
    return wc2.astype(BF16), jnp.swapaxes(wc2, 1, 2).astype(BF16), bs2


def _spatial_weight_grad(dwc2, dbs2):
    reps = TM // CHUNK
    tril = jnp.tril(jnp.ones((CHUNK, CHUNK), F32))
    blocks = dwc2.reshape(A_GROUPS, reps, CHUNK, reps, CHUNK)
    dws = sum(blocks[:, a, :, a, :] for a in range(reps)) * tril
    dbs = dbs2.reshape(A_GROUPS, reps, CHUNK).sum(axis=1)
    return dws, dbs


def _row2(a):
    return a.reshape(1, -1)


def kernel(x, a0_w_in, a0_v_gain, a0_v_bias, a0_w_s, a0_b_s, a0_w_out, ln0_gain, ln0_bias, b1_w_in, b1_w_grp, b1_scale, b1_w_out, ln1_gain, ln1_bias, c2_w_in, c2_conv_w, c2_w_out, ln2_gain, ln2_bias, a3_w_in, a3_v_gain, a3_v_bias, a3_w_s, a3_b_s, a3_w_out, ln3_gain, ln3_bias, loss_target, m_a0_w_in, m_a0_v_gain, m_a0_v_bias, m_a0_w_s, m_a0_b_s, m_a0_w_out, m_ln0_gain, m_ln0_bias, m_b1_w_in, m_b1_w_grp, m_b1_scale, m_b1_w_out, m_ln1_gain, m_ln1_bias, m_c2_w_in, m_c2_conv_w, m_c2_w_out, m_ln2_gain, m_ln2_bias, m_a3_w_in, m_a3_v_gain, m_a3_v_bias, m_a3_w_s, m_a3_b_s, m_a3_w_out, m_ln3_gain, m_ln3_bias, v_a0_w_in, v_a0_v_gain, v_a0_v_bias, v_a0_w_s, v_a0_b_s, v_a0_w_out, v_ln0_gain, v_ln0_bias, v_b1_w_in, v_b1_w_grp, v_b1_scale, v_b1_w_out, v_ln1_gain, v_ln1_bias, v_c2_w_in, v_c2_conv_w, v_c2_w_out, v_ln2_gain, v_ln2_bias, v_a3_w_in, v_a3_v_gain, v_a3_v_bias, v_a3_w_s, v_a3_b_s, v_a3_w_out, v_ln3_gain, v_ln3_bias):
    names = ["a0_w_in", "a0_v_gain", "a0_v_bias", "a0_w_s", "a0_b_s", "a0_w_out", "ln0_gain", "ln0_bias",
             "b1_w_in", "b1_w_grp", "b1_scale", "b1_w_out", "ln1_gain", "ln1_bias",
             "c2_w_in", "c2_conv_w", "c2_w_out", "ln2_gain", "ln2_bias",
             "a3_w_in", "a3_v_gain", "a3_v_bias", "a3_w_s", "a3_b_s", "a3_w_out", "ln3_gain", "ln3_bias"]
    env = dict(locals())
    w = {nm: env[nm] for nm in names}
    mom = {nm: env["m_" + nm] for nm in names}
    var = {nm: env["v_" + nm] for nm in names}

    x0 = x[0]
    target = loss_target[0]
    d_model = x0.shape[1]
    di = N_DEV * a0_w_out.shape[0]
    n_grp = len(POOL_WINDOWS)
    gd_b = di // n_grp

    layers = ("a0", "b1", "c2", "a3")
    big_of = {"a0": ["a0_w_in", "a0_w_out"], "b1": ["b1_w_in", "b1_w_grp", "b1_w_out"],
              "c2": ["c2_w_in", "c2_w_out"], "a3": ["a3_w_in", "a3_w_out"]}
    big = [nm for p in layers for nm in big_of[p]]
    bucket_of = {"a0": ["a0_v_gain", "a0_v_bias", "a0_w_s", "a0_b_s", "ln0_gain", "ln0_bias"],
                 "b1": ["b1_scale", "ln1_gain", "ln1_bias"], "c2": ["ln2_gain", "ln2_bias"],
                 "a3": ["a3_v_gain", "a3_v_bias", "a3_w_s", "a3_b_s", "ln3_gain", "ln3_bias"]}
    conv_shape = c2_conv_w.shape
    spatial = {p: _spatial_weights(w[p + "_w_s"], w[p + "_b_s"], TM) for p in ("a0", "a3")}

    def weight_gather(p):
        return _Gather([w[nm].astype(BF16) for nm in big_of[p]])

    (first,) = _exchange_only([_Gather([w[nm].astype(BF16) for nm in big_of["a0"]] + [_pack([c2_conv_w])])],
                              "gather_first")
    gathered = dict(zip(big_of["a0"], first))
    conv_all = jnp.stack([_unpack(first[-1][j], [conv_shape])[0] for j in range(N_DEV)], axis=1)
    conv_full = conv_all.reshape(conv_shape[0], di)
    w_in = lambda p: gathered[p + "_w_in"]
    w_out = lambda p: gathered[p + "_w_out"].reshape(di, d_model)
    saved = {}
    h = x0
    for i, p in enumerate(layers):
        lng, lnb = _row2(w[f"ln{i}_gain"]), _row2(w[f"ln{i}_bias"])
        nxt = layers[i + 1] if i + 1 < len(layers) else None
        ex = [weight_gather(nxt)] if nxt else []
        if p[0] == "a":
            wc2, _, bs2 = spatial[p]
            outs, got = _fwd_a(h, w_in(p), w_out(p), _row2(w[p + "_v_gain"]), _row2(w[p + "_v_bias"]),
                               wc2, bs2, lng, lnb, target=None if nxt else target, exchanges=ex)
        elif p[0] == "b":
            wgrp = jnp.swapaxes(gathered["b1_w_grp"], 0, 1).reshape(n_grp, gd_b, gd_b)
            outs, got = _fwd_b(h, w_in(p), wgrp, _row2(w[p + "_scale"]), w_out(p), lng, lnb, exchanges=ex)
        else:
            outs, got = _fwd_c(h, w_in(p), conv_full, w_out(p), lng, lnb, exchanges=ex)
        if nxt:
            saved[p], h = outs[:-1], outs[-1]
            gathered.update(zip(big_of[nxt], got[0]))
        else:
            saved[p], gcur, sq = outs[:-2], outs[-2], outs[-1]

    loss = lax.psum(jnp.sum(sq) * (0.5 / d_model), ("x", "y", "c"))

    part, full, landed, small_all = {}, {}, {}, {}

    def bucket_gather(bucket):
        return _Gather([_pack([part[nm] for nm in bucket_of[bucket]])])

    core = lax.axis_index("c").astype(jnp.int32).reshape(1)

    def chip_sums(names, got):
        return [_pair_sum(full[nm], theirs, core, "pair_sum_" + nm) for nm, theirs in zip(names, got)]

    pending = None
    for i, p in reversed(list(enumerate(layers))):
        lng = _row2(w[f"ln{i}_gain"])
        ex = []
        if pending:
            ex = [_PairExchange([full[nm] for nm in big_of[pending]]), bucket_gather(pending)]
            if pending == "c2":
                ex.append(_Exchange(scatters=[full["c2_conv_w"]]))
        *factors, yt, xt, pre = saved[p]
        if p[0] == "a":
            wc2, wc2t, bs2 = spatial[p]
            (dpre, dp, dlng, dlnb, dgain, dbias, dbs2, dwc2), got = _bwd_a(
                gcur, pre, *factors, w_out(p), _row2(w[p + "_v_gain"]), wc2, wc2t, bs2, lng, exchanges=ex)
            part[p + "_v_gain"], part[p + "_v_bias"] = dgain.sum(axis=0), dbias.sum(axis=0)
            part[p + "_w_s"], part[p + "_b_s"] = _spatial_weight_grad(dwc2, dbs2)
        elif p[0] == "b":
            b1f, b2f, b3f, poolt = factors
            (dpre, dp, dmixed, dlng, dlnb, dscale), got = _bwd_b(
                gcur, pre, b1f, b2f, b3f, w_out(p), wgrp, lng, exchanges=ex)
            part[p + "_scale"] = dscale.sum(axis=0)
            dwg, _ = _wgrad(poolt, dmixed, n_grp, True, "wgrad_grp")
            full[p + "_w_grp"] = jnp.swapaxes(dwg.reshape(n_grp, N_DEV, gd_b // N_DEV, gd_b), 0, 1)
        else:
            (dpre, dp, dlng, dlnb, dcw), got = _bwd_c(gcur, pre, *factors, w_out(p), conv_full, lng, exchanges=ex)
            dconv = dcw.sum(axis=1).reshape(conv_shape[0], N_DEV, conv_shape[1])
            full["c2_conv_w"] = jnp.stack([_pack([dconv[:, j]]) for j in range(N_DEV)])
        part[f"ln{i}_gain"], part[f"ln{i}_bias"] = dlng.sum(axis=0), dlnb.sum(axis=0)
        if pending:
            small_all[pending] = got[1][0]
            if pending == "c2":
                small_all["conv"] = got[2][0]
        dwo, _ = _wgrad(yt, dpre, 1, False, "wgrad_out_" + p)
        full[p + "_w_out"] = dwo.reshape(N_DEV, di // N_DEV, d_model)
        ex = [_ChipScatter(chip_sums(big_of[pending], got[0]))] if pending else []
        if i == 0:
            ex += [bucket_gather(p)]
        full[p + "_w_in"], got = _wgrad(xt, dp, N_DEV, False, "wgrad_in_" + p, exchanges=ex)
        if pending:
            landed.update(zip(big_of[pending], got[0]))
        if i > 0:
            gcur, _ = _dx(dp, dpre, w_in(p))
            pending = p
        else:
            small_all[p] = got[-1][0]
            (theirs,) = _exchange_only([_PairExchange([full[p + "_w_in"]])], "pair_exchange_last")
            gcur, got = _dx(dp, dpre, w_in(p), exchanges=[
                _ChipScatter(chip_sums([p + "_w_in"], theirs)), _Exchange(scatters=[full[p + "_w_out"]])])
            landed[p + "_w_in"], landed[p + "_w_out"] = got[0][0], got[1][0]
    grad_x = gcur[None]

    grads, deltas, new_m, new_v = {}, {}, {}, {}
    for nm in big:
        shp = w[nm].shape
        r2 = (math.prod(shp[:-1]), shp[-1])
        outs = _adamw(landed[nm].reshape((-1,) + r2), w[nm].reshape(r2), mom[nm].reshape(r2), var[nm].reshape(r2),
                      "adamw_" + nm)
        grads[nm], deltas[nm], new_m[nm], new_v[nm] = (o.reshape(shp) for o in outs)
    buckets = dict(bucket_of, conv=["c2_conv_w"])
    for key, members in buckets.items():
        shapes = [w[nm].shape for nm in members]
        outs = _adamw(small_all[key], _pack([w[nm] for nm in members]), _pack([mom[nm] for nm in members]),
                      _pack([var[nm] for nm in members]), "adamw_small_" + key)
        for tgt, o in zip((grads, deltas, new_m, new_v), outs):
            tgt.update(zip(members, _unpack(o, shapes)))

    return (loss, grad_x, *[grads[nm] for nm in names], *[deltas[nm] for nm in names],
            *[new_m[nm] for nm in names], *[new_v[nm] for nm in names])
```

```python
import functools
import math

import jax
import jax.numpy as jnp
from jax import lax
from jax.experimental import pallas as pl
from jax.experimental.pallas import tpu as pltpu

F32 = jnp.float32
BF16 = jnp.bfloat16

N_DEV = 8
DEPTH = 4
CHUNK = 128
A_GROUPS = 8
POOL_WINDOWS = (2, 4, 8, 16)
LN_EPS = 1e-5
ALPHA = (2.0 * DEPTH) ** 0.25
ADAM_LR = 0.001
ADAM_B1 = 0.9
ADAM_B2 = 0.999
ADAM_EPS = 1e-08
ADAM_WD = 0.01
ADAM_STEP = 10

TM = 256
HALO = 16
CHALO = 8
CW = 512
WGRAD_TK = 2048
VMEM_LIMIT_BYTES = 58 * 1024 * 1024

_NT = (((1,), (1,)), ((), ()))
_SQRT_2_OVER_PI = math.sqrt(2.0 / math.pi)
_MESH = pl.DeviceIdType.MESH


def _vmem():
    return pl.BlockSpec(memory_space=pltpu.VMEM)


def _params(sem=None):
    return pltpu.CompilerParams(dimension_semantics=sem, vmem_limit_bytes=VMEM_LIMIT_BYTES)


def _gelu(x):
    t = jnp.tanh(_SQRT_2_OVER_PI * (x + 0.044715 * (x * x * x)))
    return x * (0.5 * (1.0 + t))


def _gelu_and_grad(x):
    x2 = x * x
    t = jnp.tanh(_SQRT_2_OVER_PI * (x + 0.044715 * (x * x2)))
    cdf = 0.5 * (1.0 + t)
    grad = cdf + 0.5 * x * (1.0 - t * t) * (_SQRT_2_OVER_PI * (1.0 + 3.0 * 0.044715 * x2))
    return x * cdf, grad


def _silu_and_grad(z):
    sg = 1.0 / (1.0 + jnp.exp(-z))
    return z * sg, sg * (1.0 + z * (1.0 - sg))


def _fold8(a):
    return a.reshape(a.shape[0] // 8, 8, a.shape[1]).sum(axis=0)


def _row_mean(a):
    return jnp.mean(a, axis=-1, keepdims=True)


def _ln_stats(x):
    mu = _row_mean(x)
    xc = x - mu
    rstd = lax.rsqrt(_row_mean(xc * xc) + LN_EPS)
    return xc * rstd, rstd


def _post_norm(x, out, lng_ref, lnb_ref, pre_ref, xn_ref):
    pre = ALPHA * x + out
    pre_ref[...] = pre
    xhat, _ = _ln_stats(pre)
    xn_ref[...] = xhat * lng_ref[...] + lnb_ref[...]


def _post_norm_bwd(g_ref, pre_ref, lng_ref, dpre_ref, dlng_ref, dlnb_ref):
    go = g_ref[...]
    xhat, rstd = _ln_stats(pre_ref[...])
    dlng_ref[...] += _fold8(go * xhat)
    dlnb_ref[...] += _fold8(go)
    dxh = go * lng_ref[...]
    dpre = rstd * (dxh - _row_mean(dxh) - xhat * _row_mean(dxh * xhat))
    dpre_ref[...] = dpre
    return dpre


def _in_proj(xb, win_ref, p_ref):
    cs = win_ref.shape[2]
    for j in range(N_DEV):
        p_ref[:, j * cs:(j + 1) * cs] = jnp.dot(xb, win_ref[j], preferred_element_type=F32)


def _zero_at_first_step(*refs):
    @pl.when(pl.program_id(0) == 0)
    def _():
        for r in refs:
            r[...] = jnp.zeros(r.shape, r.dtype)


def _f32(ref, sl):
    return ref[:, sl].astype(F32)


def _my_position():
    x, y, c = lax.axis_index("x"), lax.axis_index("y"), lax.axis_index("c")
    return (x, y, c), 4 * x + 2 * y + c


def _peer(k):
    (x, y, c), _ = _my_position()
    peer = (x ^ (k >> 2), y ^ ((k >> 1) & 1), c ^ (k & 1))
    return peer, 4 * peer[0] + 2 * peer[1] + peer[2]


class _Exchange:
    def __init__(self, gathers=(), scatters=()):
        self.args = list(gathers) + list(scatters)
        self.n_gather = len(gathers)
        self.out_shape = ([jax.ShapeDtypeStruct((N_DEV,) + a.shape, a.dtype) for a in gathers]
                          + [jax.ShapeDtypeStruct(a.shape, a.dtype) for a in scatters])
        n = len(self.args)
        self.scratch = [pltpu.SemaphoreType.DMA((n, N_DEV)), pltpu.SemaphoreType.DMA((n, N_DEV)),
                        pltpu.SemaphoreType.DMA((n,))]

    def _src(self, ins, w, pos):
        return ins[w] if w < self.n_gather else ins[w].at[pos]

    def _copies(self, ins, outs, sems, arrivals):
        send_sems, recv_sems, local_sems = sems
        _, me = _my_position()
        n = len(self.args)
        copies = []
        if not arrivals:
            copies = [pltpu.make_async_copy(self._src(ins, w, me), outs[w].at[me], local_sems.at[w]) for w in range(n)]
        for k in range(1, N_DEV):
            peer, peer_pos = _peer(k)
            for w in range(n):
                copies.append(pltpu.make_async_remote_copy(
                    src_ref=self._src(ins, w, me if arrivals else peer_pos),
                    dst_ref=outs[w].at[peer_pos if arrivals else me],
                    send_sem=send_sems.at[w, k], recv_sem=recv_sems.at[w, k], device_id=peer, device_id_type=_MESH))
        return copies

    def start(self, ins, outs, sems):
        for cp in self._copies(ins, outs, sems, False):
            cp.start()

    def mid(self, ins, outs, sems):
        pass

    def wait(self, ins, outs, sems):
        n = len(self.args)
        for cp in self._copies(ins, outs, sems, True):
            cp.wait_recv()
        own = self._copies(ins, outs, sems, False)
        for cp in own[n:]:
            cp.wait_send()
        for cp in own[:n]:
            cp.wait()


def _remote(src, dst, send_sem, recv_sem, peer):
    return pltpu.make_async_remote_copy(src_ref=src, dst_ref=dst, send_sem=send_sem, recv_sem=recv_sem,
                                        device_id=peer, device_id_type=_MESH)


class _Gather:
    def __init__(self, shards):
        self.args = list(shards)
        n = len(self.args)
        self.out_shape = [jax.ShapeDtypeStruct((N_DEV,) + a.shape, a.dtype) for a in shards]
        self.scratch = [pltpu.SemaphoreType.DMA((n, N_DEV)), pltpu.SemaphoreType.DMA((n, N_DEV)),
                        pltpu.SemaphoreType.DMA((n,))]

    def _own(self, ins, outs, sems):
        send, recv, loc = sems
        _, me = _my_position()
        local = [pltpu.make_async_copy(ins[w], outs[w].at[me], loc.at[w]) for w in range(len(ins))]
        first = [_remote(ins[w], outs[w].at[me], send.at[w, k], recv.at[w, k], _peer(k)[0])
                 for k in (1, 2, 4, 6) for w in range(len(ins))]
        return local, first

    def _passed_on(self, ins, outs, sems):
        send, recv, _ = sems
        sibling, _ = _peer(1)
        return [_remote(outs[w].at[_peer(k)[1]], outs[w].at[_peer(k)[1]], send.at[w, k + 1], recv.at[w, k + 1], sibling)
                for k in (2, 4, 6) for w in range(len(ins))]

    def _arrival(self, ins, outs, sems, k, w):
        send, recv, _ = sems
        peer, pos = _peer(k)
        return _remote(ins[w], outs[w].at[pos], send.at[w, k], recv.at[w, k], peer)

    def start(self, ins, outs, sems):
        local, first = self._own(ins, outs, sems)
        for cp in local + first:
            cp.start()

    def mid(self, ins, outs, sems):
        for k in (2, 4, 6):
            for w in range(len(ins)):
                self._arrival(ins, outs, sems, k, w).wait_recv()
        for cp in self._passed_on(ins, outs, sems):
            cp.start()

    def wait(self, ins, outs, sems):
        for k in (1, 3, 5, 7):
            for w in range(len(ins)):
                self._arrival(ins, outs, sems, k, w).wait_recv()
        local, first = self._own(ins, outs, sems)
        for cp in first + self._passed_on(ins, outs, sems):
            cp.wait_send()
        for cp in local:
            cp.wait()


class _PairExchange:
    def __init__(self, fulls):
        self.args = list(fulls)
        n = len(self.args)
        self.out_shape = [jax.ShapeDtypeStruct((N_DEV // 2,) + a.shape[1:], a.dtype) for a in fulls]
        self.scratch = [pltpu.SemaphoreType.DMA((n, N_DEV // 2)), pltpu.SemaphoreType.DMA((n, N_DEV // 2))]

    def _copies(self, ins, outs, sems):
        send, recv = sems
        (x, y, c), _ = _my_position()
        sibling, _ = _peer(1)
        return [_remote(ins[w].at[2 * q + 1 - c], outs[w].at[q], send.at[w, q], recv.at[w, q], sibling)
                for q in range(N_DEV // 2) for w in range(len(ins))]

    def start(self, ins, outs, sems):
        for cp in self._copies(ins, outs, sems):
            cp.start()

    def mid(self, ins, outs, sems):
        pass

    def wait(self, ins, outs, sems):
        for cp in self._copies(ins, outs, sems):
            cp.wait()


class _ChipScatter:
    def __init__(self, sums):
        self.args = list(sums)
        n = len(self.args)
        self.out_shape = [jax.ShapeDtypeStruct(a.shape, a.dtype) for a in sums]
        self.scratch = [pltpu.SemaphoreType.DMA((n, N_DEV // 2)), pltpu.SemaphoreType.DMA((n, N_DEV // 2)),
                        pltpu.SemaphoreType.DMA((n,))]

    def _copies(self, ins, outs, sems, arrivals):
        send, recv, loc = sems
        (x, y, c), _ = _my_position()
        my_chip = 2 * x + y
        copies = []
        if not arrivals:
            copies = [pltpu.make_async_copy(ins[w].at[my_chip], outs[w].at[my_chip], loc.at[w]) for w in range(len(ins))]
        for k in (1, 2, 3):
            peer = (x ^ (k >> 1), y ^ (k & 1), c)
            chip = my_chip ^ k
            for w in range(len(ins)):
                copies.append(_remote(ins[w].at[my_chip if arrivals else chip], outs[w].at[chip if arrivals else my_chip],
                                      send.at[w, k], recv.at[w, k], peer))
        return copies

    def start(self, ins, outs, sems):
        for cp in self._copies(ins, outs, sems, False):
            cp.start()

    def mid(self, ins, outs, sems):
        pass

    def wait(self, ins, outs, sems):
        n = len(ins)
        for cp in self._copies(ins, outs, sems, True):
            cp.wait_recv()
        own = self._copies(ins, outs, sems, False)
        for cp in own[n:]:
            cp.wait_send()
        for cp in own[:n]:
            cp.wait()


def _split(refs, sizes):
    out, off = [], 0
    for size in sizes:
        out.append(refs[off:off + size])
        off += size
    return out


def _call(body, name, grid, args, in_specs, out_shape, out_specs, scratch=(), exchanges=()):
    sem = ("arbitrary",) * len(grid)
    exchanges = [e for e in exchanges if e is not None]
    if not exchanges:
        outs = pl.pallas_call(body, name=name, grid=grid, in_specs=in_specs, out_specs=out_specs, out_shape=out_shape,
                              scratch_shapes=list(scratch), compiler_params=_params(sem))(*args)
        return outs, []
    n_in, n_out, n_scr = len(args), len(out_shape), len(scratch)
    ex_in = [len(e.args) for e in exchanges]
    ex_out = [len(e.out_shape) for e in exchanges]
    ex_scr = [len(e.scratch) for e in exchanges]
    steps = math.prod(grid)
    mid_step = min((3 * steps) // 4, steps - 1)

    def hosted(*refs):
        main_in, xin, main_out, xout, main_scr, xscr = _split(
            refs, [n_in, sum(ex_in), n_out, sum(ex_out), n_scr, sum(ex_scr)])
        parts = list(zip(exchanges, _split(xin, ex_in), _split(xout, ex_out), _split(xscr, ex_scr)))
        step = pl.program_id(0)
        for a in range(1, len(grid)):
            step = step * grid[a] + pl.program_id(a)

        @pl.when(step == 0)
        def _():
            for e, ins, outs, sems in parts:
                e.start(ins, outs, sems)

        body(*main_in, *main_out, *main_scr)

        @pl.when(step == mid_step)
        def _():
            for e, ins, outs, sems in parts:
                e.mid(ins, outs, sems)

        @pl.when(step == steps - 1)
        def _():
            for e, ins, outs, sems in parts:
                e.wait(ins, outs, sems)

    any_spec = pl.BlockSpec(memory_space=pl.ANY)
    outs = pl.pallas_call(
        hosted, name=name, grid=grid, in_specs=list(in_specs) + [any_spec] * sum(ex_in),
        out_specs=list(out_specs) + [any_spec] * sum(ex_out),
        out_shape=list(out_shape) + [s for e in exchanges for s in e.out_shape],
        scratch_shapes=list(scratch) + [s for e in exchanges for s in e.scratch],
        compiler_params=_params(sem))(*args, *[a for e in exchanges for a in e.args])
    return outs[:n_out], _split(outs[n_out:], ex_out)


def _exchange_only(exchanges, name):
    ex_in = [len(e.args) for e in exchanges]
    ex_out = [len(e.out_shape) for e in exchanges]
    ex_scr = [len(e.scratch) for e in exchanges]

    def body(*refs):
        xin, xout, xscr = _split(refs, [sum(ex_in), sum(ex_out), sum(ex_scr)])
        parts = list(zip(exchanges, _split(xin, ex_in), _split(xout, ex_out), _split(xscr, ex_scr)))
        for phase in ("start", "mid", "wait"):
            for e, ins, outs, sems in parts:
                getattr(e, phase)(ins, outs, sems)

    any_spec = pl.BlockSpec(memory_space=pl.ANY)
    outs = pl.pallas_call(
        body, name=name, in_specs=[any_spec] * sum(ex_in), out_specs=[any_spec] * sum(ex_out),
        out_shape=[s for e in exchanges for s in e.out_shape],
        scratch_shapes=[s for e in exchanges for s in e.scratch])(*[a for e in exchanges for a in e.args])
    return _split(outs, ex_out)


def _tile_specs(t, d, di):
    row = lambda i: (i, 0)
    col = lambda i: (0, i)
    return dict(
        xd=pl.BlockSpec((TM, d), row), xi=pl.BlockSpec((TM, di), row),
        td=pl.BlockSpec((d, TM), col), ti=pl.BlockSpec((di, TM), col),
        s_xd=jax.ShapeDtypeStruct((t, d), F32), s_xi=jax.ShapeDtypeStruct((t, di), BF16),
        s_td=jax.ShapeDtypeStruct((d, t), BF16), s_ti=jax.ShapeDtypeStruct((di, t), BF16))


def _fwd_a(x, win, wout, gain, bias, wc2, bs2, lng, lnb, target=None, exchanges=()):
    t, d = x.shape
    di = wout.shape[0]
    gd = di // A_GROUPS
    n_loss = 0 if target is None else 1

    def body(*refs):
        x_ref, win_ref, wout_ref, gain_ref, bias_ref, wc_ref, bs_ref, lng_ref, lnb_ref = refs[:9]
        (a1_ref, a2_ref, a3_ref, vn_ref, vh_ref, rg_ref, yt_ref, xt_ref, pre_ref,
         xn_ref) = refs[9 + n_loss:19 + n_loss]
        p_scr, vg_scr, y_scr = refs[19 + 2 * n_loss:]
        xv = x_ref[...]
        xt_ref[...] = xv.T.astype(BF16)
        _in_proj(xv.astype(BF16), win_ref, p_scr)
        s1 = jnp.zeros((TM, 1), F32)
        for c in range(di // CW):
            sl = slice(c * CW, (c + 1) * CW)
            pv = slice(di + c * CW, di + (c + 1) * CW)
            vg, dvg = _gelu_and_grad(p_scr[:, pv])
            vg_scr[:, sl] = vg
            p_scr[:, pv] = dvg
            s1 += jnp.sum(vg, axis=1, keepdims=True)
        mu = s1 * (1.0 / di)
        s2 = jnp.zeros((TM, 1), F32)
        for c in range(di // CW):
            dlt = vg_scr[:, c * CW:(c + 1) * CW] - mu
            s2 += jnp.sum(dlt * dlt, axis=1, keepdims=True)
        rstd = lax.rsqrt(s2 * (1.0 / di) + LN_EPS)
        for c in range(di // CW):
            sl = slice(c * CW, (c + 1) * CW)
            vh = (vg_scr[:, sl] - mu) * rstd
            vh_ref[:, sl] = vh.astype(BF16)
            vn_ref[:, sl] = (vh * gain_ref[:, sl] + bias_ref[:, sl]).astype(BF16)
            rg_ref[:, sl] = (p_scr[:, di + c * CW:di + (c + 1) * CW] * rstd).astype(BF16)
        for g in range(A_GROUPS):
            sl = slice(g * gd, (g + 1) * gd)
            sv = jnp.dot(wc_ref[g], vn_ref[:, sl], preferred_element_type=F32) + bs_ref[g]
            u, du = _gelu_and_grad(p_scr[:, sl])
            s, ds = _silu_and_grad(p_scr[:, 2 * di + g * gd:2 * di + (g + 1) * gd])
            us = u * s
            a1_ref[:, sl] = (s * du).astype(BF16)
            a2_ref[:, sl] = (u * ds).astype(BF16)
            a3_ref[:, sl] = us.astype(BF16)
            y = us * sv
            y_scr[:, sl] = y.astype(BF16)
            yt_ref[sl, :] = y.T.astype(BF16)
        out = jnp.dot(y_scr[...], wout_ref[...], preferred_element_type=F32)
        _post_norm(xv, out, lng_ref, lnb_ref, pre_ref, xn_ref)
        if n_loss:
            t_ref, sq_ref = refs[9], refs[20]
            _zero_at_first_step(sq_ref)
            diff = xn_ref[...] - t_ref[...]
            xn_ref[...] = diff * (1.0 / d)
            sq_ref[...] += _fold8(diff * diff)

    sp = _tile_specs(t, d, di)
    loss_in = [] if target is None else [target]
    return _call(
        body, "fwd_a", (t // TM,), (x, win, wout, gain, bias, wc2, bs2, lng, lnb, *loss_in),
        in_specs=[sp["xd"]] + [_vmem()] * 8 + [sp["xd"]] * n_loss,
        out_specs=([sp["xi"]] * 6 + [sp["ti"], sp["td"], sp["xd"], sp["xd"]]
                   + [pl.BlockSpec((8, d), lambda i: (0, 0))] * n_loss),
        out_shape=([sp["s_xi"]] * 6 + [sp["s_ti"], sp["s_td"], sp["s_xd"], sp["s_xd"]]
                   + [jax.ShapeDtypeStruct((8, d), F32)] * n_loss),
        scratch=[pltpu.VMEM((TM, 3 * di), F32), pltpu.VMEM((TM, di), F32), pltpu.VMEM((TM, di), BF16)],
        exchanges=exchanges)


def _bwd_a(g, pre, a1, a2, a3, vn, vh, rg, wout, gain, wc2, wc2t, bs2, lng, exchanges=()):
    t, d = g.shape
    di = wout.shape[0]
    gd = di // A_GROUPS

    def body(g_ref, pre_ref, a1_ref, a2_ref, a3_ref, vn_ref, vh_ref, rg_ref,
             wout_ref, gain_ref, wc_ref, wct_ref, bs_ref, lng_ref,
             dpre_ref, dp_ref, dlng_ref, dlnb_ref, dgain_ref, dbias_ref, dbs_ref, dwc_ref,
             dy_scr, dv_scr):
        _zero_at_first_step(dlng_ref, dlnb_ref, dgain_ref, dbias_ref, dbs_ref, dwc_ref)
        dpre = _post_norm_bwd(g_ref, pre_ref, lng_ref, dpre_ref, dlng_ref, dlnb_ref)
        dy_scr[...] = lax.dot_general(dpre.astype(BF16), wout_ref[...], _NT, preferred_element_type=F32)
        for grp in range(A_GROUPS):
            sl = slice(grp * gd, (grp + 1) * gd)
            vn_g = vn_ref[:, sl]
            sv = jnp.dot(wc_ref[grp], vn_g, preferred_element_type=F32) + bs_ref[grp]
            dy = dy_scr[:, sl]
            dys = dy * sv
            dp_ref[:, sl] = (dys * _f32(a1_ref, sl)).astype(BF16)
            dp_ref[:, 2 * di + grp * gd:2 * di + (grp + 1) * gd] = (dys * _f32(a2_ref, sl)).astype(BF16)
            dsv = dy * _f32(a3_ref, sl)
            dbs_ref[grp] += jnp.sum(dsv, axis=1, keepdims=True)
            dsvb = dsv.astype(BF16)
            dwc_ref[grp] += lax.dot_general(dsvb, vn_g, _NT, preferred_element_type=F32)
            dv_scr[:, sl] = jnp.dot(wct_ref[grp], dsvb, preferred_element_type=F32)
        r1 = jnp.zeros((TM, 1), F32)
        r2 = jnp.zeros((TM, 1), F32)
        for c in range(di // CW):
            sl = slice(c * CW, (c + 1) * CW)
            dv = dv_scr[:, sl]
            vhat = _f32(vh_ref, sl)
            dgain_ref[:, sl] += _fold8(dv * vhat)
            dbias_ref[:, sl] += _fold8(dv)
            dvh = dv * gain_ref[:, sl]
            dv_scr[:, sl] = dvh
            r1 += jnp.sum(dvh, axis=1, keepdims=True)
            r2 += jnp.sum(dvh * vhat, axis=1, keepdims=True)
        m1 = r1 * (1.0 / di)
        m2 = r2 * (1.0 / di)
        for c in range(di // CW):
            sl = slice(c * CW, (c + 1) * CW)
            dp_ref[:, di + c * CW:di + (c + 1) * CW] = (
                (dv_scr[:, sl] - m1 - _f32(vh_ref, sl) * m2) * _f32(rg_ref, sl)).astype(BF16)

    sp = _tile_specs(t, d, di)
    const2 = lambda i: (0, 0)
    const3 = lambda i: (0, 0, 0)
    return _call(
        body, "bwd_a", (t // TM,), (g, pre, a1, a2, a3, vn, vh, rg, wout, gain, wc2, wc2t, bs2, lng),
        in_specs=[sp["xd"], sp["xd"]] + [sp["xi"]] * 6 + [_vmem()] * 6,
        out_specs=[sp["xd"], pl.BlockSpec((TM, 3 * di), lambda i: (i, 0)),
                   pl.BlockSpec((8, d), const2), pl.BlockSpec((8, d), const2),
                   pl.BlockSpec((8, di), const2), pl.BlockSpec((8, di), const2),
                   pl.BlockSpec((A_GROUPS, TM, 1), const3), pl.BlockSpec((A_GROUPS, TM, TM), const3)],
        out_shape=[sp["s_xd"], jax.ShapeDtypeStruct((t, 3 * di), BF16),
                   jax.ShapeDtypeStruct((8, d), F32), jax.ShapeDtypeStruct((8, d), F32),
                   jax.ShapeDtypeStruct((8, di), F32), jax.ShapeDtypeStruct((8, di), F32),
                   jax.ShapeDtypeStruct((A_GROUPS, TM, 1), F32), jax.ShapeDtypeStruct((A_GROUPS, TM, TM), F32)],
        scratch=[pltpu.VMEM((TM, di), F32), pltpu.VMEM((TM, di), F32)],
        exchanges=exchanges)


def _inv_count(tile, window, rows=TM):
    pos = tile * rows + lax.broadcasted_iota(jnp.int32, (rows, 1), 0)
    return 1.0 / jnp.minimum(pos + 1, window).astype(F32)


def _window_sum(ext, window, down):
    rows = ext.shape[0]
    k = 1
    while k < window:
        ext = ext + pltpu.roll(ext, k if down else rows - k, 0)
        k *= 2
    return ext


def _fwd_b(x, win, wgrp, scale, wout, lng, lnb, exchanges=()):
    t, d = x.shape
    di = wout.shape[0]
    gd = di // len(POOL_WINDOWS)

    def body(x_ref, win_ref, wgrp_ref, scale_ref, wout_ref, lng_ref, lnb_ref,
             b1_ref, b2_ref, b3_ref, poolt_ref, yt_ref, xt_ref, pre_ref, xn_ref, p_scr, ext_scr, y_scr):
        i = pl.program_id(0)

        @pl.when(i == 0)
        def _():
            ext_scr[0:HALO, :] = jnp.zeros((HALO, di), F32)

        xv = x_ref[...]
        xt_ref[...] = xv.T.astype(BF16)
        _in_proj(xv.astype(BF16), win_ref, p_scr)
        ext_scr[HALO:, :] = p_scr[:, :di]
        for grp, window in enumerate(POOL_WINDOWS):
            sl = slice(grp * gd, (grp + 1) * gd)
            ext = ext_scr[:, sl]
            pooled = (_window_sum(ext, window, True)[HALO:] * _inv_count(i, window) - ext[HALO:]).astype(BF16)
            poolt_ref[sl, :] = pooled.astype(F32).T.astype(BF16)
            mixed = jnp.dot(pooled, wgrp_ref[grp], preferred_element_type=F32)
            s, ds = _silu_and_grad(p_scr[:, di + grp * gd:di + (grp + 1) * gd])
            sc = scale_ref[:, sl]
            ms = mixed * s
            b1_ref[:, sl] = (mixed * sc * ds).astype(BF16)
            b2_ref[:, sl] = ms.astype(BF16)
            b3_ref[:, sl] = (sc * s).astype(BF16)
            y = ms * sc
            y_scr[:, sl] = y.astype(BF16)
            yt_ref[sl, :] = y.T.astype(BF16)
        ext_scr[0:HALO, :] = ext_scr[TM:TM + HALO, :]
        out = jnp.dot(y_scr[...], wout_ref[...], preferred_element_type=F32)
        _post_norm(xv, out, lng_ref, lnb_ref, pre_ref, xn_ref)

    sp = _tile_specs(t, d, di)
    return _call(
        body, "fwd_b", (t // TM,), (x, win, wgrp, scale, wout, lng, lnb),
        in_specs=[sp["xd"]] + [_vmem()] * 6,
        out_specs=[sp["xi"]] * 3 + [sp["ti"], sp["ti"], sp["td"], sp["xd"], sp["xd"]],
        out_shape=[sp["s_xi"]] * 3 + [sp["s_ti"], sp["s_ti"], sp["s_td"], sp["s_xd"], sp["s_xd"]],
        scratch=[pltpu.VMEM((TM, 2 * di), F32), pltpu.VMEM((TM + HALO, di), F32), pltpu.VMEM((TM, di), BF16)],
        exchanges=exchanges)


def _bwd_b(g, pre, b1, b2, b3, wout, wgrp, lng, exchanges=()):
    t, d = g.shape
    di = wout.shape[0]
    gd = di // len(POOL_WINDOWS)
    nt = t // TM

    def body(g_ref, pre_ref, b1_ref, b2_ref, b3_ref, wout_ref, wgrp_ref, lng_ref,
             dpre_ref, dp_ref, dmix_ref, dlng_ref, dlnb_ref, dscale_ref, dy_scr, ext_scr):
        i = pl.program_id(0)
        tile = nt - 1 - i
        _zero_at_first_step(dlng_ref, dlnb_ref, dscale_ref)

        @pl.when(i == 0)
        def _():
            ext_scr[TM:, :] = jnp.zeros((HALO, di), F32)

        dpre = _post_norm_bwd(g_ref, pre_ref, lng_ref, dpre_ref, dlng_ref, dlnb_ref)
        dy_scr[...] = lax.dot_general(dpre.astype(BF16), wout_ref[...], _NT, preferred_element_type=F32)
        for grp, window in enumerate(POOL_WINDOWS):
            sl = slice(grp * gd, (grp + 1) * gd)
            dy = dy_scr[:, sl]
            dp_ref[:, di + grp * gd:di + (grp + 1) * gd] = (dy * _f32(b1_ref, sl)).astype(BF16)
            dscale_ref[:, sl] += _fold8(dy * _f32(b2_ref, sl))
            dmixed = (dy * _f32(b3_ref, sl)).astype(BF16)
            dmix_ref[:, sl] = dmixed
            dpooled = lax.dot_general(dmixed, wgrp_ref[grp], _NT, preferred_element_type=F32)
            ext_scr[0:TM, sl] = dpooled * _inv_count(tile, window)
            dv = _window_sum(ext_scr[:, sl], window, False)[0:TM] - dpooled
            dp_ref[:, sl] = dv.astype(BF16)
        ext_scr[TM:, :] = ext_scr[0:HALO, :]

    rrow = lambda i: (nt - 1 - i, 0)
    const2 = lambda i: (0, 0)
    xd, xi = pl.BlockSpec((TM, d), rrow), pl.BlockSpec((TM, di), rrow)
    return _call(
        body, "bwd_b", (nt,), (g, pre, b1, b2, b3, wout, wgrp, lng),
        in_specs=[xd, xd, xi, xi, xi, _vmem(), _vmem(), _vmem()],
        out_specs=[xd, pl.BlockSpec((TM, 2 * di), rrow), xi,
                   pl.BlockSpec((8, d), const2), pl.BlockSpec((8, d), const2), pl.BlockSpec((8, di), const2)],
        out_shape=[jax.ShapeDtypeStruct((t, d), F32), jax.ShapeDtypeStruct((t, 2 * di), BF16),
                   jax.ShapeDtypeStruct((t, di), BF16),
                   jax.ShapeDtypeStruct((8, d), F32), jax.ShapeDtypeStruct((8, d), F32),
                   jax.ShapeDtypeStruct((8, di), F32)],
        scratch=[pltpu.VMEM((TM, di), F32), pltpu.VMEM((TM + HALO, di), F32)],
        exchanges=exchanges)


def _fwd_c(x, win, convw, wout, lng, lnb, exchanges=()):
    t, d = x.shape
    di = wout.shape[0]

    def body(x_ref, win_ref, cw_ref, wout_ref, lng_ref, lnb_ref,
             c1_ref, c2_ref, c3_ref, cg_ref, hg_ref, yt_ref, xt_ref, pre_ref, xn_ref, p_scr, ext_scr, y_scr):
        i = pl.program_id(0)

        @pl.when(i == 0)
        def _():
            ext_scr[0:CHALO, :] = jnp.zeros((CHALO, di), F32)

        xv = x_ref[...]
        xt_ref[...] = xv.T.astype(BF16)
        _in_proj(xv.astype(BF16), win_ref, p_scr)
        for c in range(di // CW):
            sl = slice(c * CW, (c + 1) * CW)
            bb = p_scr[:, sl]
            cc = p_scr[:, di + c * CW:di + (c + 1) * CW]
            hh = p_scr[:, 2 * di + c * CW:2 * di + (c + 1) * CW]
            s, ds = _silu_and_grad(p_scr[:, 3 * di + c * CW:3 * di + (c + 1) * CW])
            ext_scr[CHALO:, sl] = cc * hh
            ext = ext_scr[:, sl]
            conv = (pltpu.roll(ext, 2, 0)[CHALO:] * cw_ref[0:1, sl] + pltpu.roll(ext, 1, 0)[CHALO:] * cw_ref[1:2, sl]
                    + ext[CHALO:] * cw_ref[2:3, sl])
            cs = conv * s
            c1_ref[:, sl] = cs.astype(BF16)
            c2_ref[:, sl] = (bb * conv * ds).astype(BF16)
            c3_ref[:, sl] = (bb * s).astype(BF16)
            cg_ref[:, sl] = cc.astype(BF16)
            hg_ref[:, sl] = hh.astype(BF16)
            y = bb * cs
            y_scr[:, sl] = y.astype(BF16)
            yt_ref[sl, :] = y.T.astype(BF16)
        ext_scr[0:CHALO, :] = ext_scr[TM:TM + CHALO, :]
        out = jnp.dot(y_scr[...], wout_ref[...], preferred_element_type=F32)
        _post_norm(xv, out, lng_ref, lnb_ref, pre_ref, xn_ref)

    sp = _tile_specs(t, d, di)
    return _call(
        body, "fwd_c", (t // TM,), (x, win, convw, wout, lng, lnb),
        in_specs=[sp["xd"]] + [_vmem()] * 5,
        out_specs=[sp["xi"]] * 5 + [sp["ti"], sp["td"], sp["xd"], sp["xd"]],
        out_shape=[sp["s_xi"]] * 5 + [sp["s_ti"], sp["s_td"], sp["s_xd"], sp["s_xd"]],
        scratch=[pltpu.VMEM((TM, 4 * di), F32), pltpu.VMEM((TM + CHALO, di), F32), pltpu.VMEM((TM, di), BF16)],
        exchanges=exchanges)


def _bwd_c(g, pre, c1, c2, c3, cg, hg, wout, convw, lng, exchanges=()):
    t, d = g.shape
    di = wout.shape[0]
    nt = t // TM

    def body(g_ref, pre_ref, c1_ref, c2_ref, c3_ref, cg_ref, hg_ref, wout_ref, cw_ref, lng_ref,
             dpre_ref, dp_ref, dlng_ref, dlnb_ref, dcw_ref, dy_scr, ext_scr):
        i = pl.program_id(0)
        _zero_at_first_step(dlng_ref, dlnb_ref, dcw_ref)

        @pl.when(i == 0)
        def _():
            ext_scr[TM:, :] = jnp.zeros((CHALO, di), F32)

        dpre = _post_norm_bwd(g_ref, pre_ref, lng_ref, dpre_ref, dlng_ref, dlnb_ref)
        dy_scr[...] = lax.dot_general(dpre.astype(BF16), wout_ref[...], _NT, preferred_element_type=F32)
        rows = TM + CHALO
        for c in range(di // CW):
            sl = slice(c * CW, (c + 1) * CW)
            dy = dy_scr[:, sl]
            cc = _f32(cg_ref, sl)
            hh = _f32(hg_ref, sl)
            dp_ref[:, sl] = (dy * _f32(c1_ref, sl)).astype(BF16)
            dp_ref[:, 3 * di + c * CW:3 * di + (c + 1) * CW] = (dy * _f32(c2_ref, sl)).astype(BF16)
            dconv = dy * _f32(c3_ref, sl)
            ext_scr[0:TM, sl] = dconv
            ext = ext_scr[:, sl]
            d1 = pltpu.roll(ext, rows - 1, 0)[0:TM]
            d2 = pltpu.roll(ext, rows - 2, 0)[0:TM]
            dq = dconv * cw_ref[2:3, sl] + d1 * cw_ref[1:2, sl] + d2 * cw_ref[0:1, sl]
            q = cc * hh
            dcw_ref[0, :, sl] += _fold8(q * d2)
            dcw_ref[1, :, sl] += _fold8(q * d1)
            dcw_ref[2, :, sl] += _fold8(q * dconv)
            dp_ref[:, di + c * CW:di + (c + 1) * CW] = (dq * hh).astype(BF16)
            dp_ref[:, 2 * di + c * CW:2 * di + (c + 1) * CW] = (dq * cc).astype(BF16)
        ext_scr[TM:, :] = ext_scr[0:CHALO, :]

    rrow = lambda i: (nt - 1 - i, 0)
    const2 = lambda i: (0, 0)
    xd, xi = pl.BlockSpec((TM, d), rrow), pl.BlockSpec((TM, di), rrow)
    return _call(
        body, "bwd_c", (nt,), (g, pre, c1, c2, c3, cg, hg, wout, convw, lng),
        in_specs=[xd, xd] + [xi] * 5 + [_vmem()] * 3,
        out_specs=[xd, pl.BlockSpec((TM, 4 * di), rrow),
                   pl.BlockSpec((8, d), const2), pl.BlockSpec((8, d), const2),
                   pl.BlockSpec((3, 8, di), lambda i: (0, 0, 0))],
        out_shape=[jax.ShapeDtypeStruct((t, d), F32), jax.ShapeDtypeStruct((t, 4 * di), BF16),
                   jax.ShapeDtypeStruct((8, d), F32), jax.ShapeDtypeStruct((8, d), F32),
                   jax.ShapeDtypeStruct((3, 8, di), F32)],
        scratch=[pltpu.VMEM((TM, di), F32), pltpu.VMEM((TM + CHALO, di), F32)],
        exchanges=exchanges)


def _dx(dp, dpre, win, exchanges=()):
    t, d = dpre.shape
    n = dp.shape[1]
    cs = win.shape[2]

    def body(dp_ref, dpre_ref, win_ref, dx_ref):
        acc = ALPHA * dpre_ref[...]
        for j in range(N_DEV):
            acc += lax.dot_general(dp_ref[:, j * cs:(j + 1) * cs], win_ref[j], _NT, preferred_element_type=F32)
        dx_ref[...] = acc

    row = lambda i: (i, 0)
    (dx,), ex = _call(
        body, "dx", (t // TM,), (dp, dpre, win),
        in_specs=[pl.BlockSpec((TM, n), row), pl.BlockSpec((TM, d), row), _vmem()],
        out_specs=[pl.BlockSpec((TM, d), row)],
        out_shape=[jax.ShapeDtypeStruct((t, d), F32)],
        exchanges=exchanges)
    return dx, ex


def _wgrad(at, b, nb, per_block_rows, name, exchanges=()):
    m_all, t = at.shape
    tn = b.shape[1] // nb
    m = m_all // nb if per_block_rows else m_all
    tk = min(WGRAD_TK, t)
    nk = t // tk

    def body(at_ref, b_ref, out_ref, acc):
        k = pl.program_id(1)

        @pl.when(k == 0)
        def _():
            acc[...] = jnp.zeros(acc.shape, F32)

        acc[...] += jnp.dot(at_ref[...], b_ref[...].astype(BF16), preferred_element_type=F32)

        @pl.when(k == nk - 1)
        def _():
            out_ref[0] = acc[...].astype(BF16)

    at_map = (lambda j, k: (j, k)) if per_block_rows else (lambda j, k: (0, k))
    (out,), ex = _call(
        body, name, (nb, nk), (at, b),
        in_specs=[pl.BlockSpec((m, tk), at_map), pl.BlockSpec((tk, tn), lambda j, k: (k, j))],
        out_specs=[pl.BlockSpec((1, m, tn), lambda j, k: (j, 0, 0))],
        out_shape=[jax.ShapeDtypeStruct((nb, m, tn), BF16)],
        scratch=[pltpu.VMEM((m, tn), F32)],
        exchanges=exchanges)
    return out, ex


ADAMW_BLOCK_BYTES = 6 * 1024 * 1024


def _sum_parts(parts_ref):
    g = parts_ref[0].astype(F32)
    for s in range(1, parts_ref.shape[0]):
        g = g + parts_ref[s].astype(F32)
    return g


def _row_tile(rows, bytes_per_row):
    if rows * bytes_per_row <= ADAMW_BLOCK_BYTES:
        return rows
    best = 8
    for cand in range(8, rows, 8):
        if rows % cand == 0 and cand * bytes_per_row <= ADAMW_BLOCK_BYTES:
            best = cand
    return best


def _pair_sum(full, theirs, core, name):
    shape = theirs.shape
    r, c = math.prod(shape[1:-1]), shape[-1]
    tr = _row_tile(r, c * 3 * full.dtype.itemsize)
    half = N_DEV // 2

    def body(core_ref, a_ref, b_ref, out_ref):
        out_ref[...] = (a_ref[...].astype(F32) + b_ref[...].astype(F32)).astype(out_ref.dtype)

    blk = pl.BlockSpec((None, tr, c), lambda q, i, core_ref: (q, i, 0))
    grid_spec = pltpu.PrefetchScalarGridSpec(
        num_scalar_prefetch=1, grid=(half, r // tr),
        in_specs=[pl.BlockSpec((None, None, tr, c), lambda q, i, core_ref: (q, core_ref[0], i, 0)), blk],
        out_specs=blk)
    return pl.pallas_call(
        body, name=name, grid_spec=grid_spec, out_shape=jax.ShapeDtypeStruct((half, r, c), full.dtype),
        compiler_params=_params(("arbitrary", "arbitrary")),
    )(core, full.reshape(half, 2, r, c), theirs.reshape(half, r, c)).reshape(shape)


def _adamw(parts, w, m, v, name):
    s, r, c = parts.shape
    tr = _row_tile(r, c * (s * parts.dtype.itemsize + 7 * 4))

    def body(parts_ref, w_ref, m_ref, v_ref, g_ref, d_ref, nm_ref, nv_ref):
        g = _sum_parts(parts_ref)
        g_ref[...] = g
        d_ref[...], nm_ref[...], nv_ref[...] = _adamw_update(g, w_ref[...], m_ref[...], v_ref[...])

    blk = pl.BlockSpec((tr, c), lambda i: (i, 0))
    return pl.pallas_call(
        body, name=name, grid=(r // tr,),
        in_specs=[pl.BlockSpec((s, tr, c), lambda i: (0, i, 0)), blk, blk, blk],
        out_specs=[blk, blk, blk, blk],
        out_shape=[jax.ShapeDtypeStruct((r, c), F32)] * 4,
        compiler_params=_params(("arbitrary",)),
    )(parts, w, m, v)


def _adamw_update(g, w, m, v):
    bc1 = 1.0 - ADAM_B1 ** ADAM_STEP
    bc2 = 1.0 - ADAM_B2 ** ADAM_STEP
    nm = ADAM_B1 * m + (1.0 - ADAM_B1) * g
    nv = ADAM_B2 * v + (1.0 - ADAM_B2) * (g * g)
    return -ADAM_LR * ((nm / bc1) / (jnp.sqrt(nv / bc2) + ADAM_EPS) + ADAM_WD * w), nm, nv


def _adamw_bucket(parts, members, name):
    n = len(members)
    shapes = [w.shape for w, _, _ in members]
    rows = [math.prod(shp) // _LANES for shp in shapes]
    starts = [sum(rows[:k]) for k in range(n)]

    def body(parts_ref, *refs):
        for k in range(n):
            w_ref, m_ref, v_ref = refs[3 * k:3 * k + 3]
            g_ref, d_ref, nm_ref, nv_ref = refs[3 * n + 4 * k:3 * n + 4 * k + 4]
            g = _sum_parts(parts_ref.at[:, starts[k]:starts[k] + rows[k], :])
            g_ref[...] = g
            d_ref[...], nm_ref[...], nv_ref[...] = _adamw_update(g, w_ref[...], m_ref[...], v_ref[...])

    flat = [a.reshape(-1, _LANES) for mem in members for a in mem]
    outs = pl.pallas_call(
        body, name=name, in_specs=[_vmem()] * (1 + 3 * n), out_specs=[_vmem()] * (4 * n),
        out_shape=[jax.ShapeDtypeStruct((rows[k], _LANES), F32) for k in range(n) for _ in range(4)],
        compiler_params=_params(),
    )(parts, *flat)
    return [[o.reshape(shapes[k]) for o in outs[4 * k:4 * k + 4]] for k in range(n)]


_LANES = 128


def _pack(arrays):
    flat = jnp.concatenate([a.reshape(-1) for a in arrays])
    pad = (-flat.shape[0]) % (8 * _LANES)
    return jnp.pad(flat, (0, pad)).reshape(-1, _LANES)


def _unpack(packed, shapes):
    flat = packed.reshape(-1)
    out, off = [], 0
    for shp in shapes:
        size = math.prod(shp)
        out.append(flat[off:off + size].reshape(shp))
        off += size
    return out


def _spatial_weights(w_s, b_s, rows):
    reps = rows // CHUNK
    tril = jnp.tril(jnp.ones((CHUNK, CHUNK), F32))
    wc = w_s * tril
    eye = jnp.eye(reps, dtype=F32)
    wc2 = jnp.einsum("ab,gts->gatbs", eye, wc).reshape(A_GROUPS, rows, rows)
    bs2 = jnp.tile(b_s, (1, reps)).reshape(A_GROUPS, rows, 1)
    return wc2.astype(BF16), jnp.swapaxes(wc2, 1, 2).astype(BF16), bs2


def _spatial_weight_grad(dwc2, dbs2):
    reps = TM // CHUNK
    tril = jnp.tril(jnp.ones((CHUNK, CHUNK), F32))
    blocks = dwc2.reshape(A_GROUPS, reps, CHUNK, reps, CHUNK)
    dws = sum(blocks[:, a, :, a, :] for a in range(reps)) * tril
    dbs = dbs2.reshape(A_GROUPS, reps, CHUNK).sum(axis=1)
    return dws, dbs


def _row2(a):
    return a.reshape(1, -1)


def kernel(x, a0_w_in, a0_v_gain, a0_v_bias, a0_w_s, a0_b_s, a0_w_out, ln0_gain, ln0_bias, b1_w_in, b1_w_grp, b1_scale, b1_w_out, ln1_gain, ln1_bias, c2_w_in, c2_conv_w, c2_w_out, ln2_gain, ln2_bias, a3_w_in, a3_v_gain, a3_v_bias, a3_w_s, a3_b_s, a3_w_out, ln3_gain, ln3_bias, loss_target, m_a0_w_in, m_a0_v_gain, m_a0_v_bias, m_a0_w_s, m_a0_b_s, m_a0_w_out, m_ln0_gain, m_ln0_bias, m_b1_w_in, m_b1_w_grp, m_b1_scale, m_b1_w_out, m_ln1_gain, m_ln1_bias, m_c2_w_in, m_c2_conv_w, m_c2_w_out, m_ln2_gain, m_ln2_bias, m_a3_w_in, m_a3_v_gain, m_a3_v_bias, m_a3_w_s, m_a3_b_s, m_a3_w_out, m_ln3_gain, m_ln3_bias, v_a0_w_in, v_a0_v_gain, v_a0_v_bias, v_a0_w_s, v_a0_b_s, v_a0_w_out, v_ln0_gain, v_ln0_bias, v_b1_w_in, v_b1_w_grp, v_b1_scale, v_b1_w_out, v_ln1_gain, v_ln1_bias, v_c2_w_in, v_c2_conv_w, v_c2_w_out, v_ln2_gain, v_ln2_bias, v_a3_w_in, v_a3_v_gain, v_a3_v_bias, v_a3_w_s, v_a3_b_s, v_a3_w_out, v_ln3_gain, v_ln3_bias):
    names = ["a0_w_in", "a0_v_gain", "a0_v_bias", "a0_w_s", "a0_b_s", "a0_w_out", "ln0_gain", "ln0_bias",
             "b1_w_in", "b1_w_grp", "b1_scale", "b1_w_out", "ln1_gain", "ln1_bias",
             "c2_w_in", "c2_conv_w", "c2_w_out", "ln2_gain", "ln2_bias",
             "a3_w_in", "a3_v_gain", "a3_v_bias", "a3_w_s", "a3_b_s", "a3_w_out", "ln3_gain", "ln3_bias"]
    env = dict(locals())
    w = {nm: env[nm] for nm in names}
    mom = {nm: env["m_" + nm] for nm in names}
    var = {nm: env["v_" + nm] for nm in names}

    x0 = x[0]
    target = loss_target[0]
    d_model = x0.shape[1]
    di = N_DEV * a0_w_out.shape[0]
    n_grp = len(POOL_WINDOWS)
    gd_b = di // n_grp

    layers = ("a0", "b1", "c2", "a3")
    big_of = {"a0": ["a0_w_in", "a0_w_out"], "b1": ["b1_w_in", "b1_w_grp", "b1_w_out"],
              "c2": ["c2_w_in", "c2_w_out"], "a3": ["a3_w_in", "a3_w_out"]}
    big = [nm for p in layers for nm in big_of[p]]
    bucket_of = {"a0": ["a0_v_gain", "a0_v_bias", "a0_w_s", "a0_b_s", "ln0_gain", "ln0_bias"],
                 "b1": ["b1_scale", "ln1_gain", "ln1_bias"], "c2": ["ln2_gain", "ln2_bias"],
                 "a3": ["a3_v_gain", "a3_v_bias", "a3_w_s", "a3_b_s", "ln3_gain", "ln3_bias"]}
    conv_shape = c2_conv_w.shape
    spatial = {p: _spatial_weights(w[p + "_w_s"], w[p + "_b_s"], TM) for p in ("a0", "a3")}

    def weight_gather(p):
        return _Gather([w[nm].astype(BF16) for nm in big_of[p]])

    (first,) = _exchange_only([_Gather([w[nm].astype(BF16) for nm in big_of["a0"]] + [_pack([c2_conv_w])])],
                              "gather_first")
    gathered = dict(zip(big_of["a0"], first))
    conv_all = jnp.stack([_unpack(first[-1][j], [conv_shape])[0] for j in range(N_DEV)], axis=1)
    conv_full = conv_all.reshape(conv_shape[0], di)
    w_in = lambda p: gathered[p + "_w_in"]
    w_out = lambda p: gathered[p + "_w_out"].reshape(di, d_model)
    saved = {}
    h = x0
    for i, p in enumerate(layers):
        lng, lnb = _row2(w[f"ln{i}_gain"]), _row2(w[f"ln{i}_bias"])
        nxt = layers[i + 1] if i + 1 < len(layers) else None
        ex = [weight_gather(nxt)] if nxt else []
        if p[0] == "a":
            wc2, _, bs2 = spatial[p]
            outs, got = _fwd_a(h, w_in(p), w_out(p), _row2(w[p + "_v_gain"]), _row2(w[p + "_v_bias"]),
                               wc2, bs2, lng, lnb, target=None if nxt else target, exchanges=ex)
        elif p[0] == "b":
            wgrp = jnp.swapaxes(gathered["b1_w_grp"], 0, 1).reshape(n_grp, gd_b, gd_b)
            outs, got = _fwd_b(h, w_in(p), wgrp, _row2(w[p + "_scale"]), w_out(p), lng, lnb, exchanges=ex)
        else:
            outs, got = _fwd_c(h, w_in(p), conv_full, w_out(p), lng, lnb, exchanges=ex)
        if nxt:
            saved[p], h = outs[:-1], outs[-1]
            gathered.update(zip(big_of[nxt], got[0]))
        else:
            saved[p], gcur, sq = outs[:-2], outs[-2], outs[-1]

    loss = lax.psum(jnp.sum(sq) * (0.5 / d_model), ("x", "y", "c"))

    part, full, landed, small_all = {}, {}, {}, {}

    def bucket_gather(bucket):
        return _Gather([_pack([part[nm] for nm in bucket_of[bucket]])])

    core = lax.axis_index("c").astype(jnp.int32).reshape(1)

    def chip_sums(names, got):
        return [_pair_sum(full[nm], theirs, core, "pair_sum_" + nm) for nm, theirs in zip(names, got)]

    pending = None
    for i, p in reversed(list(enumerate(layers))):
        lng = _row2(w[f"ln{i}_gain"])
        ex = []
        if pending:
            ex = [_PairExchange([full[nm] for nm in big_of[pending]]), bucket_gather(pending)]
            if pending == "c2":
                ex.append(_Exchange(scatters=[full["c2_conv_w"]]))
        *factors, yt, xt, pre = saved[p]
        if p[0] == "a":
            wc2, wc2t, bs2 = spatial[p]
            (dpre, dp, dlng, dlnb, dgain, dbias, dbs2, dwc2), got = _bwd_a(
                gcur, pre, *factors, w_out(p), _row2(w[p + "_v_gain"]), wc2, wc2t, bs2, lng, exchanges=ex)
            part[p + "_v_gain"], part[p + "_v_bias"] = dgain.sum(axis=0), dbias.sum(axis=0)
            part[p + "_w_s"], part[p + "_b_s"] = _spatial_weight_grad(dwc2, dbs2)
        elif p[0] == "b":
            b1f, b2f, b3f, poolt = factors
            (dpre, dp, dmixed, dlng, dlnb, dscale), got = _bwd_b(
                gcur, pre, b1f, b2f, b3f, w_out(p), wgrp, lng, exchanges=ex)
            part[p + "_scale"] = dscale.sum(axis=0)
            dwg, _ = _wgrad(poolt, dmixed, n_grp, True, "wgrad_grp")
            full[p + "_w_grp"] = jnp.swapaxes(dwg.reshape(n_grp, N_DEV, gd_b // N_DEV, gd_b), 0, 1)
        else:
            (dpre, dp, dlng, dlnb, dcw), got = _bwd_c(gcur, pre, *factors, w_out(p), conv_full, lng, exchanges=ex)
            dconv = dcw.sum(axis=1).reshape(conv_shape[0], N_DEV, conv_shape[1])
            full["c2_conv_w"] = jnp.stack([_pack([dconv[:, j]]) for j in range(N_DEV)])
        part[f"ln{i}_gain"], part[f"ln{i}_bias"] = dlng.sum(axis=0), dlnb.sum(axis=0)
        if pending:
            small_all[pending] = got[1][0]
            if pending == "c2":
                small_all["conv"] = got[2][0]
        dwo, _ = _wgrad(yt, dpre, 1, False, "wgrad_out_" + p)
        full[p + "_w_out"] = dwo.reshape(N_DEV, di // N_DEV, d_model)
        ex = [_ChipScatter(chip_sums(big_of[pending], got[0]))] if pending else []
        if i == 0:
            ex += [bucket_gather(p)]
        full[p + "_w_in"], got = _wgrad(xt, dp, N_DEV, False, "wgrad_in_" + p, exchanges=ex)
        if pending:
            landed.update(zip(big_of[pending], got[0]))
        if i > 0:
            gcur, _ = _dx(dp, dpre, w_in(p))
            pending = p
        else:
            small_all[p] = got[-1][0]
            (theirs,) = _exchange_only([_PairExchange([full[p + "_w_in"]])], "pair_exchange_last")
            gcur, got = _dx(dp, dpre, w_in(p), exchanges=[
                _ChipScatter(chip_sums([p + "_w_in"], theirs)), _Exchange(scatters=[full[p + "_w_out"]])])
            landed[p + "_w_in"], landed[p + "_w_out"] = got[0][0], got[1][0]
    grad_x = gcur[None]

    grads, deltas, new_m, new_v = {}, {}, {}, {}
    for nm in big:
        shp = w[nm].shape
        r2 = (math.prod(shp[:-1]), shp[-1])
        outs = _adamw(landed[nm].reshape((-1,) + r2), w[nm].reshape(r2), mom[nm].reshape(r2), var[nm].reshape(r2),
                      "adamw_" + nm)
        grads[nm], deltas[nm], new_m[nm], new_v[nm] = (o.reshape(shp) for o in outs)
    buckets = dict(bucket_of, conv=["c2_conv_w"])
    for key, members in buckets.items():
        outs = _adamw_bucket(small_all[key], [(w[nm], mom[nm], var[nm]) for nm in members], "adamw_small_" + key)
        for nm, (g_out, d_out, m_out, v_out) in zip(members, outs):
            grads[nm], deltas[nm], new_m[nm], new_v[nm] = g_out, d_out, m_out, v_out

    return (loss, grad_x, *[grads[nm] for nm in names], *[deltas[nm] for nm in names],
            *[new_m[nm] for nm in names], *[new_v[nm] for nm in names])
```

```python
import functools
import math

import jax
import jax.numpy as jnp
from jax import lax
from jax.experimental import pallas as pl
from jax.experimental.pallas import tpu as pltpu

F32 = jnp.float32
BF16 = jnp.bfloat16

N_DEV = 8
DEPTH = 4
CHUNK = 128
A_GROUPS = 8
POOL_WINDOWS = (2, 4, 8, 16)
LN_EPS = 1e-5
ALPHA = (2.0 * DEPTH) ** 0.25
ADAM_LR = 0.001
ADAM_B1 = 0.9
ADAM_B2 = 0.999
ADAM_EPS = 1e-08
ADAM_WD = 0.01
ADAM_STEP = 10

TM = 256
HALO = 16
CHALO = 8
CW = 512
WGRAD_TK = 2048
VMEM_LIMIT_BYTES = 58 * 1024 * 1024

_NT = (((1,), (1,)), ((), ()))
_SQRT_2_OVER_PI = math.sqrt(2.0 / math.pi)
_MESH = pl.DeviceIdType.MESH


def _vmem():
    return pl.BlockSpec(memory_space=pltpu.VMEM)


def _params(sem=None):
    return pltpu.CompilerParams(dimension_semantics=sem, vmem_limit_bytes=VMEM_LIMIT_BYTES)


def _gelu_and_grad(x):
    c1 = _SQRT_2_OVER_PI * 0.044715
    x2 = x * x
    t = jnp.tanh(x * (_SQRT_2_OVER_PI + c1 * x2))
    cdf = 0.5 + 0.5 * t
    grad = cdf + x * (1.0 - t * t) * (0.5 * _SQRT_2_OVER_PI + (1.5 * c1) * x2)
    return x * cdf, grad


def _silu_and_grad(z):
    sg = 1.0 / (1.0 + jnp.exp(-z))
    s = z * sg
    return s, sg + s - s * sg


def _fold8(a):
    return a.reshape(a.shape[0] // 8, 8, a.shape[1]).sum(axis=0)


def _row_mean(a):
    return jnp.mean(a, axis=-1, keepdims=True)


def _ln_stats(x):
    mu = _row_mean(x)
    xc = x - mu
    rstd = lax.rsqrt(_row_mean(xc * xc) + LN_EPS)
    return xc * rstd, rstd


def _post_norm(x, out, lng_ref, lnb_ref, pre_ref, xn_ref):
    pre = ALPHA * x + out
    pre_ref[...] = pre
    xhat, _ = _ln_stats(pre)
    xn_ref[...] = xhat * lng_ref[...] + lnb_ref[...]


def _post_norm_bwd(g_ref, pre_ref, lng_ref, dpre_ref, dlng_ref, dlnb_ref):
    go = g_ref[...]
    xhat, rstd = _ln_stats(pre_ref[...])
    dlng_ref[...] += _fold8(go * xhat)
    dlnb_ref[...] += _fold8(go)
    dxh = go * lng_ref[...]
    dpre = rstd * (dxh - _row_mean(dxh) - xhat * _row_mean(dxh * xhat))
    dpre_ref[...] = dpre
    return dpre


def _in_proj(xb, win_ref, p_ref):
    cs = win_ref.shape[2]
    for j in range(N_DEV):
        p_ref[:, j * cs:(j + 1) * cs] = jnp.dot(xb, win_ref[j], preferred_element_type=F32)


def _zero_at_first_step(*refs):
    @pl.when(pl.program_id(0) == 0)
    def _():
        for r in refs:
            r[...] = jnp.zeros(r.shape, r.dtype)


def _f32(ref, sl):
    return ref[:, sl].astype(F32)


def _my_position():
    x, y, c = lax.axis_index("x"), lax.axis_index("y"), lax.axis_index("c")
    return (x, y, c), 4 * x + 2 * y + c


def _peer(k):
    (x, y, c), _ = _my_position()
    peer = (x ^ (k >> 2), y ^ ((k >> 1) & 1), c ^ (k & 1))
    return peer, 4 * peer[0] + 2 * peer[1] + peer[2]


class _Exchange:
    def __init__(self, gathers=(), scatters=()):
        self.args = list(gathers) + list(scatters)
        self.n_gather = len(gathers)
        self.out_shape = ([jax.ShapeDtypeStruct((N_DEV,) + a.shape, a.dtype) for a in gathers]
                          + [jax.ShapeDtypeStruct(a.shape, a.dtype) for a in scatters])
        n = len(self.args)
        self.scratch = [pltpu.SemaphoreType.DMA((n, N_DEV)), pltpu.SemaphoreType.DMA((n, N_DEV)),
                        pltpu.SemaphoreType.DMA((n,))]

    def _src(self, ins, w, pos):
        return ins[w] if w < self.n_gather else ins[w].at[pos]

    def _copies(self, ins, outs, sems, arrivals):
        send_sems, recv_sems, local_sems = sems
        _, me = _my_position()
        n = len(self.args)
        copies = []
        if not arrivals:
            copies = [pltpu.make_async_copy(self._src(ins, w, me), outs[w].at[me], local_sems.at[w]) for w in range(n)]
        for k in range(1, N_DEV):
            peer, peer_pos = _peer(k)
            for w in range(n):
                copies.append(pltpu.make_async_remote_copy(
                    src_ref=self._src(ins, w, me if arrivals else peer_pos),
                    dst_ref=outs[w].at[peer_pos if arrivals else me],
                    send_sem=send_sems.at[w, k], recv_sem=recv_sems.at[w, k], device_id=peer, device_id_type=_MESH))
        return copies

    def start(self, ins, outs, sems):
        for cp in self._copies(ins, outs, sems, False):
            cp.start()

    def mid(self, ins, outs, sems):
        pass

    def wait(self, ins, outs, sems):
        n = len(self.args)
        for cp in self._copies(ins, outs, sems, True):
            cp.wait_recv()
        own = self._copies(ins, outs, sems, False)
        for cp in own[n:]:
            cp.wait_send()
        for cp in own[:n]:
            cp.wait()


def _remote(src, dst, send_sem, recv_sem, peer):
    return pltpu.make_async_remote_copy(src_ref=src, dst_ref=dst, send_sem=send_sem, recv_sem=recv_sem,
                                        device_id=peer, device_id_type=_MESH)


class _Gather:
    def __init__(self, shards):
        self.args = list(shards)
        n = len(self.args)
        self.out_shape = [jax.ShapeDtypeStruct((N_DEV,) + a.shape, a.dtype) for a in shards]
        self.scratch = [pltpu.SemaphoreType.DMA((n, N_DEV)), pltpu.SemaphoreType.DMA((n, N_DEV)),
                        pltpu.SemaphoreType.DMA((n,))]

    def _own(self, ins, outs, sems):
        send, recv, loc = sems
        _, me = _my_position()
        local = [pltpu.make_async_copy(ins[w], outs[w].at[me], loc.at[w]) for w in range(len(ins))]
        first = [_remote(ins[w], outs[w].at[me], send.at[w, k], recv.at[w, k], _peer(k)[0])
                 for k in (1, 2, 4, 6) for w in range(len(ins))]
        return local, first

    def _passed_on(self, ins, outs, sems):
        send, recv, _ = sems
        sibling, _ = _peer(1)
        return [_remote(outs[w].at[_peer(k)[1]], outs[w].at[_peer(k)[1]], send.at[w, k + 1], recv.at[w, k + 1], sibling)
                for k in (2, 4, 6) for w in range(len(ins))]

    def _arrival(self, ins, outs, sems, k, w):
        send, recv, _ = sems
        peer, pos = _peer(k)
        return _remote(ins[w], outs[w].at[pos], send.at[w, k], recv.at[w, k], peer)

    def start(self, ins, outs, sems):
        local, first = self._own(ins, outs, sems)
        for cp in local + first:
            cp.start()

    def mid(self, ins, outs, sems):
        for k in (2, 4, 6):
            for w in range(len(ins)):
                self._arrival(ins, outs, sems, k, w).wait_recv()
        for cp in self._passed_on(ins, outs, sems):
            cp.start()

    def wait(self, ins, outs, sems):
        for k in (1, 3, 5, 7):
            for w in range(len(ins)):
                self._arrival(ins, outs, sems, k, w).wait_recv()
        local, first = self._own(ins, outs, sems)
        for cp in first + self._passed_on(ins, outs, sems):
            cp.wait_send()
        for cp in local:
            cp.wait()


class _PairExchange:
    def __init__(self, fulls):
        self.args = list(fulls)
        n = len(self.args)
        self.out_shape = [jax.ShapeDtypeStruct((N_DEV // 2,) + a.shape[1:], a.dtype) for a in fulls]
        self.scratch = [pltpu.SemaphoreType.DMA((n, N_DEV // 2)), pltpu.SemaphoreType.DMA((n, N_DEV // 2))]

    def _copies(self, ins, outs, sems):
        send, recv = sems
        (x, y, c), _ = _my_position()
        sibling, _ = _peer(1)
        return [_remote(ins[w].at[2 * q + 1 - c], outs[w].at[q], send.at[w, q], recv.at[w, q], sibling)
                for q in range(N_DEV // 2) for w in range(len(ins))]

    def start(self, ins, outs, sems):
        for cp in self._copies(ins, outs, sems):
            cp.start()

    def mid(self, ins, outs, sems):
        pass

    def wait(self, ins, outs, sems):
        for cp in self._copies(ins, outs, sems):
            cp.wait()


class _ChipScatter:
    def __init__(self, sums):
        self.args = list(sums)
        n = len(self.args)
        self.out_shape = [jax.ShapeDtypeStruct(a.shape, a.dtype) for a in sums]
        self.scratch = [pltpu.SemaphoreType.DMA((n, N_DEV // 2)), pltpu.SemaphoreType.DMA((n, N_DEV // 2)),
                        pltpu.SemaphoreType.DMA((n,))]

    def _copies(self, ins, outs, sems, arrivals):
        send, recv, loc = sems
        (x, y, c), _ = _my_position()
        my_chip = 2 * x + y
        copies = []
        if not arrivals:
            copies = [pltpu.make_async_copy(ins[w].at[my_chip], outs[w].at[my_chip], loc.at[w]) for w in range(len(ins))]
        for k in (1, 2, 3):
            peer = (x ^ (k >> 1), y ^ (k & 1), c)
            chip = my_chip ^ k
            for w in range(len(ins)):
                copies.append(_remote(ins[w].at[my_chip if arrivals else chip], outs[w].at[chip if arrivals else my_chip],
                                      send.at[w, k], recv.at[w, k], peer))
        return copies

    def start(self, ins, outs, sems):
        for cp in self._copies(ins, outs, sems, False):
            cp.start()

    def mid(self, ins, outs, sems):
        pass

    def wait(self, ins, outs, sems):
        n = len(ins)
        for cp in self._copies(ins, outs, sems, True):
            cp.wait_recv()
        own = self._copies(ins, outs, sems, False)
        for cp in own[n:]:
            cp.wait_send()
        for cp in own[:n]:
            cp.wait()


def _split(refs, sizes):
    out, off = [], 0
    for size in sizes:
        out.append(refs[off:off + size])
        off += size
    return out


def _call(body, name, grid, args, in_specs, out_shape, out_specs, scratch=(), exchanges=()):
    sem = ("arbitrary",) * len(grid)
    exchanges = [e for e in exchanges if e is not None]
    if not exchanges:
        outs = pl.pallas_call(body, name=name, grid=grid, in_specs=in_specs, out_specs=out_specs, out_shape=out_shape,
                              scratch_shapes=list(scratch), compiler_params=_params(sem))(*args)
        return outs, []
    n_in, n_out, n_scr = len(args), len(out_shape), len(scratch)
    ex_in = [len(e.args) for e in exchanges]
    ex_out = [len(e.out_shape) for e in exchanges]
    ex_scr = [len(e.scratch) for e in exchanges]
    steps = math.prod(grid)
    mid_step = min((3 * steps) // 4, steps - 1)

    def hosted(*refs):
        main_in, xin, main_out, xout, main_scr, xscr = _split(
            refs, [n_in, sum(ex_in), n_out, sum(ex_out), n_scr, sum(ex_scr)])
        parts = list(zip(exchanges, _split(xin, ex_in), _split(xout, ex_out), _split(xscr, ex_scr)))
        step = pl.program_id(0)
        for a in range(1, len(grid)):
            step = step * grid[a] + pl.program_id(a)

        @pl.when(step == 0)
        def _():
            for e, ins, outs, sems in parts:
                e.start(ins, outs, sems)

        body(*main_in, *main_out, *main_scr)

        @pl.when(step == mid_step)
        def _():
            for e, ins, outs, sems in parts:
                e.mid(ins, outs, sems)

        @pl.when(step == steps - 1)
        def _():
            for e, ins, outs, sems in parts:
                e.wait(ins, outs, sems)

    any_spec = pl.BlockSpec(memory_space=pl.ANY)
    outs = pl.pallas_call(
        hosted, name=name, grid=grid, in_specs=list(in_specs) + [any_spec] * sum(ex_in),
        out_specs=list(out_specs) + [any_spec] * sum(ex_out),
        out_shape=list(out_shape) + [s for e in exchanges for s in e.out_shape],
        scratch_shapes=list(scratch) + [s for e in exchanges for s in e.scratch],
        compiler_params=_params(sem))(*args, *[a for e in exchanges for a in e.args])
    return outs[:n_out], _split(outs[n_out:], ex_out)


def _exchange_only(exchanges, name):
    ex_in = [len(e.args) for e in exchanges]
    ex_out = [len(e.out_shape) for e in exchanges]
    ex_scr = [len(e.scratch) for e in exchanges]

    def body(*refs):
        xin, xout, xscr = _split(refs, [sum(ex_in), sum(ex_out), sum(ex_scr)])
        parts = list(zip(exchanges, _split(xin, ex_in), _split(xout, ex_out), _split(xscr, ex_scr)))
        for phase in ("start", "mid", "wait"):
            for e, ins, outs, sems in parts:
                getattr(e, phase)(ins, outs, sems)

    any_spec = pl.BlockSpec(memory_space=pl.ANY)
    outs = pl.pallas_call(
        body, name=name, in_specs=[any_spec] * sum(ex_in), out_specs=[any_spec] * sum(ex_out),
        out_shape=[s for e in exchanges for s in e.out_shape],
        scratch_shapes=[s for e in exchanges for s in e.scratch])(*[a for e in exchanges for a in e.args])
    return _split(outs, ex_out)


def _tile_specs(t, d, di):
    row = lambda i: (i, 0)
    col = lambda i: (0, i)
    return dict(
        xd=pl.BlockSpec((TM, d), row), xi=pl.BlockSpec((TM, di), row),
        td=pl.BlockSpec((d, TM), col), ti=pl.BlockSpec((di, TM), col),
        s_xd=jax.ShapeDtypeStruct((t, d), F32), s_xi=jax.ShapeDtypeStruct((t, di), BF16),
        s_td=jax.ShapeDtypeStruct((d, t), BF16), s_ti=jax.ShapeDtypeStruct((di, t), BF16))


def _fwd_a(x, win, wout, gain, bias, wc2, bs2, lng, lnb, target=None, exchanges=()):
    t, d = x.shape
    di = wout.shape[0]
    gd = di // A_GROUPS
    n_loss = 0 if target is None else 1

    def body(*refs):
        x_ref, win_ref, wout_ref, gain_ref, bias_ref, wc_ref, bs_ref, lng_ref, lnb_ref = refs[:9]
        (a1_ref, a2_ref, a3_ref, vn_ref, vh_ref, rg_ref, yt_ref, xt_ref, pre_ref,
         xn_ref) = refs[9 + n_loss:19 + n_loss]
        p_scr, vg_scr, y_scr = refs[19 + 2 * n_loss:]
        xv = x_ref[...]
        xt_ref[...] = xv.T.astype(BF16)
        _in_proj(xv.astype(BF16), win_ref, p_scr)
        s1 = jnp.zeros((TM, 1), F32)
        for c in range(di // CW):
            sl = slice(c * CW, (c + 1) * CW)
            pv = slice(di + c * CW, di + (c + 1) * CW)
            vg, dvg = _gelu_and_grad(p_scr[:, pv])
            vg_scr[:, sl] = vg
            p_scr[:, pv] = dvg
            s1 += jnp.sum(vg, axis=1, keepdims=True)
        mu = s1 * (1.0 / di)
        s2 = jnp.zeros((TM, 1), F32)
        for c in range(di // CW):
            dlt = vg_scr[:, c * CW:(c + 1) * CW] - mu
            s2 += jnp.sum(dlt * dlt, axis=1, keepdims=True)
        rstd = lax.rsqrt(s2 * (1.0 / di) + LN_EPS)
        for c in range(di // CW):
            sl = slice(c * CW, (c + 1) * CW)
            vh = (vg_scr[:, sl] - mu) * rstd
            vh_ref[:, sl] = vh.astype(BF16)
            vn_ref[:, sl] = (vh * gain_ref[:, sl] + bias_ref[:, sl]).astype(BF16)
            rg_ref[:, sl] = (p_scr[:, di + c * CW:di + (c + 1) * CW] * rstd).astype(BF16)
        for g in range(A_GROUPS):
            sl = slice(g * gd, (g + 1) * gd)
            sv = jnp.dot(wc_ref[g], vn_ref[:, sl], preferred_element_type=F32) + bs_ref[g]
            u, du = _gelu_and_grad(p_scr[:, sl])
            s, ds = _silu_and_grad(p_scr[:, 2 * di + g * gd:2 * di + (g + 1) * gd])
            us = u * s
            a1_ref[:, sl] = (s * du).astype(BF16)
            a2_ref[:, sl] = (u * ds).astype(BF16)
            a3_ref[:, sl] = us.astype(BF16)
            y = us * sv
            y_scr[:, sl] = y.astype(BF16)
            yt_ref[sl, :] = y.T.astype(BF16)
        out = jnp.dot(y_scr[...], wout_ref[...], preferred_element_type=F32)
        _post_norm(xv, out, lng_ref, lnb_ref, pre_ref, xn_ref)
        if n_loss:
            t_ref, sq_ref = refs[9], refs[20]
            _zero_at_first_step(sq_ref)
            diff = xn_ref[...] - t_ref[...]
            xn_ref[...] = diff * (1.0 / d)
            sq_ref[...] += _fold8(diff * diff)

    sp = _tile_specs(t, d, di)
    loss_in = [] if target is None else [target]
    return _call(
        body, "fwd_a", (t // TM,), (x, win, wout, gain, bias, wc2, bs2, lng, lnb, *loss_in),
        in_specs=[sp["xd"]] + [_vmem()] * 8 + [sp["xd"]] * n_loss,
        out_specs=([sp["xi"]] * 6 + [sp["ti"], sp["td"], sp["xd"], sp["xd"]]
                   + [pl.BlockSpec((8, d), lambda i: (0, 0))] * n_loss),
        out_shape=([sp["s_xi"]] * 6 + [sp["s_ti"], sp["s_td"], sp["s_xd"], sp["s_xd"]]
                   + [jax.ShapeDtypeStruct((8, d), F32)] * n_loss),
        scratch=[pltpu.VMEM((TM, 3 * di), F32), pltpu.VMEM((TM, di), F32), pltpu.VMEM((TM, di), BF16)],
        exchanges=exchanges)


def _bwd_a(g, pre, a1, a2, a3, vn, vh, rg, wout, gain, wc2, wc2t, bs2, lng, exchanges=()):
    t, d = g.shape
    di = wout.shape[0]
    gd = di // A_GROUPS

    def body(g_ref, pre_ref, a1_ref, a2_ref, a3_ref, vn_ref, vh_ref, rg_ref,
             wout_ref, gain_ref, wc_ref, wct_ref, bs_ref, lng_ref,
             dpre_ref, dp_ref, dlng_ref, dlnb_ref, dgain_ref, dbias_ref, dbs_ref, dwc_ref,
             dy_scr, dv_scr):
        _zero_at_first_step(dlng_ref, dlnb_ref, dgain_ref, dbias_ref, dbs_ref, dwc_ref)
        dpre = _post_norm_bwd(g_ref, pre_ref, lng_ref, dpre_ref, dlng_ref, dlnb_ref)
        dy_scr[...] = lax.dot_general(dpre.astype(BF16), wout_ref[...], _NT, preferred_element_type=F32)
        for grp in range(A_GROUPS):
            sl = slice(grp * gd, (grp + 1) * gd)
            vn_g = vn_ref[:, sl]
            sv = jnp.dot(wc_ref[grp], vn_g, preferred_element_type=F32) + bs_ref[grp]
            dy = dy_scr[:, sl]
            dys = dy * sv
            dp_ref[:, sl] = (dys * _f32(a1_ref, sl)).astype(BF16)
            dp_ref[:, 2 * di + grp * gd:2 * di + (grp + 1) * gd] = (dys * _f32(a2_ref, sl)).astype(BF16)
            dsv = dy * _f32(a3_ref, sl)
            dbs_ref[grp] += jnp.sum(dsv, axis=1, keepdims=True)
            dsvb = dsv.astype(BF16)
            dwc_ref[grp] += lax.dot_general(dsvb, vn_g, _NT, preferred_element_type=F32)
            dv_scr[:, sl] = jnp.dot(wct_ref[grp], dsvb, preferred_element_type=F32)
        r1 = jnp.zeros((TM, 1), F32)
        r2 = jnp.zeros((TM, 1), F32)
        for c in range(di // CW):
            sl = slice(c * CW, (c + 1) * CW)
            dv = dv_scr[:, sl]
            vhat = _f32(vh_ref, sl)
            dgain_ref[:, sl] += _fold8(dv * vhat)
            dbias_ref[:, sl] += _fold8(dv)
            dvh = dv * gain_ref[:, sl]
            dv_scr[:, sl] = dvh
            r1 += jnp.sum(dvh, axis=1, keepdims=True)
            r2 += jnp.sum(dvh * vhat, axis=1, keepdims=True)
        m1 = r1 * (1.0 / di)
        m2 = r2 * (1.0 / di)
        for c in range(di // CW):
            sl = slice(c * CW, (c + 1) * CW)
            dp_ref[:, di + c * CW:di + (c + 1) * CW] = (
                (dv_scr[:, sl] - m1 - _f32(vh_ref, sl) * m2) * _f32(rg_ref, sl)).astype(BF16)

    sp = _tile_specs(t, d, di)
    const2 = lambda i: (0, 0)
    const3 = lambda i: (0, 0, 0)
    return _call(
        body, "bwd_a", (t // TM,), (g, pre, a1, a2, a3, vn, vh, rg, wout, gain, wc2, wc2t, bs2, lng),
        in_specs=[sp["xd"], sp["xd"]] + [sp["xi"]] * 6 + [_vmem()] * 6,
        out_specs=[sp["xd"], pl.BlockSpec((TM, 3 * di), lambda i: (i, 0)),
                   pl.BlockSpec((8, d), const2), pl.BlockSpec((8, d), const2),
                   pl.BlockSpec((8, di), const2), pl.BlockSpec((8, di), const2),
                   pl.BlockSpec((A_GROUPS, TM, 1), const3), pl.BlockSpec((A_GROUPS, TM, TM), const3)],
        out_shape=[sp["s_xd"], jax.ShapeDtypeStruct((t, 3 * di), BF16),
                   jax.ShapeDtypeStruct((8, d), F32), jax.ShapeDtypeStruct((8, d), F32),
                   jax.ShapeDtypeStruct((8, di), F32), jax.ShapeDtypeStruct((8, di), F32),
                   jax.ShapeDtypeStruct((A_GROUPS, TM, 1), F32), jax.ShapeDtypeStruct((A_GROUPS, TM, TM), F32)],
        scratch=[pltpu.VMEM((TM, di), F32), pltpu.VMEM((TM, di), F32)],
        exchanges=exchanges)


def _inv_count(tile, window, rows=TM):
    pos = tile * rows + lax.broadcasted_iota(jnp.int32, (rows, 1), 0)
    return 1.0 / jnp.minimum(pos + 1, window).astype(F32)


def _window_sum(ext, window, down):
    rows = ext.shape[0]
    k = 1
    while k < window:
        ext = ext + pltpu.roll(ext, k if down else rows - k, 0)
        k *= 2
    return ext


def _fwd_b(x, win, wgrp, scale, wout, lng, lnb, exchanges=()):
    t, d = x.shape
    di = wout.shape[0]
    gd = di // len(POOL_WINDOWS)

    def body(x_ref, win_ref, wgrp_ref, scale_ref, wout_ref, lng_ref, lnb_ref,
             b1_ref, b2_ref, b3_ref, poolt_ref, yt_ref, xt_ref, pre_ref, xn_ref, p_scr, ext_scr, y_scr):
        i = pl.program_id(0)

        @pl.when(i == 0)
        def _():
            ext_scr[0:HALO, :] = jnp.zeros((HALO, di), F32)

        xv = x_ref[...]
        xt_ref[...] = xv.T.astype(BF16)
        _in_proj(xv.astype(BF16), win_ref, p_scr)
        ext_scr[HALO:, :] = p_scr[:, :di]
        for grp, window in enumerate(POOL_WINDOWS):
            sl = slice(grp * gd, (grp + 1) * gd)
            ext = ext_scr[:, sl]
            pooled = (_window_sum(ext, window, True)[HALO:] * _inv_count(i, window) - ext[HALO:]).astype(BF16)
            poolt_ref[sl, :] = pooled.astype(F32).T.astype(BF16)
            mixed = jnp.dot(pooled, wgrp_ref[grp], preferred_element_type=F32)
            s, ds = _silu_and_grad(p_scr[:, di + grp * gd:di + (grp + 1) * gd])
            sc = scale_ref[:, sl]
            ms = mixed * s
            b1_ref[:, sl] = (mixed * sc * ds).astype(BF16)
            b2_ref[:, sl] = ms.astype(BF16)
            b3_ref[:, sl] = (sc * s).astype(BF16)
            y = ms * sc
            y_scr[:, sl] = y.astype(BF16)
            yt_ref[sl, :] = y.T.astype(BF16)
        ext_scr[0:HALO, :] = ext_scr[TM:TM + HALO, :]
        out = jnp.dot(y_scr[...], wout_ref[...], preferred_element_type=F32)
        _post_norm(xv, out, lng_ref, lnb_ref, pre_ref, xn_ref)

    sp = _tile_specs(t, d, di)
    return _call(
        body, "fwd_b", (t // TM,), (x, win, wgrp, scale, wout, lng, lnb),
        in_specs=[sp["xd"]] + [_vmem()] * 6,
        out_specs=[sp["xi"]] * 3 + [sp["ti"], sp["ti"], sp["td"], sp["xd"], sp["xd"]],
        out_shape=[sp["s_xi"]] * 3 + [sp["s_ti"], sp["s_ti"], sp["s_td"], sp["s_xd"], sp["s_xd"]],
        scratch=[pltpu.VMEM((TM, 2 * di), F32), pltpu.VMEM((TM + HALO, di), F32), pltpu.VMEM((TM, di), BF16)],
        exchanges=exchanges)


def _bwd_b(g, pre, b1, b2, b3, wout, wgrp, lng, exchanges=()):
    t, d = g.shape
    di = wout.shape[0]
    gd = di // len(POOL_WINDOWS)
    nt = t // TM

    def body(g_ref, pre_ref, b1_ref, b2_ref, b3_ref, wout_ref, wgrp_ref, lng_ref,
             dpre_ref, dp_ref, dmix_ref, dlng_ref, dlnb_ref, dscale_ref, dy_scr, ext_scr):
        i = pl.program_id(0)
        tile = nt - 1 - i
        _zero_at_first_step(dlng_ref, dlnb_ref, dscale_ref)

        @pl.when(i == 0)
        def _():
            ext_scr[TM:, :] = jnp.zeros((HALO, di), F32)

        dpre = _post_norm_bwd(g_ref, pre_ref, lng_ref, dpre_ref, dlng_ref, dlnb_ref)
        dy_scr[...] = lax.dot_general(dpre.astype(BF16), wout_ref[...], _NT, preferred_element_type=F32)
        for grp, window in enumerate(POOL_WINDOWS):
            sl = slice(grp * gd, (grp + 1) * gd)
            dy = dy_scr[:, sl]
            dp_ref[:, di + grp * gd:di + (grp + 1) * gd] = (dy * _f32(b1_ref, sl)).astype(BF16)
            dscale_ref[:, sl] += _fold8(dy * _f32(b2_ref, sl))
            dmixed = (dy * _f32(b3_ref, sl)).astype(BF16)
            dmix_ref[:, sl] = dmixed
            dpooled = lax.dot_general(dmixed, wgrp_ref[grp], _NT, preferred_element_type=F32)
            ext_scr[0:TM, sl] = dpooled * _inv_count(tile, window)
            dv = _window_sum(ext_scr[:, sl], window, False)[0:TM] - dpooled
            dp_ref[:, sl] = dv.astype(BF16)
        ext_scr[TM:, :] = ext_scr[0:HALO, :]

    rrow = lambda i: (nt - 1 - i, 0)
    const2 = lambda i: (0, 0)
    xd, xi = pl.BlockSpec((TM, d), rrow), pl.BlockSpec((TM, di), rrow)
    return _call(
        body, "bwd_b", (nt,), (g, pre, b1, b2, b3, wout, wgrp, lng),
        in_specs=[xd, xd, xi, xi, xi, _vmem(), _vmem(), _vmem()],
        out_specs=[xd, pl.BlockSpec((TM, 2 * di), rrow), xi,
                   pl.BlockSpec((8, d), const2), pl.BlockSpec((8, d), const2), pl.BlockSpec((8, di), const2)],
        out_shape=[jax.ShapeDtypeStruct((t, d), F32), jax.ShapeDtypeStruct((t, 2 * di), BF16),
                   jax.ShapeDtypeStruct((t, di), BF16),
                   jax.ShapeDtypeStruct((8, d), F32), jax.ShapeDtypeStruct((8, d), F32),
                   jax.ShapeDtypeStruct((8, di), F32)],
        scratch=[pltpu.VMEM((TM, di), F32), pltpu.VMEM((TM + HALO, di), F32)],
        exchanges=exchanges)


def _fwd_c(x, win, convw, wout, lng, lnb, exchanges=()):
    t, d = x.shape
    di = wout.shape[0]

    def body(x_ref, win_ref, cw_ref, wout_ref, lng_ref, lnb_ref,
             c1_ref, c2_ref, c3_ref, cg_ref, hg_ref, yt_ref, xt_ref, pre_ref, xn_ref, p_scr, ext_scr, y_scr):
        i = pl.program_id(0)

        @pl.when(i == 0)
        def _():
            ext_scr[0:CHALO, :] = jnp.zeros((CHALO, di), F32)

        xv = x_ref[...]
        xt_ref[...] = xv.T.astype(BF16)
        _in_proj(xv.astype(BF16), win_ref, p_scr)
        for c in range(di // CW):
            sl = slice(c * CW, (c + 1) * CW)
            bb = p_scr[:, sl]
            cc = p_scr[:, di + c * CW:di + (c + 1) * CW]
            hh = p_scr[:, 2 * di + c * CW:2 * di + (c + 1) * CW]
            s, ds = _silu_and_grad(p_scr[:, 3 * di + c * CW:3 * di + (c + 1) * CW])
            ext_scr[CHALO:, sl] = cc * hh
            ext = ext_scr[:, sl]
            conv = (pltpu.roll(ext, 2, 0)[CHALO:] * cw_ref[0:1, sl] + pltpu.roll(ext, 1, 0)[CHALO:] * cw_ref[1:2, sl]
                    + ext[CHALO:] * cw_ref[2:3, sl])
            cs = conv * s
            c1_ref[:, sl] = cs.astype(BF16)
            c2_ref[:, sl] = (bb * conv * ds).astype(BF16)
            c3_ref[:, sl] = (bb * s).astype(BF16)
            cg_ref[:, sl] = cc.astype(BF16)
            hg_ref[:, sl] = hh.astype(BF16)
            y = bb * cs
            y_scr[:, sl] = y.astype(BF16)
            yt_ref[sl, :] = y.T.astype(BF16)
        ext_scr[0:CHALO, :] = ext_scr[TM:TM + CHALO, :]
        out = jnp.dot(y_scr[...], wout_ref[...], preferred_element_type=F32)
        _post_norm(xv, out, lng_ref, lnb_ref, pre_ref, xn_ref)

    sp = _tile_specs(t, d, di)
    return _call(
        body, "fwd_c", (t // TM,), (x, win, convw, wout, lng, lnb),
        in_specs=[sp["xd"]] + [_vmem()] * 5,
        out_specs=[sp["xi"]] * 5 + [sp["ti"], sp["td"], sp["xd"], sp["xd"]],
        out_shape=[sp["s_xi"]] * 5 + [sp["s_ti"], sp["s_td"], sp["s_xd"], sp["s_xd"]],
        scratch=[pltpu.VMEM((TM, 4 * di), F32), pltpu.VMEM((TM + CHALO, di), F32), pltpu.VMEM((TM, di), BF16)],
        exchanges=exchanges)


def _bwd_c(g, pre, c1, c2, c3, cg, hg, wout, convw, lng, exchanges=()):
    t, d = g.shape
    di = wout.shape[0]
    nt = t // TM

    def body(g_ref, pre_ref, c1_ref, c2_ref, c3_ref, cg_ref, hg_ref, wout_ref, cw_ref, lng_ref,
             dpre_ref, dp_ref, dlng_ref, dlnb_ref, dcw_ref, dy_scr, ext_scr):
        i = pl.program_id(0)
        _zero_at_first_step(dlng_ref, dlnb_ref, dcw_ref)

        @pl.when(i == 0)
        def _():
            ext_scr[TM:, :] = jnp.zeros((CHALO, di), F32)

        dpre = _post_norm_bwd(g_ref, pre_ref, lng_ref, dpre_ref, dlng_ref, dlnb_ref)
        dy_scr[...] = lax.dot_general(dpre.astype(BF16), wout_ref[...], _NT, preferred_element_type=F32)
        rows = TM + CHALO
        for c in range(di // CW):
            sl = slice(c * CW, (c + 1) * CW)
            dy = dy_scr[:, sl]
            cc = _f32(cg_ref, sl)
            hh = _f32(hg_ref, sl)
            dp_ref[:, sl] = (dy * _f32(c1_ref, sl)).astype(BF16)
            dp_ref[:, 3 * di + c * CW:3 * di + (c + 1) * CW] = (dy * _f32(c2_ref, sl)).astype(BF16)
            dconv = dy * _f32(c3_ref, sl)
            ext_scr[0:TM, sl] = dconv
            ext = ext_scr[:, sl]
            d1 = pltpu.roll(ext, rows - 1, 0)[0:TM]
            d2 = pltpu.roll(ext, rows - 2, 0)[0:TM]
            dq = dconv * cw_ref[2:3, sl] + d1 * cw_ref[1:2, sl] + d2 * cw_ref[0:1, sl]
            q = cc * hh
            dcw_ref[0, :, sl] += _fold8(q * d2)
            dcw_ref[1, :, sl] += _fold8(q * d1)
            dcw_ref[2, :, sl] += _fold8(q * dconv)
            dp_ref[:, di + c * CW:di + (c + 1) * CW] = (dq * hh).astype(BF16)
            dp_ref[:, 2 * di + c * CW:2 * di + (c + 1) * CW] = (dq * cc).astype(BF16)
        ext_scr[TM:, :] = ext_scr[0:CHALO, :]

    rrow = lambda i: (nt - 1 - i, 0)
    const2 = lambda i: (0, 0)
    xd, xi = pl.BlockSpec((TM, d), rrow), pl.BlockSpec((TM, di), rrow)
    return _call(
        body, "bwd_c", (nt,), (g, pre, c1, c2, c3, cg, hg, wout, convw, lng),
        in_specs=[xd, xd] + [xi] * 5 + [_vmem()] * 3,
        out_specs=[xd, pl.BlockSpec((TM, 4 * di), rrow),
                   pl.BlockSpec((8, d), const2), pl.BlockSpec((8, d), const2),
                   pl.BlockSpec((3, 8, di), lambda i: (0, 0, 0))],
        out_shape=[jax.ShapeDtypeStruct((t, d), F32), jax.ShapeDtypeStruct((t, 4 * di), BF16),
                   jax.ShapeDtypeStruct((8, d), F32), jax.ShapeDtypeStruct((8, d), F32),
                   jax.ShapeDtypeStruct((3, 8, di), F32)],
        scratch=[pltpu.VMEM((TM, di), F32), pltpu.VMEM((TM + CHALO, di), F32)],
        exchanges=exchanges)


def _dx(dp, dpre, win, exchanges=()):
    t, d = dpre.shape
    n = dp.shape[1]
    cs = win.shape[2]

    def body(dp_ref, dpre_ref, win_ref, dx_ref):
        acc = ALPHA * dpre_ref[...]
        for j in range(N_DEV):
            acc += lax.dot_general(dp_ref[:, j * cs:(j + 1) * cs], win_ref[j], _NT, preferred_element_type=F32)
        dx_ref[...] = acc

    row = lambda i: (i, 0)
    (dx,), ex = _call(
        body, "dx", (t // TM,), (dp, dpre, win),
        in_specs=[pl.BlockSpec((TM, n), row), pl.BlockSpec((TM, d), row), _vmem()],
        out_specs=[pl.BlockSpec((TM, d), row)],
        out_shape=[jax.ShapeDtypeStruct((t, d), F32)],
        exchanges=exchanges)
    return dx, ex


def _wgrad(at, b, nb, per_block_rows, name, exchanges=()):
    m_all, t = at.shape
    tn = b.shape[1] // nb
    m = m_all // nb if per_block_rows else m_all
    tk = min(WGRAD_TK, t)
    nk = t // tk

    def body(at_ref, b_ref, out_ref, acc):
        k = pl.program_id(1)

        @pl.when(k == 0)
        def _():
            acc[...] = jnp.zeros(acc.shape, F32)

        acc[...] += jnp.dot(at_ref[...], b_ref[...].astype(BF16), preferred_element_type=F32)

        @pl.when(k == nk - 1)
        def _():
            out_ref[0] = acc[...].astype(BF16)

    at_map = (lambda j, k: (j, k)) if per_block_rows else (lambda j, k: (0, k))
    (out,), ex = _call(
        body, name, (nb, nk), (at, b),
        in_specs=[pl.BlockSpec((m, tk), at_map), pl.BlockSpec((tk, tn), lambda j, k: (k, j))],
        out_specs=[pl.BlockSpec((1, m, tn), lambda j, k: (j, 0, 0))],
        out_shape=[jax.ShapeDtypeStruct((nb, m, tn), BF16)],
        scratch=[pltpu.VMEM((m, tn), F32)],
        exchanges=exchanges)
    return out, ex


ADAMW_BLOCK_BYTES = 14 * 1024 * 1024


def _sum_parts(parts_ref):
    g = parts_ref[0].astype(F32)
    for s in range(1, parts_ref.shape[0]):
        g = g + parts_ref[s].astype(F32)
    return g


def _row_tile(rows, bytes_per_row):
    if rows * bytes_per_row <= ADAMW_BLOCK_BYTES:
        return rows
    best = 8
    for cand in range(8, rows, 8):
        if rows % cand == 0 and cand * bytes_per_row <= ADAMW_BLOCK_BYTES:
            best = cand
    return best


def _pair_sum(full, theirs, core, name):
    shape = theirs.shape
    r, c = math.prod(shape[1:-1]), shape[-1]
    tr = _row_tile(r, c * 3 * full.dtype.itemsize)
    half = N_DEV // 2

    def body(core_ref, a_ref, b_ref, out_ref):
        out_ref[...] = (a_ref[...].astype(F32) + b_ref[...].astype(F32)).astype(out_ref.dtype)

    blk = pl.BlockSpec((None, tr, c), lambda q, i, core_ref: (q, i, 0))
    grid_spec = pltpu.PrefetchScalarGridSpec(
        num_scalar_prefetch=1, grid=(half, r // tr),
        in_specs=[pl.BlockSpec((None, None, tr, c), lambda q, i, core_ref: (q, core_ref[0], i, 0)), blk],
        out_specs=blk)
    return pl.pallas_call(
        body, name=name, grid_spec=grid_spec, out_shape=jax.ShapeDtypeStruct((half, r, c), full.dtype),
        compiler_params=_params(("arbitrary", "arbitrary")),
    )(core, full.reshape(half, 2, r, c), theirs.reshape(half, r, c)).reshape(shape)


def _adamw(parts, w, m, v, name):
    s, r, c = parts.shape
    tr = _row_tile(r, c * (s * parts.dtype.itemsize + 7 * 4))

    def body(parts_ref, w_ref, m_ref, v_ref, g_ref, d_ref, nm_ref, nv_ref):
        g = _sum_parts(parts_ref)
        g_ref[...] = g
        d_ref[...], nm_ref[...], nv_ref[...] = _adamw_update(g, w_ref[...], m_ref[...], v_ref[...])

    blk = pl.BlockSpec((tr, c), lambda i: (i, 0))
    return pl.pallas_call(
        body, name=name, grid=(r // tr,),
        in_specs=[pl.BlockSpec((s, tr, c), lambda i: (0, i, 0)), blk, blk, blk],
        out_specs=[blk, blk, blk, blk],
        out_shape=[jax.ShapeDtypeStruct((r, c), F32)] * 4,
        compiler_params=_params(("arbitrary",)),
    )(parts, w, m, v)


def _adamw_update(g, w, m, v):
    bc1 = 1.0 - ADAM_B1 ** ADAM_STEP
    bc2 = 1.0 - ADAM_B2 ** ADAM_STEP
    nm = ADAM_B1 * m + (1.0 - ADAM_B1) * g
    nv = ADAM_B2 * v + (1.0 - ADAM_B2) * (g * g)
    return -ADAM_LR * ((nm / bc1) / (jnp.sqrt(nv / bc2) + ADAM_EPS) + ADAM_WD * w), nm, nv


def _adamw_bucket(parts, members, name):
    n = len(members)
    shapes = [w.shape for w, _, _ in members]
    rows = [math.prod(shp) // _LANES for shp in shapes]
    starts = [sum(rows[:k]) for k in range(n)]

    def body(parts_ref, *refs):
        for k in range(n):
            w_ref, m_ref, v_ref = refs[3 * k:3 * k + 3]
            g_ref, d_ref, nm_ref, nv_ref = refs[3 * n + 4 * k:3 * n + 4 * k + 4]
            g = _sum_parts(parts_ref.at[:, starts[k]:starts[k] + rows[k], :])
            g_ref[...] = g
            d_ref[...], nm_ref[...], nv_ref[...] = _adamw_update(g, w_ref[...], m_ref[...], v_ref[...])

    flat = [a.reshape(-1, _LANES) for mem in members for a in mem]
    outs = pl.pallas_call(
        body, name=name, in_specs=[_vmem()] * (1 + 3 * n), out_specs=[_vmem()] * (4 * n),
        out_shape=[jax.ShapeDtypeStruct((rows[k], _LANES), F32) for k in range(n) for _ in range(4)],
        compiler_params=_params(),
    )(parts, *flat)
    return [[o.reshape(shapes[k]) for o in outs[4 * k:4 * k + 4]] for k in range(n)]


_LANES = 128


def _pack(arrays):
    flat = jnp.concatenate([a.reshape(-1) for a in arrays])
    pad = (-flat.shape[0]) % (8 * _LANES)
    return jnp.pad(flat, (0, pad)).reshape(-1, _LANES)


def _unpack(packed, shapes):
    flat = packed.reshape(-1)
    out, off = [], 0
    for shp in shapes:
        size = math.prod(shp)
        out.append(flat[off:off + size].reshape(shp))
        off += size
    return out


def _spatial_weights(w_s, b_s, rows):
    reps = rows // CHUNK
    tril = jnp.tril(jnp.ones((CHUNK, CHUNK), F32))
    wc = w_s * tril
    eye = jnp.eye(reps, dtype=F32)
    wc2 = jnp.einsum("ab,gts->gatbs", eye, wc).reshape(A_GROUPS, rows, rows)
    bs2 = jnp.tile(b_s, (1, reps)).reshape(A_GROUPS, rows, 1)
    return wc2.astype(BF16), jnp.swapaxes(wc2, 1, 2).astype(BF16), bs2


def _spatial_weight_grad(dwc2, dbs2):
    reps = TM // CHUNK
    tril = jnp.tril(jnp.ones((CHUNK, CHUNK), F32))
    blocks = dwc2.reshape(A_GROUPS, reps, CHUNK, reps, CHUNK)
    dws = sum(blocks[:, a, :, a, :] for a in range(reps)) * tril
    dbs = dbs2.reshape(A_GROUPS, reps, CHUNK).sum(axis=1)
    return dws, dbs


def _row2(a):
    return a.reshape(1, -1)


def kernel(x, a0_w_in, a0_v_gain, a0_v_bias, a0_w_s, a0_b_s, a0_w_out, ln0_gain, ln0_bias, b1_w_in, b1_w_grp, b1_scale, b1_w_out, ln1_gain, ln1_bias, c2_w_in, c2_conv_w, c2_w_out, ln2_gain, ln2_bias, a3_w_in, a3_v_gain, a3_v_bias, a3_w_s, a3_b_s, a3_w_out, ln3_gain, ln3_bias, loss_target, m_a0_w_in, m_a0_v_gain, m_a0_v_bias, m_a0_w_s, m_a0_b_s, m_a0_w_out, m_ln0_gain, m_ln0_bias, m_b1_w_in, m_b1_w_grp, m_b1_scale, m_b1_w_out, m_ln1_gain, m_ln1_bias, m_c2_w_in, m_c2_conv_w, m_c2_w_out, m_ln2_gain, m_ln2_bias, m_a3_w_in, m_a3_v_gain, m_a3_v_bias, m_a3_w_s, m_a3_b_s, m_a3_w_out, m_ln3_gain, m_ln3_bias, v_a0_w_in, v_a0_v_gain, v_a0_v_bias, v_a0_w_s, v_a0_b_s, v_a0_w_out, v_ln0_gain, v_ln0_bias, v_b1_w_in, v_b1_w_grp, v_b1_scale, v_b1_w_out, v_ln1_gain, v_ln1_bias, v_c2_w_in, v_c2_conv_w, v_c2_w_out, v_ln2_gain, v_ln2_bias, v_a3_w_in, v_a3_v_gain, v_a3_v_bias, v_a3_w_s, v_a3_b_s, v_a3_w_out, v_ln3_gain, v_ln3_bias):
    names = ["a0_w_in", "a0_v_gain", "a0_v_bias", "a0_w_s", "a0_b_s", "a0_w_out", "ln0_gain", "ln0_bias",
             "b1_w_in", "b1_w_grp", "b1_scale", "b1_w_out", "ln1_gain", "ln1_bias",
             "c2_w_in", "c2_conv_w", "c2_w_out", "ln2_gain", "ln2_bias",
             "a3_w_in", "a3_v_gain", "a3_v_bias", "a3_w_s", "a3_b_s", "a3_w_out", "ln3_gain", "ln3_bias"]
    env = dict(locals())
    w = {nm: env[nm] for nm in names}
    mom = {nm: env["m_" + nm] for nm in names}
    var = {nm: env["v_" + nm] for nm in names}

    x0 = x[0]
    target = loss_target[0]
    d_model = x0.shape[1]
    di = N_DEV * a0_w_out.shape[0]
    n_grp = len(POOL_WINDOWS)
    gd_b = di // n_grp

    layers = ("a0", "b1", "c2", "a3")
    big_of = {"a0": ["a0_w_in", "a0_w_out"], "b1": ["b1_w_in", "b1_w_grp", "b1_w_out"],
              "c2": ["c2_w_in", "c2_w_out"], "a3": ["a3_w_in", "a3_w_out"]}
    big = [nm for p in layers for nm in big_of[p]]
    bucket_of = {"a0": ["a0_v_gain", "a0_v_bias", "a0_w_s", "a0_b_s", "ln0_gain", "ln0_bias"],
                 "b1": ["b1_scale", "ln1_gain", "ln1_bias"], "c2": ["ln2_gain", "ln2_bias"],
                 "a3": ["a3_v_gain", "a3_v_bias", "a3_w_s", "a3_b_s", "ln3_gain", "ln3_bias"]}
    conv_shape = c2_conv_w.shape
    spatial = {p: _spatial_weights(w[p + "_w_s"], w[p + "_b_s"], TM) for p in ("a0", "a3")}

    def weight_gather(p):
        return _Gather([w[nm].astype(BF16) for nm in big_of[p]])

    (first,) = _exchange_only([_Gather([w[nm].astype(BF16) for nm in big_of["a0"]] + [_pack([c2_conv_w])])],
                              "gather_first")
    gathered = dict(zip(big_of["a0"], first))
    conv_all = jnp.stack([_unpack(first[-1][j], [conv_shape])[0] for j in range(N_DEV)], axis=1)
    conv_full = conv_all.reshape(conv_shape[0], di)
    w_in = lambda p: gathered[p + "_w_in"]
    w_out = lambda p: gathered[p + "_w_out"].reshape(di, d_model)
    saved = {}
    h = x0
    for i, p in enumerate(layers):
        lng, lnb = _row2(w[f"ln{i}_gain"]), _row2(w[f"ln{i}_bias"])
        nxt = layers[i + 1] if i + 1 < len(layers) else None
        ex = [weight_gather(nxt)] if nxt else []
        if p[0] == "a":
            wc2, _, bs2 = spatial[p]
            outs, got = _fwd_a(h, w_in(p), w_out(p), _row2(w[p + "_v_gain"]), _row2(w[p + "_v_bias"]),
                               wc2, bs2, lng, lnb, target=None if nxt else target, exchanges=ex)
        elif p[0] == "b":
            wgrp = jnp.swapaxes(gathered["b1_w_grp"], 0, 1).reshape(n_grp, gd_b, gd_b)
            outs, got = _fwd_b(h, w_in(p), wgrp, _row2(w[p + "_scale"]), w_out(p), lng, lnb, exchanges=ex)
        else:
            outs, got = _fwd_c(h, w_in(p), conv_full, w_out(p), lng, lnb, exchanges=ex)
        if nxt:
            saved[p], h = outs[:-1], outs[-1]
            gathered.update(zip(big_of[nxt], got[0]))
        else:
            saved[p], gcur, sq = outs[:-2], outs[-2], outs[-1]

    loss_share = (jnp.sum(sq) * (0.5 / d_model)).reshape(1)

    part, full, landed, small_all = {}, {}, {}, {}

    def bucket_gather(bucket):
        extra = [loss_share] if bucket == layers[-1] else []
        return _Gather([_pack([part[nm] for nm in bucket_of[bucket]] + extra)])

    core = lax.axis_index("c").astype(jnp.int32).reshape(1)

    def chip_sums(names, got):
        return [_pair_sum(full[nm], theirs, core, "pair_sum_" + nm) for nm, theirs in zip(names, got)]

    pending = None
    for i, p in reversed(list(enumerate(layers))):
        lng = _row2(w[f"ln{i}_gain"])
        ex = []
        if pending:
            ex = [_PairExchange([full[nm] for nm in big_of[pending]]), bucket_gather(pending)]
            if pending == "c2":
                ex.append(_Exchange(scatters=[full["c2_conv_w"]]))
        *factors, yt, xt, pre = saved[p]
        if p[0] == "a":
            wc2, wc2t, bs2 = spatial[p]
            (dpre, dp, dlng, dlnb, dgain, dbias, dbs2, dwc2), got = _bwd_a(
                gcur, pre, *factors, w_out(p), _row2(w[p + "_v_gain"]), wc2, wc2t, bs2, lng, exchanges=ex)
            part[p + "_v_gain"], part[p + "_v_bias"] = dgain.sum(axis=0), dbias.sum(axis=0)
            part[p + "_w_s"], part[p + "_b_s"] = _spatial_weight_grad(dwc2, dbs2)
        elif p[0] == "b":
            b1f, b2f, b3f, poolt = factors
            (dpre, dp, dmixed, dlng, dlnb, dscale), got = _bwd_b(
                gcur, pre, b1f, b2f, b3f, w_out(p), wgrp, lng, exchanges=ex)
            part[p + "_scale"] = dscale.sum(axis=0)
            dwg, _ = _wgrad(poolt, dmixed, n_grp, True, "wgrad_grp")
            full[p + "_w_grp"] = jnp.swapaxes(dwg.reshape(n_grp, N_DEV, gd_b // N_DEV, gd_b), 0, 1)
        else:
            (dpre, dp, dlng, dlnb, dcw), got = _bwd_c(gcur, pre, *factors, w_out(p), conv_full, lng, exchanges=ex)
            dconv = dcw.sum(axis=1).reshape(conv_shape[0], N_DEV, conv_shape[1])
            full["c2_conv_w"] = jnp.stack([_pack([dconv[:, j]]) for j in range(N_DEV)])
        part[f"ln{i}_gain"], part[f"ln{i}_bias"] = dlng.sum(axis=0), dlnb.sum(axis=0)
        if pending:
            small_all[pending] = got[1][0]
            if pending == "c2":
                small_all["conv"] = got[2][0]
        dwo, _ = _wgrad(yt, dpre, 1, False, "wgrad_out_" + p)
        full[p + "_w_out"] = dwo.reshape(N_DEV, di // N_DEV, d_model)
        ex = [_ChipScatter(chip_sums(big_of[pending], got[0]))] if pending else []
        if i == 0:
            ex += [bucket_gather(p)]
        full[p + "_w_in"], got = _wgrad(xt, dp, N_DEV, False, "wgrad_in_" + p, exchanges=ex)
        if pending:
            landed.update(zip(big_of[pending], got[0]))
        if i > 0:
            gcur, _ = _dx(dp, dpre, w_in(p))
            pending = p
        else:
            small_all[p] = got[-1][0]
            (theirs,) = _exchange_only([_PairExchange([full[p + "_w_in"]])], "pair_exchange_last")
            gcur, got = _dx(dp, dpre, w_in(p), exchanges=[
                _ChipScatter(chip_sums([p + "_w_in"], theirs)), _Exchange(scatters=[full[p + "_w_out"]])])
            landed[p + "_w_in"], landed[p + "_w_out"] = got[0][0], got[1][0]
    grad_x = gcur[None]

    grads, deltas, new_m, new_v = {}, {}, {}, {}
    for nm in big:
        shp = w[nm].shape
        r2 = (math.prod(shp[:-1]), shp[-1])
        outs = _adamw(landed[nm].reshape((-1,) + r2), w[nm].reshape(r2), mom[nm].reshape(r2), var[nm].reshape(r2),
                      "adamw_" + nm)
        grads[nm], deltas[nm], new_m[nm], new_v[nm] = (o.reshape(shp) for o in outs)
    buckets = dict(bucket_of, conv=["c2_conv_w"])
    for key, members in buckets.items():
        outs = _adamw_bucket(small_all[key], [(w[nm], mom[nm], var[nm]) for nm in members], "adamw_small_" + key)
        for nm, (g_out, d_out, m_out, v_out) in zip(members, outs):
            grads[nm], deltas[nm], new_m[nm], new_v[nm] = g_out, d_out, m_out, v_out

    loss_row = sum(math.prod(w[nm].shape) for nm in bucket_of[layers[-1]]) // _LANES
    loss = jnp.sum(small_all[layers[-1]][:, loss_row, 0])

    return (loss, grad_x, *[grads[nm] for nm in names], *[deltas[nm] for nm in names],
            *[new_m[nm] for nm in names], *[new_v[nm] for nm in names])
```

```python
import functools
import math

import jax
import jax.numpy as jnp
from jax import lax
from jax.experimental import pallas as pl
from jax.experimental.pallas import tpu as pltpu

F32 = jnp.float32
BF16 = jnp.bfloat16

N_DEV = 8
DEPTH = 4
CHUNK = 128
A_GROUPS = 8
POOL_WINDOWS = (2, 4, 8, 16)
LN_EPS = 1e-5
ALPHA = (2.0 * DEPTH) ** 0.25
ADAM_LR = 0.001
ADAM_B1 = 0.9
ADAM_B2 = 0.999
ADAM_EPS = 1e-08
ADAM_WD = 0.01
ADAM_STEP = 10

TM = 256
HALO = 16
CHALO = 8
CW = 512
WGRAD_TK = 2048
VMEM_LIMIT_BYTES = 58 * 1024 * 1024

_NT = (((1,), (1,)), ((), ()))
_SQRT_2_OVER_PI = math.sqrt(2.0 / math.pi)
_MESH = pl.DeviceIdType.MESH


def _vmem():
    return pl.BlockSpec(memory_space=pltpu.VMEM)


def _params(sem=None):
    return pltpu.CompilerParams(dimension_semantics=sem, vmem_limit_bytes=VMEM_LIMIT_BYTES)


def _gelu_and_grad(x):
    c1 = _SQRT_2_OVER_PI * 0.044715
    x2 = x * x
    t = jnp.tanh(x * (_SQRT_2_OVER_PI + c1 * x2))
    cdf = 0.5 + 0.5 * t
    grad = cdf + x * (1.0 - t * t) * (0.5 * _SQRT_2_OVER_PI + (1.5 * c1) * x2)
    return x * cdf, grad


def _silu_and_grad(z):
    sg = 1.0 / (1.0 + jnp.exp(-z))
    s = z * sg
    return s, sg + s - s * sg


def _fold8(a):
    return a.reshape(a.shape[0] // 8, 8, a.shape[1]).sum(axis=0)


def _row_mean(a):
    return jnp.mean(a, axis=-1, keepdims=True)


def _ln_stats(x):
    mu = _row_mean(x)
    xc = x - mu
    rstd = lax.rsqrt(_row_mean(xc * xc) + LN_EPS)
    return xc * rstd, rstd


def _post_norm(x, out, lng_ref, lnb_ref, pre_ref, xn_ref):
    pre = ALPHA * x + out
    pre_ref[...] = pre
    xhat, _ = _ln_stats(pre)
    xn_ref[...] = xhat * lng_ref[...] + lnb_ref[...]


def _post_norm_bwd(g_ref, pre_ref, lng_ref, dpre_ref, dlng_ref, dlnb_ref):
    go = g_ref[...]
    xhat, rstd = _ln_stats(pre_ref[...])
    dlng_ref[...] += _fold8(go * xhat)
    dlnb_ref[...] += _fold8(go)
    dxh = go * lng_ref[...]
    dpre = rstd * (dxh - _row_mean(dxh) - xhat * _row_mean(dxh * xhat))
    dpre_ref[...] = dpre
    return dpre


def _in_proj(xb, win_ref, p_ref, lo=0, hi=None):
    cs = win_ref.shape[2]
    hi = N_DEV * cs if hi is None else hi
    for j in range(N_DEV):
        a, b = max(lo, j * cs), min(hi, (j + 1) * cs)
        if a < b:
            p_ref[:, a:b] = jnp.dot(xb, win_ref[j, :, a - j * cs:b - j * cs], preferred_element_type=F32)


def _zero_at_first_step(*refs):
    @pl.when(pl.program_id(0) == 0)
    def _():
        for r in refs:
            r[...] = jnp.zeros(r.shape, r.dtype)


def _f32(ref, sl):
    return ref[:, sl].astype(F32)


def _my_position():
    x, y, c = lax.axis_index("x"), lax.axis_index("y"), lax.axis_index("c")
    return (x, y, c), 4 * x + 2 * y + c


def _peer(k):
    (x, y, c), _ = _my_position()
    peer = (x ^ (k >> 2), y ^ ((k >> 1) & 1), c ^ (k & 1))
    return peer, 4 * peer[0] + 2 * peer[1] + peer[2]


class _Exchange:
    def __init__(self, gathers=(), scatters=()):
        self.args = list(gathers) + list(scatters)
        self.n_gather = len(gathers)
        self.out_shape = ([jax.ShapeDtypeStruct((N_DEV,) + a.shape, a.dtype) for a in gathers]
                          + [jax.ShapeDtypeStruct(a.shape, a.dtype) for a in scatters])
        n = len(self.args)
        self.scratch = [pltpu.SemaphoreType.DMA((n, N_DEV)), pltpu.SemaphoreType.DMA((n, N_DEV)),
                        pltpu.SemaphoreType.DMA((n,))]

    def _src(self, ins, w, pos):
        return ins[w] if w < self.n_gather else ins[w].at[pos]

    def _copies(self, ins, outs, sems, arrivals):
        send_sems, recv_sems, local_sems = sems
        _, me = _my_position()
        n = len(self.args)
        copies = []
        if not arrivals:
            copies = [pltpu.make_async_copy(self._src(ins, w, me), outs[w].at[me], local_sems.at[w]) for w in range(n)]
        for k in range(1, N_DEV):
            peer, peer_pos = _peer(k)
            for w in range(n):
                copies.append(pltpu.make_async_remote_copy(
                    src_ref=self._src(ins, w, me if arrivals else peer_pos),
                    dst_ref=outs[w].at[peer_pos if arrivals else me],
                    send_sem=send_sems.at[w, k], recv_sem=recv_sems.at[w, k], device_id=peer, device_id_type=_MESH))
        return copies

    def start(self, ins, outs, sems):
        for cp in self._copies(ins, outs, sems, False):
            cp.start()

    def mid(self, ins, outs, sems):
        pass

    def wait(self, ins, outs, sems):
        n = len(self.args)
        for cp in self._copies(ins, outs, sems, True):
            cp.wait_recv()
        own = self._copies(ins, outs, sems, False)
        for cp in own[n:]:
            cp.wait_send()
        for cp in own[:n]:
            cp.wait()


def _remote(src, dst, send_sem, recv_sem, peer):
    return pltpu.make_async_remote_copy(src_ref=src, dst_ref=dst, send_sem=send_sem, recv_sem=recv_sem,
                                        device_id=peer, device_id_type=_MESH)


class _Gather:
    def __init__(self, shards):
        self.args = list(shards)
        n = len(self.args)
        self.out_shape = [jax.ShapeDtypeStruct((N_DEV,) + a.shape, a.dtype) for a in shards]
        self.scratch = [pltpu.SemaphoreType.DMA((n, N_DEV)), pltpu.SemaphoreType.DMA((n, N_DEV)),
                        pltpu.SemaphoreType.DMA((n,))]

    def _own(self, ins, outs, sems):
        send, recv, loc = sems
        _, me = _my_position()
        local = [pltpu.make_async_copy(ins[w], outs[w].at[me], loc.at[w]) for w in range(len(ins))]
        first = [_remote(ins[w], outs[w].at[me], send.at[w, k], recv.at[w, k], _peer(k)[0])
                 for k in (1, 2, 4, 6) for w in range(len(ins))]
        return local, first

    def _passed_on(self, ins, outs, sems):
        send, recv, _ = sems
        sibling, _ = _peer(1)
        return [_remote(outs[w].at[_peer(k)[1]], outs[w].at[_peer(k)[1]], send.at[w, k + 1], recv.at[w, k + 1], sibling)
                for k in (2, 4, 6) for w in range(len(ins))]

    def _arrival(self, ins, outs, sems, k, w):
        send, recv, _ = sems
        peer, pos = _peer(k)
        return _remote(ins[w], outs[w].at[pos], send.at[w, k], recv.at[w, k], peer)

    def start(self, ins, outs, sems):
        local, first = self._own(ins, outs, sems)
        for cp in local + first:
            cp.start()

    def mid(self, ins, outs, sems):
        for k in (2, 4, 6):
            for w in range(len(ins)):
                self._arrival(ins, outs, sems, k, w).wait_recv()
        for cp in self._passed_on(ins, outs, sems):
            cp.start()

    def wait(self, ins, outs, sems):
        for k in (1, 3, 5, 7):
            for w in range(len(ins)):
                self._arrival(ins, outs, sems, k, w).wait_recv()
        local, first = self._own(ins, outs, sems)
        for cp in first + self._passed_on(ins, outs, sems):
            cp.wait_send()
        for cp in local:
            cp.wait()


class _PairExchange:
    def __init__(self, fulls):
        self.args = list(fulls)
        n = len(self.args)
        self.out_shape = [jax.ShapeDtypeStruct((N_DEV // 2,) + a.shape[1:], a.dtype) for a in fulls]
        self.scratch = [pltpu.SemaphoreType.DMA((n, N_DEV // 2)), pltpu.SemaphoreType.DMA((n, N_DEV // 2))]

    def _copies(self, ins, outs, sems):
        send, recv = sems
        (x, y, c), _ = _my_position()
        sibling, _ = _peer(1)
        return [_remote(ins[w].at[2 * q + 1 - c], outs[w].at[q], send.at[w, q], recv.at[w, q], sibling)
                for q in range(N_DEV // 2) for w in range(len(ins))]

    def start(self, ins, outs, sems):
        for cp in self._copies(ins, outs, sems):
            cp.start()

    def mid(self, ins, outs, sems):
        pass

    def wait(self, ins, outs, sems):
        for cp in self._copies(ins, outs, sems):
            cp.wait()


class _ChipScatter:
    def __init__(self, sums):
        self.args = list(sums)
        n = len(self.args)
        self.out_shape = [jax.ShapeDtypeStruct(a.shape, a.dtype) for a in sums]
        self.scratch = [pltpu.SemaphoreType.DMA((n, N_DEV // 2)), pltpu.SemaphoreType.DMA((n, N_DEV // 2)),
                        pltpu.SemaphoreType.DMA((n,))]

    def _copies(self, ins, outs, sems, arrivals):
        send, recv, loc = sems
        (x, y, c), _ = _my_position()
        my_chip = 2 * x + y
        copies = []
        if not arrivals:
            copies = [pltpu.make_async_copy(ins[w].at[my_chip], outs[w].at[my_chip], loc.at[w]) for w in range(len(ins))]
        for k in (1, 2, 3):
            peer = (x ^ (k >> 1), y ^ (k & 1), c)
            chip = my_chip ^ k
            for w in range(len(ins)):
                copies.append(_remote(ins[w].at[my_chip if arrivals else chip], outs[w].at[chip if arrivals else my_chip],
                                      send.at[w, k], recv.at[w, k], peer))
        return copies

    def start(self, ins, outs, sems):
        for cp in self._copies(ins, outs, sems, False):
            cp.start()

    def mid(self, ins, outs, sems):
        pass

    def wait(self, ins, outs, sems):
        n = len(ins)
        for cp in self._copies(ins, outs, sems, True):
            cp.wait_recv()
        own = self._copies(ins, outs, sems, False)
        for cp in own[n:]:
            cp.wait_send()
        for cp in own[:n]:
            cp.wait()


def _split(refs, sizes):
    out, off = [], 0
    for size in sizes:
        out.append(refs[off:off + size])
        off += size
    return out


def _call(body, name, grid, args, in_specs, out_shape, out_specs, scratch=(), exchanges=()):
    sem = ("arbitrary",) * len(grid)
    exchanges = [e for e in exchanges if e is not None]
    if not exchanges:
        outs = pl.pallas_call(body, name=name, grid=grid, in_specs=in_specs, out_specs=out_specs, out_shape=out_shape,
                              scratch_shapes=list(scratch), compiler_params=_params(sem))(*args)
        return outs, []
    n_in, n_out, n_scr = len(args), len(out_shape), len(scratch)
    ex_in = [len(e.args) for e in exchanges]
    ex_out = [len(e.out_shape) for e in exchanges]
    ex_scr = [len(e.scratch) for e in exchanges]
    steps = math.prod(grid)
    mid_step = min((3 * steps) // 4, steps - 1)

    def hosted(*refs):
        main_in, xin, main_out, xout, main_scr, xscr = _split(
            refs, [n_in, sum(ex_in), n_out, sum(ex_out), n_scr, sum(ex_scr)])
        parts = list(zip(exchanges, _split(xin, ex_in), _split(xout, ex_out), _split(xscr, ex_scr)))
        step = pl.program_id(0)
        for a in range(1, len(grid)):
            step = step * grid[a] + pl.program_id(a)

        @pl.when(step == 0)
        def _():
            for e, ins, outs, sems in parts:
                e.start(ins, outs, sems)

        body(*main_in, *main_out, *main_scr)

        @pl.when(step == mid_step)
        def _():
            for e, ins, outs, sems in parts:
                e.mid(ins, outs, sems)

        @pl.when(step == steps - 1)
        def _():
            for e, ins, outs, sems in parts:
                e.wait(ins, outs, sems)

    any_spec = pl.BlockSpec(memory_space=pl.ANY)
    outs = pl.pallas_call(
        hosted, name=name, grid=grid, in_specs=list(in_specs) + [any_spec] * sum(ex_in),
        out_specs=list(out_specs) + [any_spec] * sum(ex_out),
        out_shape=list(out_shape) + [s for e in exchanges for s in e.out_shape],
        scratch_shapes=list(scratch) + [s for e in exchanges for s in e.scratch],
        compiler_params=_params(sem))(*args, *[a for e in exchanges for a in e.args])
    return outs[:n_out], _split(outs[n_out:], ex_out)


def _exchange_only(exchanges, name):
    ex_in = [len(e.args) for e in exchanges]
    ex_out = [len(e.out_shape) for e in exchanges]
    ex_scr = [len(e.scratch) for e in exchanges]

    def body(*refs):
        xin, xout, xscr = _split(refs, [sum(ex_in), sum(ex_out), sum(ex_scr)])
        parts = list(zip(exchanges, _split(xin, ex_in), _split(xout, ex_out), _split(xscr, ex_scr)))
        for phase in ("start", "mid", "wait"):
            for e, ins, outs, sems in parts:
                getattr(e, phase)(ins, outs, sems)

    any_spec = pl.BlockSpec(memory_space=pl.ANY)
    outs = pl.pallas_call(
        body, name=name, in_specs=[any_spec] * sum(ex_in), out_specs=[any_spec] * sum(ex_out),
        out_shape=[s for e in exchanges for s in e.out_shape],
        scratch_shapes=[s for e in exchanges for s in e.scratch])(*[a for e in exchanges for a in e.args])
    return _split(outs, ex_out)


def _tile_specs(t, d, di):
    row = lambda i: (i, 0)
    col = lambda i: (0, i)
    return dict(
        xd=pl.BlockSpec((TM, d), row), xi=pl.BlockSpec((TM, di), row),
        td=pl.BlockSpec((d, TM), col), ti=pl.BlockSpec((di, TM), col),
        s_xd=jax.ShapeDtypeStruct((t, d), F32), s_xi=jax.ShapeDtypeStruct((t, di), BF16),
        s_td=jax.ShapeDtypeStruct((d, t), BF16), s_ti=jax.ShapeDtypeStruct((di, t), BF16))


def _fwd_a(x, win, wout, gain, bias, wc2, bs2, lng, lnb, target=None, exchanges=()):
    t, d = x.shape
    di = wout.shape[0]
    gd = di // A_GROUPS
    n_loss = 0 if target is None else 1

    def body(*refs):
        x_ref, win_ref, wout_ref, gain_ref, bias_ref, wc_ref, bs_ref, lng_ref, lnb_ref = refs[:9]
        (a1_ref, a2_ref, a3_ref, vn_ref, vh_ref, rg_ref, yt_ref, xt_ref, pre_ref,
         xn_ref) = refs[9 + n_loss:19 + n_loss]
        p_scr, vg_scr, y_scr = refs[19 + 2 * n_loss:]
        xv = x_ref[...]
        xt_ref[...] = xv.T.astype(BF16)
        xb = xv.astype(BF16)
        _in_proj(xb, win_ref, p_scr, di, 2 * di)
        s1 = jnp.zeros((TM, 1), F32)
        for c in range(di // CW):
            sl = slice(c * CW, (c + 1) * CW)
            pv = slice(di + c * CW, di + (c + 1) * CW)
            vg, dvg = _gelu_and_grad(p_scr[:, pv])
            vg_scr[:, sl] = vg
            p_scr[:, pv] = dvg
            s1 += jnp.sum(vg, axis=1, keepdims=True)
        mu = s1 * (1.0 / di)
        s2 = jnp.zeros((TM, 1), F32)
        for c in range(di // CW):
            dlt = vg_scr[:, c * CW:(c + 1) * CW] - mu
            s2 += jnp.sum(dlt * dlt, axis=1, keepdims=True)
        rstd = lax.rsqrt(s2 * (1.0 / di) + LN_EPS)
        for c in range(di // CW):
            sl = slice(c * CW, (c + 1) * CW)
            vh = (vg_scr[:, sl] - mu) * rstd
            vh_ref[:, sl] = vh.astype(BF16)
            vn_ref[:, sl] = (vh * gain_ref[:, sl] + bias_ref[:, sl]).astype(BF16)
            rg_ref[:, sl] = (p_scr[:, di + c * CW:di + (c + 1) * CW] * rstd).astype(BF16)
        for g in range(A_GROUPS):
            sl = slice(g * gd, (g + 1) * gd)
            _in_proj(xb, win_ref, p_scr, g * gd, (g + 1) * gd)
            _in_proj(xb, win_ref, p_scr, 2 * di + g * gd, 2 * di + (g + 1) * gd)
            sv = jnp.dot(wc_ref[g], vn_ref[:, sl], preferred_element_type=F32) + bs_ref[g]
            u, du = _gelu_and_grad(p_scr[:, sl])
            s, ds = _silu_and_grad(p_scr[:, 2 * di + g * gd:2 * di + (g + 1) * gd])
            us = u * s
            a1_ref[:, sl] = (s * du).astype(BF16)
            a2_ref[:, sl] = (u * ds).astype(BF16)
            a3_ref[:, sl] = us.astype(BF16)
            y = us * sv
            y_scr[:, sl] = y.astype(BF16)
            yt_ref[sl, :] = y.T.astype(BF16)
        out = jnp.dot(y_scr[...], wout_ref[...], preferred_element_type=F32)
        _post_norm(xv, out, lng_ref, lnb_ref, pre_ref, xn_ref)
        if n_loss:
            t_ref, sq_ref = refs[9], refs[20]
            _zero_at_first_step(sq_ref)
            diff = xn_ref[...] - t_ref[...]
            xn_ref[...] = diff * (1.0 / d)
            sq_ref[...] += _fold8(diff * diff)

    sp = _tile_specs(t, d, di)
    loss_in = [] if target is None else [target]
    return _call(
        body, "fwd_a", (t // TM,), (x, win, wout, gain, bias, wc2, bs2, lng, lnb, *loss_in),
        in_specs=[sp["xd"]] + [_vmem()] * 8 + [sp["xd"]] * n_loss,
        out_specs=([sp["xi"]] * 6 + [sp["ti"], sp["td"], sp["xd"], sp["xd"]]
                   + [pl.BlockSpec((8, d), lambda i: (0, 0))] * n_loss),
        out_shape=([sp["s_xi"]] * 6 + [sp["s_ti"], sp["s_td"], sp["s_xd"], sp["s_xd"]]
                   + [jax.ShapeDtypeStruct((8, d), F32)] * n_loss),
        scratch=[pltpu.VMEM((TM, 3 * di), F32), pltpu.VMEM((TM, di), F32), pltpu.VMEM((TM, di), BF16)],
        exchanges=exchanges)


def _bwd_a(g, pre, a1, a2, a3, vn, vh, rg, wout, gain, wc2, wc2t, bs2, lng, exchanges=()):
    t, d = g.shape
    di = wout.shape[0]
    gd = di // A_GROUPS

    def body(g_ref, pre_ref, a1_ref, a2_ref, a3_ref, vn_ref, vh_ref, rg_ref,
             wout_ref, gain_ref, wc_ref, wct_ref, bs_ref, lng_ref,
             dpre_ref, dp_ref, dlng_ref, dlnb_ref, dgain_ref, dbias_ref, dbs_ref, dwc_ref,
             dy_scr, dv_scr):
        _zero_at_first_step(dlng_ref, dlnb_ref, dgain_ref, dbias_ref, dbs_ref, dwc_ref)
        dpre = _post_norm_bwd(g_ref, pre_ref, lng_ref, dpre_ref, dlng_ref, dlnb_ref)
        dy_scr[...] = lax.dot_general(dpre.astype(BF16), wout_ref[...], _NT, preferred_element_type=F32)
        for grp in range(A_GROUPS):
            sl = slice(grp * gd, (grp + 1) * gd)
            vn_g = vn_ref[:, sl]
            sv = jnp.dot(wc_ref[grp], vn_g, preferred_element_type=F32) + bs_ref[grp]
            dy = dy_scr[:, sl]
            dys = dy * sv
            dp_ref[:, sl] = (dys * _f32(a1_ref, sl)).astype(BF16)
            dp_ref[:, 2 * di + grp * gd:2 * di + (grp + 1) * gd] = (dys * _f32(a2_ref, sl)).astype(BF16)
            dsv = dy * _f32(a3_ref, sl)
            dbs_ref[grp] += jnp.sum(dsv, axis=1, keepdims=True)
            dsvb = dsv.astype(BF16)
            dwc_ref[grp] += lax.dot_general(dsvb, vn_g, _NT, preferred_element_type=F32)
            dv_scr[:, sl] = jnp.dot(wct_ref[grp], dsvb, preferred_element_type=F32)
        r1 = jnp.zeros((TM, 1), F32)
        r2 = jnp.zeros((TM, 1), F32)
        for c in range(di // CW):
            sl = slice(c * CW, (c + 1) * CW)
            dv = dv_scr[:, sl]
            vhat = _f32(vh_ref, sl)
            dgain_ref[:, sl] += _fold8(dv * vhat)
            dbias_ref[:, sl] += _fold8(dv)
            dvh = dv * gain_ref[:, sl]
            dv_scr[:, sl] = dvh
            r1 += jnp.sum(dvh, axis=1, keepdims=True)
            r2 += jnp.sum(dvh * vhat, axis=1, keepdims=True)
        m1 = r1 * (1.0 / di)
        m2 = r2 * (1.0 / di)
        for c in range(di // CW):
            sl = slice(c * CW, (c + 1) * CW)
            dp_ref[:, di + c * CW:di + (c + 1) * CW] = (
                (dv_scr[:, sl] - m1 - _f32(vh_ref, sl) * m2) * _f32(rg_ref, sl)).astype(BF16)

    sp = _tile_specs(t, d, di)
    const2 = lambda i: (0, 0)
    const3 = lambda i: (0, 0, 0)
    return _call(
        body, "bwd_a", (t // TM,), (g, pre, a1, a2, a3, vn, vh, rg, wout, gain, wc2, wc2t, bs2, lng),
        in_specs=[sp["xd"], sp["xd"]] + [sp["xi"]] * 6 + [_vmem()] * 6,
        out_specs=[sp["xd"], pl.BlockSpec((TM, 3 * di), lambda i: (i, 0)),
                   pl.BlockSpec((8, d), const2), pl.BlockSpec((8, d), const2),
                   pl.BlockSpec((8, di), const2), pl.BlockSpec((8, di), const2),
                   pl.BlockSpec((A_GROUPS, TM, 1), const3), pl.BlockSpec((A_GROUPS, TM, TM), const3)],
        out_shape=[sp["s_xd"], jax.ShapeDtypeStruct((t, 3 * di), BF16),
                   jax.ShapeDtypeStruct((8, d), F32), jax.ShapeDtypeStruct((8, d), F32),
                   jax.ShapeDtypeStruct((8, di), F32), jax.ShapeDtypeStruct((8, di), F32),
                   jax.ShapeDtypeStruct((A_GROUPS, TM, 1), F32), jax.ShapeDtypeStruct((A_GROUPS, TM, TM), F32)],
        scratch=[pltpu.VMEM((TM, di), F32), pltpu.VMEM((TM, di), F32)],
        exchanges=exchanges)


def _inv_count(tile, window, rows=TM):
    pos = tile * rows + lax.broadcasted_iota(jnp.int32, (rows, 1), 0)
    return 1.0 / jnp.minimum(pos + 1, window).astype(F32)


def _window_sum(ext, window, down):
    rows = ext.shape[0]
    k = 1
    while k < window:
        ext = ext + pltpu.roll(ext, k if down else rows - k, 0)
        k *= 2
    return ext


def _fwd_b(x, win, wgrp, scale, wout, lng, lnb, exchanges=()):
    t, d = x.shape
    di = wout.shape[0]
    gd = di // len(POOL_WINDOWS)

    def body(x_ref, win_ref, wgrp_ref, scale_ref, wout_ref, lng_ref, lnb_ref,
             b1_ref, b2_ref, b3_ref, poolt_ref, yt_ref, xt_ref, pre_ref, xn_ref, p_scr, ext_scr, y_scr):
        i = pl.program_id(0)

        @pl.when(i == 0)
        def _():
            ext_scr[0:HALO, :] = jnp.zeros((HALO, di), F32)

        xv = x_ref[...]
        xt_ref[...] = xv.T.astype(BF16)
        _in_proj(xv.astype(BF16), win_ref, p_scr)
        ext_scr[HALO:, :] = p_scr[:, :di]
        for grp, window in enumerate(POOL_WINDOWS):
            sl = slice(grp * gd, (grp + 1) * gd)
            ext = ext_scr[:, sl]
            pooled = (_window_sum(ext, window, True)[HALO:] * _inv_count(i, window) - ext[HALO:]).astype(BF16)
            poolt_ref[sl, :] = pooled.astype(F32).T.astype(BF16)
            mixed = jnp.dot(pooled, wgrp_ref[grp], preferred_element_type=F32)
            s, ds = _silu_and_grad(p_scr[:, di + grp * gd:di + (grp + 1) * gd])
            sc = scale_ref[:, sl]
            ms = mixed * s
            b1_ref[:, sl] = (mixed * sc * ds).astype(BF16)
            b2_ref[:, sl] = ms.astype(BF16)
            b3_ref[:, sl] = (sc * s).astype(BF16)
            y = ms * sc
            y_scr[:, sl] = y.astype(BF16)
            yt_ref[sl, :] = y.T.astype(BF16)
        ext_scr[0:HALO, :] = ext_scr[TM:TM + HALO, :]
        out = jnp.dot(y_scr[...], wout_ref[...], preferred_element_type=F32)
        _post_norm(xv, out, lng_ref, lnb_ref, pre_ref, xn_ref)

    sp = _tile_specs(t, d, di)
    return _call(
        body, "fwd_b", (t // TM,), (x, win, wgrp, scale, wout, lng, lnb),
        in_specs=[sp["xd"]] + [_vmem()] * 6,
        out_specs=[sp["xi"]] * 3 + [sp["ti"], sp["ti"], sp["td"], sp["xd"], sp["xd"]],
        out_shape=[sp["s_xi"]] * 3 + [sp["s_ti"], sp["s_ti"], sp["s_td"], sp["s_xd"], sp["s_xd"]],
        scratch=[pltpu.VMEM((TM, 2 * di), F32), pltpu.VMEM((TM + HALO, di), F32), pltpu.VMEM((TM, di), BF16)],
        exchanges=exchanges)


def _bwd_b(g, pre, b1, b2, b3, wout, wgrp, lng, exchanges=()):
    t, d = g.shape
    di = wout.shape[0]
    gd = di // len(POOL_WINDOWS)
    nt = t // TM

    def body(g_ref, pre_ref, b1_ref, b2_ref, b3_ref, wout_ref, wgrp_ref, lng_ref,
             dpre_ref, dp_ref, dmix_ref, dlng_ref, dlnb_ref, dscale_ref, dy_scr, ext_scr):
        i = pl.program_id(0)
        tile = nt - 1 - i
        _zero_at_first_step(dlng_ref, dlnb_ref, dscale_ref)

        @pl.when(i == 0)
        def _():
            ext_scr[TM:, :] = jnp.zeros((HALO, di), F32)

        dpre = _post_norm_bwd(g_ref, pre_ref, lng_ref, dpre_ref, dlng_ref, dlnb_ref)
        dy_scr[...] = lax.dot_general(dpre.astype(BF16), wout_ref[...], _NT, preferred_element_type=F32)
        for grp, window in enumerate(POOL_WINDOWS):
            sl = slice(grp * gd, (grp + 1) * gd)
            dy = dy_scr[:, sl]
            dp_ref[:, di + grp * gd:di + (grp + 1) * gd] = (dy * _f32(b1_ref, sl)).astype(BF16)
            dscale_ref[:, sl] += _fold8(dy * _f32(b2_ref, sl))
            dmixed = (dy * _f32(b3_ref, sl)).astype(BF16)
            dmix_ref[:, sl] = dmixed
            dpooled = lax.dot_general(dmixed, wgrp_ref[grp], _NT, preferred_element_type=F32)
            ext_scr[0:TM, sl] = dpooled * _inv_count(tile, window)
            dv = _window_sum(ext_scr[:, sl], window, False)[0:TM] - dpooled
            dp_ref[:, sl] = dv.astype(BF16)
        ext_scr[TM:, :] = ext_scr[0:HALO, :]

    rrow = lambda i: (nt - 1 - i, 0)
    const2 = lambda i: (0, 0)
    xd, xi = pl.BlockSpec((TM, d), rrow), pl.BlockSpec((TM, di), rrow)
    return _call(
        body, "bwd_b", (nt,), (g, pre, b1, b2, b3, wout, wgrp, lng),
        in_specs=[xd, xd, xi, xi, xi, _vmem(), _vmem(), _vmem()],
        out_specs=[xd, pl.BlockSpec((TM, 2 * di), rrow), xi,
                   pl.BlockSpec((8, d), const2), pl.BlockSpec((8, d), const2), pl.BlockSpec((8, di), const2)],
        out_shape=[jax.ShapeDtypeStruct((t, d), F32), jax.ShapeDtypeStruct((t, 2 * di), BF16),
                   jax.ShapeDtypeStruct((t, di), BF16),
                   jax.ShapeDtypeStruct((8, d), F32), jax.ShapeDtypeStruct((8, d), F32),
                   jax.ShapeDtypeStruct((8, di), F32)],
        scratch=[pltpu.VMEM((TM, di), F32), pltpu.VMEM((TM + HALO, di), F32)],
        exchanges=exchanges)


def _fwd_c(x, win, convw, wout, lng, lnb, exchanges=()):
    t, d = x.shape
    di = wout.shape[0]

    def body(x_ref, win_ref, cw_ref, wout_ref, lng_ref, lnb_ref,
             c1_ref, c2_ref, c3_ref, cg_ref, hg_ref, yt_ref, xt_ref, pre_ref, xn_ref, p_scr, ext_scr, y_scr):
        i = pl.program_id(0)

        @pl.when(i == 0)
        def _():
            ext_scr[0:CHALO, :] = jnp.zeros((CHALO, di), F32)

        xv = x_ref[...]
        xt_ref[...] = xv.T.astype(BF16)
        _in_proj(xv.astype(BF16), win_ref, p_scr)
        for c in range(di // CW):
            sl = slice(c * CW, (c + 1) * CW)
            bb = p_scr[:, sl]
            cc = p_scr[:, di + c * CW:di + (c + 1) * CW]
            hh = p_scr[:, 2 * di + c * CW:2 * di + (c + 1) * CW]
            s, ds = _silu_and_grad(p_scr[:, 3 * di + c * CW:3 * di + (c + 1) * CW])
            ext_scr[CHALO:, sl] = cc * hh
            ext = ext_scr[:, sl]
            conv = (pltpu.roll(ext, 2, 0)[CHALO:] * cw_ref[0:1, sl] + pltpu.roll(ext, 1, 0)[CHALO:] * cw_ref[1:2, sl]
                    + ext[CHALO:] * cw_ref[2:3, sl])
            cs = conv * s
            c1_ref[:, sl] = cs.astype(BF16)
            c2_ref[:, sl] = (bb * conv * ds).astype(BF16)
            c3_ref[:, sl] = (bb * s).astype(BF16)
            cg_ref[:, sl] = cc.astype(BF16)
            hg_ref[:, sl] = hh.astype(BF16)
            y = bb * cs
            y_scr[:, sl] = y.astype(BF16)
            yt_ref[sl, :] = y.T.astype(BF16)
        ext_scr[0:CHALO, :] = ext_scr[TM:TM + CHALO, :]
        out = jnp.dot(y_scr[...], wout_ref[...], preferred_element_type=F32)
        _post_norm(xv, out, lng_ref, lnb_ref, pre_ref, xn_ref)

    sp = _tile_specs(t, d, di)
    return _call(
        body, "fwd_c", (t // TM,), (x, win, convw, wout, lng, lnb),
        in_specs=[sp["xd"]] + [_vmem()] * 5,
        out_specs=[sp["xi"]] * 5 + [sp["ti"], sp["td"], sp["xd"], sp["xd"]],
        out_shape=[sp["s_xi"]] * 5 + [sp["s_ti"], sp["s_td"], sp["s_xd"], sp["s_xd"]],
        scratch=[pltpu.VMEM((TM, 4 * di), F32), pltpu.VMEM((TM + CHALO, di), F32), pltpu.VMEM((TM, di), BF16)],
        exchanges=exchanges)


def _bwd_c(g, pre, c1, c2, c3, cg, hg, wout, convw, lng, exchanges=()):
    t, d = g.shape
    di = wout.shape[0]
    nt = t // TM

    def body(g_ref, pre_ref, c1_ref, c2_ref, c3_ref, cg_ref, hg_ref, wout_ref, cw_ref, lng_ref,
             dpre_ref, dp_ref, dlng_ref, dlnb_ref, dcw_ref, dy_scr, ext_scr):
        i = pl.program_id(0)
        _zero_at_first_step(dlng_ref, dlnb_ref, dcw_ref)

        @pl.when(i == 0)
        def _():
            ext_scr[TM:, :] = jnp.zeros((CHALO, di), F32)

        dpre = _post_norm_bwd(g_ref, pre_ref, lng_ref, dpre_ref, dlng_ref, dlnb_ref)
        dy_scr[...] = lax.dot_general(dpre.astype(BF16), wout_ref[...], _NT, preferred_element_type=F32)
        rows = TM + CHALO
        for c in range(di // CW):
            sl = slice(c * CW, (c + 1) * CW)
            dy = dy_scr[:, sl]
            cc = _f32(cg_ref, sl)
            hh = _f32(hg_ref, sl)
            dp_ref[:, sl] = (dy * _f32(c1_ref, sl)).astype(BF16)
            dp_ref[:, 3 * di + c * CW:3 * di + (c + 1) * CW] = (dy * _f32(c2_ref, sl)).astype(BF16)
            dconv = dy * _f32(c3_ref, sl)
            ext_scr[0:TM, sl] = dconv
            ext = ext_scr[:, sl]
            d1 = pltpu.roll(ext, rows - 1, 0)[0:TM]
            d2 = pltpu.roll(ext, rows - 2, 0)[0:TM]
            dq = dconv * cw_ref[2:3, sl] + d1 * cw_ref[1:2, sl] + d2 * cw_ref[0:1, sl]
            q = cc * hh
            dcw_ref[0, :, sl] += _fold8(q * d2)
            dcw_ref[1, :, sl] += _fold8(q * d1)
            dcw_ref[2, :, sl] += _fold8(q * dconv)
            dp_ref[:, di + c * CW:di + (c + 1) * CW] = (dq * hh).astype(BF16)
            dp_ref[:, 2 * di + c * CW:2 * di + (c + 1) * CW] = (dq * cc).astype(BF16)
        ext_scr[TM:, :] = ext_scr[0:CHALO, :]

    rrow = lambda i: (nt - 1 - i, 0)
    const2 = lambda i: (0, 0)
    xd, xi = pl.BlockSpec((TM, d), rrow), pl.BlockSpec((TM, di), rrow)
    return _call(
        body, "bwd_c", (nt,), (g, pre, c1, c2, c3, cg, hg, wout, convw, lng),
        in_specs=[xd, xd] + [xi] * 5 + [_vmem()] * 3,
        out_specs=[xd, pl.BlockSpec((TM, 4 * di), rrow),
                   pl.BlockSpec((8, d), const2), pl.BlockSpec((8, d), const2),
                   pl.BlockSpec((3, 8, di), lambda i: (0, 0, 0))],
        out_shape=[jax.ShapeDtypeStruct((t, d), F32), jax.ShapeDtypeStruct((t, 4 * di), BF16),
                   jax.ShapeDtypeStruct((8, d), F32), jax.ShapeDtypeStruct((8, d), F32),
                   jax.ShapeDtypeStruct((3, 8, di), F32)],
        scratch=[pltpu.VMEM((TM, di), F32), pltpu.VMEM((TM + CHALO, di), F32)],
        exchanges=exchanges)


def _dx(dp, dpre, win, exchanges=()):
    t, d = dpre.shape
    n = dp.shape[1]
    cs = win.shape[2]

    def body(dp_ref, dpre_ref, win_ref, dx_ref):
        acc = ALPHA * dpre_ref[...]
        for j in range(N_DEV):
            acc += lax.dot_general(dp_ref[:, j * cs:(j + 1) * cs], win_ref[j], _NT, preferred_element_type=F32)
        dx_ref[...] = acc

    row = lambda i: (i, 0)
    (dx,), ex = _call(
        body, "dx", (t // TM,), (dp, dpre, win),
        in_specs=[pl.BlockSpec((TM, n), row), pl.BlockSpec((TM, d), row), _vmem()],
        out_specs=[pl.BlockSpec((TM, d), row)],
        out_shape=[jax.ShapeDtypeStruct((t, d), F32)],
        exchanges=exchanges)
    return dx, ex


def _wgrad(at, b, nb, per_block_rows, name, exchanges=()):
    m_all, t = at.shape
    tn = b.shape[1] // nb
    m = m_all // nb if per_block_rows else m_all
    tk = min(WGRAD_TK, t)
    nk = t // tk

    def body(at_ref, b_ref, out_ref, acc):
        k = pl.program_id(1)

        @pl.when(k == 0)
        def _():
            acc[...] = jnp.zeros(acc.shape, F32)

        acc[...] += jnp.dot(at_ref[...], b_ref[...].astype(BF16), preferred_element_type=F32)

        @pl.when(k == nk - 1)
        def _():
            out_ref[0] = acc[...].astype(BF16)

    at_map = (lambda j, k: (j, k)) if per_block_rows else (lambda j, k: (0, k))
    (out,), ex = _call(
        body, name, (nb, nk), (at, b),
        in_specs=[pl.BlockSpec((m, tk), at_map), pl.BlockSpec((tk, tn), lambda j, k: (k, j))],
        out_specs=[pl.BlockSpec((1, m, tn), lambda j, k: (j, 0, 0))],
        out_shape=[jax.ShapeDtypeStruct((nb, m, tn), BF16)],
        scratch=[pltpu.VMEM((m, tn), F32)],
        exchanges=exchanges)
    return out, ex


ADAMW_BLOCK_BYTES = 14 * 1024 * 1024


def _sum_parts(parts_ref):
    g = parts_ref[0].astype(F32)
    for s in range(1, parts_ref.shape[0]):
        g = g + parts_ref[s].astype(F32)
    return g


def _row_tile(rows, bytes_per_row):
    if rows * bytes_per_row <= ADAMW_BLOCK_BYTES:
        return rows
    best = 8
    for cand in range(8, rows, 8):
        if rows % cand == 0 and cand * bytes_per_row <= ADAMW_BLOCK_BYTES:
            best = cand
    return best


def _pair_sum(full, theirs, core, name):
    shape = theirs.shape
    r, c = math.prod(shape[1:-1]), shape[-1]
    tr = _row_tile(r, c * 3 * full.dtype.itemsize)
    half = N_DEV // 2

    def body(core_ref, a_ref, b_ref, out_ref):
        out_ref[...] = (a_ref[...].astype(F32) + b_ref[...].astype(F32)).astype(out_ref.dtype)

    blk = pl.BlockSpec((None, tr, c), lambda q, i, core_ref: (q, i, 0))
    grid_spec = pltpu.PrefetchScalarGridSpec(
        num_scalar_prefetch=1, grid=(half, r // tr),
        in_specs=[pl.BlockSpec((None, None, tr, c), lambda q, i, core_ref: (q, core_ref[0], i, 0)), blk],
        out_specs=blk)
    return pl.pallas_call(
        body, name=name, grid_spec=grid_spec, out_shape=jax.ShapeDtypeStruct((half, r, c), full.dtype),
        compiler_params=_params(("arbitrary", "arbitrary")),
    )(core, full.reshape(half, 2, r, c), theirs.reshape(half, r, c)).reshape(shape)


def _adamw(parts, w, m, v, name):
    s, r, c = parts.shape
    tr = _row_tile(r, c * (s * parts.dtype.itemsize + 7 * 4))

    def body(parts_ref, w_ref, m_ref, v_ref, g_ref, d_ref, nm_ref, nv_ref):
        g = _sum_parts(parts_ref)
        g_ref[...] = g
        d_ref[...], nm_ref[...], nv_ref[...] = _adamw_update(g, w_ref[...], m_ref[...], v_ref[...])

    blk = pl.BlockSpec((tr, c), lambda i: (i, 0))
    return pl.pallas_call(
        body, name=name, grid=(r // tr,),
        in_specs=[pl.BlockSpec((s, tr, c), lambda i: (0, i, 0)), blk, blk, blk],
        out_specs=[blk, blk, blk, blk],
        out_shape=[jax.ShapeDtypeStruct((r, c), F32)] * 4,
        compiler_params=_params(("arbitrary",)),
    )(parts, w, m, v)


def _adamw_update(g, w, m, v):
    bc1 = 1.0 - ADAM_B1 ** ADAM_STEP
    bc2 = 1.0 - ADAM_B2 ** ADAM_STEP
    nm = ADAM_B1 * m + (1.0 - ADAM_B1) * g
    nv = ADAM_B2 * v + (1.0 - ADAM_B2) * (g * g)
    return -ADAM_LR * ((nm / bc1) / (jnp.sqrt(nv / bc2) + ADAM_EPS) + ADAM_WD * w), nm, nv


def _adamw_bucket(parts, members, name):
    n = len(members)
    shapes = [w.shape for w, _, _ in members]
    rows = [math.prod(shp) // _LANES for shp in shapes]
    starts = [sum(rows[:k]) for k in range(n)]

    def body(parts_ref, *refs):
        for k in range(n):
            w_ref, m_ref, v_ref = refs[3 * k:3 * k + 3]
            g_ref, d_ref, nm_ref, nv_ref = refs[3 * n + 4 * k:3 * n + 4 * k + 4]
            g = _sum_parts(parts_ref.at[:, starts[k]:starts[k] + rows[k], :])
            g_ref[...] = g
            d_ref[...], nm_ref[...], nv_ref[...] = _adamw_update(g, w_ref[...], m_ref[...], v_ref[...])

    flat = [a.reshape(-1, _LANES) for mem in members for a in mem]
    outs = pl.pallas_call(
        body, name=name, in_specs=[_vmem()] * (1 + 3 * n), out_specs=[_vmem()] * (4 * n),
        out_shape=[jax.ShapeDtypeStruct((rows[k], _LANES), F32) for k in range(n) for _ in range(4)],
        compiler_params=_params(),
    )(parts, *flat)
    return [[o.reshape(shapes[k]) for o in outs[4 * k:4 * k + 4]] for k in range(n)]


_LANES = 128


def _pack(arrays):
    flat = jnp.concatenate([a.reshape(-1) for a in arrays])
    pad = (-flat.shape[0]) % (8 * _LANES)
    return jnp.pad(flat, (0, pad)).reshape(-1, _LANES)


def _unpack(packed, shapes):
    flat = packed.reshape(-1)
    out, off = [], 0
    for shp in shapes:
        size = math.prod(shp)
        out.append(flat[off:off + size].reshape(shp))
        off += size
    return out


def _spatial_weights(w_s, b_s, rows):
    reps = rows // CHUNK
    tril = jnp.tril(jnp.ones((CHUNK, CHUNK), F32))
    wc = w_s * tril
    eye = jnp.eye(reps, dtype=F32)
    wc2 = jnp.einsum("ab,gts->gatbs", eye, wc).reshape(A_GROUPS, rows, rows)
    bs2 = jnp.tile(b_s, (1, reps)).reshape(A_GROUPS, rows, 1)
    return wc2.astype(BF16), jnp.swapaxes(wc2, 1, 2).astype(BF16), bs2


def _spatial_weight_grad(dwc2, dbs2):
    reps = TM // CHUNK
    tril = jnp.tril(jnp.ones((CHUNK, CHUNK), F32))
    blocks = dwc2.reshape(A_GROUPS, reps, CHUNK, reps, CHUNK)
    dws = sum(blocks[:, a, :, a, :] for a in range(reps)) * tril
    dbs = dbs2.reshape(A_GROUPS, reps, CHUNK).sum(axis=1)
    return dws, dbs


def _row2(a):
    return a.reshape(1, -1)


def kernel(x, a0_w_in, a0_v_gain, a0_v_bias, a0_w_s, a0_b_s, a0_w_out, ln0_gain, ln0_bias, b1_w_in, b1_w_grp, b1_scale, b1_w_out, ln1_gain, ln1_bias, c2_w_in, c2_conv_w, c2_w_out, ln2_gain, ln2_bias, a3_w_in, a3_v_gain, a3_v_bias, a3_w_s, a3_b_s, a3_w_out, ln3_gain, ln3_bias, loss_target, m_a0_w_in, m_a0_v_gain, m_a0_v_bias, m_a0_w_s, m_a0_b_s, m_a0_w_out, m_ln0_gain, m_ln0_bias, m_b1_w_in, m_b1_w_grp, m_b1_scale, m_b1_w_out, m_ln1_gain, m_ln1_bias, m_c2_w_in, m_c2_conv_w, m_c2_w_out, m_ln2_gain, m_ln2_bias, m_a3_w_in, m_a3_v_gain, m_a3_v_bias, m_a3_w_s, m_a3_b_s, m_a3_w_out, m_ln3_gain, m_ln3_bias, v_a0_w_in, v_a0_v_gain, v_a0_v_bias, v_a0_w_s, v_a0_b_s, v_a0_w_out, v_ln0_gain, v_ln0_bias, v_b1_w_in, v_b1_w_grp, v_b1_scale, v_b1_w_out, v_ln1_gain, v_ln1_bias, v_c2_w_in, v_c2_conv_w, v_c2_w_out, v_ln2_gain, v_ln2_bias, v_a3_w_in, v_a3_v_gain, v_a3_v_bias, v_a3_w_s, v_a3_b_s, v_a3_w_out, v_ln3_gain, v_ln3_bias):
    names = ["a0_w_in", "a0_v_gain", "a0_v_bias", "a0_w_s", "a0_b_s", "a0_w_out", "ln0_gain", "ln0_bias",
             "b1_w_in", "b1_w_grp", "b1_scale", "b1_w_out", "ln1_gain", "ln1_bias",
             "c2_w_in", "c2_conv_w", "c2_w_out", "ln2_gain", "ln2_bias",
             "a3_w_in", "a3_v_gain", "a3_v_bias", "a3_w_s", "a3_b_s", "a3_w_out", "ln3_gain", "ln3_bias"]
    env = dict(locals())
    w = {nm: env[nm] for nm in names}
    mom = {nm: env["m_" + nm] for nm in names}
    var = {nm: env["v_" + nm] for nm in names}

    x0 = x[0]
    target = loss_target[0]
    d_model = x0.shape[1]
    di = N_DEV * a0_w_out.shape[0]
    n_grp = len(POOL_WINDOWS)
    gd_b = di // n_grp

    layers = ("a0", "b1", "c2", "a3")
    big_of = {"a0": ["a0_w_in", "a0_w_out"], "b1": ["b1_w_in", "b1_w_grp", "b1_w_out"],
              "c2": ["c2_w_in", "c2_w_out"], "a3": ["a3_w_in", "a3_w_out"]}
    big = [nm for p in layers for nm in big_of[p]]
    bucket_of = {"a0": ["a0_v_gain", "a0_v_bias", "a0_w_s", "a0_b_s", "ln0_gain", "ln0_bias"],
                 "b1": ["b1_scale", "ln1_gain", "ln1_bias"], "c2": ["ln2_gain", "ln2_bias"],
                 "a3": ["a3_v_gain", "a3_v_bias", "a3_w_s", "a3_b_s", "ln3_gain", "ln3_bias"]}
    conv_shape = c2_conv_w.shape
    spatial = {p: _spatial_weights(w[p + "_w_s"], w[p + "_b_s"], TM) for p in ("a0", "a3")}

    def weight_gather(p):
        return _Gather([w[nm].astype(BF16) for nm in big_of[p]])

    (first,) = _exchange_only([_Gather([w[nm].astype(BF16) for nm in big_of["a0"]] + [_pack([c2_conv_w])])],
                              "gather_first")
    gathered = dict(zip(big_of["a0"], first))
    conv_all = jnp.stack([_unpack(first[-1][j], [conv_shape])[0] for j in range(N_DEV)], axis=1)
    conv_full = conv_all.reshape(conv_shape[0], di)
    w_in = lambda p: gathered[p + "_w_in"]
    w_out = lambda p: gathered[p + "_w_out"].reshape(di, d_model)
    saved = {}
    h = x0
    for i, p in enumerate(layers):
        lng, lnb = _row2(w[f"ln{i}_gain"]), _row2(w[f"ln{i}_bias"])
        nxt = layers[i + 1] if i + 1 < len(layers) else None
        ex = [weight_gather(nxt)] if nxt else []
        if p[0] == "a":
            wc2, _, bs2 = spatial[p]
            outs, got = _fwd_a(h, w_in(p), w_out(p), _row2(w[p + "_v_gain"]), _row2(w[p + "_v_bias"]),
                               wc2, bs2, lng, lnb, target=None if nxt else target, exchanges=ex)
        elif p[0] == "b":
            wgrp = jnp.swapaxes(gathered["b1_w_grp"], 0, 1).reshape(n_grp, gd_b, gd_b)
            outs, got = _fwd_b(h, w_in(p), wgrp, _row2(w[p + "_scale"]), w_out(p), lng, lnb, exchanges=ex)
        else:
            outs, got = _fwd_c(h, w_in(p), conv_full, w_out(p), lng, lnb, exchanges=ex)
        if nxt:
            saved[p], h = outs[:-1], outs[-1]
            gathered.update(zip(big_of[nxt], got[0]))
        else:
            saved[p], gcur, sq = outs[:-2], outs[-2], outs[-1]

    loss_share = (jnp.sum(sq) * (0.5 / d_model)).reshape(1)

    part, full, landed, small_all = {}, {}, {}, {}

    def bucket_gather(bucket):
        extra = [loss_share] if bucket == layers[-1] else []
        return _Gather([_pack([part[nm] for nm in bucket_of[bucket]] + extra)])

    core = lax.axis_index("c").astype(jnp.int32).reshape(1)

    def chip_sums(names, got):
        return [_pair_sum(full[nm], theirs, core, "pair_sum_" + nm) for nm, theirs in zip(names, got)]

    pending = None
    for i, p in reversed(list(enumerate(layers))):
        lng = _row2(w[f"ln{i}_gain"])
        ex = []
        if pending:
            ex = [_PairExchange([full[nm] for nm in big_of[pending]]), bucket_gather(pending)]
            if pending == "c2":
                ex.append(_Exchange(scatters=[full["c2_conv_w"]]))
        *factors, yt, xt, pre = saved[p]
        if p[0] == "a":
            wc2, wc2t, bs2 = spatial[p]
            (dpre, dp, dlng, dlnb, dgain, dbias, dbs2, dwc2), got = _bwd_a(
                gcur, pre, *factors, w_out(p), _row2(w[p + "_v_gain"]), wc2, wc2t, bs2, lng, exchanges=ex)
            part[p + "_v_gain"], part[p + "_v_bias"] = dgain.sum(axis=0), dbias.sum(axis=0)
            part[p + "_w_s"], part[p + "_b_s"] = _spatial_weight_grad(dwc2, dbs2)
        elif p[0] == "b":
            b1f, b2f, b3f, poolt = factors
            (dpre, dp, dmixed, dlng, dlnb, dscale), got = _bwd_b(
                gcur, pre, b1f, b2f, b3f, w_out(p), wgrp, lng, exchanges=ex)
            part[p + "_scale"] = dscale.sum(axis=0)
            dwg, _ = _wgrad(poolt, dmixed, n_grp, True, "wgrad_grp")
            full[p + "_w_grp"] = jnp.swapaxes(dwg.reshape(n_grp, N_DEV, gd_b // N_DEV, gd_b), 0, 1)
        else:
            (dpre, dp, dlng, dlnb, dcw), got = _bwd_c(gcur, pre, *factors, w_out(p), conv_full, lng, exchanges=ex)
            dconv = dcw.sum(axis=1).reshape(conv_shape[0], N_DEV, conv_shape[1])
            full["c2_conv_w"] = jnp.stack([_pack([dconv[:, j]]) for j in range(N_DEV)])
        part[f"ln{i}_gain"], part[f"ln{i}_bias"] = dlng.sum(axis=0), dlnb.sum(axis=0)
        if pending:
            small_all[pending] = got[1][0]
            if pending == "c2":
                small_all["conv"] = got[2][0]
        dwo, _ = _wgrad(yt, dpre, 1, False, "wgrad_out_" + p)
        full[p + "_w_out"] = dwo.reshape(N_DEV, di // N_DEV, d_model)
        ex = [_ChipScatter(chip_sums(big_of[pending], got[0]))] if pending else []
        if i == 0:
            ex += [bucket_gather(p)]
        full[p + "_w_in"], got = _wgrad(xt, dp, N_DEV, False, "wgrad_in_" + p, exchanges=ex)
        if pending:
            landed.update(zip(big_of[pending], got[0]))
        if i > 0:
            gcur, _ = _dx(dp, dpre, w_in(p))
            pending = p
        else:
            small_all[p] = got[-1][0]
            (theirs,) = _exchange_only([_PairExchange([full[p + "_w_in"]])], "pair_exchange_last")
            gcur, got = _dx(dp, dpre, w_in(p), exchanges=[
                _ChipScatter(chip_sums([p + "_w_in"], theirs)), _Exchange(scatters=[full[p + "_w_out"]])])
            landed[p + "_w_in"], landed[p + "_w_out"] = got[0][0], got[1][0]
    grad_x = gcur[None]

    grads, deltas, new_m, new_v = {}, {}, {}, {}
    for nm in big:
        shp = w[nm].shape
        r2 = (math.prod(shp[:-1]), shp[-1])
        outs = _adamw(landed[nm].reshape((-1,) + r2), w[nm].reshape(r2), mom[nm].reshape(r2), var[nm].reshape(r2),
                      "adamw_" + nm)
        grads[nm], deltas[nm], new_m[nm], new_v[nm] = (o.reshape(shp) for o in outs)
    buckets = dict(bucket_of, conv=["c2_conv_w"])
    for key, members in buckets.items():
        outs = _adamw_bucket(small_all[key], [(w[nm], mom[nm], var[nm]) for nm in members], "adamw_small_" + key)
        for nm, (g_out, d_out, m_out, v_out) in zip(members, outs):
            grads[nm], deltas[nm], new_m[nm], new_v[nm] = g_out, d_out, m_out, v_out

    loss_row = sum(math.prod(w[nm].shape) for nm in bucket_of[layers[-1]]) // _LANES
    loss = jnp.sum(small_all[layers[-1]][:, loss_row, 0])

    return (loss, grad_x, *[grads[nm] for nm in names], *[deltas[nm] for nm in names],
            *[new_m[nm] for nm in names], *[new_v[nm] for nm in names])
```

```python
import functools
import math

import jax
import jax.numpy as jnp
from jax import lax
from jax.experimental import pallas as pl
from jax.experimental.pallas import tpu as pltpu

F32 = jnp.float32
BF16 = jnp.bfloat16

N_DEV = 8
DEPTH = 4
CHUNK = 128
A_GROUPS = 8
POOL_WINDOWS = (2, 4, 8, 16)
LN_EPS = 1e-5
ALPHA = (2.0 * DEPTH) ** 0.25
ADAM_LR = 0.001
ADAM_B1 = 0.9
ADAM_B2 = 0.999
ADAM_EPS = 1e-08
ADAM_WD = 0.01
ADAM_STEP = 10

TM = 256
HALO = 16
CHALO = 8
CW = 512
WGRAD_BLOCK_BYTES = 36 * 1024 * 1024
VMEM_LIMIT_BYTES = 58 * 1024 * 1024

_NT = (((1,), (1,)), ((), ()))
_SQRT_2_OVER_PI = math.sqrt(2.0 / math.pi)
_MESH = pl.DeviceIdType.MESH


def _vmem():
    return pl.BlockSpec(memory_space=pltpu.VMEM)


def _params(sem=None):
    return pltpu.CompilerParams(dimension_semantics=sem, vmem_limit_bytes=VMEM_LIMIT_BYTES)


def _gelu_and_grad(x):
    c1 = _SQRT_2_OVER_PI * 0.044715
    x2 = x * x
    t = jnp.tanh(x * (_SQRT_2_OVER_PI + c1 * x2))
    cdf = 0.5 + 0.5 * t
    grad = cdf + x * (1.0 - t * t) * (0.5 * _SQRT_2_OVER_PI + (1.5 * c1) * x2)
    return x * cdf, grad


def _silu_and_grad(z):
    sg = 1.0 / (1.0 + jnp.exp(-z))
    s = z * sg
    return s, sg + s - s * sg


def _fold8(a):
    return a.reshape(a.shape[0] // 8, 8, a.shape[1]).sum(axis=0)


def _row_mean(a):
    return jnp.mean(a, axis=-1, keepdims=True)


def _ln_stats(x):
    mu = _row_mean(x)
    xc = x - mu
    rstd = lax.rsqrt(_row_mean(xc * xc) + LN_EPS)
    return xc * rstd, rstd


def _post_norm(x, out, lng_ref, lnb_ref, pre_ref, xn_ref):
    pre = ALPHA * x + out
    pre_ref[...] = pre
    xhat, _ = _ln_stats(pre)
    xn_ref[...] = xhat * lng_ref[...] + lnb_ref[...]


def _post_norm_bwd(g_ref, pre_ref, lng_ref, dpre_ref, dlng_ref, dlnb_ref):
    go = g_ref[...]
    xhat, rstd = _ln_stats(pre_ref[...])
    dlng_ref[...] += _fold8(go * xhat)
    dlnb_ref[...] += _fold8(go)
    dxh = go * lng_ref[...]
    dpre = rstd * (dxh - _row_mean(dxh) - xhat * _row_mean(dxh * xhat))
    dpre_ref[...] = dpre
    return dpre


def _in_proj(xb, win_ref, p_ref, lo=0, hi=None):
    cs = win_ref.shape[2]
    hi = N_DEV * cs if hi is None else hi
    for j in range(N_DEV):
        a, b = max(lo, j * cs), min(hi, (j + 1) * cs)
        if a < b:
            p_ref[:, a:b] = jnp.dot(xb, win_ref[j, :, a - j * cs:b - j * cs], preferred_element_type=F32)


def _zero_at_first_step(*refs):
    @pl.when(pl.program_id(0) == 0)
    def _():
        for r in refs:
            r[...] = jnp.zeros(r.shape, r.dtype)


def _f32(ref, sl):
    return ref[:, sl].astype(F32)


def _my_position():
    x, y, c = lax.axis_index("x"), lax.axis_index("y"), lax.axis_index("c")
    return (x, y, c), 4 * x + 2 * y + c


def _peer(k):
    (x, y, c), _ = _my_position()
    peer = (x ^ (k >> 2), y ^ ((k >> 1) & 1), c ^ (k & 1))
    return peer, 4 * peer[0] + 2 * peer[1] + peer[2]


class _Exchange:
    def __init__(self, gathers=(), scatters=()):
        self.args = list(gathers) + list(scatters)
        self.n_gather = len(gathers)
        self.out_shape = ([jax.ShapeDtypeStruct((N_DEV,) + a.shape, a.dtype) for a in gathers]
                          + [jax.ShapeDtypeStruct(a.shape, a.dtype) for a in scatters])
        n = len(self.args)
        self.scratch = [pltpu.SemaphoreType.DMA((n, N_DEV)), pltpu.SemaphoreType.DMA((n, N_DEV)),
                        pltpu.SemaphoreType.DMA((n,))]

    def _src(self, ins, w, pos):
        return ins[w] if w < self.n_gather else ins[w].at[pos]

    def _copies(self, ins, outs, sems, arrivals):
        send_sems, recv_sems, local_sems = sems
        _, me = _my_position()
        n = len(self.args)
        copies = []
        if not arrivals:
            copies = [pltpu.make_async_copy(self._src(ins, w, me), outs[w].at[me], local_sems.at[w]) for w in range(n)]
        for k in range(1, N_DEV):
            peer, peer_pos = _peer(k)
            for w in range(n):
                copies.append(pltpu.make_async_remote_copy(
                    src_ref=self._src(ins, w, me if arrivals else peer_pos),
                    dst_ref=outs[w].at[peer_pos if arrivals else me],
                    send_sem=send_sems.at[w, k], recv_sem=recv_sems.at[w, k], device_id=peer, device_id_type=_MESH))
        return copies

    def start(self, ins, outs, sems):
        for cp in self._copies(ins, outs, sems, False):
            cp.start()

    def mid(self, ins, outs, sems):
        pass

    def wait(self, ins, outs, sems):
        n = len(self.args)
        for cp in self._copies(ins, outs, sems, True):
            cp.wait_recv()
        own = self._copies(ins, outs, sems, False)
        for cp in own[n:]:
            cp.wait_send()
        for cp in own[:n]:
            cp.wait()


def _remote(src, dst, send_sem, recv_sem, peer):
    return pltpu.make_async_remote_copy(src_ref=src, dst_ref=dst, send_sem=send_sem, recv_sem=recv_sem,
                                        device_id=peer, device_id_type=_MESH)


class _Gather:
    def __init__(self, shards):
        self.args = list(shards)
        n = len(self.args)
        self.out_shape = [jax.ShapeDtypeStruct((N_DEV,) + a.shape, a.dtype) for a in shards]
        self.scratch = [pltpu.SemaphoreType.DMA((n, N_DEV)), pltpu.SemaphoreType.DMA((n, N_DEV)),
                        pltpu.SemaphoreType.DMA((n,))]

    def _own(self, ins, outs, sems):
        send, recv, loc = sems
        _, me = _my_position()
        local = [pltpu.make_async_copy(ins[w], outs[w].at[me], loc.at[w]) for w in range(len(ins))]
        first = [_remote(ins[w], outs[w].at[me], send.at[w, k], recv.at[w, k], _peer(k)[0])
                 for k in (1, 2, 4, 6) for w in range(len(ins))]
        return local, first

    def _passed_on(self, ins, outs, sems):
        send, recv, _ = sems
        sibling, _ = _peer(1)
        return [_remote(outs[w].at[_peer(k)[1]], outs[w].at[_peer(k)[1]], send.at[w, k + 1], recv.at[w, k + 1], sibling)
                for k in (2, 4, 6) for w in range(len(ins))]

    def _arrival(self, ins, outs, sems, k, w):
        send, recv, _ = sems
        peer, pos = _peer(k)
        return _remote(ins[w], outs[w].at[pos], send.at[w, k], recv.at[w, k], peer)

    def start(self, ins, outs, sems):
        local, first = self._own(ins, outs, sems)
        for cp in local + first:
            cp.start()

    def mid(self, ins, outs, sems):
        for k in (2, 4, 6):
            for w in range(len(ins)):
                self._arrival(ins, outs, sems, k, w).wait_recv()
        for cp in self._passed_on(ins, outs, sems):
            cp.start()

    def wait(self, ins, outs, sems):
        for k in (1, 3, 5, 7):
            for w in range(len(ins)):
                self._arrival(ins, outs, sems, k, w).wait_recv()
        local, first = self._own(ins, outs, sems)
        for cp in first + self._passed_on(ins, outs, sems):
            cp.wait_send()
        for cp in local:
            cp.wait()


class _PairExchange:
    def __init__(self, fulls):
        self.args = list(fulls)
        n = len(self.args)
        self.out_shape = [jax.ShapeDtypeStruct((N_DEV // 2,) + a.shape[1:], a.dtype) for a in fulls]
        self.scratch = [pltpu.SemaphoreType.DMA((n, N_DEV // 2)), pltpu.SemaphoreType.DMA((n, N_DEV // 2))]

    def _copies(self, ins, outs, sems):
        send, recv = sems
        (x, y, c), _ = _my_position()
        sibling, _ = _peer(1)
        return [_remote(ins[w].at[2 * q + 1 - c], outs[w].at[q], send.at[w, q], recv.at[w, q], sibling)
                for q in range(N_DEV // 2) for w in range(len(ins))]

    def start(self, ins, outs, sems):
        for cp in self._copies(ins, outs, sems):
            cp.start()

    def mid(self, ins, outs, sems):
        pass

    def wait(self, ins, outs, sems):
        for cp in self._copies(ins, outs, sems):
            cp.wait()


class _ChipScatter:
    def __init__(self, sums):
        self.args = list(sums)
        n = len(self.args)
        self.out_shape = [jax.ShapeDtypeStruct(a.shape, a.dtype) for a in sums]
        self.scratch = [pltpu.SemaphoreType.DMA((n, N_DEV // 2)), pltpu.SemaphoreType.DMA((n, N_DEV // 2)),
                        pltpu.SemaphoreType.DMA((n,))]

    def _copies(self, ins, outs, sems, arrivals):
        send, recv, loc = sems
        (x, y, c), _ = _my_position()
        my_chip = 2 * x + y
        copies = []
        if not arrivals:
            copies = [pltpu.make_async_copy(ins[w].at[my_chip], outs[w].at[my_chip], loc.at[w]) for w in range(len(ins))]
        for k in (1, 2, 3):
            peer = (x ^ (k >> 1), y ^ (k & 1), c)
            chip = my_chip ^ k
            for w in range(len(ins)):
                copies.append(_remote(ins[w].at[my_chip if arrivals else chip], outs[w].at[chip if arrivals else my_chip],
                                      send.at[w, k], recv.at[w, k], peer))
        return copies

    def start(self, ins, outs, sems):
        for cp in self._copies(ins, outs, sems, False):
            cp.start()

    def mid(self, ins, outs, sems):
        pass

    def wait(self, ins, outs, sems):
        n = len(ins)
        for cp in self._copies(ins, outs, sems, True):
            cp.wait_recv()
        own = self._copies(ins, outs, sems, False)
        for cp in own[n:]:
            cp.wait_send()
        for cp in own[:n]:
            cp.wait()


def _split(refs, sizes):
    out, off = [], 0
    for size in sizes:
        out.append(refs[off:off + size])
        off += size
    return out


def _call(body, name, grid, args, in_specs, out_shape, out_specs, scratch=(), exchanges=()):
    sem = ("arbitrary",) * len(grid)
    exchanges = [e for e in exchanges if e is not None]
    if not exchanges:
        outs = pl.pallas_call(body, name=name, grid=grid, in_specs=in_specs, out_specs=out_specs, out_shape=out_shape,
                              scratch_shapes=list(scratch), compiler_params=_params(sem))(*args)
        return outs, []
    n_in, n_out, n_scr = len(args), len(out_shape), len(scratch)
    ex_in = [len(e.args) for e in exchanges]
    ex_out = [len(e.out_shape) for e in exchanges]
    ex_scr = [len(e.scratch) for e in exchanges]
    steps = math.prod(grid)
    mid_step = min((3 * steps) // 4, steps - 1)

    def hosted(*refs):
        main_in, xin, main_out, xout, main_scr, xscr = _split(
            refs, [n_in, sum(ex_in), n_out, sum(ex_out), n_scr, sum(ex_scr)])
        parts = list(zip(exchanges, _split(xin, ex_in), _split(xout, ex_out), _split(xscr, ex_scr)))
        step = pl.program_id(0)
        for a in range(1, len(grid)):
            step = step * grid[a] + pl.program_id(a)

        @pl.when(step == 0)
        def _():
            for e, ins, outs, sems in parts:
                e.start(ins, outs, sems)

        body(*main_in, *main_out, *main_scr)

        @pl.when(step == mid_step)
        def _():
            for e, ins, outs, sems in parts:
                e.mid(ins, outs, sems)

        @pl.when(step == steps - 1)
        def _():
            for e, ins, outs, sems in parts:
                e.wait(ins, outs, sems)

    any_spec = pl.BlockSpec(memory_space=pl.ANY)
    outs = pl.pallas_call(
        hosted, name=name, grid=grid, in_specs=list(in_specs) + [any_spec] * sum(ex_in),
        out_specs=list(out_specs) + [any_spec] * sum(ex_out),
        out_shape=list(out_shape) + [s for e in exchanges for s in e.out_shape],
        scratch_shapes=list(scratch) + [s for e in exchanges for s in e.scratch],
        compiler_params=_params(sem))(*args, *[a for e in exchanges for a in e.args])
    return outs[:n_out], _split(outs[n_out:], ex_out)


def _exchange_only(exchanges, name):
    ex_in = [len(e.args) for e in exchanges]
    ex_out = [len(e.out_shape) for e in exchanges]
    ex_scr = [len(e.scratch) for e in exchanges]

    def body(*refs):
        xin, xout, xscr = _split(refs, [sum(ex_in), sum(ex_out), sum(ex_scr)])
        parts = list(zip(exchanges, _split(xin, ex_in), _split(xout, ex_out), _split(xscr, ex_scr)))
        for phase in ("start", "mid", "wait"):
            for e, ins, outs, sems in parts:
                getattr(e, phase)(ins, outs, sems)

    any_spec = pl.BlockSpec(memory_space=pl.ANY)
    outs = pl.pallas_call(
        body, name=name, in_specs=[any_spec] * sum(ex_in), out_specs=[any_spec] * sum(ex_out),
        out_shape=[s for e in exchanges for s in e.out_shape],
        scratch_shapes=[s for e in exchanges for s in e.scratch])(*[a for e in exchanges for a in e.args])
    return _split(outs, ex_out)


def _tile_specs(t, d, di):
    row = lambda i: (i, 0)
    col = lambda i: (0, i)
    return dict(
        xd=pl.BlockSpec((TM, d), row), xi=pl.BlockSpec((TM, di), row),
        td=pl.BlockSpec((d, TM), col), ti=pl.BlockSpec((di, TM), col),
        s_xd=jax.ShapeDtypeStruct((t, d), F32), s_xi=jax.ShapeDtypeStruct((t, di), BF16),
        s_td=jax.ShapeDtypeStruct((d, t), BF16), s_ti=jax.ShapeDtypeStruct((di, t), BF16))


def _fwd_a(x, win, wout, gain, bias, wc2, bs2, lng, lnb, target=None, exchanges=()):
    t, d = x.shape
    di = wout.shape[0]
    gd = di // A_GROUPS
    n_loss = 0 if target is None else 1

    def body(*refs):
        x_ref, win_ref, wout_ref, gain_ref, bias_ref, wc_ref, bs_ref, lng_ref, lnb_ref = refs[:9]
        (a1_ref, a2_ref, a3_ref, vn_ref, vh_ref, rg_ref, yt_ref, xt_ref, pre_ref,
         xn_ref) = refs[9 + n_loss:19 + n_loss]
        p_scr, vg_scr, y_scr = refs[19 + 2 * n_loss:]
        xv = x_ref[...]
        xt_ref[...] = xv.T.astype(BF16)
        xb = xv.astype(BF16)
        _in_proj(xb, win_ref, p_scr, di, 2 * di)
        s1 = jnp.zeros((TM, 1), F32)
        for c in range(di // CW):
            sl = slice(c * CW, (c + 1) * CW)
            pv = slice(di + c * CW, di + (c + 1) * CW)
            vg, dvg = _gelu_and_grad(p_scr[:, pv])
            vg_scr[:, sl] = vg
            p_scr[:, pv] = dvg
            s1 += jnp.sum(vg, axis=1, keepdims=True)
        mu = s1 * (1.0 / di)
        s2 = jnp.zeros((TM, 1), F32)
        for c in range(di // CW):
            dlt = vg_scr[:, c * CW:(c + 1) * CW] - mu
            s2 += jnp.sum(dlt * dlt, axis=1, keepdims=True)
        rstd = lax.rsqrt(s2 * (1.0 / di) + LN_EPS)
        for c in range(di // CW):
            sl = slice(c * CW, (c + 1) * CW)
            vh = (vg_scr[:, sl] - mu) * rstd
            vh_ref[:, sl] = vh.astype(BF16)
            vn_ref[:, sl] = (vh * gain_ref[:, sl] + bias_ref[:, sl]).astype(BF16)
            rg_ref[:, sl] = (p_scr[:, di + c * CW:di + (c + 1) * CW] * rstd).astype(BF16)
        for g in range(A_GROUPS):
            sl = slice(g * gd, (g + 1) * gd)
            _in_proj(xb, win_ref, p_scr, g * gd, (g + 1) * gd)
            _in_proj(xb, win_ref, p_scr, 2 * di + g * gd, 2 * di + (g + 1) * gd)
            sv = jnp.dot(wc_ref[g], vn_ref[:, sl], preferred_element_type=F32) + bs_ref[g]
            u, du = _gelu_and_grad(p_scr[:, sl])
            s, ds = _silu_and_grad(p_scr[:, 2 * di + g * gd:2 * di + (g + 1) * gd])
            us = u * s
            a1_ref[:, sl] = (s * du).astype(BF16)
            a2_ref[:, sl] = (u * ds).astype(BF16)
            a3_ref[:, sl] = us.astype(BF16)
            y = us * sv
            y_scr[:, sl] = y.astype(BF16)
            yt_ref[sl, :] = y.T.astype(BF16)
        out = jnp.dot(y_scr[...], wout_ref[...], preferred_element_type=F32)
        _post_norm(xv, out, lng_ref, lnb_ref, pre_ref, xn_ref)
        if n_loss:
            t_ref, sq_ref = refs[9], refs[20]
            _zero_at_first_step(sq_ref)
            diff = xn_ref[...] - t_ref[...]
            xn_ref[...] = diff * (1.0 / d)
            sq_ref[...] += _fold8(diff * diff)

    sp = _tile_specs(t, d, di)
    loss_in = [] if target is None else [target]
    return _call(
        body, "fwd_a", (t // TM,), (x, win, wout, gain, bias, wc2, bs2, lng, lnb, *loss_in),
        in_specs=[sp["xd"]] + [_vmem()] * 8 + [sp["xd"]] * n_loss,
        out_specs=([sp["xi"]] * 6 + [sp["ti"], sp["td"], sp["xd"], sp["xd"]]
                   + [pl.BlockSpec((8, d), lambda i: (0, 0))] * n_loss),
        out_shape=([sp["s_xi"]] * 6 + [sp["s_ti"], sp["s_td"], sp["s_xd"], sp["s_xd"]]
                   + [jax.ShapeDtypeStruct((8, d), F32)] * n_loss),
        scratch=[pltpu.VMEM((TM, 3 * di), F32), pltpu.VMEM((TM, di), F32), pltpu.VMEM((TM, di), BF16)],
        exchanges=exchanges)


def _bwd_a(g, pre, a1, a2, a3, vn, vh, rg, wout, gain, wc2, wc2t, bs2, lng, exchanges=()):
    t, d = g.shape
    di = wout.shape[0]
    gd = di // A_GROUPS

    def body(g_ref, pre_ref, a1_ref, a2_ref, a3_ref, vn_ref, vh_ref, rg_ref,
             wout_ref, gain_ref, wc_ref, wct_ref, bs_ref, lng_ref,
             dpre_ref, dp_ref, dlng_ref, dlnb_ref, dgain_ref, dbias_ref, dbs_ref, dwc_ref,
             dy_scr, dv_scr):
        _zero_at_first_step(dlng_ref, dlnb_ref, dgain_ref, dbias_ref, dbs_ref, dwc_ref)
        dpre = _post_norm_bwd(g_ref, pre_ref, lng_ref, dpre_ref, dlng_ref, dlnb_ref)
        dy_scr[...] = lax.dot_general(dpre.astype(BF16), wout_ref[...], _NT, preferred_element_type=F32)
        for grp in range(A_GROUPS):
            sl = slice(grp * gd, (grp + 1) * gd)
            vn_g = vn_ref[:, sl]
            sv = jnp.dot(wc_ref[grp], vn_g, preferred_element_type=F32) + bs_ref[grp]
            dy = dy_scr[:, sl]
            dys = dy * sv
            dp_ref[:, sl] = (dys * _f32(a1_ref, sl)).astype(BF16)
            dp_ref[:, 2 * di + grp * gd:2 * di + (grp + 1) * gd] = (dys * _f32(a2_ref, sl)).astype(BF16)
            dsv = dy * _f32(a3_ref, sl)
            dbs_ref[grp] += jnp.sum(dsv, axis=1, keepdims=True)
            dsvb = dsv.astype(BF16)
            dwc_ref[grp] += lax.dot_general(dsvb, vn_g, _NT, preferred_element_type=F32)
            dv_scr[:, sl] = jnp.dot(wct_ref[grp], dsvb, preferred_element_type=F32)
        r1 = jnp.zeros((TM, 1), F32)
        r2 = jnp.zeros((TM, 1), F32)
        for c in range(di // CW):
            sl = slice(c * CW, (c + 1) * CW)
            dv = dv_scr[:, sl]
            vhat = _f32(vh_ref, sl)
            dgain_ref[:, sl] += _fold8(dv * vhat)
            dbias_ref[:, sl] += _fold8(dv)
            dvh = dv * gain_ref[:, sl]
            dv_scr[:, sl] = dvh
            r1 += jnp.sum(dvh, axis=1, keepdims=True)
            r2 += jnp.sum(dvh * vhat, axis=1, keepdims=True)
        m1 = r1 * (1.0 / di)
        m2 = r2 * (1.0 / di)
        for c in range(di // CW):
            sl = slice(c * CW, (c + 1) * CW)
            dp_ref[:, di + c * CW:di + (c + 1) * CW] = (
                (dv_scr[:, sl] - m1 - _f32(vh_ref, sl) * m2) * _f32(rg_ref, sl)).astype(BF16)

    sp = _tile_specs(t, d, di)
    const2 = lambda i: (0, 0)
    const3 = lambda i: (0, 0, 0)
    return _call(
        body, "bwd_a", (t // TM,), (g, pre, a1, a2, a3, vn, vh, rg, wout, gain, wc2, wc2t, bs2, lng),
        in_specs=[sp["xd"], sp["xd"]] + [sp["xi"]] * 6 + [_vmem()] * 6,
        out_specs=[sp["xd"], pl.BlockSpec((TM, 3 * di), lambda i: (i, 0)),
                   pl.BlockSpec((8, d), const2), pl.BlockSpec((8, d), const2),
                   pl.BlockSpec((8, di), const2), pl.BlockSpec((8, di), const2),
                   pl.BlockSpec((A_GROUPS, TM, 1), const3), pl.BlockSpec((A_GROUPS, TM, TM), const3)],
        out_shape=[sp["s_xd"], jax.ShapeDtypeStruct((t, 3 * di), BF16),
                   jax.ShapeDtypeStruct((8, d), F32), jax.ShapeDtypeStruct((8, d), F32),
                   jax.ShapeDtypeStruct((8, di), F32), jax.ShapeDtypeStruct((8, di), F32),
                   jax.ShapeDtypeStruct((A_GROUPS, TM, 1), F32), jax.ShapeDtypeStruct((A_GROUPS, TM, TM), F32)],
        scratch=[pltpu.VMEM((TM, di), F32), pltpu.VMEM((TM, di), F32)],
        exchanges=exchanges)


def _inv_count(tile, window, rows=TM):
    pos = tile * rows + lax.broadcasted_iota(jnp.int32, (rows, 1), 0)
    return 1.0 / jnp.minimum(pos + 1, window).astype(F32)


def _window_sum(ext, window, down):
    rows = ext.shape[0]
    k = 1
    while k < window:
        ext = ext + pltpu.roll(ext, k if down else rows - k, 0)
        k *= 2
    return ext


def _fwd_b(x, win, wgrp, scale, wout, lng, lnb, exchanges=()):
    t, d = x.shape
    di = wout.shape[0]
    gd = di // len(POOL_WINDOWS)

    def body(x_ref, win_ref, wgrp_ref, scale_ref, wout_ref, lng_ref, lnb_ref,
             b1_ref, b2_ref, b3_ref, poolt_ref, yt_ref, xt_ref, pre_ref, xn_ref, p_scr, ext_scr, y_scr):
        i = pl.program_id(0)

        @pl.when(i == 0)
        def _():
            ext_scr[0:HALO, :] = jnp.zeros((HALO, di), F32)

        xv = x_ref[...]
        xt_ref[...] = xv.T.astype(BF16)
        _in_proj(xv.astype(BF16), win_ref, p_scr)
        ext_scr[HALO:, :] = p_scr[:, :di]
        for grp, window in enumerate(POOL_WINDOWS):
            sl = slice(grp * gd, (grp + 1) * gd)
            ext = ext_scr[:, sl]
            pooled = (_window_sum(ext, window, True)[HALO:] * _inv_count(i, window) - ext[HALO:]).astype(BF16)
            poolt_ref[sl, :] = pooled.astype(F32).T.astype(BF16)
            mixed = jnp.dot(pooled, wgrp_ref[grp], preferred_element_type=F32)
            s, ds = _silu_and_grad(p_scr[:, di + grp * gd:di + (grp + 1) * gd])
            sc = scale_ref[:, sl]
            ms = mixed * s
            b1_ref[:, sl] = (mixed * sc * ds).astype(BF16)
            b2_ref[:, sl] = ms.astype(BF16)
            b3_ref[:, sl] = (sc * s).astype(BF16)
            y = ms * sc
            y_scr[:, sl] = y.astype(BF16)
            yt_ref[sl, :] = y.T.astype(BF16)
        ext_scr[0:HALO, :] = ext_scr[TM:TM + HALO, :]
        out = jnp.dot(y_scr[...], wout_ref[...], preferred_element_type=F32)
        _post_norm(xv, out, lng_ref, lnb_ref, pre_ref, xn_ref)

    sp = _tile_specs(t, d, di)
    return _call(
        body, "fwd_b", (t // TM,), (x, win, wgrp, scale, wout, lng, lnb),
        in_specs=[sp["xd"]] + [_vmem()] * 6,
        out_specs=[sp["xi"]] * 3 + [sp["ti"], sp["ti"], sp["td"], sp["xd"], sp["xd"]],
        out_shape=[sp["s_xi"]] * 3 + [sp["s_ti"], sp["s_ti"], sp["s_td"], sp["s_xd"], sp["s_xd"]],
        scratch=[pltpu.VMEM((TM, 2 * di), F32), pltpu.VMEM((TM + HALO, di), F32), pltpu.VMEM((TM, di), BF16)],
        exchanges=exchanges)


def _bwd_b(g, pre, b1, b2, b3, wout, wgrp, lng, exchanges=()):
    t, d = g.shape
    di = wout.shape[0]
    gd = di // len(POOL_WINDOWS)
    nt = t // TM

    def body(g_ref, pre_ref, b1_ref, b2_ref, b3_ref, wout_ref, wgrp_ref, lng_ref,
             dpre_ref, dp_ref, dmix_ref, dlng_ref, dlnb_ref, dscale_ref, dy_scr, ext_scr):
        i = pl.program_id(0)
        tile = nt - 1 - i
        _zero_at_first_step(dlng_ref, dlnb_ref, dscale_ref)

        @pl.when(i == 0)
        def _():
            ext_scr[TM:, :] = jnp.zeros((HALO, di), F32)

        dpre = _post_norm_bwd(g_ref, pre_ref, lng_ref, dpre_ref, dlng_ref, dlnb_ref)
        dy_scr[...] = lax.dot_general(dpre.astype(BF16), wout_ref[...], _NT, preferred_element_type=F32)
        for grp, window in enumerate(POOL_WINDOWS):
            sl = slice(grp * gd, (grp + 1) * gd)
            dy = dy_scr[:, sl]
            dp_ref[:, di + grp * gd:di + (grp + 1) * gd] = (dy * _f32(b1_ref, sl)).astype(BF16)
            dscale_ref[:, sl] += _fold8(dy * _f32(b2_ref, sl))
            dmixed = (dy * _f32(b3_ref, sl)).astype(BF16)
            dmix_ref[:, sl] = dmixed
            dpooled = lax.dot_general(dmixed, wgrp_ref[grp], _NT, preferred_element_type=F32)
            ext_scr[0:TM, sl] = dpooled * _inv_count(tile, window)
            dv = _window_sum(ext_scr[:, sl], window, False)[0:TM] - dpooled
            dp_ref[:, sl] = dv.astype(BF16)
        ext_scr[TM:, :] = ext_scr[0:HALO, :]

    rrow = lambda i: (nt - 1 - i, 0)
    const2 = lambda i: (0, 0)
    xd, xi = pl.BlockSpec((TM, d), rrow), pl.BlockSpec((TM, di), rrow)
    return _call(
        body, "bwd_b", (nt,), (g, pre, b1, b2, b3, wout, wgrp, lng),
        in_specs=[xd, xd, xi, xi, xi, _vmem(), _vmem(), _vmem()],
        out_specs=[xd, pl.BlockSpec((TM, 2 * di), rrow), xi,
                   pl.BlockSpec((8, d), const2), pl.BlockSpec((8, d), const2), pl.BlockSpec((8, di), const2)],
        out_shape=[jax.ShapeDtypeStruct((t, d), F32), jax.ShapeDtypeStruct((t, 2 * di), BF16),
                   jax.ShapeDtypeStruct((t, di), BF16),
                   jax.ShapeDtypeStruct((8, d), F32), jax.ShapeDtypeStruct((8, d), F32),
                   jax.ShapeDtypeStruct((8, di), F32)],
        scratch=[pltpu.VMEM((TM, di), F32), pltpu.VMEM((TM + HALO, di), F32)],
        exchanges=exchanges)


def _fwd_c(x, win, convw, wout, lng, lnb, exchanges=()):
    t, d = x.shape
    di = wout.shape[0]

    def body(x_ref, win_ref, cw_ref, wout_ref, lng_ref, lnb_ref,
             c1_ref, c2_ref, c3_ref, cg_ref, hg_ref, yt_ref, xt_ref, pre_ref, xn_ref, p_scr, ext_scr, y_scr):
        i = pl.program_id(0)

        @pl.when(i == 0)
        def _():
            ext_scr[0:CHALO, :] = jnp.zeros((CHALO, di), F32)

        xv = x_ref[...]
        xt_ref[...] = xv.T.astype(BF16)
        _in_proj(xv.astype(BF16), win_ref, p_scr)
        for c in range(di // CW):
            sl = slice(c * CW, (c + 1) * CW)
            bb = p_scr[:, sl]
            cc = p_scr[:, di + c * CW:di + (c + 1) * CW]
            hh = p_scr[:, 2 * di + c * CW:2 * di + (c + 1) * CW]
            s, ds = _silu_and_grad(p_scr[:, 3 * di + c * CW:3 * di + (c + 1) * CW])
            ext_scr[CHALO:, sl] = cc * hh
            ext = ext_scr[:, sl]
            conv = (pltpu.roll(ext, 2, 0)[CHALO:] * cw_ref[0:1, sl] + pltpu.roll(ext, 1, 0)[CHALO:] * cw_ref[1:2, sl]
                    + ext[CHALO:] * cw_ref[2:3, sl])
            cs = conv * s
            c1_ref[:, sl] = cs.astype(BF16)
            c2_ref[:, sl] = (bb * conv * ds).astype(BF16)
            c3_ref[:, sl] = (bb * s).astype(BF16)
            cg_ref[:, sl] = cc.astype(BF16)
            hg_ref[:, sl] = hh.astype(BF16)
            y = bb * cs
            y_scr[:, sl] = y.astype(BF16)
            yt_ref[sl, :] = y.T.astype(BF16)
        ext_scr[0:CHALO, :] = ext_scr[TM:TM + CHALO, :]
        out = jnp.dot(y_scr[...], wout_ref[...], preferred_element_type=F32)
        _post_norm(xv, out, lng_ref, lnb_ref, pre_ref, xn_ref)

    sp = _tile_specs(t, d, di)
    return _call(
        body, "fwd_c", (t // TM,), (x, win, convw, wout, lng, lnb),
        in_specs=[sp["xd"]] + [_vmem()] * 5,
        out_specs=[sp["xi"]] * 5 + [sp["ti"], sp["td"], sp["xd"], sp["xd"]],
        out_shape=[sp["s_xi"]] * 5 + [sp["s_ti"], sp["s_td"], sp["s_xd"], sp["s_xd"]],
        scratch=[pltpu.VMEM((TM, 4 * di), F32), pltpu.VMEM((TM + CHALO, di), F32), pltpu.VMEM((TM, di), BF16)],
        exchanges=exchanges)


def _bwd_c(g, pre, c1, c2, c3, cg, hg, wout, convw, lng, exchanges=()):
    t, d = g.shape
    di = wout.shape[0]
    nt = t // TM

    def body(g_ref, pre_ref, c1_ref, c2_ref, c3_ref, cg_ref, hg_ref, wout_ref, cw_ref, lng_ref,
             dpre_ref, dp_ref, dlng_ref, dlnb_ref, dcw_ref, dy_scr, ext_scr):
        i = pl.program_id(0)
        _zero_at_first_step(dlng_ref, dlnb_ref, dcw_ref)

        @pl.when(i == 0)
        def _():
            ext_scr[TM:, :] = jnp.zeros((CHALO, di), F32)

        dpre = _post_norm_bwd(g_ref, pre_ref, lng_ref, dpre_ref, dlng_ref, dlnb_ref)
        dy_scr[...] = lax.dot_general(dpre.astype(BF16), wout_ref[...], _NT, preferred_element_type=F32)
        rows = TM + CHALO
        for c in range(di // CW):
            sl = slice(c * CW, (c + 1) * CW)
            dy = dy_scr[:, sl]
            cc = _f32(cg_ref, sl)
            hh = _f32(hg_ref, sl)
            dp_ref[:, sl] = (dy * _f32(c1_ref, sl)).astype(BF16)
            dp_ref[:, 3 * di + c * CW:3 * di + (c + 1) * CW] = (dy * _f32(c2_ref, sl)).astype(BF16)
            dconv = dy * _f32(c3_ref, sl)
            ext_scr[0:TM, sl] = dconv
            ext = ext_scr[:, sl]
            d1 = pltpu.roll(ext, rows - 1, 0)[0:TM]
            d2 = pltpu.roll(ext, rows - 2, 0)[0:TM]
            dq = dconv * cw_ref[2:3, sl] + d1 * cw_ref[1:2, sl] + d2 * cw_ref[0:1, sl]
            q = cc * hh
            dcw_ref[0, :, sl] += _fold8(q * d2)
            dcw_ref[1, :, sl] += _fold8(q * d1)
            dcw_ref[2, :, sl] += _fold8(q * dconv)
            dp_ref[:, di + c * CW:di + (c + 1) * CW] = (dq * hh).astype(BF16)
            dp_ref[:, 2 * di + c * CW:2 * di + (c + 1) * CW] = (dq * cc).astype(BF16)
        ext_scr[TM:, :] = ext_scr[0:CHALO, :]

    rrow = lambda i: (nt - 1 - i, 0)
    const2 = lambda i: (0, 0)
    xd, xi = pl.BlockSpec((TM, d), rrow), pl.BlockSpec((TM, di), rrow)
    return _call(
        body, "bwd_c", (nt,), (g, pre, c1, c2, c3, cg, hg, wout, convw, lng),
        in_specs=[xd, xd] + [xi] * 5 + [_vmem()] * 3,
        out_specs=[xd, pl.BlockSpec((TM, 4 * di), rrow),
                   pl.BlockSpec((8, d), const2), pl.BlockSpec((8, d), const2),
                   pl.BlockSpec((3, 8, di), lambda i: (0, 0, 0))],
        out_shape=[jax.ShapeDtypeStruct((t, d), F32), jax.ShapeDtypeStruct((t, 4 * di), BF16),
                   jax.ShapeDtypeStruct((8, d), F32), jax.ShapeDtypeStruct((8, d), F32),
                   jax.ShapeDtypeStruct((3, 8, di), F32)],
        scratch=[pltpu.VMEM((TM, di), F32), pltpu.VMEM((TM + CHALO, di), F32)],
        exchanges=exchanges)


def _dx(dp, dpre, win, exchanges=()):
    t, d = dpre.shape
    n = dp.shape[1]
    cs = win.shape[2]

    def body(dp_ref, dpre_ref, win_ref, dx_ref):
        acc = ALPHA * dpre_ref[...]
        for j in range(N_DEV):
            acc += lax.dot_general(dp_ref[:, j * cs:(j + 1) * cs], win_ref[j], _NT, preferred_element_type=F32)
        dx_ref[...] = acc

    row = lambda i: (i, 0)
    (dx,), ex = _call(
        body, "dx", (t // TM,), (dp, dpre, win),
        in_specs=[pl.BlockSpec((TM, n), row), pl.BlockSpec((TM, d), row), _vmem()],
        out_specs=[pl.BlockSpec((TM, d), row)],
        out_shape=[jax.ShapeDtypeStruct((t, d), F32)],
        exchanges=exchanges)
    return dx, ex


def _wgrad(at, b, nb, per_block_rows, name, exchanges=()):
    m_all, t = at.shape
    tn = b.shape[1] // nb
    m = m_all // nb if per_block_rows else m_all
    tk = t
    while tk > 128 and 2 * tk * (m * at.dtype.itemsize + tn * b.dtype.itemsize) > WGRAD_BLOCK_BYTES:
        tk //= 2
    nk = t // tk

    def body(at_ref, b_ref, out_ref, acc):
        k = pl.program_id(1)

        @pl.when(k == 0)
        def _():
            acc[...] = jnp.zeros(acc.shape, F32)

        acc[...] += jnp.dot(at_ref[...], b_ref[...].astype(BF16), preferred_element_type=F32)

        @pl.when(k == nk - 1)
        def _():
            out_ref[0] = acc[...].astype(BF16)

    at_map = (lambda j, k: (j, k)) if per_block_rows else (lambda j, k: (0, k))
    (out,), ex = _call(
        body, name, (nb, nk), (at, b),
        in_specs=[pl.BlockSpec((m, tk), at_map), pl.BlockSpec((tk, tn), lambda j, k: (k, j))],
        out_specs=[pl.BlockSpec((1, m, tn), lambda j, k: (j, 0, 0))],
        out_shape=[jax.ShapeDtypeStruct((nb, m, tn), BF16)],
        scratch=[pltpu.VMEM((m, tn), F32)],
        exchanges=exchanges)
    return out, ex


ADAMW_BLOCK_BYTES = 14 * 1024 * 1024


def _sum_parts(parts_ref):
    g = parts_ref[0].astype(F32)
    for s in range(1, parts_ref.shape[0]):
        g = g + parts_ref[s].astype(F32)
    return g


def _row_tile(rows, bytes_per_row):
    if rows * bytes_per_row <= ADAMW_BLOCK_BYTES:
        return rows
    best = 8
    for cand in range(8, rows, 8):
        if rows % cand == 0 and cand * bytes_per_row <= ADAMW_BLOCK_BYTES:
            best = cand
    return best


def _pair_sum(full, theirs, core, name):
    shape = theirs.shape
    r, c = math.prod(shape[1:-1]), shape[-1]
    tr = _row_tile(r, c * 3 * full.dtype.itemsize)
    half = N_DEV // 2

    def body(core_ref, a_ref, b_ref, out_ref):
        out_ref[...] = (a_ref[...].astype(F32) + b_ref[...].astype(F32)).astype(out_ref.dtype)

    blk = pl.BlockSpec((None, tr, c), lambda q, i, core_ref: (q, i, 0))
    grid_spec = pltpu.PrefetchScalarGridSpec(
        num_scalar_prefetch=1, grid=(half, r // tr),
        in_specs=[pl.BlockSpec((None, None, tr, c), lambda q, i, core_ref: (q, core_ref[0], i, 0)), blk],
        out_specs=blk)
    return pl.pallas_call(
        body, name=name, grid_spec=grid_spec, out_shape=jax.ShapeDtypeStruct((half, r, c), full.dtype),
        compiler_params=_params(("arbitrary", "arbitrary")),
    )(core, full.reshape(half, 2, r, c), theirs.reshape(half, r, c)).reshape(shape)


def _adamw(parts, w, m, v, name):
    s, r, c = parts.shape
    tr = _row_tile(r, c * (s * parts.dtype.itemsize + 7 * 4))

    def body(parts_ref, w_ref, m_ref, v_ref, g_ref, d_ref, nm_ref, nv_ref):
        g = _sum_parts(parts_ref)
        g_ref[...] = g
        d_ref[...], nm_ref[...], nv_ref[...] = _adamw_update(g, w_ref[...], m_ref[...], v_ref[...])

    blk = pl.BlockSpec((tr, c), lambda i: (i, 0))
    return pl.pallas_call(
        body, name=name, grid=(r // tr,),
        in_specs=[pl.BlockSpec((s, tr, c), lambda i: (0, i, 0)), blk, blk, blk],
        out_specs=[blk, blk, blk, blk],
        out_shape=[jax.ShapeDtypeStruct((r, c), F32)] * 4,
        compiler_params=_params(("arbitrary",)),
    )(parts, w, m, v)


def _adamw_update(g, w, m, v):
    bc1 = 1.0 - ADAM_B1 ** ADAM_STEP
    bc2 = 1.0 - ADAM_B2 ** ADAM_STEP
    nm = ADAM_B1 * m + (1.0 - ADAM_B1) * g
    nv = ADAM_B2 * v + (1.0 - ADAM_B2) * (g * g)
    return -ADAM_LR * ((nm / bc1) / (jnp.sqrt(nv / bc2) + ADAM_EPS) + ADAM_WD * w), nm, nv


def _adamw_bucket(parts, members, name):
    n = len(members)
    shapes = [w.shape for w, _, _ in members]
    rows = [math.prod(shp) // _LANES for shp in shapes]
    starts = [sum(rows[:k]) for k in range(n)]

    def body(parts_ref, *refs):
        for k in range(n):
            w_ref, m_ref, v_ref = refs[3 * k:3 * k + 3]
            g_ref, d_ref, nm_ref, nv_ref = refs[3 * n + 4 * k:3 * n + 4 * k + 4]
            g = _sum_parts(parts_ref.at[:, starts[k]:starts[k] + rows[k], :])
            g_ref[...] = g
            d_ref[...], nm_ref[...], nv_ref[...] = _adamw_update(g, w_ref[...], m_ref[...], v_ref[...])

    flat = [a.reshape(-1, _LANES) for mem in members for a in mem]
    outs = pl.pallas_call(
        body, name=name, in_specs=[_vmem()] * (1 + 3 * n), out_specs=[_vmem()] * (4 * n),
        out_shape=[jax.ShapeDtypeStruct((rows[k], _LANES), F32) for k in range(n) for _ in range(4)],
        compiler_params=_params(),
    )(parts, *flat)
    return [[o.reshape(shapes[k]) for o in outs[4 * k:4 * k + 4]] for k in range(n)]


_LANES = 128


def _pack(arrays):
    flat = jnp.concatenate([a.reshape(-1) for a in arrays])
    pad = (-flat.shape[0]) % (8 * _LANES)
    return jnp.pad(flat, (0, pad)).reshape(-1, _LANES)


def _unpack(packed, shapes):
    flat = packed.reshape(-1)
    out, off = [], 0
    for shp in shapes:
        size = math.prod(shp)
        out.append(flat[off:off + size].reshape(shp))
        off += size
    return out


def _spatial_weights(w_s, b_s, rows):
    reps = rows // CHUNK
    tril = jnp.tril(jnp.ones((CHUNK, CHUNK), F32))
    wc = w_s * tril
    eye = jnp.eye(reps, dtype=F32)
    wc2 = jnp.einsum("ab,gts->gatbs", eye, wc).reshape(A_GROUPS, rows, rows)
    bs2 = jnp.tile(b_s, (1, reps)).reshape(A_GROUPS, rows, 1)
    return wc2.astype(BF16), jnp.swapaxes(wc2, 1, 2).astype(BF16), bs2


def _spatial_weight_grad(dwc2, dbs2):
    reps = TM // CHUNK
    tril = jnp.tril(jnp.ones((CHUNK, CHUNK), F32))
    blocks = dwc2.reshape(A_GROUPS, reps, CHUNK, reps, CHUNK)
    dws = sum(blocks[:, a, :, a, :] for a in range(reps)) * tril
    dbs = dbs2.reshape(A_GROUPS, reps, CHUNK).sum(axis=1)
    return dws, dbs


def _row2(a):
    return a.reshape(1, -1)


def kernel(x, a0_w_in, a0_v_gain, a0_v_bias, a0_w_s, a0_b_s, a0_w_out, ln0_gain, ln0_bias, b1_w_in, b1_w_grp, b1_scale, b1_w_out, ln1_gain, ln1_bias, c2_w_in, c2_conv_w, c2_w_out, ln2_gain, ln2_bias, a3_w_in, a3_v_gain, a3_v_bias, a3_w_s, a3_b_s, a3_w_out, ln3_gain, ln3_bias, loss_target, m_a0_w_in, m_a0_v_gain, m_a0_v_bias, m_a0_w_s, m_a0_b_s, m_a0_w_out, m_ln0_gain, m_ln0_bias, m_b1_w_in, m_b1_w_grp, m_b1_scale, m_b1_w_out, m_ln1_gain, m_ln1_bias, m_c2_w_in, m_c2_conv_w, m_c2_w_out, m_ln2_gain, m_ln2_bias, m_a3_w_in, m_a3_v_gain, m_a3_v_bias, m_a3_w_s, m_a3_b_s, m_a3_w_out, m_ln3_gain, m_ln3_bias, v_a0_w_in, v_a0_v_gain, v_a0_v_bias, v_a0_w_s, v_a0_b_s, v_a0_w_out, v_ln0_gain, v_ln0_bias, v_b1_w_in, v_b1_w_grp, v_b1_scale, v_b1_w_out, v_ln1_gain, v_ln1_bias, v_c2_w_in, v_c2_conv_w, v_c2_w_out, v_ln2_gain, v_ln2_bias, v_a3_w_in, v_a3_v_gain, v_a3_v_bias, v_a3_w_s, v_a3_b_s, v_a3_w_out, v_ln3_gain, v_ln3_bias):
    names = ["a0_w_in", "a0_v_gain", "a0_v_bias", "a0_w_s", "a0_b_s", "a0_w_out", "ln0_gain", "ln0_bias",
             "b1_w_in", "b1_w_grp", "b1_scale", "b1_w_out", "ln1_gain", "ln1_bias",
             "c2_w_in", "c2_conv_w", "c2_w_out", "ln2_gain", "ln2_bias",
             "a3_w_in", "a3_v_gain", "a3_v_bias", "a3_w_s", "a3_b_s", "a3_w_out", "ln3_gain", "ln3_bias"]
    env = dict(locals())
    w = {nm: env[nm] for nm in names}
    mom = {nm: env["m_" + nm] for nm in names}
    var = {nm: env["v_" + nm] for nm in names}

    x0 = x[0]
    target = loss_target[0]
    d_model = x0.shape[1]
    di = N_DEV * a0_w_out.shape[0]
    n_grp = len(POOL_WINDOWS)
    gd_b = di // n_grp

    layers = ("a0", "b1", "c2", "a3")
    big_of = {"a0": ["a0_w_in", "a0_w_out"], "b1": ["b1_w_in", "b1_w_grp", "b1_w_out"],
              "c2": ["c2_w_in", "c2_w_out"], "a3": ["a3_w_in", "a3_w_out"]}
    big = [nm for p in layers for nm in big_of[p]]
    bucket_of = {"a0": ["a0_v_gain", "a0_v_bias", "a0_w_s", "a0_b_s", "ln0_gain", "ln0_bias"],
                 "b1": ["b1_scale", "ln1_gain", "ln1_bias"], "c2": ["ln2_gain", "ln2_bias"],
                 "a3": ["a3_v_gain", "a3_v_bias", "a3_w_s", "a3_b_s", "ln3_gain", "ln3_bias"]}
    conv_shape = c2_conv_w.shape
    spatial = {p: _spatial_weights(w[p + "_w_s"], w[p + "_b_s"], TM) for p in ("a0", "a3")}

    def weight_gather(p):
        return _Gather([w[nm].astype(BF16) for nm in big_of[p]])

    (first,) = _exchange_only([_Gather([w[nm].astype(BF16) for nm in big_of["a0"]] + [_pack([c2_conv_w])])],
                              "gather_first")
    gathered = dict(zip(big_of["a0"], first))
    conv_all = jnp.stack([_unpack(first[-1][j], [conv_shape])[0] for j in range(N_DEV)], axis=1)
    conv_full = conv_all.reshape(conv_shape[0], di)
    w_in = lambda p: gathered[p + "_w_in"]
    w_out = lambda p: gathered[p + "_w_out"].reshape(di, d_model)
    saved = {}
    h = x0
    for i, p in enumerate(layers):
        lng, lnb = _row2(w[f"ln{i}_gain"]), _row2(w[f"ln{i}_bias"])
        nxt = layers[i + 1] if i + 1 < len(layers) else None
        ex = [weight_gather(nxt)] if nxt else []
        if p[0] == "a":
            wc2, _, bs2 = spatial[p]
            outs, got = _fwd_a(h, w_in(p), w_out(p), _row2(w[p + "_v_gain"]), _row2(w[p + "_v_bias"]),
                               wc2, bs2, lng, lnb, target=None if nxt else target, exchanges=ex)
        elif p[0] == "b":
            wgrp = jnp.swapaxes(gathered["b1_w_grp"], 0, 1).reshape(n_grp, gd_b, gd_b)
            outs, got = _fwd_b(h, w_in(p), wgrp, _row2(w[p + "_scale"]), w_out(p), lng, lnb, exchanges=ex)
        else:
            outs, got = _fwd_c(h, w_in(p), conv_full, w_out(p), lng, lnb, exchanges=ex)
        if nxt:
            saved[p], h = outs[:-1], outs[-1]
            gathered.update(zip(big_of[nxt], got[0]))
        else:
            saved[p], gcur, sq = outs[:-2], outs[-2], outs[-1]

    loss_share = (jnp.sum(sq) * (0.5 / d_model)).reshape(1)

    part, full, landed, small_all = {}, {}, {}, {}

    def bucket_gather(bucket):
        extra = [loss_share] if bucket == layers[-1] else []
        return _Gather([_pack([part[nm] for nm in bucket_of[bucket]] + extra)])

    core = lax.axis_index("c").astype(jnp.int32).reshape(1)

    def chip_sums(names, got):
        return [_pair_sum(full[nm], theirs, core, "pair_sum_" + nm) for nm, theirs in zip(names, got)]

    pending = None
    for i, p in reversed(list(enumerate(layers))):
        lng = _row2(w[f"ln{i}_gain"])
        ex = []
        if pending:
            ex = [_PairExchange([full[nm] for nm in big_of[pending]]), bucket_gather(pending)]
            if pending == "c2":
                ex.append(_Exchange(scatters=[full["c2_conv_w"]]))
        *factors, yt, xt, pre = saved[p]
        if p[0] == "a":
            wc2, wc2t, bs2 = spatial[p]
            (dpre, dp, dlng, dlnb, dgain, dbias, dbs2, dwc2), got = _bwd_a(
                gcur, pre, *factors, w_out(p), _row2(w[p + "_v_gain"]), wc2, wc2t, bs2, lng, exchanges=ex)
            part[p + "_v_gain"], part[p + "_v_bias"] = dgain.sum(axis=0), dbias.sum(axis=0)
            part[p + "_w_s"], part[p + "_b_s"] = _spatial_weight_grad(dwc2, dbs2)
        elif p[0] == "b":
            b1f, b2f, b3f, poolt = factors
            (dpre, dp, dmixed, dlng, dlnb, dscale), got = _bwd_b(
                gcur, pre, b1f, b2f, b3f, w_out(p), wgrp, lng, exchanges=ex)
            part[p + "_scale"] = dscale.sum(axis=0)
            dwg, _ = _wgrad(poolt, dmixed, n_grp, True, "wgrad_grp")
            full[p + "_w_grp"] = jnp.swapaxes(dwg.reshape(n_grp, N_DEV, gd_b // N_DEV, gd_b), 0, 1)
        else:
            (dpre, dp, dlng, dlnb, dcw), got = _bwd_c(gcur, pre, *factors, w_out(p), conv_full, lng, exchanges=ex)
            dconv = dcw.sum(axis=1).reshape(conv_shape[0], N_DEV, conv_shape[1])
            full["c2_conv_w"] = jnp.stack([_pack([dconv[:, j]]) for j in range(N_DEV)])
        part[f"ln{i}_gain"], part[f"ln{i}_bias"] = dlng.sum(axis=0), dlnb.sum(axis=0)
        if pending:
            small_all[pending] = got[1][0]
            if pending == "c2":
                small_all["conv"] = got[2][0]
        dwo, _ = _wgrad(yt, dpre, 1, False, "wgrad_out_" + p)
        full[p + "_w_out"] = dwo.reshape(N_DEV, di // N_DEV, d_model)
        ex = [_ChipScatter(chip_sums(big_of[pending], got[0]))] if pending else []
        if i == 0:
            ex += [bucket_gather(p)]
        full[p + "_w_in"], got = _wgrad(xt, dp, N_DEV, False, "wgrad_in_" + p, exchanges=ex)
        if pending:
            landed.update(zip(big_of[pending], got[0]))
        if i > 0:
            gcur, _ = _dx(dp, dpre, w_in(p))
            pending = p
        else:
            small_all[p] = got[-1][0]
            (theirs,) = _exchange_only([_PairExchange([full[p + "_w_in"]])], "pair_exchange_last")
            gcur, got = _dx(dp, dpre, w_in(p), exchanges=[
                _ChipScatter(chip_sums([p + "_w_in"], theirs)), _Exchange(scatters=[full[p + "_w_out"]])])
            landed[p + "_w_in"], landed[p + "_w_out"] = got[0][0], got[1][0]
    grad_x = gcur[None]

    grads, deltas, new_m, new_v = {}, {}, {}, {}
    for nm in big:
        shp = w[nm].shape
        r2 = (math.prod(shp[:-1]), shp[-1])
        outs = _adamw(landed[nm].reshape((-1,) + r2), w[nm].reshape(r2), mom[nm].reshape(r2), var[nm].reshape(r2),
                      "adamw_" + nm)
        grads[nm], deltas[nm], new_m[nm], new_v[nm] = (o.reshape(shp) for o in outs)
    buckets = dict(bucket_of, conv=["c2_conv_w"])
    for key, members in buckets.items():
        outs = _adamw_bucket(small_all[key], [(w[nm], mom[nm], var[nm]) for nm in members], "adamw_small_" + key)
        for nm, (g_out, d_out, m_out, v_out) in zip(members, outs):
            grads[nm], deltas[nm], new_m[nm], new_v[nm] = g_out, d_out, m_out, v_out

    loss_row = sum(math.prod(w[nm].shape) for nm in bucket_of[layers[-1]]) // _LANES
    loss = jnp.sum(small_all[layers[-1]][:, loss_row, 0])

    return (loss, grad_x, *[grads[nm] for nm in names], *[deltas[nm] for nm in names],
            *[new_m[nm] for nm in names], *[new_v[nm] for nm in names])
```

```python
import functools
import math

import jax
import jax.numpy as jnp
from jax import lax
from jax.experimental import pallas as pl
from jax.experimental.pallas import tpu as pltpu

F32 = jnp.float32
BF16 = jnp.bfloat16

N_DEV = 8
DEPTH = 4
CHUNK = 128
A_GROUPS = 8
POOL_WINDOWS = (2, 4, 8, 16)
LN_EPS = 1e-5
ALPHA = (2.0 * DEPTH) ** 0.25
ADAM_LR = 0.001
ADAM_B1 = 0.9
ADAM_B2 = 0.999
ADAM_EPS = 1e-08
ADAM_WD = 0.01
ADAM_STEP = 10

TM = 256
DX_TM = 512
HALO = 16
CHALO = 8
CW = 512
WGRAD_BLOCK_BYTES = 36 * 1024 * 1024
VMEM_LIMIT_BYTES = 58 * 1024 * 1024

_NT = (((1,), (1,)), ((), ()))
_SQRT_2_OVER_PI = math.sqrt(2.0 / math.pi)
_MESH = pl.DeviceIdType.MESH


def _vmem():
    return pl.BlockSpec(memory_space=pltpu.VMEM)


def _params(sem=None):
    return pltpu.CompilerParams(dimension_semantics=sem, vmem_limit_bytes=VMEM_LIMIT_BYTES)


def _gelu_and_grad(x):
    c1 = _SQRT_2_OVER_PI * 0.044715
    x2 = x * x
    t = jnp.tanh(x * (_SQRT_2_OVER_PI + c1 * x2))
    cdf = 0.5 + 0.5 * t
    grad = cdf + x * (1.0 - t * t) * (0.5 * _SQRT_2_OVER_PI + (1.5 * c1) * x2)
    return x * cdf, grad


def _silu_and_grad(z):
    sg = 1.0 / (1.0 + jnp.exp(-z))
    s = z * sg
    return s, sg + s - s * sg


def _fold8(a):
    return a.reshape(a.shape[0] // 8, 8, a.shape[1]).sum(axis=0)


def _row_mean(a):
    return jnp.mean(a, axis=-1, keepdims=True)


def _ln_stats(x):
    mu = _row_mean(x)
    xc = x - mu
    rstd = lax.rsqrt(_row_mean(xc * xc) + LN_EPS)
    return xc * rstd, rstd


def _post_norm(x, out, lng_ref, lnb_ref, pre_ref, xn_ref):
    pre = ALPHA * x + out
    pre_ref[...] = pre
    xhat, _ = _ln_stats(pre)
    xn_ref[...] = xhat * lng_ref[...] + lnb_ref[...]


def _post_norm_bwd(g_ref, pre_ref, lng_ref, dpre_ref, dlng_ref, dlnb_ref):
    go = g_ref[...]
    xhat, rstd = _ln_stats(pre_ref[...])
    dlng_ref[...] += _fold8(go * xhat)
    dlnb_ref[...] += _fold8(go)
    dxh = go * lng_ref[...]
    dpre = rstd * (dxh - _row_mean(dxh) - xhat * _row_mean(dxh * xhat))
    dpre_ref[...] = dpre
    return dpre


def _in_proj(xb, win_ref, p_ref, lo=0, hi=None):
    cs = win_ref.shape[2]
    hi = N_DEV * cs if hi is None else hi
    for j in range(N_DEV):
        a, b = max(lo, j * cs), min(hi, (j + 1) * cs)
        if a < b:
            p_ref[:, a:b] = jnp.dot(xb, win_ref[j, :, a - j * cs:b - j * cs], preferred_element_type=F32)


def _zero_at_first_step(*refs):
    @pl.when(pl.program_id(0) == 0)
    def _():
        for r in refs:
            r[...] = jnp.zeros(r.shape, r.dtype)


def _f32(ref, sl):
    return ref[:, sl].astype(F32)


def _my_position():
    x, y, c = lax.axis_index("x"), lax.axis_index("y"), lax.axis_index("c")
    return (x, y, c), 4 * x + 2 * y + c


def _peer(k):
    (x, y, c), _ = _my_position()
    peer = (x ^ (k >> 2), y ^ ((k >> 1) & 1), c ^ (k & 1))
    return peer, 4 * peer[0] + 2 * peer[1] + peer[2]


class _Exchange:
    def __init__(self, gathers=(), scatters=()):
        self.args = list(gathers) + list(scatters)
        self.n_gather = len(gathers)
        self.out_shape = ([jax.ShapeDtypeStruct((N_DEV,) + a.shape, a.dtype) for a in gathers]
                          + [jax.ShapeDtypeStruct(a.shape, a.dtype) for a in scatters])
        n = len(self.args)
        self.scratch = [pltpu.SemaphoreType.DMA((n, N_DEV)), pltpu.SemaphoreType.DMA((n, N_DEV)),
                        pltpu.SemaphoreType.DMA((n,))]

    def _src(self, ins, w, pos):
        return ins[w] if w < self.n_gather else ins[w].at[pos]

    def _copies(self, ins, outs, sems, arrivals):
        send_sems, recv_sems, local_sems = sems
        _, me = _my_position()
        n = len(self.args)
        copies = []
        if not arrivals:
            copies = [pltpu.make_async_copy(self._src(ins, w, me), outs[w].at[me], local_sems.at[w]) for w in range(n)]
        for k in range(1, N_DEV):
            peer, peer_pos = _peer(k)
            for w in range(n):
                copies.append(pltpu.make_async_remote_copy(
                    src_ref=self._src(ins, w, me if arrivals else peer_pos),
                    dst_ref=outs[w].at[peer_pos if arrivals else me],
                    send_sem=send_sems.at[w, k], recv_sem=recv_sems.at[w, k], device_id=peer, device_id_type=_MESH))
        return copies

    def start(self, ins, outs, sems):
        for cp in self._copies(ins, outs, sems, False):
            cp.start()

    def mid(self, ins, outs, sems):
        pass

    def wait(self, ins, outs, sems):
        n = len(self.args)
        for cp in self._copies(ins, outs, sems, True):
            cp.wait_recv()
        own = self._copies(ins, outs, sems, False)
        for cp in own[n:]:
            cp.wait_send()
        for cp in own[:n]:
            cp.wait()


def _remote(src, dst, send_sem, recv_sem, peer):
    return pltpu.make_async_remote_copy(src_ref=src, dst_ref=dst, send_sem=send_sem, recv_sem=recv_sem,
                                        device_id=peer, device_id_type=_MESH)


class _Gather:
    def __init__(self, shards):
        self.args = list(shards)
        n = len(self.args)
        self.out_shape = [jax.ShapeDtypeStruct((N_DEV,) + a.shape, a.dtype) for a in shards]
        self.scratch = [pltpu.SemaphoreType.DMA((n, N_DEV)), pltpu.SemaphoreType.DMA((n, N_DEV)),
                        pltpu.SemaphoreType.DMA((n,))]

    def _own(self, ins, outs, sems):
        send, recv, loc = sems
        _, me = _my_position()
        local = [pltpu.make_async_copy(ins[w], outs[w].at[me], loc.at[w]) for w in range(len(ins))]
        first = [_remote(ins[w], outs[w].at[me], send.at[w, k], recv.at[w, k], _peer(k)[0])
                 for k in (1, 2, 4, 6) for w in range(len(ins))]
        return local, first

    def _passed_on(self, ins, outs, sems):
        send, recv, _ = sems
        sibling, _ = _peer(1)
        return [_remote(outs[w].at[_peer(k)[1]], outs[w].at[_peer(k)[1]], send.at[w, k + 1], recv.at[w, k + 1], sibling)
                for k in (2, 4, 6) for w in range(len(ins))]

    def _arrival(self, ins, outs, sems, k, w):
        send, recv, _ = sems
        peer, pos = _peer(k)
        return _remote(ins[w], outs[w].at[pos], send.at[w, k], recv.at[w, k], peer)

    def start(self, ins, outs, sems):
        local, first = self._own(ins, outs, sems)
        for cp in local + first:
            cp.start()

    def mid(self, ins, outs, sems):
        for k in (2, 4, 6):
            for w in range(len(ins)):
                self._arrival(ins, outs, sems, k, w).wait_recv()
        for cp in self._passed_on(ins, outs, sems):
            cp.start()

    def wait(self, ins, outs, sems):
        for k in (1, 3, 5, 7):
            for w in range(len(ins)):
                self._arrival(ins, outs, sems, k, w).wait_recv()
        local, first = self._own(ins, outs, sems)
        for cp in first + self._passed_on(ins, outs, sems):
            cp.wait_send()
        for cp in local:
            cp.wait()


class _PairExchange:
    def __init__(self, fulls):
        self.args = list(fulls)
        n = len(self.args)
        self.out_shape = [jax.ShapeDtypeStruct((N_DEV // 2,) + a.shape[1:], a.dtype) for a in fulls]
        self.scratch = [pltpu.SemaphoreType.DMA((n, N_DEV // 2)), pltpu.SemaphoreType.DMA((n, N_DEV // 2))]

    def _copies(self, ins, outs, sems):
        send, recv = sems
        (x, y, c), _ = _my_position()
        sibling, _ = _peer(1)
        return [_remote(ins[w].at[2 * q + 1 - c], outs[w].at[q], send.at[w, q], recv.at[w, q], sibling)
                for q in range(N_DEV // 2) for w in range(len(ins))]

    def start(self, ins, outs, sems):
        for cp in self._copies(ins, outs, sems):
            cp.start()

    def mid(self, ins, outs, sems):
        pass

    def wait(self, ins, outs, sems):
        for cp in self._copies(ins, outs, sems):
            cp.wait()


class _ChipScatter:
    def __init__(self, sums):
        self.args = list(sums)
        n = len(self.args)
        self.out_shape = [jax.ShapeDtypeStruct(a.shape, a.dtype) for a in sums]
        self.scratch = [pltpu.SemaphoreType.DMA((n, N_DEV // 2)), pltpu.SemaphoreType.DMA((n, N_DEV // 2)),
                        pltpu.SemaphoreType.DMA((n,))]

    def _copies(self, ins, outs, sems, arrivals):
        send, recv, loc = sems
        (x, y, c), _ = _my_position()
        my_chip = 2 * x + y
        copies = []
        if not arrivals:
            copies = [pltpu.make_async_copy(ins[w].at[my_chip], outs[w].at[my_chip], loc.at[w]) for w in range(len(ins))]
        for k in (1, 2, 3):
            peer = (x ^ (k >> 1), y ^ (k & 1), c)
            chip = my_chip ^ k
            for w in range(len(ins)):
                copies.append(_remote(ins[w].at[my_chip if arrivals else chip], outs[w].at[chip if arrivals else my_chip],
                                      send.at[w, k], recv.at[w, k], peer))
        return copies

    def start(self, ins, outs, sems):
        for cp in self._copies(ins, outs, sems, False):
            cp.start()

    def mid(self, ins, outs, sems):
        pass

    def wait(self, ins, outs, sems):
        n = len(ins)
        for cp in self._copies(ins, outs, sems, True):
            cp.wait_recv()
        own = self._copies(ins, outs, sems, False)
        for cp in own[n:]:
            cp.wait_send()
        for cp in own[:n]:
            cp.wait()


def _split(refs, sizes):
    out, off = [], 0
    for size in sizes:
        out.append(refs[off:off + size])
        off += size
    return out


def _call(body, name, grid, args, in_specs, out_shape, out_specs, scratch=(), exchanges=()):
    sem = ("arbitrary",) * len(grid)
    exchanges = [e for e in exchanges if e is not None]
    if not exchanges:
        outs = pl.pallas_call(body, name=name, grid=grid, in_specs=in_specs, out_specs=out_specs, out_shape=out_shape,
                              scratch_shapes=list(scratch), compiler_params=_params(sem))(*args)
        return outs, []
    n_in, n_out, n_scr = len(args), len(out_shape), len(scratch)
    ex_in = [len(e.args) for e in exchanges]
    ex_out = [len(e.out_shape) for e in exchanges]
    ex_scr = [len(e.scratch) for e in exchanges]
    steps = math.prod(grid)
    mid_step = min((3 * steps) // 4, steps - 1)

    def hosted(*refs):
        main_in, xin, main_out, xout, main_scr, xscr = _split(
            refs, [n_in, sum(ex_in), n_out, sum(ex_out), n_scr, sum(ex_scr)])
        parts = list(zip(exchanges, _split(xin, ex_in), _split(xout, ex_out), _split(xscr, ex_scr)))
        step = pl.program_id(0)
        for a in range(1, len(grid)):
            step = step * grid[a] + pl.program_id(a)

        @pl.when(step == 0)
        def _():
            for e, ins, outs, sems in parts:
                e.start(ins, outs, sems)

        body(*main_in, *main_out, *main_scr)

        @pl.when(step == mid_step)
        def _():
            for e, ins, outs, sems in parts:
                e.mid(ins, outs, sems)

        @pl.when(step == steps - 1)
        def _():
            for e, ins, outs, sems in parts:
                e.wait(ins, outs, sems)

    any_spec = pl.BlockSpec(memory_space=pl.ANY)
    outs = pl.pallas_call(
        hosted, name=name, grid=grid, in_specs=list(in_specs) + [any_spec] * sum(ex_in),
        out_specs=list(out_specs) + [any_spec] * sum(ex_out),
        out_shape=list(out_shape) + [s for e in exchanges for s in e.out_shape],
        scratch_shapes=list(scratch) + [s for e in exchanges for s in e.scratch],
        compiler_params=_params(sem))(*args, *[a for e in exchanges for a in e.args])
    return outs[:n_out], _split(outs[n_out:], ex_out)


def _exchange_only(exchanges, name):
    ex_in = [len(e.args) for e in exchanges]
    ex_out = [len(e.out_shape) for e in exchanges]
    ex_scr = [len(e.scratch) for e in exchanges]

    def body(*refs):
        xin, xout, xscr = _split(refs, [sum(ex_in), sum(ex_out), sum(ex_scr)])
        parts = list(zip(exchanges, _split(xin, ex_in), _split(xout, ex_out), _split(xscr, ex_scr)))
        for phase in ("start", "mid", "wait"):
            for e, ins, outs, sems in parts:
                getattr(e, phase)(ins, outs, sems)

    any_spec = pl.BlockSpec(memory_space=pl.ANY)
    outs = pl.pallas_call(
        body, name=name, in_specs=[any_spec] * sum(ex_in), out_specs=[any_spec] * sum(ex_out),
        out_shape=[s for e in exchanges for s in e.out_shape],
        scratch_shapes=[s for e in exchanges for s in e.scratch])(*[a for e in exchanges for a in e.args])
    return _split(outs, ex_out)


def _tile_specs(t, d, di):
    row = lambda i: (i, 0)
    col = lambda i: (0, i)
    return dict(
        xd=pl.BlockSpec((TM, d), row), xi=pl.BlockSpec((TM, di), row),
        td=pl.BlockSpec((d, TM), col), ti=pl.BlockSpec((di, TM), col),
        s_xd=jax.ShapeDtypeStruct((t, d), F32), s_xi=jax.ShapeDtypeStruct((t, di), BF16),
        s_td=jax.ShapeDtypeStruct((d, t), BF16), s_ti=jax.ShapeDtypeStruct((di, t), BF16))


def _fwd_a(x, win, wout, gain, bias, wc2, bs2, lng, lnb, target=None, exchanges=()):
    t, d = x.shape
    di = wout.shape[0]
    gd = di // A_GROUPS
    n_loss = 0 if target is None else 1

    def body(*refs):
        x_ref, win_ref, wout_ref, gain_ref, bias_ref, wc_ref, bs_ref, lng_ref, lnb_ref = refs[:9]
        (a1_ref, a2_ref, a3_ref, vn_ref, vh_ref, rg_ref, yt_ref, xt_ref, pre_ref,
         xn_ref) = refs[9 + n_loss:19 + n_loss]
        p_scr, vg_scr, y_scr = refs[19 + 2 * n_loss:]
        xv = x_ref[...]
        xt_ref[...] = xv.T.astype(BF16)
        xb = xv.astype(BF16)
        _in_proj(xb, win_ref, p_scr, di, 2 * di)
        s1 = jnp.zeros((TM, 1), F32)
        for c in range(di // CW):
            sl = slice(c * CW, (c + 1) * CW)
            pv = slice(di + c * CW, di + (c + 1) * CW)
            vg, dvg = _gelu_and_grad(p_scr[:, pv])
            vg_scr[:, sl] = vg
            p_scr[:, pv] = dvg
            s1 += jnp.sum(vg, axis=1, keepdims=True)
        mu = s1 * (1.0 / di)
        s2 = jnp.zeros((TM, 1), F32)
        for c in range(di // CW):
            dlt = vg_scr[:, c * CW:(c + 1) * CW] - mu
            s2 += jnp.sum(dlt * dlt, axis=1, keepdims=True)
        rstd = lax.rsqrt(s2 * (1.0 / di) + LN_EPS)
        for c in range(di // CW):
            sl = slice(c * CW, (c + 1) * CW)
            vh = (vg_scr[:, sl] - mu) * rstd
            vh_ref[:, sl] = vh.astype(BF16)
            vn_ref[:, sl] = (vh * gain_ref[:, sl] + bias_ref[:, sl]).astype(BF16)
            rg_ref[:, sl] = (p_scr[:, di + c * CW:di + (c + 1) * CW] * rstd).astype(BF16)
        for g in range(A_GROUPS):
            sl = slice(g * gd, (g + 1) * gd)
            _in_proj(xb, win_ref, p_scr, g * gd, (g + 1) * gd)
            _in_proj(xb, win_ref, p_scr, 2 * di + g * gd, 2 * di + (g + 1) * gd)
            sv = jnp.dot(wc_ref[g], vn_ref[:, sl], preferred_element_type=F32) + bs_ref[g]
            u, du = _gelu_and_grad(p_scr[:, sl])
            s, ds = _silu_and_grad(p_scr[:, 2 * di + g * gd:2 * di + (g + 1) * gd])
            us = u * s
            a1_ref[:, sl] = (s * du).astype(BF16)
            a2_ref[:, sl] = (u * ds).astype(BF16)
            a3_ref[:, sl] = us.astype(BF16)
            y = us * sv
            y_scr[:, sl] = y.astype(BF16)
            yt_ref[sl, :] = y.T.astype(BF16)
        out = jnp.dot(y_scr[...], wout_ref[...], preferred_element_type=F32)
        _post_norm(xv, out, lng_ref, lnb_ref, pre_ref, xn_ref)
        if n_loss:
            t_ref, sq_ref = refs[9], refs[20]
            _zero_at_first_step(sq_ref)
            diff = xn_ref[...] - t_ref[...]
            xn_ref[...] = diff * (1.0 / d)
            sq_ref[...] += _fold8(diff * diff)

    sp = _tile_specs(t, d, di)
    loss_in = [] if target is None else [target]
    return _call(
        body, "fwd_a", (t // TM,), (x, win, wout, gain, bias, wc2, bs2, lng, lnb, *loss_in),
        in_specs=[sp["xd"]] + [_vmem()] * 8 + [sp["xd"]] * n_loss,
        out_specs=([sp["xi"]] * 6 + [sp["ti"], sp["td"], sp["xd"], sp["xd"]]
                   + [pl.BlockSpec((8, d), lambda i: (0, 0))] * n_loss),
        out_shape=([sp["s_xi"]] * 6 + [sp["s_ti"], sp["s_td"], sp["s_xd"], sp["s_xd"]]
                   + [jax.ShapeDtypeStruct((8, d), F32)] * n_loss),
        scratch=[pltpu.VMEM((TM, 3 * di), F32), pltpu.VMEM((TM, di), F32), pltpu.VMEM((TM, di), BF16)],
        exchanges=exchanges)


def _bwd_a(g, pre, a1, a2, a3, vn, vh, rg, wout, gain, wc2, wc2t, bs2, lng, exchanges=()):
    t, d = g.shape
    di = wout.shape[0]
    gd = di // A_GROUPS

    def body(g_ref, pre_ref, a1_ref, a2_ref, a3_ref, vn_ref, vh_ref, rg_ref,
             wout_ref, gain_ref, wc_ref, wct_ref, bs_ref, lng_ref,
             dpre_ref, dp_ref, dlng_ref, dlnb_ref, dgain_ref, dbias_ref, dbs_ref, dwc_ref,
             dy_scr, dv_scr):
        _zero_at_first_step(dlng_ref, dlnb_ref, dgain_ref, dbias_ref, dbs_ref, dwc_ref)
        dpre = _post_norm_bwd(g_ref, pre_ref, lng_ref, dpre_ref, dlng_ref, dlnb_ref)
        dy_scr[...] = lax.dot_general(dpre.astype(BF16), wout_ref[...], _NT, preferred_element_type=F32)
        for grp in range(A_GROUPS):
            sl = slice(grp * gd, (grp + 1) * gd)
            vn_g = vn_ref[:, sl]
            sv = jnp.dot(wc_ref[grp], vn_g, preferred_element_type=F32) + bs_ref[grp]
            dy = dy_scr[:, sl]
            dys = dy * sv
            dp_ref[:, sl] = (dys * _f32(a1_ref, sl)).astype(BF16)
            dp_ref[:, 2 * di + grp * gd:2 * di + (grp + 1) * gd] = (dys * _f32(a2_ref, sl)).astype(BF16)
            dsv = dy * _f32(a3_ref, sl)
            dbs_ref[grp] += jnp.sum(dsv, axis=1, keepdims=True)
            dsvb = dsv.astype(BF16)
            dwc_ref[grp] += lax.dot_general(dsvb, vn_g, _NT, preferred_element_type=F32)
            dv_scr[:, sl] = jnp.dot(wct_ref[grp], dsvb, preferred_element_type=F32)
        r1 = jnp.zeros((TM, 1), F32)
        r2 = jnp.zeros((TM, 1), F32)
        for c in range(di // CW):
            sl = slice(c * CW, (c + 1) * CW)
            dv = dv_scr[:, sl]
            vhat = _f32(vh_ref, sl)
            dgain_ref[:, sl] += _fold8(dv * vhat)
            dbias_ref[:, sl] += _fold8(dv)
            dvh = dv * gain_ref[:, sl]
            dv_scr[:, sl] = dvh
            r1 += jnp.sum(dvh, axis=1, keepdims=True)
            r2 += jnp.sum(dvh * vhat, axis=1, keepdims=True)
        m1 = r1 * (1.0 / di)
        m2 = r2 * (1.0 / di)
        for c in range(di // CW):
            sl = slice(c * CW, (c + 1) * CW)
            dp_ref[:, di + c * CW:di + (c + 1) * CW] = (
                (dv_scr[:, sl] - m1 - _f32(vh_ref, sl) * m2) * _f32(rg_ref, sl)).astype(BF16)

    sp = _tile_specs(t, d, di)
    const2 = lambda i: (0, 0)
    const3 = lambda i: (0, 0, 0)
    return _call(
        body, "bwd_a", (t // TM,), (g, pre, a1, a2, a3, vn, vh, rg, wout, gain, wc2, wc2t, bs2, lng),
        in_specs=[sp["xd"], sp["xd"]] + [sp["xi"]] * 6 + [_vmem()] * 6,
        out_specs=[sp["xd"], pl.BlockSpec((TM, 3 * di), lambda i: (i, 0)),
                   pl.BlockSpec((8, d), const2), pl.BlockSpec((8, d), const2),
                   pl.BlockSpec((8, di), const2), pl.BlockSpec((8, di), const2),
                   pl.BlockSpec((A_GROUPS, TM, 1), const3), pl.BlockSpec((A_GROUPS, TM, TM), const3)],
        out_shape=[sp["s_xd"], jax.ShapeDtypeStruct((t, 3 * di), BF16),
                   jax.ShapeDtypeStruct((8, d), F32), jax.ShapeDtypeStruct((8, d), F32),
                   jax.ShapeDtypeStruct((8, di), F32), jax.ShapeDtypeStruct((8, di), F32),
                   jax.ShapeDtypeStruct((A_GROUPS, TM, 1), F32), jax.ShapeDtypeStruct((A_GROUPS, TM, TM), F32)],
        scratch=[pltpu.VMEM((TM, di), F32), pltpu.VMEM((TM, di), F32)],
        exchanges=exchanges)


def _inv_count(tile, window, rows=TM):
    pos = tile * rows + lax.broadcasted_iota(jnp.int32, (rows, 1), 0)
    return 1.0 / jnp.minimum(pos + 1, window).astype(F32)


def _window_sum(ext, window, down):
    rows = ext.shape[0]
    k = 1
    while k < window:
        ext = ext + pltpu.roll(ext, k if down else rows - k, 0)
        k *= 2
    return ext


def _fwd_b(x, win, wgrp, scale, wout, lng, lnb, exchanges=()):
    t, d = x.shape
    di = wout.shape[0]
    gd = di // len(POOL_WINDOWS)

    def body(x_ref, win_ref, wgrp_ref, scale_ref, wout_ref, lng_ref, lnb_ref,
             b1_ref, b2_ref, b3_ref, poolt_ref, yt_ref, xt_ref, pre_ref, xn_ref, p_scr, ext_scr, y_scr):
        i = pl.program_id(0)

        @pl.when(i == 0)
        def _():
            ext_scr[0:HALO, :] = jnp.zeros((HALO, di), F32)

        xv = x_ref[...]
        xt_ref[...] = xv.T.astype(BF16)
        _in_proj(xv.astype(BF16), win_ref, p_scr)
        ext_scr[HALO:, :] = p_scr[:, :di]
        for grp, window in enumerate(POOL_WINDOWS):
            sl = slice(grp * gd, (grp + 1) * gd)
            ext = ext_scr[:, sl]
            pooled = (_window_sum(ext, window, True)[HALO:] * _inv_count(i, window) - ext[HALO:]).astype(BF16)
            poolt_ref[sl, :] = pooled.astype(F32).T.astype(BF16)
            mixed = jnp.dot(pooled, wgrp_ref[grp], preferred_element_type=F32)
            s, ds = _silu_and_grad(p_scr[:, di + grp * gd:di + (grp + 1) * gd])
            sc = scale_ref[:, sl]
            ms = mixed * s
            b1_ref[:, sl] = (mixed * sc * ds).astype(BF16)
            b2_ref[:, sl] = ms.astype(BF16)
            b3_ref[:, sl] = (sc * s).astype(BF16)
            y = ms * sc
            y_scr[:, sl] = y.astype(BF16)
            yt_ref[sl, :] = y.T.astype(BF16)
        ext_scr[0:HALO, :] = ext_scr[TM:TM + HALO, :]
        out = jnp.dot(y_scr[...], wout_ref[...], preferred_element_type=F32)
        _post_norm(xv, out, lng_ref, lnb_ref, pre_ref, xn_ref)

    sp = _tile_specs(t, d, di)
    return _call(
        body, "fwd_b", (t // TM,), (x, win, wgrp, scale, wout, lng, lnb),
        in_specs=[sp["xd"]] + [_vmem()] * 6,
        out_specs=[sp["xi"]] * 3 + [sp["ti"], sp["ti"], sp["td"], sp["xd"], sp["xd"]],
        out_shape=[sp["s_xi"]] * 3 + [sp["s_ti"], sp["s_ti"], sp["s_td"], sp["s_xd"], sp["s_xd"]],
        scratch=[pltpu.VMEM((TM, 2 * di), F32), pltpu.VMEM((TM + HALO, di), F32), pltpu.VMEM((TM, di), BF16)],
        exchanges=exchanges)


def _bwd_b(g, pre, b1, b2, b3, wout, wgrp, lng, exchanges=()):
    t, d = g.shape
    di = wout.shape[0]
    gd = di // len(POOL_WINDOWS)
    nt = t // TM

    def body(g_ref, pre_ref, b1_ref, b2_ref, b3_ref, wout_ref, wgrp_ref, lng_ref,
             dpre_ref, dp_ref, dmix_ref, dlng_ref, dlnb_ref, dscale_ref, dy_scr, ext_scr):
        i = pl.program_id(0)
        tile = nt - 1 - i
        _zero_at_first_step(dlng_ref, dlnb_ref, dscale_ref)

        @pl.when(i == 0)
        def _():
            ext_scr[TM:, :] = jnp.zeros((HALO, di), F32)

        dpre = _post_norm_bwd(g_ref, pre_ref, lng_ref, dpre_ref, dlng_ref, dlnb_ref)
        dy_scr[...] = lax.dot_general(dpre.astype(BF16), wout_ref[...], _NT, preferred_element_type=F32)
        for grp, window in enumerate(POOL_WINDOWS):
            sl = slice(grp * gd, (grp + 1) * gd)
            dy = dy_scr[:, sl]
            dp_ref[:, di + grp * gd:di + (grp + 1) * gd] = (dy * _f32(b1_ref, sl)).astype(BF16)
            dscale_ref[:, sl] += _fold8(dy * _f32(b2_ref, sl))
            dmixed = (dy * _f32(b3_ref, sl)).astype(BF16)
            dmix_ref[:, sl] = dmixed
            dpooled = lax.dot_general(dmixed, wgrp_ref[grp], _NT, preferred_element_type=F32)
            ext_scr[0:TM, sl] = dpooled * _inv_count(tile, window)
            dv = _window_sum(ext_scr[:, sl], window, False)[0:TM] - dpooled
            dp_ref[:, sl] = dv.astype(BF16)
        ext_scr[TM:, :] = ext_scr[0:HALO, :]

    rrow = lambda i: (nt - 1 - i, 0)
    const2 = lambda i: (0, 0)
    xd, xi = pl.BlockSpec((TM, d), rrow), pl.BlockSpec((TM, di), rrow)
    return _call(
        body, "bwd_b", (nt,), (g, pre, b1, b2, b3, wout, wgrp, lng),
        in_specs=[xd, xd, xi, xi, xi, _vmem(), _vmem(), _vmem()],
        out_specs=[xd, pl.BlockSpec((TM, 2 * di), rrow), xi,
                   pl.BlockSpec((8, d), const2), pl.BlockSpec((8, d), const2), pl.BlockSpec((8, di), const2)],
        out_shape=[jax.ShapeDtypeStruct((t, d), F32), jax.ShapeDtypeStruct((t, 2 * di), BF16),
                   jax.ShapeDtypeStruct((t, di), BF16),
                   jax.ShapeDtypeStruct((8, d), F32), jax.ShapeDtypeStruct((8, d), F32),
                   jax.ShapeDtypeStruct((8, di), F32)],
        scratch=[pltpu.VMEM((TM, di), F32), pltpu.VMEM((TM + HALO, di), F32)],
        exchanges=exchanges)


def _fwd_c(x, win, convw, wout, lng, lnb, exchanges=()):
    t, d = x.shape
    di = wout.shape[0]

    def body(x_ref, win_ref, cw_ref, wout_ref, lng_ref, lnb_ref,
             c1_ref, c2_ref, c3_ref, cg_ref, hg_ref, yt_ref, xt_ref, pre_ref, xn_ref, p_scr, ext_scr, y_scr):
        i = pl.program_id(0)

        @pl.when(i == 0)
        def _():
            ext_scr[0:CHALO, :] = jnp.zeros((CHALO, di), F32)

        xv = x_ref[...]
        xt_ref[...] = xv.T.astype(BF16)
        _in_proj(xv.astype(BF16), win_ref, p_scr)
        for c in range(di // CW):
            sl = slice(c * CW, (c + 1) * CW)
            bb = p_scr[:, sl]
            cc = p_scr[:, di + c * CW:di + (c + 1) * CW]
            hh = p_scr[:, 2 * di + c * CW:2 * di + (c + 1) * CW]
            s, ds = _silu_and_grad(p_scr[:, 3 * di + c * CW:3 * di + (c + 1) * CW])
            ext_scr[CHALO:, sl] = cc * hh
            ext = ext_scr[:, sl]
            conv = (pltpu.roll(ext, 2, 0)[CHALO:] * cw_ref[0:1, sl] + pltpu.roll(ext, 1, 0)[CHALO:] * cw_ref[1:2, sl]
                    + ext[CHALO:] * cw_ref[2:3, sl])
            cs = conv * s
            c1_ref[:, sl] = cs.astype(BF16)
            c2_ref[:, sl] = (bb * conv * ds).astype(BF16)
            c3_ref[:, sl] = (bb * s).astype(BF16)
            cg_ref[:, sl] = cc.astype(BF16)
            hg_ref[:, sl] = hh.astype(BF16)
            y = bb * cs
            y_scr[:, sl] = y.astype(BF16)
            yt_ref[sl, :] = y.T.astype(BF16)
        ext_scr[0:CHALO, :] = ext_scr[TM:TM + CHALO, :]
        out = jnp.dot(y_scr[...], wout_ref[...], preferred_element_type=F32)
        _post_norm(xv, out, lng_ref, lnb_ref, pre_ref, xn_ref)

    sp = _tile_specs(t, d, di)
    return _call(
        body, "fwd_c", (t // TM,), (x, win, convw, wout, lng, lnb),
        in_specs=[sp["xd"]] + [_vmem()] * 5,
        out_specs=[sp["xi"]] * 5 + [sp["ti"], sp["td"], sp["xd"], sp["xd"]],
        out_shape=[sp["s_xi"]] * 5 + [sp["s_ti"], sp["s_td"], sp["s_xd"], sp["s_xd"]],
        scratch=[pltpu.VMEM((TM, 4 * di), F32), pltpu.VMEM((TM + CHALO, di), F32), pltpu.VMEM((TM, di), BF16)],
        exchanges=exchanges)


def _bwd_c(g, pre, c1, c2, c3, cg, hg, wout, convw, lng, exchanges=()):
    t, d = g.shape
    di = wout.shape[0]
    nt = t // TM

    def body(g_ref, pre_ref, c1_ref, c2_ref, c3_ref, cg_ref, hg_ref, wout_ref, cw_ref, lng_ref,
             dpre_ref, dp_ref, dlng_ref, dlnb_ref, dcw_ref, dy_scr, ext_scr):
        i = pl.program_id(0)
        _zero_at_first_step(dlng_ref, dlnb_ref, dcw_ref)

        @pl.when(i == 0)
        def _():
            ext_scr[TM:, :] = jnp.zeros((CHALO, di), F32)

        dpre = _post_norm_bwd(g_ref, pre_ref, lng_ref, dpre_ref, dlng_ref, dlnb_ref)
        dy_scr[...] = lax.dot_general(dpre.astype(BF16), wout_ref[...], _NT, preferred_element_type=F32)
        rows = TM + CHALO
        for c in range(di // CW):
            sl = slice(c * CW, (c + 1) * CW)
            dy = dy_scr[:, sl]
            cc = _f32(cg_ref, sl)
            hh = _f32(hg_ref, sl)
            dp_ref[:, sl] = (dy * _f32(c1_ref, sl)).astype(BF16)
            dp_ref[:, 3 * di + c * CW:3 * di + (c + 1) * CW] = (dy * _f32(c2_ref, sl)).astype(BF16)
            dconv = dy * _f32(c3_ref, sl)
            ext_scr[0:TM, sl] = dconv
            ext = ext_scr[:, sl]
            d1 = pltpu.roll(ext, rows - 1, 0)[0:TM]
            d2 = pltpu.roll(ext, rows - 2, 0)[0:TM]
            dq = dconv * cw_ref[2:3, sl] + d1 * cw_ref[1:2, sl] + d2 * cw_ref[0:1, sl]
            q = cc * hh
            dcw_ref[0, :, sl] += _fold8(q * d2)
            dcw_ref[1, :, sl] += _fold8(q * d1)
            dcw_ref[2, :, sl] += _fold8(q * dconv)
            dp_ref[:, di + c * CW:di + (c + 1) * CW] = (dq * hh).astype(BF16)
            dp_ref[:, 2 * di + c * CW:2 * di + (c + 1) * CW] = (dq * cc).astype(BF16)
        ext_scr[TM:, :] = ext_scr[0:CHALO, :]

    rrow = lambda i: (nt - 1 - i, 0)
    const2 = lambda i: (0, 0)
    xd, xi = pl.BlockSpec((TM, d), rrow), pl.BlockSpec((TM, di), rrow)
    return _call(
        body, "bwd_c", (nt,), (g, pre, c1, c2, c3, cg, hg, wout, convw, lng),
        in_specs=[xd, xd] + [xi] * 5 + [_vmem()] * 3,
        out_specs=[xd, pl.BlockSpec((TM, 4 * di), rrow),
                   pl.BlockSpec((8, d), const2), pl.BlockSpec((8, d), const2),
                   pl.BlockSpec((3, 8, di), lambda i: (0, 0, 0))],
        out_shape=[jax.ShapeDtypeStruct((t, d), F32), jax.ShapeDtypeStruct((t, 4 * di), BF16),
                   jax.ShapeDtypeStruct((8, d), F32), jax.ShapeDtypeStruct((8, d), F32),
                   jax.ShapeDtypeStruct((3, 8, di), F32)],
        scratch=[pltpu.VMEM((TM, di), F32), pltpu.VMEM((TM + CHALO, di), F32)],
        exchanges=exchanges)


def _dx(dp, dpre, win, exchanges=()):
    t, d = dpre.shape
    n = dp.shape[1]
    cs = win.shape[2]

    def body(dp_ref, dpre_ref, win_ref, dx_ref):
        acc = ALPHA * dpre_ref[...]
        for j in range(N_DEV):
            acc += lax.dot_general(dp_ref[:, j * cs:(j + 1) * cs], win_ref[j], _NT, preferred_element_type=F32)
        dx_ref[...] = acc

    row = lambda i: (i, 0)
    tm = min(DX_TM, t)
    (dx,), ex = _call(
        body, "dx", (t // tm,), (dp, dpre, win),
        in_specs=[pl.BlockSpec((tm, n), row), pl.BlockSpec((tm, d), row), _vmem()],
        out_specs=[pl.BlockSpec((tm, d), row)],
        out_shape=[jax.ShapeDtypeStruct((t, d), F32)],
        exchanges=exchanges)
    return dx, ex


def _wgrad(at, b, nb, per_block_rows, name, exchanges=()):
    m_all, t = at.shape
    tn = b.shape[1] // nb
    m = m_all // nb if per_block_rows else m_all
    tk = t
    while tk > 128 and 2 * tk * (m * at.dtype.itemsize + tn * b.dtype.itemsize) > WGRAD_BLOCK_BYTES:
        tk //= 2
    nk = t // tk

    def body(at_ref, b_ref, out_ref, acc):
        k = pl.program_id(1)

        @pl.when(k == 0)
        def _():
            acc[...] = jnp.zeros(acc.shape, F32)

        acc[...] += jnp.dot(at_ref[...], b_ref[...].astype(BF16), preferred_element_type=F32)

        @pl.when(k == nk - 1)
        def _():
            out_ref[0] = acc[...].astype(BF16)

    at_map = (lambda j, k: (j, k)) if per_block_rows else (lambda j, k: (0, k))
    (out,), ex = _call(
        body, name, (nb, nk), (at, b),
        in_specs=[pl.BlockSpec((m, tk), at_map), pl.BlockSpec((tk, tn), lambda j, k: (k, j))],
        out_specs=[pl.BlockSpec((1, m, tn), lambda j, k: (j, 0, 0))],
        out_shape=[jax.ShapeDtypeStruct((nb, m, tn), BF16)],
        scratch=[pltpu.VMEM((m, tn), F32)],
        exchanges=exchanges)
    return out, ex


ADAMW_BLOCK_BYTES = 14 * 1024 * 1024


def _sum_parts(parts_ref):
    g = parts_ref[0].astype(F32)
    for s in range(1, parts_ref.shape[0]):
        g = g + parts_ref[s].astype(F32)
    return g


def _row_tile(rows, bytes_per_row):
    if rows * bytes_per_row <= ADAMW_BLOCK_BYTES:
        return rows
    best = 8
    for cand in range(8, rows, 8):
        if rows % cand == 0 and cand * bytes_per_row <= ADAMW_BLOCK_BYTES:
            best = cand
    return best


def _pair_sum(full, theirs, core, name):
    shape = theirs.shape
    r, c = math.prod(shape[1:-1]), shape[-1]
    tr = _row_tile(r, c * 3 * full.dtype.itemsize)
    half = N_DEV // 2

    def body(core_ref, a_ref, b_ref, out_ref):
        out_ref[...] = (a_ref[...].astype(F32) + b_ref[...].astype(F32)).astype(out_ref.dtype)

    blk = pl.BlockSpec((None, tr, c), lambda q, i, core_ref: (q, i, 0))
    grid_spec = pltpu.PrefetchScalarGridSpec(
        num_scalar_prefetch=1, grid=(half, r // tr),
        in_specs=[pl.BlockSpec((None, None, tr, c), lambda q, i, core_ref: (q, core_ref[0], i, 0)), blk],
        out_specs=blk)
    return pl.pallas_call(
        body, name=name, grid_spec=grid_spec, out_shape=jax.ShapeDtypeStruct((half, r, c), full.dtype),
        compiler_params=_params(("arbitrary", "arbitrary")),
    )(core, full.reshape(half, 2, r, c), theirs.reshape(half, r, c)).reshape(shape)


def _adamw(parts, w, m, v, name):
    s, r, c = parts.shape
    tr = _row_tile(r, c * (s * parts.dtype.itemsize + 7 * 4))

    def body(parts_ref, w_ref, m_ref, v_ref, g_ref, d_ref, nm_ref, nv_ref):
        g = _sum_parts(parts_ref)
        g_ref[...] = g
        d_ref[...], nm_ref[...], nv_ref[...] = _adamw_update(g, w_ref[...], m_ref[...], v_ref[...])

    blk = pl.BlockSpec((tr, c), lambda i: (i, 0))
    return pl.pallas_call(
        body, name=name, grid=(r // tr,),
        in_specs=[pl.BlockSpec((s, tr, c), lambda i: (0, i, 0)), blk, blk, blk],
        out_specs=[blk, blk, blk, blk],
        out_shape=[jax.ShapeDtypeStruct((r, c), F32)] * 4,
        compiler_params=_params(("arbitrary",)),
    )(parts, w, m, v)


def _adamw_update(g, w, m, v):
    bc1 = 1.0 - ADAM_B1 ** ADAM_STEP
    bc2 = 1.0 - ADAM_B2 ** ADAM_STEP
    nm = ADAM_B1 * m + (1.0 - ADAM_B1) * g
    nv = ADAM_B2 * v + (1.0 - ADAM_B2) * (g * g)
    return -ADAM_LR * ((nm / bc1) / (jnp.sqrt(nv / bc2) + ADAM_EPS) + ADAM_WD * w), nm, nv


def _adamw_bucket(parts, members, name):
    n = len(members)
    shapes = [w.shape for w, _, _ in members]
    rows = [math.prod(shp) // _LANES for shp in shapes]
    starts = [sum(rows[:k]) for k in range(n)]

    def body(parts_ref, *refs):
        for k in range(n):
            w_ref, m_ref, v_ref = refs[3 * k:3 * k + 3]
            g_ref, d_ref, nm_ref, nv_ref = refs[3 * n + 4 * k:3 * n + 4 * k + 4]
            g = _sum_parts(parts_ref.at[:, starts[k]:starts[k] + rows[k], :])
            g_ref[...] = g
            d_ref[...], nm_ref[...], nv_ref[...] = _adamw_update(g, w_ref[...], m_ref[...], v_ref[...])

    flat = [a.reshape(-1, _LANES) for mem in members for a in mem]
    outs = pl.pallas_call(
        body, name=name, in_specs=[_vmem()] * (1 + 3 * n), out_specs=[_vmem()] * (4 * n),
        out_shape=[jax.ShapeDtypeStruct((rows[k], _LANES), F32) for k in range(n) for _ in range(4)],
        compiler_params=_params(),
    )(parts, *flat)
    return [[o.reshape(shapes[k]) for o in outs[4 * k:4 * k + 4]] for k in range(n)]


_LANES = 128


def _pack(arrays):
    flat = jnp.concatenate([a.reshape(-1) for a in arrays])
    pad = (-flat.shape[0]) % (8 * _LANES)
    return jnp.pad(flat, (0, pad)).reshape(-1, _LANES)


def _unpack(packed, shapes):
    flat = packed.reshape(-1)
    out, off = [], 0
    for shp in shapes:
        size = math.prod(shp)
        out.append(flat[off:off + size].reshape(shp))
        off += size
    return out


def _spatial_weights(w_s, b_s, rows):
    reps = rows // CHUNK
    tril = jnp.tril(jnp.ones((CHUNK, CHUNK), F32))
    wc = w_s * tril
    eye = jnp.eye(reps, dtype=F32)
    wc2 = jnp.einsum("ab,gts->gatbs", eye, wc).reshape(A_GROUPS, rows, rows)
    bs2 = jnp.tile(b_s, (1, reps)).reshape(A_GROUPS, rows, 1)
    return wc2.astype(BF16), jnp.swapaxes(wc2, 1, 2).astype(BF16), bs2


def _spatial_weight_grad(dwc2, dbs2):
    reps = TM // CHUNK
    tril = jnp.tril(jnp.ones((CHUNK, CHUNK), F32))
    blocks = dwc2.reshape(A_GROUPS, reps, CHUNK, reps, CHUNK)
    dws = sum(blocks[:, a, :, a, :] for a in range(reps)) * tril
    dbs = dbs2.reshape(A_GROUPS, reps, CHUNK).sum(axis=1)
    return dws, dbs


def _row2(a):
    return a.reshape(1, -1)


def kernel(x, a0_w_in, a0_v_gain, a0_v_bias, a0_w_s, a0_b_s, a0_w_out, ln0_gain, ln0_bias, b1_w_in, b1_w_grp, b1_scale, b1_w_out, ln1_gain, ln1_bias, c2_w_in, c2_conv_w, c2_w_out, ln2_gain, ln2_bias, a3_w_in, a3_v_gain, a3_v_bias, a3_w_s, a3_b_s, a3_w_out, ln3_gain, ln3_bias, loss_target, m_a0_w_in, m_a0_v_gain, m_a0_v_bias, m_a0_w_s, m_a0_b_s, m_a0_w_out, m_ln0_gain, m_ln0_bias, m_b1_w_in, m_b1_w_grp, m_b1_scale, m_b1_w_out, m_ln1_gain, m_ln1_bias, m_c2_w_in, m_c2_conv_w, m_c2_w_out, m_ln2_gain, m_ln2_bias, m_a3_w_in, m_a3_v_gain, m_a3_v_bias, m_a3_w_s, m_a3_b_s, m_a3_w_out, m_ln3_gain, m_ln3_bias, v_a0_w_in, v_a0_v_gain, v_a0_v_bias, v_a0_w_s, v_a0_b_s, v_a0_w_out, v_ln0_gain, v_ln0_bias, v_b1_w_in, v_b1_w_grp, v_b1_scale, v_b1_w_out, v_ln1_gain, v_ln1_bias, v_c2_w_in, v_c2_conv_w, v_c2_w_out, v_ln2_gain, v_ln2_bias, v_a3_w_in, v_a3_v_gain, v_a3_v_bias, v_a3_w_s, v_a3_b_s, v_a3_w_out, v_ln3_gain, v_ln3_bias):
    names = ["a0_w_in", "a0_v_gain", "a0_v_bias", "a0_w_s", "a0_b_s", "a0_w_out", "ln0_gain", "ln0_bias",
             "b1_w_in", "b1_w_grp", "b1_scale", "b1_w_out", "ln1_gain", "ln1_bias",
             "c2_w_in", "c2_conv_w", "c2_w_out", "ln2_gain", "ln2_bias",
             "a3_w_in", "a3_v_gain", "a3_v_bias", "a3_w_s", "a3_b_s", "a3_w_out", "ln3_gain", "ln3_bias"]
    env = dict(locals())
    w = {nm: env[nm] for nm in names}
    mom = {nm: env["m_" + nm] for nm in names}
    var = {nm: env["v_" + nm] for nm in names}

    x0 = x[0]
    target = loss_target[0]
    d_model = x0.shape[1]
    di = N_DEV * a0_w_out.shape[0]
    n_grp = len(POOL_WINDOWS)
    gd_b = di // n_grp

    layers = ("a0", "b1", "c2", "a3")
    big_of = {"a0": ["a0_w_in", "a0_w_out"], "b1": ["b1_w_in", "b1_w_grp", "b1_w_out"],
              "c2": ["c2_w_in", "c2_w_out"], "a3": ["a3_w_in", "a3_w_out"]}
    big = [nm for p in layers for nm in big_of[p]]
    bucket_of = {"a0": ["a0_v_gain", "a0_v_bias", "a0_w_s", "a0_b_s", "ln0_gain", "ln0_bias"],
                 "b1": ["b1_scale", "ln1_gain", "ln1_bias"], "c2": ["ln2_gain", "ln2_bias"],
                 "a3": ["a3_v_gain", "a3_v_bias", "a3_w_s", "a3_b_s", "ln3_gain", "ln3_bias"]}
    conv_shape = c2_conv_w.shape
    spatial = {p: _spatial_weights(w[p + "_w_s"], w[p + "_b_s"], TM) for p in ("a0", "a3")}

    def weight_gather(p):
        return _Gather([w[nm].astype(BF16) for nm in big_of[p]])

    (first,) = _exchange_only([_Gather([w[nm].astype(BF16) for nm in big_of["a0"]] + [_pack([c2_conv_w])])],
                              "gather_first")
    gathered = dict(zip(big_of["a0"], first))
    conv_all = jnp.stack([_unpack(first[-1][j], [conv_shape])[0] for j in range(N_DEV)], axis=1)
    conv_full = conv_all.reshape(conv_shape[0], di)
    w_in = lambda p: gathered[p + "_w_in"]
    w_out = lambda p: gathered[p + "_w_out"].reshape(di, d_model)
    saved = {}
    h = x0
    for i, p in enumerate(layers):
        lng, lnb = _row2(w[f"ln{i}_gain"]), _row2(w[f"ln{i}_bias"])
        nxt = layers[i + 1] if i + 1 < len(layers) else None
        ex = [weight_gather(nxt)] if nxt else []
        if p[0] == "a":
            wc2, _, bs2 = spatial[p]
            outs, got = _fwd_a(h, w_in(p), w_out(p), _row2(w[p + "_v_gain"]), _row2(w[p + "_v_bias"]),
                               wc2, bs2, lng, lnb, target=None if nxt else target, exchanges=ex)
        elif p[0] == "b":
            wgrp = jnp.swapaxes(gathered["b1_w_grp"], 0, 1).reshape(n_grp, gd_b, gd_b)
            outs, got = _fwd_b(h, w_in(p), wgrp, _row2(w[p + "_scale"]), w_out(p), lng, lnb, exchanges=ex)
        else:
            outs, got = _fwd_c(h, w_in(p), conv_full, w_out(p), lng, lnb, exchanges=ex)
        if nxt:
            saved[p], h = outs[:-1], outs[-1]
            gathered.update(zip(big_of[nxt], got[0]))
        else:
            saved[p], gcur, sq = outs[:-2], outs[-2], outs[-1]

    loss_share = (jnp.sum(sq) * (0.5 / d_model)).reshape(1)

    part, full, landed, small_all = {}, {}, {}, {}

    def bucket_gather(bucket):
        extra = [loss_share] if bucket == layers[-1] else []
        return _Gather([_pack([part[nm] for nm in bucket_of[bucket]] + extra)])

    core = lax.axis_index("c").astype(jnp.int32).reshape(1)

    def chip_sums(names, got):
        return [_pair_sum(full[nm], theirs, core, "pair_sum_" + nm) for nm, theirs in zip(names, got)]

    pending = None
    for i, p in reversed(list(enumerate(layers))):
        lng = _row2(w[f"ln{i}_gain"])
        ex = []
        if pending:
            ex = [_PairExchange([full[nm] for nm in big_of[pending]]), bucket_gather(pending)]
            if pending == "c2":
                ex.append(_Exchange(scatters=[full["c2_conv_w"]]))
        *factors, yt, xt, pre = saved[p]
        if p[0] == "a":
            wc2, wc2t, bs2 = spatial[p]
            (dpre, dp, dlng, dlnb, dgain, dbias, dbs2, dwc2), got = _bwd_a(
                gcur, pre, *factors, w_out(p), _row2(w[p + "_v_gain"]), wc2, wc2t, bs2, lng, exchanges=ex)
            part[p + "_v_gain"], part[p + "_v_bias"] = dgain.sum(axis=0), dbias.sum(axis=0)
            part[p + "_w_s"], part[p + "_b_s"] = _spatial_weight_grad(dwc2, dbs2)
        elif p[0] == "b":
            b1f, b2f, b3f, poolt = factors
            (dpre, dp, dmixed, dlng, dlnb, dscale), got = _bwd_b(
                gcur, pre, b1f, b2f, b3f, w_out(p), wgrp, lng, exchanges=ex)
            part[p + "_scale"] = dscale.sum(axis=0)
            dwg, _ = _wgrad(poolt, dmixed, n_grp, True, "wgrad_grp")
            full[p + "_w_grp"] = jnp.swapaxes(dwg.reshape(n_grp, N_DEV, gd_b // N_DEV, gd_b), 0, 1)
        else:
            (dpre, dp, dlng, dlnb, dcw), got = _bwd_c(gcur, pre, *factors, w_out(p), conv_full, lng, exchanges=ex)
            dconv = dcw.sum(axis=1).reshape(conv_shape[0], N_DEV, conv_shape[1])
            full["c2_conv_w"] = jnp.stack([_pack([dconv[:, j]]) for j in range(N_DEV)])
        part[f"ln{i}_gain"], part[f"ln{i}_bias"] = dlng.sum(axis=0), dlnb.sum(axis=0)
        if pending:
            small_all[pending] = got[1][0]
            if pending == "c2":
                small_all["conv"] = got[2][0]
        dwo, _ = _wgrad(yt, dpre, 1, False, "wgrad_out_" + p)
        full[p + "_w_out"] = dwo.reshape(N_DEV, di // N_DEV, d_model)
        ex = [_ChipScatter(chip_sums(big_of[pending], got[0]))] if pending else []
        if i == 0:
            ex += [bucket_gather(p)]
        full[p + "_w_in"], got = _wgrad(xt, dp, N_DEV, False, "wgrad_in_" + p, exchanges=ex)
        if pending:
            landed.update(zip(big_of[pending], got[0]))
        if i > 0:
            gcur, _ = _dx(dp, dpre, w_in(p))
            pending = p
        else:
            small_all[p] = got[-1][0]
            (theirs,) = _exchange_only([_PairExchange([full[p + "_w_in"]])], "pair_exchange_last")
            gcur, got = _dx(dp, dpre, w_in(p), exchanges=[
                _ChipScatter(chip_sums([p + "_w_in"], theirs)), _Exchange(scatters=[full[p + "_w_out"]])])
            landed[p + "_w_in"], landed[p + "_w_out"] = got[0][0], got[1][0]
    grad_x = gcur[None]

    grads, deltas, new_m, new_v = {}, {}, {}, {}
    for nm in big:
        shp = w[nm].shape
        r2 = (math.prod(shp[:-1]), shp[-1])
        outs = _adamw(landed[nm].reshape((-1,) + r2), w[nm].reshape(r2), mom[nm].reshape(r2), var[nm].reshape(r2),
                      "adamw_" + nm)
        grads[nm], deltas[nm], new_m[nm], new_v[nm] = (o.reshape(shp) for o in outs)
    buckets = dict(bucket_of, conv=["c2_conv_w"])
    for key, members in buckets.items():
        outs = _adamw_bucket(small_all[key], [(w[nm], mom[nm], var[nm]) for nm in members], "adamw_small_" + key)
        for nm, (g_out, d_out, m_out, v_out) in zip(members, outs):
            grads[nm], deltas[nm], new_m[nm], new_v[nm] = g_out, d_out, m_out, v_out

    loss_row = sum(math.prod(w[nm].shape) for nm in bucket_of[layers[-1]]) // _LANES
    loss = jnp.sum(small_all[layers[-1]][:, loss_row, 0])

    return (loss, grad_x, *[grads[nm] for nm in names], *[deltas[nm] for nm in names],
            *[new_m[nm] for nm in names], *[new_v[nm] for nm in names])
```

```python
import functools
import math

import jax
import jax.numpy as jnp
from jax import lax
from jax.experimental import pallas as pl
from jax.experimental.pallas import tpu as pltpu

F32 = jnp.float32
BF16 = jnp.bfloat16

N_DEV = 8
DEPTH = 4
CHUNK = 128
A_GROUPS = 8
POOL_WINDOWS = (2, 4, 8, 16)
LN_EPS = 1e-5
ALPHA = (2.0 * DEPTH) ** 0.25
ADAM_LR = 0.001
ADAM_B1 = 0.9
ADAM_B2 = 0.999
ADAM_EPS = 1e-08
ADAM_WD = 0.01
ADAM_STEP = 10

TM = 256
DX_TM = 512
GATHER_PIECES = 4
HALO = 16
CHALO = 8
CW = 512
WGRAD_BLOCK_BYTES = 36 * 1024 * 1024
VMEM_LIMIT_BYTES = 58 * 1024 * 1024

_NT = (((1,), (1,)), ((), ()))
_SQRT_2_OVER_PI = math.sqrt(2.0 / math.pi)
_MESH = pl.DeviceIdType.MESH


def _vmem():
    return pl.BlockSpec(memory_space=pltpu.VMEM)


def _params(sem=None):
    return pltpu.CompilerParams(dimension_semantics=sem, vmem_limit_bytes=VMEM_LIMIT_BYTES)


def _gelu_and_grad(x):
    c1 = _SQRT_2_OVER_PI * 0.044715
    x2 = x * x
    t = jnp.tanh(x * (_SQRT_2_OVER_PI + c1 * x2))
    cdf = 0.5 + 0.5 * t
    grad = cdf + x * (1.0 - t * t) * (0.5 * _SQRT_2_OVER_PI + (1.5 * c1) * x2)
    return x * cdf, grad


def _silu_and_grad(z):
    sg = 1.0 / (1.0 + jnp.exp(-z))
    s = z * sg
    return s, sg + s - s * sg


def _fold8(a):
    return a.reshape(a.shape[0] // 8, 8, a.shape[1]).sum(axis=0)


def _row_mean(a):
    return jnp.mean(a, axis=-1, keepdims=True)


def _ln_stats(x):
    mu = _row_mean(x)
    xc = x - mu
    rstd = lax.rsqrt(_row_mean(xc * xc) + LN_EPS)
    return xc * rstd, rstd


def _post_norm(x, out, lng_ref, lnb_ref, pre_ref, xn_ref):
    pre = ALPHA * x + out
    pre_ref[...] = pre
    xhat, _ = _ln_stats(pre)
    xn_ref[...] = xhat * lng_ref[...] + lnb_ref[...]


def _post_norm_bwd(g_ref, pre_ref, lng_ref, dpre_ref, dlng_ref, dlnb_ref):
    go = g_ref[...]
    xhat, rstd = _ln_stats(pre_ref[...])
    dlng_ref[...] += _fold8(go * xhat)
    dlnb_ref[...] += _fold8(go)
    dxh = go * lng_ref[...]
    dpre = rstd * (dxh - _row_mean(dxh) - xhat * _row_mean(dxh * xhat))
    dpre_ref[...] = dpre
    return dpre


def _in_proj(xb, win_ref, p_ref, lo=0, hi=None):
    cs = win_ref.shape[2]
    hi = N_DEV * cs if hi is None else hi
    for j in range(N_DEV):
        a, b = max(lo, j * cs), min(hi, (j + 1) * cs)
        if a < b:
            p_ref[:, a:b] = jnp.dot(xb, win_ref[j, :, a - j * cs:b - j * cs], preferred_element_type=F32)


def _zero_at_first_step(*refs):
    @pl.when(pl.program_id(0) == 0)
    def _():
        for r in refs:
            r[...] = jnp.zeros(r.shape, r.dtype)


def _f32(ref, sl):
    return ref[:, sl].astype(F32)


def _my_position():
    x, y, c = lax.axis_index("x"), lax.axis_index("y"), lax.axis_index("c")
    return (x, y, c), 4 * x + 2 * y + c


def _peer(k):
    (x, y, c), _ = _my_position()
    peer = (x ^ (k >> 2), y ^ ((k >> 1) & 1), c ^ (k & 1))
    return peer, 4 * peer[0] + 2 * peer[1] + peer[2]


class _Exchange:
    def __init__(self, gathers=(), scatters=()):
        self.args = list(gathers) + list(scatters)
        self.n_gather = len(gathers)
        self.out_shape = ([jax.ShapeDtypeStruct((N_DEV,) + a.shape, a.dtype) for a in gathers]
                          + [jax.ShapeDtypeStruct(a.shape, a.dtype) for a in scatters])
        n = len(self.args)
        self.scratch = [pltpu.SemaphoreType.DMA((n, N_DEV)), pltpu.SemaphoreType.DMA((n, N_DEV)),
                        pltpu.SemaphoreType.DMA((n,))]

    def _src(self, ins, w, pos):
        return ins[w] if w < self.n_gather else ins[w].at[pos]

    def _copies(self, ins, outs, sems, arrivals):
        send_sems, recv_sems, local_sems = sems
        _, me = _my_position()
        n = len(self.args)
        copies = []
        if not arrivals:
            copies = [pltpu.make_async_copy(self._src(ins, w, me), outs[w].at[me], local_sems.at[w]) for w in range(n)]
        for k in range(1, N_DEV):
            peer, peer_pos = _peer(k)
            for w in range(n):
                copies.append(pltpu.make_async_remote_copy(
                    src_ref=self._src(ins, w, me if arrivals else peer_pos),
                    dst_ref=outs[w].at[peer_pos if arrivals else me],
                    send_sem=send_sems.at[w, k], recv_sem=recv_sems.at[w, k], device_id=peer, device_id_type=_MESH))
        return copies

    def start(self, ins, outs, sems):
        for cp in self._copies(ins, outs, sems, False):
            cp.start()

    def mid(self, ins, outs, sems):
        pass

    def wait(self, ins, outs, sems):
        n = len(self.args)
        for cp in self._copies(ins, outs, sems, True):
            cp.wait_recv()
        own = self._copies(ins, outs, sems, False)
        for cp in own[n:]:
            cp.wait_send()
        for cp in own[:n]:
            cp.wait()


def _remote(src, dst, send_sem, recv_sem, peer):
    return pltpu.make_async_remote_copy(src_ref=src, dst_ref=dst, send_sem=send_sem, recv_sem=recv_sem,
                                        device_id=peer, device_id_type=_MESH)


class _Gather:
    def __init__(self, shards):
        self.args = list(shards)
        n = len(self.args)
        self.out_shape = [jax.ShapeDtypeStruct((N_DEV,) + a.shape, a.dtype) for a in shards]
        self.pieces = [GATHER_PIECES if a.shape[0] % (GATHER_PIECES * 16) == 0 else 1 for a in shards]
        self.scratch = [pltpu.SemaphoreType.DMA((n, N_DEV, GATHER_PIECES)),
                        pltpu.SemaphoreType.DMA((n, N_DEV, GATHER_PIECES)), pltpu.SemaphoreType.DMA((n,))]

    def _rows(self, w, k):
        total = self.args[w].shape[0]
        count = 1 if k == 1 else self.pieces[w]
        return [(c, pl.ds(c * (total // count), total // count)) for c in range(count)]

    def _own(self, ins, outs, sems):
        send, recv, loc = sems
        _, me = _my_position()
        local = [pltpu.make_async_copy(ins[w], outs[w].at[me], loc.at[w]) for w in range(len(ins))]
        first = [_remote(ins[w].at[rows], outs[w].at[me].at[rows], send.at[w, k, c], recv.at[w, k, c], _peer(k)[0])
                 for k in (1, 2, 4, 6) for w in range(len(ins)) for c, rows in self._rows(w, k)]
        return local, first

    def _passed_on(self, ins, outs, sems, k, w, c, rows):
        send, recv, _ = sems
        sibling, _ = _peer(1)
        slot = outs[w].at[_peer(k)[1]].at[rows]
        return _remote(slot, slot, send.at[w, k + 1, c], recv.at[w, k + 1, c], sibling)

    def _arrival(self, ins, outs, sems, k, w, c, rows):
        send, recv, _ = sems
        peer, pos = _peer(k)
        return _remote(ins[w].at[rows], outs[w].at[pos].at[rows], send.at[w, k, c], recv.at[w, k, c], peer)

    def start(self, ins, outs, sems):
        local, first = self._own(ins, outs, sems)
        for cp in local + first:
            cp.start()

    def mid(self, ins, outs, sems):
        for c in range(GATHER_PIECES):
            for k in (2, 4, 6):
                for w in range(len(ins)):
                    for piece, rows in self._rows(w, k):
                        if piece == c:
                            self._arrival(ins, outs, sems, k, w, c, rows).wait_recv()
                            self._passed_on(ins, outs, sems, k, w, c, rows).start()

    def wait(self, ins, outs, sems):
        for k in (1, 3, 5, 7):
            for w in range(len(ins)):
                for c, rows in self._rows(w, k):
                    self._arrival(ins, outs, sems, k, w, c, rows).wait_recv()
        local, first = self._own(ins, outs, sems)
        for cp in first:
            cp.wait_send()
        for k in (2, 4, 6):
            for w in range(len(ins)):
                for c, rows in self._rows(w, k):
                    self._passed_on(ins, outs, sems, k, w, c, rows).wait_send()
        for cp in local:
            cp.wait()


class _PairExchange:
    def __init__(self, fulls):
        self.args = list(fulls)
        n = len(self.args)
        self.out_shape = [jax.ShapeDtypeStruct((N_DEV // 2,) + a.shape[1:], a.dtype) for a in fulls]
        self.scratch = [pltpu.SemaphoreType.DMA((n, N_DEV // 2)), pltpu.SemaphoreType.DMA((n, N_DEV // 2))]

    def _copies(self, ins, outs, sems):
        send, recv = sems
        (x, y, c), _ = _my_position()
        sibling, _ = _peer(1)
        return [_remote(ins[w].at[2 * q + 1 - c], outs[w].at[q], send.at[w, q], recv.at[w, q], sibling)
                for q in range(N_DEV // 2) for w in range(len(ins))]

    def start(self, ins, outs, sems):
        for cp in self._copies(ins, outs, sems):
            cp.start()

    def mid(self, ins, outs, sems):
        pass

    def wait(self, ins, outs, sems):
        for cp in self._copies(ins, outs, sems):
            cp.wait()


class _ChipScatter:
    def __init__(self, sums):
        self.args = list(sums)
        n = len(self.args)
        self.out_shape = [jax.ShapeDtypeStruct(a.shape, a.dtype) for a in sums]
        self.scratch = [pltpu.SemaphoreType.DMA((n, N_DEV // 2)), pltpu.SemaphoreType.DMA((n, N_DEV // 2)),
                        pltpu.SemaphoreType.DMA((n,))]

    def _copies(self, ins, outs, sems, arrivals):
        send, recv, loc = sems
        (x, y, c), _ = _my_position()
        my_chip = 2 * x + y
        copies = []
        if not arrivals:
            copies = [pltpu.make_async_copy(ins[w].at[my_chip], outs[w].at[my_chip], loc.at[w]) for w in range(len(ins))]
        for k in (1, 2, 3):
            peer = (x ^ (k >> 1), y ^ (k & 1), c)
            chip = my_chip ^ k
            for w in range(len(ins)):
                copies.append(_remote(ins[w].at[my_chip if arrivals else chip], outs[w].at[chip if arrivals else my_chip],
                                      send.at[w, k], recv.at[w, k], peer))
        return copies

    def start(self, ins, outs, sems):
        for cp in self._copies(ins, outs, sems, False):
            cp.start()

    def mid(self, ins, outs, sems):
        pass

    def wait(self, ins, outs, sems):
        n = len(ins)
        for cp in self._copies(ins, outs, sems, True):
            cp.wait_recv()
        own = self._copies(ins, outs, sems, False)
        for cp in own[n:]:
            cp.wait_send()
        for cp in own[:n]:
            cp.wait()


def _split(refs, sizes):
    out, off = [], 0
    for size in sizes:
        out.append(refs[off:off + size])
        off += size
    return out


def _call(body, name, grid, args, in_specs, out_shape, out_specs, scratch=(), exchanges=()):
    sem = ("arbitrary",) * len(grid)
    exchanges = [e for e in exchanges if e is not None]
    if not exchanges:
        outs = pl.pallas_call(body, name=name, grid=grid, in_specs=in_specs, out_specs=out_specs, out_shape=out_shape,
                              scratch_shapes=list(scratch), compiler_params=_params(sem))(*args)
        return outs, []
    n_in, n_out, n_scr = len(args), len(out_shape), len(scratch)
    ex_in = [len(e.args) for e in exchanges]
    ex_out = [len(e.out_shape) for e in exchanges]
    ex_scr = [len(e.scratch) for e in exchanges]
    steps = math.prod(grid)
    mid_step = min((3 * steps) // 4, steps - 1)

    def hosted(*refs):
        main_in, xin, main_out, xout, main_scr, xscr = _split(
            refs, [n_in, sum(ex_in), n_out, sum(ex_out), n_scr, sum(ex_scr)])
        parts = list(zip(exchanges, _split(xin, ex_in), _split(xout, ex_out), _split(xscr, ex_scr)))
        step = pl.program_id(0)
        for a in range(1, len(grid)):
            step = step * grid[a] + pl.program_id(a)

        @pl.when(step == 0)
        def _():
            for e, ins, outs, sems in parts:
                e.start(ins, outs, sems)

        body(*main_in, *main_out, *main_scr)

        @pl.when(step == mid_step)
        def _():
            for e, ins, outs, sems in parts:
                e.mid(ins, outs, sems)

        @pl.when(step == steps - 1)
        def _():
            for e, ins, outs, sems in parts:
                e.wait(ins, outs, sems)

    any_spec = pl.BlockSpec(memory_space=pl.ANY)
    outs = pl.pallas_call(
        hosted, name=name, grid=grid, in_specs=list(in_specs) + [any_spec] * sum(ex_in),
        out_specs=list(out_specs) + [any_spec] * sum(ex_out),
        out_shape=list(out_shape) + [s for e in exchanges for s in e.out_shape],
        scratch_shapes=list(scratch) + [s for e in exchanges for s in e.scratch],
        compiler_params=_params(sem))(*args, *[a for e in exchanges for a in e.args])
    return outs[:n_out], _split(outs[n_out:], ex_out)


def _exchange_only(exchanges, name):
    ex_in = [len(e.args) for e in exchanges]
    ex_out = [len(e.out_shape) for e in exchanges]
    ex_scr = [len(e.scratch) for e in exchanges]

    def body(*refs):
        xin, xout, xscr = _split(refs, [sum(ex_in), sum(ex_out), sum(ex_scr)])
        parts = list(zip(exchanges, _split(xin, ex_in), _split(xout, ex_out), _split(xscr, ex_scr)))
        for phase in ("start", "mid", "wait"):
            for e, ins, outs, sems in parts:
                getattr(e, phase)(ins, outs, sems)

    any_spec = pl.BlockSpec(memory_space=pl.ANY)
    outs = pl.pallas_call(
        body, name=name, in_specs=[any_spec] * sum(ex_in), out_specs=[any_spec] * sum(ex_out),
        out_shape=[s for e in exchanges for s in e.out_shape],
        scratch_shapes=[s for e in exchanges for s in e.scratch])(*[a for e in exchanges for a in e.args])
    return _split(outs, ex_out)


def _tile_specs(t, d, di):
    row = lambda i: (i, 0)
    col = lambda i: (0, i)
    return dict(
        xd=pl.BlockSpec((TM, d), row), xi=pl.BlockSpec((TM, di), row),
        td=pl.BlockSpec((d, TM), col), ti=pl.BlockSpec((di, TM), col),
        s_xd=jax.ShapeDtypeStruct((t, d), F32), s_xi=jax.ShapeDtypeStruct((t, di), BF16),
        s_td=jax.ShapeDtypeStruct((d, t), BF16), s_ti=jax.ShapeDtypeStruct((di, t), BF16))


def _fwd_a(x, win, wout, gain, bias, wc2, bs2, lng, lnb, target=None, exchanges=()):
    t, d = x.shape
    di = wout.shape[0]
    gd = di // A_GROUPS
    n_loss = 0 if target is None else 1

    def body(*refs):
        x_ref, win_ref, wout_ref, gain_ref, bias_ref, wc_ref, bs_ref, lng_ref, lnb_ref = refs[:9]
        (a1_ref, a2_ref, a3_ref, vn_ref, vh_ref, rg_ref, yt_ref, xt_ref, pre_ref,
         xn_ref) = refs[9 + n_loss:19 + n_loss]
        p_scr, vg_scr, y_scr = refs[19 + 2 * n_loss:]
        xv = x_ref[...]
        xt_ref[...] = xv.T.astype(BF16)
        xb = xv.astype(BF16)
        _in_proj(xb, win_ref, p_scr, di, 2 * di)
        s1 = jnp.zeros((TM, 1), F32)
        for c in range(di // CW):
            sl = slice(c * CW, (c + 1) * CW)
            pv = slice(di + c * CW, di + (c + 1) * CW)
            vg, dvg = _gelu_and_grad(p_scr[:, pv])
            vg_scr[:, sl] = vg
            p_scr[:, pv] = dvg
            s1 += jnp.sum(vg, axis=1, keepdims=True)
        mu = s1 * (1.0 / di)
        s2 = jnp.zeros((TM, 1), F32)
        for c in range(di // CW):
            dlt = vg_scr[:, c * CW:(c + 1) * CW] - mu
            s2 += jnp.sum(dlt * dlt, axis=1, keepdims=True)
        rstd = lax.rsqrt(s2 * (1.0 / di) + LN_EPS)
        for c in range(di // CW):
            sl = slice(c * CW, (c + 1) * CW)
            vh = (vg_scr[:, sl] - mu) * rstd
            vh_ref[:, sl] = vh.astype(BF16)
            vn_ref[:, sl] = (vh * gain_ref[:, sl] + bias_ref[:, sl]).astype(BF16)
            rg_ref[:, sl] = (p_scr[:, di + c * CW:di + (c + 1) * CW] * rstd).astype(BF16)
        for g in range(A_GROUPS):
            sl = slice(g * gd, (g + 1) * gd)
            _in_proj(xb, win_ref, p_scr, g * gd, (g + 1) * gd)
            _in_proj(xb, win_ref, p_scr, 2 * di + g * gd, 2 * di + (g + 1) * gd)
            sv = jnp.dot(wc_ref[g], vn_ref[:, sl], preferred_element_type=F32) + bs_ref[g]
            u, du = _gelu_and_grad(p_scr[:, sl])
            s, ds = _silu_and_grad(p_scr[:, 2 * di + g * gd:2 * di + (g + 1) * gd])
            us = u * s
            a1_ref[:, sl] = (s * du).astype(BF16)
            a2_ref[:, sl] = (u * ds).astype(BF16)
            a3_ref[:, sl] = us.astype(BF16)
            y = us * sv
            y_scr[:, sl] = y.astype(BF16)
            yt_ref[sl, :] = y.T.astype(BF16)
        out = jnp.dot(y_scr[...], wout_ref[...], preferred_element_type=F32)
        _post_norm(xv, out, lng_ref, lnb_ref, pre_ref, xn_ref)
        if n_loss:
            t_ref, sq_ref = refs[9], refs[20]
            _zero_at_first_step(sq_ref)
            diff = xn_ref[...] - t_ref[...]
            xn_ref[...] = diff * (1.0 / d)
            sq_ref[...] += _fold8(diff * diff)

    sp = _tile_specs(t, d, di)
    loss_in = [] if target is None else [target]
    return _call(
        body, "fwd_a", (t // TM,), (x, win, wout, gain, bias, wc2, bs2, lng, lnb, *loss_in),
        in_specs=[sp["xd"]] + [_vmem()] * 8 + [sp["xd"]] * n_loss,
        out_specs=([sp["xi"]] * 6 + [sp["ti"], sp["td"], sp["xd"], sp["xd"]]
                   + [pl.BlockSpec((8, d), lambda i: (0, 0))] * n_loss),
        out_shape=([sp["s_xi"]] * 6 + [sp["s_ti"], sp["s_td"], sp["s_xd"], sp["s_xd"]]
                   + [jax.ShapeDtypeStruct((8, d), F32)] * n_loss),
        scratch=[pltpu.VMEM((TM, 3 * di), F32), pltpu.VMEM((TM, di), F32), pltpu.VMEM((TM, di), BF16)],
        exchanges=exchanges)


def _bwd_a(g, pre, a1, a2, a3, vn, vh, rg, wout, gain, wc2, wc2t, bs2, lng, exchanges=()):
    t, d = g.shape
    di = wout.shape[0]
    gd = di // A_GROUPS

    def body(g_ref, pre_ref, a1_ref, a2_ref, a3_ref, vn_ref, vh_ref, rg_ref,
             wout_ref, gain_ref, wc_ref, wct_ref, bs_ref, lng_ref,
             dpre_ref, dp_ref, dlng_ref, dlnb_ref, dgain_ref, dbias_ref, dbs_ref, dwc_ref,
             dy_scr, dv_scr):
        _zero_at_first_step(dlng_ref, dlnb_ref, dgain_ref, dbias_ref, dbs_ref, dwc_ref)
        dpre = _post_norm_bwd(g_ref, pre_ref, lng_ref, dpre_ref, dlng_ref, dlnb_ref)
        dy_scr[...] = lax.dot_general(dpre.astype(BF16), wout_ref[...], _NT, preferred_element_type=F32)
        for grp in range(A_GROUPS):
            sl = slice(grp * gd, (grp + 1) * gd)
            vn_g = vn_ref[:, sl]
            sv = jnp.dot(wc_ref[grp], vn_g, preferred_element_type=F32) + bs_ref[grp]
            dy = dy_scr[:, sl]
            dys = dy * sv
            dp_ref[:, sl] = (dys * _f32(a1_ref, sl)).astype(BF16)
            dp_ref[:, 2 * di + grp * gd:2 * di + (grp + 1) * gd] = (dys * _f32(a2_ref, sl)).astype(BF16)
            dsv = dy * _f32(a3_ref, sl)
            dbs_ref[grp] += jnp.sum(dsv, axis=1, keepdims=True)
            dsvb = dsv.astype(BF16)
            dwc_ref[grp] += lax.dot_general(dsvb, vn_g, _NT, preferred_element_type=F32)
            dv_scr[:, sl] = jnp.dot(wct_ref[grp], dsvb, preferred_element_type=F32)
        r1 = jnp.zeros((TM, 1), F32)
        r2 = jnp.zeros((TM, 1), F32)
        for c in range(di // CW):
            sl = slice(c * CW, (c + 1) * CW)
            dv = dv_scr[:, sl]
            vhat = _f32(vh_ref, sl)
            dgain_ref[:, sl] += _fold8(dv * vhat)
            dbias_ref[:, sl] += _fold8(dv)
            dvh = dv * gain_ref[:, sl]
            dv_scr[:, sl] = dvh
            r1 += jnp.sum(dvh, axis=1, keepdims=True)
            r2 += jnp.sum(dvh * vhat, axis=1, keepdims=True)
        m1 = r1 * (1.0 / di)
        m2 = r2 * (1.0 / di)
        for c in range(di // CW):
            sl = slice(c * CW, (c + 1) * CW)
            dp_ref[:, di + c * CW:di + (c + 1) * CW] = (
                (dv_scr[:, sl] - m1 - _f32(vh_ref, sl) * m2) * _f32(rg_ref, sl)).astype(BF16)

    sp = _tile_specs(t, d, di)
    const2 = lambda i: (0, 0)
    const3 = lambda i: (0, 0, 0)
    return _call(
        body, "bwd_a", (t // TM,), (g, pre, a1, a2, a3, vn, vh, rg, wout, gain, wc2, wc2t, bs2, lng),
        in_specs=[sp["xd"], sp["xd"]] + [sp["xi"]] * 6 + [_vmem()] * 6,
        out_specs=[sp["xd"], pl.BlockSpec((TM, 3 * di), lambda i: (i, 0)),
                   pl.BlockSpec((8, d), const2), pl.BlockSpec((8, d), const2),
                   pl.BlockSpec((8, di), const2), pl.BlockSpec((8, di), const2),
                   pl.BlockSpec((A_GROUPS, TM, 1), const3), pl.BlockSpec((A_GROUPS, TM, TM), const3)],
        out_shape=[sp["s_xd"], jax.ShapeDtypeStruct((t, 3 * di), BF16),
                   jax.ShapeDtypeStruct((8, d), F32), jax.ShapeDtypeStruct((8, d), F32),
                   jax.ShapeDtypeStruct((8, di), F32), jax.ShapeDtypeStruct((8, di), F32),
                   jax.ShapeDtypeStruct((A_GROUPS, TM, 1), F32), jax.ShapeDtypeStruct((A_GROUPS, TM, TM), F32)],
        scratch=[pltpu.VMEM((TM, di), F32), pltpu.VMEM((TM, di), F32)],
        exchanges=exchanges)


def _inv_count(tile, window, rows=TM):
    pos = tile * rows + lax.broadcasted_iota(jnp.int32, (rows, 1), 0)
    return 1.0 / jnp.minimum(pos + 1, window).astype(F32)


def _window_sum(ext, window, down):
    rows = ext.shape[0]
    k = 1
    while k < window:
        ext = ext + pltpu.roll(ext, k if down else rows - k, 0)
        k *= 2
    return ext


def _fwd_b(x, win, wgrp, scale, wout, lng, lnb, exchanges=()):
    t, d = x.shape
    di = wout.shape[0]
    gd = di // len(POOL_WINDOWS)

    def body(x_ref, win_ref, wgrp_ref, scale_ref, wout_ref, lng_ref, lnb_ref,
             b1_ref, b2_ref, b3_ref, poolt_ref, yt_ref, xt_ref, pre_ref, xn_ref, p_scr, ext_scr, y_scr):
        i = pl.program_id(0)

        @pl.when(i == 0)
        def _():
            ext_scr[0:HALO, :] = jnp.zeros((HALO, di), F32)

        xv = x_ref[...]
        xt_ref[...] = xv.T.astype(BF16)
        _in_proj(xv.astype(BF16), win_ref, p_scr)
        ext_scr[HALO:, :] = p_scr[:, :di]
        for grp, window in enumerate(POOL_WINDOWS):
            sl = slice(grp * gd, (grp + 1) * gd)
            ext = ext_scr[:, sl]
            pooled = (_window_sum(ext, window, True)[HALO:] * _inv_count(i, window) - ext[HALO:]).astype(BF16)
            poolt_ref[sl, :] = pooled.astype(F32).T.astype(BF16)
            mixed = jnp.dot(pooled, wgrp_ref[grp], preferred_element_type=F32)
            s, ds = _silu_and_grad(p_scr[:, di + grp * gd:di + (grp + 1) * gd])
            sc = scale_ref[:, sl]
            ms = mixed * s
            b1_ref[:, sl] = (mixed * sc * ds).astype(BF16)
            b2_ref[:, sl] = ms.astype(BF16)
            b3_ref[:, sl] = (sc * s).astype(BF16)
            y = ms * sc
            y_scr[:, sl] = y.astype(BF16)
            yt_ref[sl, :] = y.T.astype(BF16)
        ext_scr[0:HALO, :] = ext_scr[TM:TM + HALO, :]
        out = jnp.dot(y_scr[...], wout_ref[...], preferred_element_type=F32)
        _post_norm(xv, out, lng_ref, lnb_ref, pre_ref, xn_ref)

    sp = _tile_specs(t, d, di)
    return _call(
        body, "fwd_b", (t // TM,), (x, win, wgrp, scale, wout, lng, lnb),
        in_specs=[sp["xd"]] + [_vmem()] * 6,
        out_specs=[sp["xi"]] * 3 + [sp["ti"], sp["ti"], sp["td"], sp["xd"], sp["xd"]],
        out_shape=[sp["s_xi"]] * 3 + [sp["s_ti"], sp["s_ti"], sp["s_td"], sp["s_xd"], sp["s_xd"]],
        scratch=[pltpu.VMEM((TM, 2 * di), F32), pltpu.VMEM((TM + HALO, di), F32), pltpu.VMEM((TM, di), BF16)],
        exchanges=exchanges)


def _bwd_b(g, pre, b1, b2, b3, wout, wgrp, lng, exchanges=()):
    t, d = g.shape
    di = wout.shape[0]
    gd = di // len(POOL_WINDOWS)
    nt = t // TM

    def body(g_ref, pre_ref, b1_ref, b2_ref, b3_ref, wout_ref, wgrp_ref, lng_ref,
             dpre_ref, dp_ref, dmix_ref, dlng_ref, dlnb_ref, dscale_ref, dy_scr, ext_scr):
        i = pl.program_id(0)
        tile = nt - 1 - i
        _zero_at_first_step(dlng_ref, dlnb_ref, dscale_ref)

        @pl.when(i == 0)
        def _():
            ext_scr[TM:, :] = jnp.zeros((HALO, di), F32)

        dpre = _post_norm_bwd(g_ref, pre_ref, lng_ref, dpre_ref, dlng_ref, dlnb_ref)
        dy_scr[...] = lax.dot_general(dpre.astype(BF16), wout_ref[...], _NT, preferred_element_type=F32)
        for grp, window in enumerate(POOL_WINDOWS):
            sl = slice(grp * gd, (grp + 1) * gd)
            dy = dy_scr[:, sl]
            dp_ref[:, di + grp * gd:di + (grp + 1) * gd] = (dy * _f32(b1_ref, sl)).astype(BF16)
            dscale_ref[:, sl] += _fold8(dy * _f32(b2_ref, sl))
            dmixed = (dy * _f32(b3_ref, sl)).astype(BF16)
            dmix_ref[:, sl] = dmixed
            dpooled = lax.dot_general(dmixed, wgrp_ref[grp], _NT, preferred_element_type=F32)
            ext_scr[0:TM, sl] = dpooled * _inv_count(tile, window)
            dv = _window_sum(ext_scr[:, sl], window, False)[0:TM] - dpooled
            dp_ref[:, sl] = dv.astype(BF16)
        ext_scr[TM:, :] = ext_scr[0:HALO, :]

    rrow = lambda i: (nt - 1 - i, 0)
    const2 = lambda i: (0, 0)
    xd, xi = pl.BlockSpec((TM, d), rrow), pl.BlockSpec((TM, di), rrow)
    return _call(
        body, "bwd_b", (nt,), (g, pre, b1, b2, b3, wout, wgrp, lng),
        in_specs=[xd, xd, xi, xi, xi, _vmem(), _vmem(), _vmem()],
        out_specs=[xd, pl.BlockSpec((TM, 2 * di), rrow), xi,
                   pl.BlockSpec((8, d), const2), pl.BlockSpec((8, d), const2), pl.BlockSpec((8, di), const2)],
        out_shape=[jax.ShapeDtypeStruct((t, d), F32), jax.ShapeDtypeStruct((t, 2 * di), BF16),
                   jax.ShapeDtypeStruct((t, di), BF16),
                   jax.ShapeDtypeStruct((8, d), F32), jax.ShapeDtypeStruct((8, d), F32),
                   jax.ShapeDtypeStruct((8, di), F32)],
        scratch=[pltpu.VMEM((TM, di), F32), pltpu.VMEM((TM + HALO, di), F32)],
        exchanges=exchanges)


def _fwd_c(x, win, convw, wout, lng, lnb, exchanges=()):
    t, d = x.shape
    di = wout.shape[0]

    def body(x_ref, win_ref, cw_ref, wout_ref, lng_ref, lnb_ref,
             c1_ref, c2_ref, c3_ref, cg_ref, hg_ref, yt_ref, xt_ref, pre_ref, xn_ref, p_scr, ext_scr, y_scr):
        i = pl.program_id(0)

        @pl.when(i == 0)
        def _():
            ext_scr[0:CHALO, :] = jnp.zeros((CHALO, di), F32)

        xv = x_ref[...]
        xt_ref[...] = xv.T.astype(BF16)
        _in_proj(xv.astype(BF16), win_ref, p_scr)
        for c in range(di // CW):
            sl = slice(c * CW, (c + 1) * CW)
            bb = p_scr[:, sl]
            cc = p_scr[:, di + c * CW:di + (c + 1) * CW]
            hh = p_scr[:, 2 * di + c * CW:2 * di + (c + 1) * CW]
            s, ds = _silu_and_grad(p_scr[:, 3 * di + c * CW:3 * di + (c + 1) * CW])
            ext_scr[CHALO:, sl] = cc * hh
            ext = ext_scr[:, sl]
            conv = (pltpu.roll(ext, 2, 0)[CHALO:] * cw_ref[0:1, sl] + pltpu.roll(ext, 1, 0)[CHALO:] * cw_ref[1:2, sl]
                    + ext[CHALO:] * cw_ref[2:3, sl])
            cs = conv * s
            c1_ref[:, sl] = cs.astype(BF16)
            c2_ref[:, sl] = (bb * conv * ds).astype(BF16)
            c3_ref[:, sl] = (bb * s).astype(BF16)
            cg_ref[:, sl] = cc.astype(BF16)
            hg_ref[:, sl] = hh.astype(BF16)
            y = bb * cs
            y_scr[:, sl] = y.astype(BF16)
            yt_ref[sl, :] = y.T.astype(BF16)
        ext_scr[0:CHALO, :] = ext_scr[TM:TM + CHALO, :]
        out = jnp.dot(y_scr[...], wout_ref[...], preferred_element_type=F32)
        _post_norm(xv, out, lng_ref, lnb_ref, pre_ref, xn_ref)

    sp = _tile_specs(t, d, di)
    return _call(
        body, "fwd_c", (t // TM,), (x, win, convw, wout, lng, lnb),
        in_specs=[sp["xd"]] + [_vmem()] * 5,
        out_specs=[sp["xi"]] * 5 + [sp["ti"], sp["td"], sp["xd"], sp["xd"]],
        out_shape=[sp["s_xi"]] * 5 + [sp["s_ti"], sp["s_td"], sp["s_xd"], sp["s_xd"]],
        scratch=[pltpu.VMEM((TM, 4 * di), F32), pltpu.VMEM((TM + CHALO, di), F32), pltpu.VMEM((TM, di), BF16)],
        exchanges=exchanges)


def _bwd_c(g, pre, c1, c2, c3, cg, hg, wout, convw, lng, exchanges=()):
    t, d = g.shape
    di = wout.shape[0]
    nt = t // TM

    def body(g_ref, pre_ref, c1_ref, c2_ref, c3_ref, cg_ref, hg_ref, wout_ref, cw_ref, lng_ref,
             dpre_ref, dp_ref, dlng_ref, dlnb_ref, dcw_ref, dy_scr, ext_scr):
        i = pl.program_id(0)
        _zero_at_first_step(dlng_ref, dlnb_ref, dcw_ref)

        @pl.when(i == 0)
        def _():
            ext_scr[TM:, :] = jnp.zeros((CHALO, di), F32)

        dpre = _post_norm_bwd(g_ref, pre_ref, lng_ref, dpre_ref, dlng_ref, dlnb_ref)
        dy_scr[...] = lax.dot_general(dpre.astype(BF16), wout_ref[...], _NT, preferred_element_type=F32)
        rows = TM + CHALO
        for c in range(di // CW):
            sl = slice(c * CW, (c + 1) * CW)
            dy = dy_scr[:, sl]
            cc = _f32(cg_ref, sl)
            hh = _f32(hg_ref, sl)
            dp_ref[:, sl] = (dy * _f32(c1_ref, sl)).astype(BF16)
            dp_ref[:, 3 * di + c * CW:3 * di + (c + 1) * CW] = (dy * _f32(c2_ref, sl)).astype(BF16)
            dconv = dy * _f32(c3_ref, sl)
            ext_scr[0:TM, sl] = dconv
            ext = ext_scr[:, sl]
            d1 = pltpu.roll(ext, rows - 1, 0)[0:TM]
            d2 = pltpu.roll(ext, rows - 2, 0)[0:TM]
            dq = dconv * cw_ref[2:3, sl] + d1 * cw_ref[1:2, sl] + d2 * cw_ref[0:1, sl]
            q = cc * hh
            dcw_ref[0, :, sl] += _fold8(q * d2)
            dcw_ref[1, :, sl] += _fold8(q * d1)
            dcw_ref[2, :, sl] += _fold8(q * dconv)
            dp_ref[:, di + c * CW:di + (c + 1) * CW] = (dq * hh).astype(BF16)
            dp_ref[:, 2 * di + c * CW:2 * di + (c + 1) * CW] = (dq * cc).astype(BF16)
        ext_scr[TM:, :] = ext_scr[0:CHALO, :]

    rrow = lambda i: (nt - 1 - i, 0)
    const2 = lambda i: (0, 0)
    xd, xi = pl.BlockSpec((TM, d), rrow), pl.BlockSpec((TM, di), rrow)
    return _call(
        body, "bwd_c", (nt,), (g, pre, c1, c2, c3, cg, hg, wout, convw, lng),
        in_specs=[xd, xd] + [xi] * 5 + [_vmem()] * 3,
        out_specs=[xd, pl.BlockSpec((TM, 4 * di), rrow),
                   pl.BlockSpec((8, d), const2), pl.BlockSpec((8, d), const2),
                   pl.BlockSpec((3, 8, di), lambda i: (0, 0, 0))],
        out_shape=[jax.ShapeDtypeStruct((t, d), F32), jax.ShapeDtypeStruct((t, 4 * di), BF16),
                   jax.ShapeDtypeStruct((8, d), F32), jax.ShapeDtypeStruct((8, d), F32),
                   jax.ShapeDtypeStruct((3, 8, di), F32)],
        scratch=[pltpu.VMEM((TM, di), F32), pltpu.VMEM((TM + CHALO, di), F32)],
        exchanges=exchanges)


def _dx(dp, dpre, win, exchanges=()):
    t, d = dpre.shape
    n = dp.shape[1]
    cs = win.shape[2]

    def body(dp_ref, dpre_ref, win_ref, dx_ref):
        acc = ALPHA * dpre_ref[...]
        for j in range(N_DEV):
            acc += lax.dot_general(dp_ref[:, j * cs:(j + 1) * cs], win_ref[j], _NT, preferred_element_type=F32)
        dx_ref[...] = acc

    row = lambda i: (i, 0)
    tm = min(DX_TM, t)
    (dx,), ex = _call(
        body, "dx", (t // tm,), (dp, dpre, win),
        in_specs=[pl.BlockSpec((tm, n), row), pl.BlockSpec((tm, d), row), _vmem()],
        out_specs=[pl.BlockSpec((tm, d), row)],
        out_shape=[jax.ShapeDtypeStruct((t, d), F32)],
        exchanges=exchanges)
    return dx, ex


def _wgrad(at, b, nb, per_block_rows, name, exchanges=()):
    m_all, t = at.shape
    tn = b.shape[1] // nb
    m = m_all // nb if per_block_rows else m_all
    tk = t
    while tk > 128 and 2 * tk * (m * at.dtype.itemsize + tn * b.dtype.itemsize) > WGRAD_BLOCK_BYTES:
        tk //= 2
    nk = t // tk

    def body(at_ref, b_ref, out_ref, acc):
        k = pl.program_id(1)

        @pl.when(k == 0)
        def _():
            acc[...] = jnp.zeros(acc.shape, F32)

        acc[...] += jnp.dot(at_ref[...], b_ref[...].astype(BF16), preferred_element_type=F32)

        @pl.when(k == nk - 1)
        def _():
            out_ref[0] = acc[...].astype(BF16)

    at_map = (lambda j, k: (j, k)) if per_block_rows else (lambda j, k: (0, k))
    (out,), ex = _call(
        body, name, (nb, nk), (at, b),
        in_specs=[pl.BlockSpec((m, tk), at_map), pl.BlockSpec((tk, tn), lambda j, k: (k, j))],
        out_specs=[pl.BlockSpec((1, m, tn), lambda j, k: (j, 0, 0))],
        out_shape=[jax.ShapeDtypeStruct((nb, m, tn), BF16)],
        scratch=[pltpu.VMEM((m, tn), F32)],
        exchanges=exchanges)
    return out, ex


ADAMW_BLOCK_BYTES = 14 * 1024 * 1024


def _sum_parts(parts_ref):
    g = parts_ref[0].astype(F32)
    for s in range(1, parts_ref.shape[0]):
        g = g + parts_ref[s].astype(F32)
    return g


def _row_tile(rows, bytes_per_row):
    if rows * bytes_per_row <= ADAMW_BLOCK_BYTES:
        return rows
    best = 8
    for cand in range(8, rows, 8):
        if rows % cand == 0 and cand * bytes_per_row <= ADAMW_BLOCK_BYTES:
            best = cand
    return best


def _pair_sum(full, theirs, core, name):
    shape = theirs.shape
    r, c = math.prod(shape[1:-1]), shape[-1]
    tr = _row_tile(r, c * 3 * full.dtype.itemsize)
    half = N_DEV // 2

    def body(core_ref, a_ref, b_ref, out_ref):
        out_ref[...] = (a_ref[...].astype(F32) + b_ref[...].astype(F32)).astype(out_ref.dtype)

    blk = pl.BlockSpec((None, tr, c), lambda q, i, core_ref: (q, i, 0))
    grid_spec = pltpu.PrefetchScalarGridSpec(
        num_scalar_prefetch=1, grid=(half, r // tr),
        in_specs=[pl.BlockSpec((None, None, tr, c), lambda q, i, core_ref: (q, core_ref[0], i, 0)), blk],
        out_specs=blk)
    return pl.pallas_call(
        body, name=name, grid_spec=grid_spec, out_shape=jax.ShapeDtypeStruct((half, r, c), full.dtype),
        compiler_params=_params(("arbitrary", "arbitrary")),
    )(core, full.reshape(half, 2, r, c), theirs.reshape(half, r, c)).reshape(shape)


def _adamw(parts, w, m, v, name):
    s, r, c = parts.shape
    tr = _row_tile(r, c * (s * parts.dtype.itemsize + 7 * 4))

    def body(parts_ref, w_ref, m_ref, v_ref, g_ref, d_ref, nm_ref, nv_ref):
        g = _sum_parts(parts_ref)
        g_ref[...] = g
        d_ref[...], nm_ref[...], nv_ref[...] = _adamw_update(g, w_ref[...], m_ref[...], v_ref[...])

    blk = pl.BlockSpec((tr, c), lambda i: (i, 0))
    return pl.pallas_call(
        body, name=name, grid=(r // tr,),
        in_specs=[pl.BlockSpec((s, tr, c), lambda i: (0, i, 0)), blk, blk, blk],
        out_specs=[blk, blk, blk, blk],
        out_shape=[jax.ShapeDtypeStruct((r, c), F32)] * 4,
        compiler_params=_params(("arbitrary",)),
    )(parts, w, m, v)


def _adamw_update(g, w, m, v):
    bc1 = 1.0 - ADAM_B1 ** ADAM_STEP
    bc2 = 1.0 - ADAM_B2 ** ADAM_STEP
    nm = ADAM_B1 * m + (1.0 - ADAM_B1) * g
    nv = ADAM_B2 * v + (1.0 - ADAM_B2) * (g * g)
    return -ADAM_LR * ((nm / bc1) / (jnp.sqrt(nv / bc2) + ADAM_EPS) + ADAM_WD * w), nm, nv


def _adamw_bucket(parts, members, name):
    n = len(members)
    shapes = [w.shape for w, _, _ in members]
    rows = [math.prod(shp) // _LANES for shp in shapes]
    starts = [sum(rows[:k]) for k in range(n)]

    def body(parts_ref, *refs):
        for k in range(n):
            w_ref, m_ref, v_ref = refs[3 * k:3 * k + 3]
            g_ref, d_ref, nm_ref, nv_ref = refs[3 * n + 4 * k:3 * n + 4 * k + 4]
            g = _sum_parts(parts_ref.at[:, starts[k]:starts[k] + rows[k], :])
            g_ref[...] = g
            d_ref[...], nm_ref[...], nv_ref[...] = _adamw_update(g, w_ref[...], m_ref[...], v_ref[...])

    flat = [a.reshape(-1, _LANES) for mem in members for a in mem]
    outs = pl.pallas_call(
        body, name=name, in_specs=[_vmem()] * (1 + 3 * n), out_specs=[_vmem()] * (4 * n),
        out_shape=[jax.ShapeDtypeStruct((rows[k], _LANES), F32) for k in range(n) for _ in range(4)],
        compiler_params=_params(),
    )(parts, *flat)
    return [[o.reshape(shapes[k]) for o in outs[4 * k:4 * k + 4]] for k in range(n)]


_LANES = 128


def _pack(arrays):
    flat = jnp.concatenate([a.reshape(-1) for a in arrays])
    pad = (-flat.shape[0]) % (8 * _LANES)
    return jnp.pad(flat, (0, pad)).reshape(-1, _LANES)


def _unpack(packed, shapes):
    flat = packed.reshape(-1)
    out, off = [], 0
    for shp in shapes:
        size = math.prod(shp)
        out.append(flat[off:off + size].reshape(shp))
        off += size
    return out


def _spatial_weights(w_s, b_s, rows):
    reps = rows // CHUNK
    tril = jnp.tril(jnp.ones((CHUNK, CHUNK), F32))
    wc = w_s * tril
    eye = jnp.eye(reps, dtype=F32)
    wc2 = jnp.einsum("ab,gts->gatbs", eye, wc).reshape(A_GROUPS, rows, rows)
    bs2 = jnp.tile(b_s, (1, reps)).reshape(A_GROUPS, rows, 1)
    return wc2.astype(BF16), jnp.swapaxes(wc2, 1, 2).astype(BF16), bs2


def _spatial_weight_grad(dwc2, dbs2):
    reps = TM // CHUNK
    tril = jnp.tril(jnp.ones((CHUNK, CHUNK), F32))
    blocks = dwc2.reshape(A_GROUPS, reps, CHUNK, reps, CHUNK)
    dws = sum(blocks[:, a, :, a, :] for a in range(reps)) * tril
    dbs = dbs2.reshape(A_GROUPS, reps, CHUNK).sum(axis=1)
    return dws, dbs


def _row2(a):
    return a.reshape(1, -1)


def kernel(x, a0_w_in, a0_v_gain, a0_v_bias, a0_w_s, a0_b_s, a0_w_out, ln0_gain, ln0_bias, b1_w_in, b1_w_grp, b1_scale, b1_w_out, ln1_gain, ln1_bias, c2_w_in, c2_conv_w, c2_w_out, ln2_gain, ln2_bias, a3_w_in, a3_v_gain, a3_v_bias, a3_w_s, a3_b_s, a3_w_out, ln3_gain, ln3_bias, loss_target, m_a0_w_in, m_a0_v_gain, m_a0_v_bias, m_a0_w_s, m_a0_b_s, m_a0_w_out, m_ln0_gain, m_ln0_bias, m_b1_w_in, m_b1_w_grp, m_b1_scale, m_b1_w_out, m_ln1_gain, m_ln1_bias, m_c2_w_in, m_c2_conv_w, m_c2_w_out, m_ln2_gain, m_ln2_bias, m_a3_w_in, m_a3_v_gain, m_a3_v_bias, m_a3_w_s, m_a3_b_s, m_a3_w_out, m_ln3_gain, m_ln3_bias, v_a0_w_in, v_a0_v_gain, v_a0_v_bias, v_a0_w_s, v_a0_b_s, v_a0_w_out, v_ln0_gain, v_ln0_bias, v_b1_w_in, v_b1_w_grp, v_b1_scale, v_b1_w_out, v_ln1_gain, v_ln1_bias, v_c2_w_in, v_c2_conv_w, v_c2_w_out, v_ln2_gain, v_ln2_bias, v_a3_w_in, v_a3_v_gain, v_a3_v_bias, v_a3_w_s, v_a3_b_s, v_a3_w_out, v_ln3_gain, v_ln3_bias):
    names = ["a0_w_in", "a0_v_gain", "a0_v_bias", "a0_w_s", "a0_b_s", "a0_w_out", "ln0_gain", "ln0_bias",
             "b1_w_in", "b1_w_grp", "b1_scale", "b1_w_out", "ln1_gain", "ln1_bias",
             "c2_w_in", "c2_conv_w", "c2_w_out", "ln2_gain", "ln2_bias",
             "a3_w_in", "a3_v_gain", "a3_v_bias", "a3_w_s", "a3_b_s", "a3_w_out", "ln3_gain", "ln3_bias"]
    env = dict(locals())
    w = {nm: env[nm] for nm in names}
    mom = {nm: env["m_" + nm] for nm in names}
    var = {nm: env["v_" + nm] for nm in names}

    x0 = x[0]
    target = loss_target[0]
    d_model = x0.shape[1]
    di = N_DEV * a0_w_out.shape[0]
    n_grp = len(POOL_WINDOWS)
    gd_b = di // n_grp

    layers = ("a0", "b1", "c2", "a3")
    big_of = {"a0": ["a0_w_in", "a0_w_out"], "b1": ["b1_w_in", "b1_w_grp", "b1_w_out"],
              "c2": ["c2_w_in", "c2_w_out"], "a3": ["a3_w_in", "a3_w_out"]}
    big = [nm for p in layers for nm in big_of[p]]
    bucket_of = {"a0": ["a0_v_gain", "a0_v_bias", "a0_w_s", "a0_b_s", "ln0_gain", "ln0_bias"],
                 "b1": ["b1_scale", "ln1_gain", "ln1_bias"], "c2": ["ln2_gain", "ln2_bias"],
                 "a3": ["a3_v_gain", "a3_v_bias", "a3_w_s", "a3_b_s", "ln3_gain", "ln3_bias"]}
    conv_shape = c2_conv_w.shape
    spatial = {p: _spatial_weights(w[p + "_w_s"], w[p + "_b_s"], TM) for p in ("a0", "a3")}

    def weight_gather(p):
        return _Gather([w[nm].astype(BF16) for nm in big_of[p]])

    (first,) = _exchange_only([_Gather([w[nm].astype(BF16) for nm in big_of["a0"]] + [_pack([c2_conv_w])])],
                              "gather_first")
    gathered = dict(zip(big_of["a0"], first))
    conv_all = jnp.stack([_unpack(first[-1][j], [conv_shape])[0] for j in range(N_DEV)], axis=1)
    conv_full = conv_all.reshape(conv_shape[0], di)
    w_in = lambda p: gathered[p + "_w_in"]
    w_out = lambda p: gathered[p + "_w_out"].reshape(di, d_model)
    saved = {}
    h = x0
    for i, p in enumerate(layers):
        lng, lnb = _row2(w[f"ln{i}_gain"]), _row2(w[f"ln{i}_bias"])
        nxt = layers[i + 1] if i + 1 < len(layers) else None
        ex = [weight_gather(nxt)] if nxt else []
        if p[0] == "a":
            wc2, _, bs2 = spatial[p]
            outs, got = _fwd_a(h, w_in(p), w_out(p), _row2(w[p + "_v_gain"]), _row2(w[p + "_v_bias"]),
                               wc2, bs2, lng, lnb, target=None if nxt else target, exchanges=ex)
        elif p[0] == "b":
            wgrp = jnp.swapaxes(gathered["b1_w_grp"], 0, 1).reshape(n_grp, gd_b, gd_b)
            outs, got = _fwd_b(h, w_in(p), wgrp, _row2(w[p + "_scale"]), w_out(p), lng, lnb, exchanges=ex)
        else:
            outs, got = _fwd_c(h, w_in(p), conv_full, w_out(p), lng, lnb, exchanges=ex)
        if nxt:
            saved[p], h = outs[:-1], outs[-1]
            gathered.update(zip(big_of[nxt], got[0]))
        else:
            saved[p], gcur, sq = outs[:-2], outs[-2], outs[-1]

    loss_share = (jnp.sum(sq) * (0.5 / d_model)).reshape(1)

    part, full, landed, small_all = {}, {}, {}, {}

    def bucket_gather(bucket):
        extra = [loss_share] if bucket == layers[-1] else []
        return _Gather([_pack([part[nm] for nm in bucket_of[bucket]] + extra)])

    core = lax.axis_index("c").astype(jnp.int32).reshape(1)

    def chip_sums(names, got):
        return [_pair_sum(full[nm], theirs, core, "pair_sum_" + nm) for nm, theirs in zip(names, got)]

    pending = None
    for i, p in reversed(list(enumerate(layers))):
        lng = _row2(w[f"ln{i}_gain"])
        ex = []
        if pending:
            ex = [_PairExchange([full[nm] for nm in big_of[pending]]), bucket_gather(pending)]
            if pending == "c2":
                ex.append(_Exchange(scatters=[full["c2_conv_w"]]))
        *factors, yt, xt, pre = saved[p]
        if p[0] == "a":
            wc2, wc2t, bs2 = spatial[p]
            (dpre, dp, dlng, dlnb, dgain, dbias, dbs2, dwc2), got = _bwd_a(
                gcur, pre, *factors, w_out(p), _row2(w[p + "_v_gain"]), wc2, wc2t, bs2, lng, exchanges=ex)
            part[p + "_v_gain"], part[p + "_v_bias"] = dgain.sum(axis=0), dbias.sum(axis=0)
            part[p + "_w_s"], part[p + "_b_s"] = _spatial_weight_grad(dwc2, dbs2)
        elif p[0] == "b":
            b1f, b2f, b3f, poolt = factors
            (dpre, dp, dmixed, dlng, dlnb, dscale), got = _bwd_b(
                gcur, pre, b1f, b2f, b3f, w_out(p), wgrp, lng, exchanges=ex)
            part[p + "_scale"] = dscale.sum(axis=0)
            dwg, _ = _wgrad(poolt, dmixed, n_grp, True, "wgrad_grp")
            full[p + "_w_grp"] = jnp.swapaxes(dwg.reshape(n_grp, N_DEV, gd_b // N_DEV, gd_b), 0, 1)
        else:
            (dpre, dp, dlng, dlnb, dcw), got = _bwd_c(gcur, pre, *factors, w_out(p), conv_full, lng, exchanges=ex)
            dconv = dcw.sum(axis=1).reshape(conv_shape[0], N_DEV, conv_shape[1])
            full["c2_conv_w"] = jnp.stack([_pack([dconv[:, j]]) for j in range(N_DEV)])
        part[f"ln{i}_gain"], part[f"ln{i}_bias"] = dlng.sum(axis=0), dlnb.sum(axis=0)
        if pending:
            small_all[pending] = got[1][0]
            if pending == "c2":
                small_all["conv"] = got[2][0]
        dwo, _ = _wgrad(yt, dpre, 1, False, "wgrad_out_" + p)
        full[p + "_w_out"] = dwo.reshape(N_DEV, di // N_DEV, d_model)
        ex = [_ChipScatter(chip_sums(big_of[pending], got[0]))] if pending else []
        if i == 0:
            ex += [bucket_gather(p)]
        full[p + "_w_in"], got = _wgrad(xt, dp, N_DEV, False, "wgrad_in_" + p, exchanges=ex)
        if pending:
            landed.update(zip(big_of[pending], got[0]))
        if i > 0:
            gcur, _ = _dx(dp, dpre, w_in(p))
            pending = p
        else:
            small_all[p] = got[-1][0]
            (theirs,) = _exchange_only([_PairExchange([full[p + "_w_in"]])], "pair_exchange_last")
            gcur, got = _dx(dp, dpre, w_in(p), exchanges=[
                _ChipScatter(chip_sums([p + "_w_in"], theirs)), _Exchange(scatters=[full[p + "_w_out"]])])
            landed[p + "_w_in"], landed[p + "_w_out"] = got[0][0], got[1][0]
    grad_x = gcur[None]

    grads, deltas, new_m, new_v = {}, {}, {}, {}
    for nm in big:
        shp = w[nm].shape
        r2 = (math.prod(shp[:-1]), shp[-1])
        outs = _adamw(landed[nm].reshape((-1,) + r2), w[nm].reshape(r2), mom[nm].reshape(r2), var[nm].reshape(r2),
                      "adamw_" + nm)
        grads[nm], deltas[nm], new_m[nm], new_v[nm] = (o.reshape(shp) for o in outs)
    buckets = dict(bucket_of, conv=["c2_conv_w"])
    for key, members in buckets.items():
        outs = _adamw_bucket(small_all[key], [(w[nm], mom[nm], var[nm]) for nm in members], "adamw_small_" + key)
        for nm, (g_out, d_out, m_out, v_out) in zip(members, outs):
            grads[nm], deltas[nm], new_m[nm], new_v[nm] = g_out, d_out, m_out, v_out

    loss_row = sum(math.prod(w[nm].shape) for nm in bucket_of[layers[-1]]) // _LANES
    loss = jnp.sum(small_all[layers[-1]][:, loss_row, 0])

    return (loss, grad_x, *[grads[nm] for nm in names], *[deltas[nm] for nm in names],
            *[new_m[nm] for nm in names], *[new_v[nm] for nm in names])
```

```python
import functools
import math

import jax
import jax.numpy as jnp
from jax import lax
from jax.experimental import pallas as pl
from jax.experimental.pallas import tpu as pltpu

F32 = jnp.float32
BF16 = jnp.bfloat16

N_DEV = 8
DEPTH = 4
CHUNK = 128
A_GROUPS = 8
POOL_WINDOWS = (2, 4, 8, 16)
LN_EPS = 1e-5
ALPHA = (2.0 * DEPTH) ** 0.25
ADAM_LR = 0.001
ADAM_B1 = 0.9
ADAM_B2 = 0.999
ADAM_EPS = 1e-08
ADAM_WD = 0.01
ADAM_STEP = 10

TM = 256
DX_TM = 512
HALO = 16
CHALO = 8
CW = 512
WGRAD_BLOCK_BYTES = 36 * 1024 * 1024
VMEM_LIMIT_BYTES = 58 * 1024 * 1024

_NT = (((1,), (1,)), ((), ()))
_SQRT_2_OVER_PI = math.sqrt(2.0 / math.pi)
_MESH = pl.DeviceIdType.MESH


def _vmem():
    return pl.BlockSpec(memory_space=pltpu.VMEM)


def _params(sem=None):
    return pltpu.CompilerParams(dimension_semantics=sem, vmem_limit_bytes=VMEM_LIMIT_BYTES)


def _gelu_and_grad(x):
    c1 = _SQRT_2_OVER_PI * 0.044715
    x2 = x * x
    t = jnp.tanh(x * (_SQRT_2_OVER_PI + c1 * x2))
    cdf = 0.5 + 0.5 * t
    grad = cdf + x * (1.0 - t * t) * (0.5 * _SQRT_2_OVER_PI + (1.5 * c1) * x2)
    return x * cdf, grad


def _silu_and_grad(z):
    sg = 1.0 / (1.0 + jnp.exp(-z))
    s = z * sg
    return s, sg + s - s * sg


def _fold8(a):
    return a.reshape(a.shape[0] // 8, 8, a.shape[1]).sum(axis=0)


def _row_mean(a):
    return jnp.mean(a, axis=-1, keepdims=True)


def _ln_stats(x):
    mu = _row_mean(x)
    xc = x - mu
    rstd = lax.rsqrt(_row_mean(xc * xc) + LN_EPS)
    return xc * rstd, rstd


def _post_norm(x, out, lng_ref, lnb_ref, pre_ref, xn_ref):
    pre = ALPHA * x + out
    pre_ref[...] = pre
    xhat, _ = _ln_stats(pre)
    xn_ref[...] = xhat * lng_ref[...] + lnb_ref[...]


def _post_norm_bwd(g_ref, pre_ref, lng_ref, dpre_ref, dlng_ref, dlnb_ref):
    go = g_ref[...]
    xhat, rstd = _ln_stats(pre_ref[...])
    dlng_ref[...] += _fold8(go * xhat)
    dlnb_ref[...] += _fold8(go)
    dxh = go * lng_ref[...]
    dpre = rstd * (dxh - _row_mean(dxh) - xhat * _row_mean(dxh * xhat))
    dpre_ref[...] = dpre
    return dpre


def _in_proj(xb, win_ref, p_ref, lo=0, hi=None):
    cs = win_ref.shape[2]
    hi = N_DEV * cs if hi is None else hi
    for j in range(N_DEV):
        a, b = max(lo, j * cs), min(hi, (j + 1) * cs)
        if a < b:
            p_ref[:, a:b] = jnp.dot(xb, win_ref[j, :, a - j * cs:b - j * cs], preferred_element_type=F32)


def _zero_at_first_step(*refs):
    @pl.when(pl.program_id(0) == 0)
    def _():
        for r in refs:
            r[...] = jnp.zeros(r.shape, r.dtype)


def _f32(ref, sl):
    return ref[:, sl].astype(F32)


def _my_position():
    x, y, c = lax.axis_index("x"), lax.axis_index("y"), lax.axis_index("c")
    return (x, y, c), 4 * x + 2 * y + c


def _peer(k):
    (x, y, c), _ = _my_position()
    peer = (x ^ (k >> 2), y ^ ((k >> 1) & 1), c ^ (k & 1))
    return peer, 4 * peer[0] + 2 * peer[1] + peer[2]


class _Exchange:
    def __init__(self, gathers=(), scatters=()):
        self.args = list(gathers) + list(scatters)
        self.n_gather = len(gathers)
        self.out_shape = ([jax.ShapeDtypeStruct((N_DEV,) + a.shape, a.dtype) for a in gathers]
                          + [jax.ShapeDtypeStruct(a.shape, a.dtype) for a in scatters])
        n = len(self.args)
        self.scratch = [pltpu.SemaphoreType.DMA((n, N_DEV)), pltpu.SemaphoreType.DMA((n, N_DEV)),
                        pltpu.SemaphoreType.DMA((n,))]

    def _src(self, ins, w, pos):
        return ins[w] if w < self.n_gather else ins[w].at[pos]

    def _copies(self, ins, outs, sems, arrivals):
        send_sems, recv_sems, local_sems = sems
        _, me = _my_position()
        n = len(self.args)
        copies = []
        if not arrivals:
            copies = [pltpu.make_async_copy(self._src(ins, w, me), outs[w].at[me], local_sems.at[w]) for w in range(n)]
        for k in range(1, N_DEV):
            peer, peer_pos = _peer(k)
            for w in range(n):
                copies.append(pltpu.make_async_remote_copy(
                    src_ref=self._src(ins, w, me if arrivals else peer_pos),
                    dst_ref=outs[w].at[peer_pos if arrivals else me],
                    send_sem=send_sems.at[w, k], recv_sem=recv_sems.at[w, k], device_id=peer, device_id_type=_MESH))
        return copies

    def start(self, ins, outs, sems):
        for cp in self._copies(ins, outs, sems, False):
            cp.start()

    def mid(self, ins, outs, sems):
        pass

    def wait(self, ins, outs, sems):
        n = len(self.args)
        for cp in self._copies(ins, outs, sems, True):
            cp.wait_recv()
        own = self._copies(ins, outs, sems, False)
        for cp in own[n:]:
            cp.wait_send()
        for cp in own[:n]:
            cp.wait()


def _remote(src, dst, send_sem, recv_sem, peer):
    return pltpu.make_async_remote_copy(src_ref=src, dst_ref=dst, send_sem=send_sem, recv_sem=recv_sem,
                                        device_id=peer, device_id_type=_MESH)


class _Gather:
    def __init__(self, shards):
        self.args = list(shards)
        n = len(self.args)
        self.out_shape = [jax.ShapeDtypeStruct((N_DEV,) + a.shape, a.dtype) for a in shards]
        self.scratch = [pltpu.SemaphoreType.DMA((n, N_DEV)), pltpu.SemaphoreType.DMA((n, N_DEV)),
                        pltpu.SemaphoreType.DMA((n,))]

    def _own(self, ins, outs, sems):
        send, recv, loc = sems
        _, me = _my_position()
        local = [pltpu.make_async_copy(ins[w], outs[w].at[me], loc.at[w]) for w in range(len(ins))]
        first = [_remote(ins[w], outs[w].at[me], send.at[w, k], recv.at[w, k], _peer(k)[0])
                 for k in (1, 2, 4, 6) for w in range(len(ins))]
        return local, first

    def _passed_on(self, ins, outs, sems):
        send, recv, _ = sems
        sibling, _ = _peer(1)
        return [_remote(outs[w].at[_peer(k)[1]], outs[w].at[_peer(k)[1]], send.at[w, k + 1], recv.at[w, k + 1], sibling)
                for k in (2, 4, 6) for w in range(len(ins))]

    def _arrival(self, ins, outs, sems, k, w):
        send, recv, _ = sems
        peer, pos = _peer(k)
        return _remote(ins[w], outs[w].at[pos], send.at[w, k], recv.at[w, k], peer)

    def start(self, ins, outs, sems):
        local, first = self._own(ins, outs, sems)
        for cp in local + first:
            cp.start()

    def mid(self, ins, outs, sems):
        for k in (2, 4, 6):
            for w in range(len(ins)):
                self._arrival(ins, outs, sems, k, w).wait_recv()
        for cp in self._passed_on(ins, outs, sems):
            cp.start()

    def wait(self, ins, outs, sems):
        for k in (1, 3, 5, 7):
            for w in range(len(ins)):
                self._arrival(ins, outs, sems, k, w).wait_recv()
        local, first = self._own(ins, outs, sems)
        for cp in first + self._passed_on(ins, outs, sems):
            cp.wait_send()
        for cp in local:
            cp.wait()


class _PairExchange:
    def __init__(self, fulls):
        self.args = list(fulls)
        n = len(self.args)
        self.out_shape = [jax.ShapeDtypeStruct((N_DEV // 2,) + a.shape[1:], a.dtype) for a in fulls]
        self.scratch = [pltpu.SemaphoreType.DMA((n, N_DEV // 2)), pltpu.SemaphoreType.DMA((n, N_DEV // 2))]

    def _copies(self, ins, outs, sems):
        send, recv = sems
        (x, y, c), _ = _my_position()
        sibling, _ = _peer(1)
        return [_remote(ins[w].at[2 * q + 1 - c], outs[w].at[q], send.at[w, q], recv.at[w, q], sibling)
                for q in range(N_DEV // 2) for w in range(len(ins))]

    def start(self, ins, outs, sems):
        for cp in self._copies(ins, outs, sems):
            cp.start()

    def mid(self, ins, outs, sems):
        pass

    def wait(self, ins, outs, sems):
        for cp in self._copies(ins, outs, sems):
            cp.wait()


class _ChipScatter:
    def __init__(self, sums):
        self.args = list(sums)
        n = len(self.args)
        self.out_shape = [jax.ShapeDtypeStruct(a.shape, a.dtype) for a in sums]
        self.scratch = [pltpu.SemaphoreType.DMA((n, N_DEV // 2)), pltpu.SemaphoreType.DMA((n, N_DEV // 2)),
                        pltpu.SemaphoreType.DMA((n,))]

    def _copies(self, ins, outs, sems, arrivals):
        send, recv, loc = sems
        (x, y, c), _ = _my_position()
        my_chip = 2 * x + y
        copies = []
        if not arrivals:
            copies = [pltpu.make_async_copy(ins[w].at[my_chip], outs[w].at[my_chip], loc.at[w]) for w in range(len(ins))]
        for k in (1, 2, 3):
            peer = (x ^ (k >> 1), y ^ (k & 1), c)
            chip = my_chip ^ k
            for w in range(len(ins)):
                copies.append(_remote(ins[w].at[my_chip if arrivals else chip], outs[w].at[chip if arrivals else my_chip],
                                      send.at[w, k], recv.at[w, k], peer))
        return copies

    def start(self, ins, outs, sems):
        for cp in self._copies(ins, outs, sems, False):
            cp.start()

    def mid(self, ins, outs, sems):
        pass

    def wait(self, ins, outs, sems):
        n = len(ins)
        for cp in self._copies(ins, outs, sems, True):
            cp.wait_recv()
        own = self._copies(ins, outs, sems, False)
        for cp in own[n:]:
            cp.wait_send()
        for cp in own[:n]:
            cp.wait()


def _split(refs, sizes):
    out, off = [], 0
    for size in sizes:
        out.append(refs[off:off + size])
        off += size
    return out


def _call(body, name, grid, args, in_specs, out_shape, out_specs, scratch=(), exchanges=()):
    sem = ("arbitrary",) * len(grid)
    exchanges = [e for e in exchanges if e is not None]
    if not exchanges:
        outs = pl.pallas_call(body, name=name, grid=grid, in_specs=in_specs, out_specs=out_specs, out_shape=out_shape,
                              scratch_shapes=list(scratch), compiler_params=_params(sem))(*args)
        return outs, []
    n_in, n_out, n_scr = len(args), len(out_shape), len(scratch)
    ex_in = [len(e.args) for e in exchanges]
    ex_out = [len(e.out_shape) for e in exchanges]
    ex_scr = [len(e.scratch) for e in exchanges]
    steps = math.prod(grid)
    mid_step = min((3 * steps) // 4, steps - 1)

    def hosted(*refs):
        main_in, xin, main_out, xout, main_scr, xscr = _split(
            refs, [n_in, sum(ex_in), n_out, sum(ex_out), n_scr, sum(ex_scr)])
        parts = list(zip(exchanges, _split(xin, ex_in), _split(xout, ex_out), _split(xscr, ex_scr)))
        step = pl.program_id(0)
        for a in range(1, len(grid)):
            step = step * grid[a] + pl.program_id(a)

        @pl.when(step == 0)
        def _():
            for e, ins, outs, sems in parts:
                e.start(ins, outs, sems)

        body(*main_in, *main_out, *main_scr)

        @pl.when(step == mid_step)
        def _():
            for e, ins, outs, sems in parts:
                e.mid(ins, outs, sems)

        @pl.when(step == steps - 1)
        def _():
            for e, ins, outs, sems in parts:
                e.wait(ins, outs, sems)

    any_spec = pl.BlockSpec(memory_space=pl.ANY)
    outs = pl.pallas_call(
        hosted, name=name, grid=grid, in_specs=list(in_specs) + [any_spec] * sum(ex_in),
        out_specs=list(out_specs) + [any_spec] * sum(ex_out),
        out_shape=list(out_shape) + [s for e in exchanges for s in e.out_shape],
        scratch_shapes=list(scratch) + [s for e in exchanges for s in e.scratch],
        compiler_params=_params(sem))(*args, *[a for e in exchanges for a in e.args])
    return outs[:n_out], _split(outs[n_out:], ex_out)


def _exchange_only(exchanges, name):
    ex_in = [len(e.args) for e in exchanges]
    ex_out = [len(e.out_shape) for e in exchanges]
    ex_scr = [len(e.scratch) for e in exchanges]

    def body(*refs):
        xin, xout, xscr = _split(refs, [sum(ex_in), sum(ex_out), sum(ex_scr)])
        parts = list(zip(exchanges, _split(xin, ex_in), _split(xout, ex_out), _split(xscr, ex_scr)))
        for phase in ("start", "mid", "wait"):
            for e, ins, outs, sems in parts:
                getattr(e, phase)(ins, outs, sems)

    any_spec = pl.BlockSpec(memory_space=pl.ANY)
    outs = pl.pallas_call(
        body, name=name, in_specs=[any_spec] * sum(ex_in), out_specs=[any_spec] * sum(ex_out),
        out_shape=[s for e in exchanges for s in e.out_shape],
        scratch_shapes=[s for e in exchanges for s in e.scratch])(*[a for e in exchanges for a in e.args])
    return _split(outs, ex_out)


def _tile_specs(t, d, di):
    row = lambda i: (i, 0)
    col = lambda i: (0, i)
    return dict(
        xd=pl.BlockSpec((TM, d), row), xi=pl.BlockSpec((TM, di), row),
        td=pl.BlockSpec((d, TM), col), ti=pl.BlockSpec((di, TM), col),
        s_xd=jax.ShapeDtypeStruct((t, d), F32), s_xi=jax.ShapeDtypeStruct((t, di), BF16),
        s_td=jax.ShapeDtypeStruct((d, t), BF16), s_ti=jax.ShapeDtypeStruct((di, t), BF16))


def _fwd_a(x, win, wout, gain, bias, wc2, bs2, lng, lnb, target=None, exchanges=()):
    t, d = x.shape
    di = wout.shape[0]
    gd = di // A_GROUPS
    n_loss = 0 if target is None else 1

    def body(*refs):
        x_ref, win_ref, wout_ref, gain_ref, bias_ref, wc_ref, bs_ref, lng_ref, lnb_ref = refs[:9]
        (a1_ref, a2_ref, a3_ref, vn_ref, vh_ref, rg_ref, yt_ref, xt_ref, pre_ref,
         xn_ref) = refs[9 + n_loss:19 + n_loss]
        p_scr, vg_scr, y_scr = refs[19 + 2 * n_loss:]
        xv = x_ref[...]
        xt_ref[...] = xv.T.astype(BF16)
        xb = xv.astype(BF16)
        _in_proj(xb, win_ref, p_scr, di, 2 * di)
        s1 = jnp.zeros((TM, 1), F32)
        for c in range(di // CW):
            sl = slice(c * CW, (c + 1) * CW)
            pv = slice(di + c * CW, di + (c + 1) * CW)
            vg, dvg = _gelu_and_grad(p_scr[:, pv])
            vg_scr[:, sl] = vg
            p_scr[:, pv] = dvg
            s1 += jnp.sum(vg, axis=1, keepdims=True)
        mu = s1 * (1.0 / di)
        s2 = jnp.zeros((TM, 1), F32)
        for c in range(di // CW):
            dlt = vg_scr[:, c * CW:(c + 1) * CW] - mu
            s2 += jnp.sum(dlt * dlt, axis=1, keepdims=True)
        rstd = lax.rsqrt(s2 * (1.0 / di) + LN_EPS)
        for c in range(di // CW):
            sl = slice(c * CW, (c + 1) * CW)
            vh = (vg_scr[:, sl] - mu) * rstd
            vh_ref[:, sl] = vh.astype(BF16)
            vn_ref[:, sl] = (vh * gain_ref[:, sl] + bias_ref[:, sl]).astype(BF16)
            rg_ref[:, sl] = (p_scr[:, di + c * CW:di + (c + 1) * CW] * rstd).astype(BF16)
        for g in range(A_GROUPS):
            sl = slice(g * gd, (g + 1) * gd)
            _in_proj(xb, win_ref, p_scr, g * gd, (g + 1) * gd)
            _in_proj(xb, win_ref, p_scr, 2 * di + g * gd, 2 * di + (g + 1) * gd)
            sv = jnp.dot(wc_ref[g], vn_ref[:, sl], preferred_element_type=F32) + bs_ref[g]
            u, du = _gelu_and_grad(p_scr[:, sl])
            s, ds = _silu_and_grad(p_scr[:, 2 * di + g * gd:2 * di + (g + 1) * gd])
            us = u * s
            a1_ref[:, sl] = (s * du).astype(BF16)
            a2_ref[:, sl] = (u * ds).astype(BF16)
            a3_ref[:, sl] = us.astype(BF16)
            y = us * sv
            y_scr[:, sl] = y.astype(BF16)
            yt_ref[sl, :] = y.T.astype(BF16)
        out = jnp.dot(y_scr[...], wout_ref[...], preferred_element_type=F32)
        _post_norm(xv, out, lng_ref, lnb_ref, pre_ref, xn_ref)
        if n_loss:
            t_ref, sq_ref = refs[9], refs[20]
            _zero_at_first_step(sq_ref)
            diff = xn_ref[...] - t_ref[...]
            xn_ref[...] = diff * (1.0 / d)
            sq_ref[...] += _fold8(diff * diff)

    sp = _tile_specs(t, d, di)
    loss_in = [] if target is None else [target]
    return _call(
        body, "fwd_a", (t // TM,), (x, win, wout, gain, bias, wc2, bs2, lng, lnb, *loss_in),
        in_specs=[sp["xd"]] + [_vmem()] * 8 + [sp["xd"]] * n_loss,
        out_specs=([sp["xi"]] * 6 + [sp["ti"], sp["td"], sp["xd"], sp["xd"]]
                   + [pl.BlockSpec((8, d), lambda i: (0, 0))] * n_loss),
        out_shape=([sp["s_xi"]] * 6 + [sp["s_ti"], sp["s_td"], sp["s_xd"], sp["s_xd"]]
                   + [jax.ShapeDtypeStruct((8, d), F32)] * n_loss),
        scratch=[pltpu.VMEM((TM, 3 * di), F32), pltpu.VMEM((TM, di), F32), pltpu.VMEM((TM, di), BF16)],
        exchanges=exchanges)


def _bwd_a(g, pre, a1, a2, a3, vn, vh, rg, wout, gain, wc2, wc2t, bs2, lng, exchanges=()):
    t, d = g.shape
    di = wout.shape[0]
    gd = di // A_GROUPS

    def body(g_ref, pre_ref, a1_ref, a2_ref, a3_ref, vn_ref, vh_ref, rg_ref,
             wout_ref, gain_ref, wc_ref, wct_ref, bs_ref, lng_ref,
             dpre_ref, dp_ref, dlng_ref, dlnb_ref, dgain_ref, dbias_ref, dbs_ref, dwc_ref,
             dy_scr, dv_scr):
        _zero_at_first_step(dlng_ref, dlnb_ref, dgain_ref, dbias_ref, dbs_ref, dwc_ref)
        dpre = _post_norm_bwd(g_ref, pre_ref, lng_ref, dpre_ref, dlng_ref, dlnb_ref)
        dy_scr[...] = lax.dot_general(dpre.astype(BF16), wout_ref[...], _NT, preferred_element_type=F32)
        for grp in range(A_GROUPS):
            sl = slice(grp * gd, (grp + 1) * gd)
            vn_g = vn_ref[:, sl]
            sv = jnp.dot(wc_ref[grp], vn_g, preferred_element_type=F32) + bs_ref[grp]
            dy = dy_scr[:, sl]
            dys = dy * sv
            dp_ref[:, sl] = (dys * _f32(a1_ref, sl)).astype(BF16)
            dp_ref[:, 2 * di + grp * gd:2 * di + (grp + 1) * gd] = (dys * _f32(a2_ref, sl)).astype(BF16)
            dsv = dy * _f32(a3_ref, sl)
            dbs_ref[grp] += jnp.sum(dsv, axis=1, keepdims=True)
            dsvb = dsv.astype(BF16)
            dwc_ref[grp] += lax.dot_general(dsvb, vn_g, _NT, preferred_element_type=F32)
            dv_scr[:, sl] = jnp.dot(wct_ref[grp], dsvb, preferred_element_type=F32)
        r1 = jnp.zeros((TM, 1), F32)
        r2 = jnp.zeros((TM, 1), F32)
        for c in range(di // CW):
            sl = slice(c * CW, (c + 1) * CW)
            dv = dv_scr[:, sl]
            vhat = _f32(vh_ref, sl)
            dgain_ref[:, sl] += _fold8(dv * vhat)
            dbias_ref[:, sl] += _fold8(dv)
            dvh = dv * gain_ref[:, sl]
            dv_scr[:, sl] = dvh
            r1 += jnp.sum(dvh, axis=1, keepdims=True)
            r2 += jnp.sum(dvh * vhat, axis=1, keepdims=True)
        m1 = r1 * (1.0 / di)
        m2 = r2 * (1.0 / di)
        for c in range(di // CW):
            sl = slice(c * CW, (c + 1) * CW)
            dp_ref[:, di + c * CW:di + (c + 1) * CW] = (
                (dv_scr[:, sl] - m1 - _f32(vh_ref, sl) * m2) * _f32(rg_ref, sl)).astype(BF16)

    sp = _tile_specs(t, d, di)
    const2 = lambda i: (0, 0)
    const3 = lambda i: (0, 0, 0)
    return _call(
        body, "bwd_a", (t // TM,), (g, pre, a1, a2, a3, vn, vh, rg, wout, gain, wc2, wc2t, bs2, lng),
        in_specs=[sp["xd"], sp["xd"]] + [sp["xi"]] * 6 + [_vmem()] * 6,
        out_specs=[sp["xd"], pl.BlockSpec((TM, 3 * di), lambda i: (i, 0)),
                   pl.BlockSpec((8, d), const2), pl.BlockSpec((8, d), const2),
                   pl.BlockSpec((8, di), const2), pl.BlockSpec((8, di), const2),
                   pl.BlockSpec((A_GROUPS, TM, 1), const3), pl.BlockSpec((A_GROUPS, TM, TM), const3)],
        out_shape=[sp["s_xd"], jax.ShapeDtypeStruct((t, 3 * di), BF16),
                   jax.ShapeDtypeStruct((8, d), F32), jax.ShapeDtypeStruct((8, d), F32),
                   jax.ShapeDtypeStruct((8, di), F32), jax.ShapeDtypeStruct((8, di), F32),
                   jax.ShapeDtypeStruct((A_GROUPS, TM, 1), F32), jax.ShapeDtypeStruct((A_GROUPS, TM, TM), F32)],
        scratch=[pltpu.VMEM((TM, di), F32), pltpu.VMEM((TM, di), F32)],
        exchanges=exchanges)


def _inv_count(tile, window, rows=TM):
    pos = tile * rows + lax.broadcasted_iota(jnp.int32, (rows, 1), 0)
    return 1.0 / jnp.minimum(pos + 1, window).astype(F32)


def _window_sum(ext, window, down):
    rows = ext.shape[0]
    k = 1
    while k < window:
        ext = ext + pltpu.roll(ext, k if down else rows - k, 0)
        k *= 2
    return ext


def _fwd_b(x, win, wgrp, scale, wout, lng, lnb, exchanges=()):
    t, d = x.shape
    di = wout.shape[0]
    gd = di // len(POOL_WINDOWS)

    def body(x_ref, win_ref, wgrp_ref, scale_ref, wout_ref, lng_ref, lnb_ref,
             b1_ref, b2_ref, b3_ref, poolt_ref, yt_ref, xt_ref, pre_ref, xn_ref, p_scr, ext_scr, y_scr):
        i = pl.program_id(0)

        @pl.when(i == 0)
        def _():
            ext_scr[0:HALO, :] = jnp.zeros((HALO, di), F32)

        xv = x_ref[...]
        xt_ref[...] = xv.T.astype(BF16)
        _in_proj(xv.astype(BF16), win_ref, p_scr)
        ext_scr[HALO:, :] = p_scr[:, :di]
        for grp, window in enumerate(POOL_WINDOWS):
            sl = slice(grp * gd, (grp + 1) * gd)
            ext = ext_scr[:, sl]
            pooled = (_window_sum(ext, window, True)[HALO:] * _inv_count(i, window) - ext[HALO:]).astype(BF16)
            poolt_ref[sl, :] = pooled.astype(F32).T.astype(BF16)
            mixed = jnp.dot(pooled, wgrp_ref[grp], preferred_element_type=F32)
            s, ds = _silu_and_grad(p_scr[:, di + grp * gd:di + (grp + 1) * gd])
            sc = scale_ref[:, sl]
            ms = mixed * s
            b1_ref[:, sl] = (mixed * sc * ds).astype(BF16)
            b2_ref[:, sl] = ms.astype(BF16)
            b3_ref[:, sl] = (sc * s).astype(BF16)
            y = ms * sc
            y_scr[:, sl] = y.astype(BF16)
            yt_ref[sl, :] = y.T.astype(BF16)
        ext_scr[0:HALO, :] = ext_scr[TM:TM + HALO, :]
        out = jnp.dot(y_scr[...], wout_ref[...], preferred_element_type=F32)
        _post_norm(xv, out, lng_ref, lnb_ref, pre_ref, xn_ref)

    sp = _tile_specs(t, d, di)
    return _call(
        body, "fwd_b", (t // TM,), (x, win, wgrp, scale, wout, lng, lnb),
        in_specs=[sp["xd"]] + [_vmem()] * 6,
        out_specs=[sp["xi"]] * 3 + [sp["ti"], sp["ti"], sp["td"], sp["xd"], sp["xd"]],
        out_shape=[sp["s_xi"]] * 3 + [sp["s_ti"], sp["s_ti"], sp["s_td"], sp["s_xd"], sp["s_xd"]],
        scratch=[pltpu.VMEM((TM, 2 * di), F32), pltpu.VMEM((TM + HALO, di), F32), pltpu.VMEM((TM, di), BF16)],
        exchanges=exchanges)


def _bwd_b(g, pre, b1, b2, b3, wout, wgrp, lng, exchanges=()):
    t, d = g.shape
    di = wout.shape[0]
    gd = di // len(POOL_WINDOWS)
    nt = t // TM

    def body(g_ref, pre_ref, b1_ref, b2_ref, b3_ref, wout_ref, wgrp_ref, lng_ref,
             dpre_ref, dp_ref, dmix_ref, dlng_ref, dlnb_ref, dscale_ref, dy_scr, ext_scr):
        i = pl.program_id(0)
        tile = nt - 1 - i
        _zero_at_first_step(dlng_ref, dlnb_ref, dscale_ref)

        @pl.when(i == 0)
        def _():
            ext_scr[TM:, :] = jnp.zeros((HALO, di), F32)

        dpre = _post_norm_bwd(g_ref, pre_ref, lng_ref, dpre_ref, dlng_ref, dlnb_ref)
        dy_scr[...] = lax.dot_general(dpre.astype(BF16), wout_ref[...], _NT, preferred_element_type=F32)
        for grp, window in enumerate(POOL_WINDOWS):
            sl = slice(grp * gd, (grp + 1) * gd)
            dy = dy_scr[:, sl]
            dp_ref[:, di + grp * gd:di + (grp + 1) * gd] = (dy * _f32(b1_ref, sl)).astype(BF16)
            dscale_ref[:, sl] += _fold8(dy * _f32(b2_ref, sl))
            dmixed = (dy * _f32(b3_ref, sl)).astype(BF16)
            dmix_ref[:, sl] = dmixed
            dpooled = lax.dot_general(dmixed, wgrp_ref[grp], _NT, preferred_element_type=F32)
            ext_scr[0:TM, sl] = dpooled * _inv_count(tile, window)
            dv = _window_sum(ext_scr[:, sl], window, False)[0:TM] - dpooled
            dp_ref[:, sl] = dv.astype(BF16)
        ext_scr[TM:, :] = ext_scr[0:HALO, :]

    rrow = lambda i: (nt - 1 - i, 0)
    const2 = lambda i: (0, 0)
    xd, xi = pl.BlockSpec((TM, d), rrow), pl.BlockSpec((TM, di), rrow)
    return _call(
        body, "bwd_b", (nt,), (g, pre, b1, b2, b3, wout, wgrp, lng),
        in_specs=[xd, xd, xi, xi, xi, _vmem(), _vmem(), _vmem()],
        out_specs=[xd, pl.BlockSpec((TM, 2 * di), rrow), xi,
                   pl.BlockSpec((8, d), const2), pl.BlockSpec((8, d), const2), pl.BlockSpec((8, di), const2)],
        out_shape=[jax.ShapeDtypeStruct((t, d), F32), jax.ShapeDtypeStruct((t, 2 * di), BF16),
                   jax.ShapeDtypeStruct((t, di), BF16),
                   jax.ShapeDtypeStruct((8, d), F32), jax.ShapeDtypeStruct((8, d), F32),
                   jax.ShapeDtypeStruct((8, di), F32)],
        scratch=[pltpu.VMEM((TM, di), F32), pltpu.VMEM((TM + HALO, di), F32)],
        exchanges=exchanges)


def _fwd_c(x, win, convw, wout, lng, lnb, exchanges=()):
    t, d = x.shape
    di = wout.shape[0]

    def body(x_ref, win_ref, cw_ref, wout_ref, lng_ref, lnb_ref,
             c1_ref, c2_ref, c3_ref, cg_ref, hg_ref, yt_ref, xt_ref, pre_ref, xn_ref, p_scr, ext_scr, y_scr):
        i = pl.program_id(0)

        @pl.when(i == 0)
        def _():
            ext_scr[0:CHALO, :] = jnp.zeros((CHALO, di), F32)

        xv = x_ref[...]
        xt_ref[...] = xv.T.astype(BF16)
        _in_proj(xv.astype(BF16), win_ref, p_scr)
        for c in range(di // CW):
            sl = slice(c * CW, (c + 1) * CW)
            bb = p_scr[:, sl]
            cc = p_scr[:, di + c * CW:di + (c + 1) * CW]
            hh = p_scr[:, 2 * di + c * CW:2 * di + (c + 1) * CW]
            s, ds = _silu_and_grad(p_scr[:, 3 * di + c * CW:3 * di + (c + 1) * CW])
            ext_scr[CHALO:, sl] = cc * hh
            ext = ext_scr[:, sl]
            conv = (pltpu.roll(ext, 2, 0)[CHALO:] * cw_ref[0:1, sl] + pltpu.roll(ext, 1, 0)[CHALO:] * cw_ref[1:2, sl]
                    + ext[CHALO:] * cw_ref[2:3, sl])
            cs = conv * s
            c1_ref[:, sl] = cs.astype(BF16)
            c2_ref[:, sl] = (bb * conv * ds).astype(BF16)
            c3_ref[:, sl] = (bb * s).astype(BF16)
            cg_ref[:, sl] = cc.astype(BF16)
            hg_ref[:, sl] = hh.astype(BF16)
            y = bb * cs
            y_scr[:, sl] = y.astype(BF16)
            yt_ref[sl, :] = y.T.astype(BF16)
        ext_scr[0:CHALO, :] = ext_scr[TM:TM + CHALO, :]
        out = jnp.dot(y_scr[...], wout_ref[...], preferred_element_type=F32)
        _post_norm(xv, out, lng_ref, lnb_ref, pre_ref, xn_ref)

    sp = _tile_specs(t, d, di)
    return _call(
        body, "fwd_c", (t // TM,), (x, win, convw, wout, lng, lnb),
        in_specs=[sp["xd"]] + [_vmem()] * 5,
        out_specs=[sp["xi"]] * 5 + [sp["ti"], sp["td"], sp["xd"], sp["xd"]],
        out_shape=[sp["s_xi"]] * 5 + [sp["s_ti"], sp["s_td"], sp["s_xd"], sp["s_xd"]],
        scratch=[pltpu.VMEM((TM, 4 * di), F32), pltpu.VMEM((TM + CHALO, di), F32), pltpu.VMEM((TM, di), BF16)],
        exchanges=exchanges)


def _bwd_c(g, pre, c1, c2, c3, cg, hg, wout, convw, lng, exchanges=()):
    t, d = g.shape
    di = wout.shape[0]
    nt = t // TM

    def body(g_ref, pre_ref, c1_ref, c2_ref, c3_ref, cg_ref, hg_ref, wout_ref, cw_ref, lng_ref,
             dpre_ref, dp_ref, dlng_ref, dlnb_ref, dcw_ref, dy_scr, ext_scr):
        i = pl.program_id(0)
        _zero_at_first_step(dlng_ref, dlnb_ref, dcw_ref)

        @pl.when(i == 0)
        def _():
            ext_scr[TM:, :] = jnp.zeros((CHALO, di), F32)

        dpre = _post_norm_bwd(g_ref, pre_ref, lng_ref, dpre_ref, dlng_ref, dlnb_ref)
        dy_scr[...] = lax.dot_general(dpre.astype(BF16), wout_ref[...], _NT, preferred_element_type=F32)
        rows = TM + CHALO
        for c in range(di // CW):
            sl = slice(c * CW, (c + 1) * CW)
            dy = dy_scr[:, sl]
            cc = _f32(cg_ref, sl)
            hh = _f32(hg_ref, sl)
            dp_ref[:, sl] = (dy * _f32(c1_ref, sl)).astype(BF16)
            dp_ref[:, 3 * di + c * CW:3 * di + (c + 1) * CW] = (dy * _f32(c2_ref, sl)).astype(BF16)
            dconv = dy * _f32(c3_ref, sl)
            ext_scr[0:TM, sl] = dconv
            ext = ext_scr[:, sl]
            d1 = pltpu.roll(ext, rows - 1, 0)[0:TM]
            d2 = pltpu.roll(ext, rows - 2, 0)[0:TM]
            dq = dconv * cw_ref[2:3, sl] + d1 * cw_ref[1:2, sl] + d2 * cw_ref[0:1, sl]
            q = cc * hh
            dcw_ref[0, :, sl] += _fold8(q * d2)
            dcw_ref[1, :, sl] += _fold8(q * d1)
            dcw_ref[2, :, sl] += _fold8(q * dconv)
            dp_ref[:, di + c * CW:di + (c + 1) * CW] = (dq * hh).astype(BF16)
            dp_ref[:, 2 * di + c * CW:2 * di + (c + 1) * CW] = (dq * cc).astype(BF16)
        ext_scr[TM:, :] = ext_scr[0:CHALO, :]

    rrow = lambda i: (nt - 1 - i, 0)
    const2 = lambda i: (0, 0)
    xd, xi = pl.BlockSpec((TM, d), rrow), pl.BlockSpec((TM, di), rrow)
    return _call(
        body, "bwd_c", (nt,), (g, pre, c1, c2, c3, cg, hg, wout, convw, lng),
        in_specs=[xd, xd] + [xi] * 5 + [_vmem()] * 3,
        out_specs=[xd, pl.BlockSpec((TM, 4 * di), rrow),
                   pl.BlockSpec((8, d), const2), pl.BlockSpec((8, d), const2),
                   pl.BlockSpec((3, 8, di), lambda i: (0, 0, 0))],
        out_shape=[jax.ShapeDtypeStruct((t, d), F32), jax.ShapeDtypeStruct((t, 4 * di), BF16),
                   jax.ShapeDtypeStruct((8, d), F32), jax.ShapeDtypeStruct((8, d), F32),
                   jax.ShapeDtypeStruct((3, 8, di), F32)],
        scratch=[pltpu.VMEM((TM, di), F32), pltpu.VMEM((TM + CHALO, di), F32)],
        exchanges=exchanges)


def _dx(dp, dpre, win, exchanges=()):
    t, d = dpre.shape
    n = dp.shape[1]
    cs = win.shape[2]

    def body(dp_ref, dpre_ref, win_ref, dx_ref):
        acc = ALPHA * dpre_ref[...]
        for j in range(N_DEV):
            acc += lax.dot_general(dp_ref[:, j * cs:(j + 1) * cs], win_ref[j], _NT, preferred_element_type=F32)
        dx_ref[...] = acc

    row = lambda i: (i, 0)
    tm = min(DX_TM, t)
    (dx,), ex = _call(
        body, "dx", (t // tm,), (dp, dpre, win),
        in_specs=[pl.BlockSpec((tm, n), row), pl.BlockSpec((tm, d), row), _vmem()],
        out_specs=[pl.BlockSpec((tm, d), row)],
        out_shape=[jax.ShapeDtypeStruct((t, d), F32)],
        exchanges=exchanges)
    return dx, ex


def _wgrad(at, b, nb, per_block_rows, name, exchanges=()):
    m_all, t = at.shape
    tn = b.shape[1] // nb
    m = m_all // nb if per_block_rows else m_all
    tk = t
    while tk > 128 and 2 * tk * (m * at.dtype.itemsize + tn * b.dtype.itemsize) > WGRAD_BLOCK_BYTES:
        tk //= 2
    nk = t // tk

    def body(at_ref, b_ref, out_ref, acc):
        k = pl.program_id(1)

        @pl.when(k == 0)
        def _():
            acc[...] = jnp.zeros(acc.shape, F32)

        acc[...] += jnp.dot(at_ref[...], b_ref[...].astype(BF16), preferred_element_type=F32)

        @pl.when(k == nk - 1)
        def _():
            out_ref[0] = acc[...].astype(BF16)

    at_map = (lambda j, k: (j, k)) if per_block_rows else (lambda j, k: (0, k))
    (out,), ex = _call(
        body, name, (nb, nk), (at, b),
        in_specs=[pl.BlockSpec((m, tk), at_map), pl.BlockSpec((tk, tn), lambda j, k: (k, j))],
        out_specs=[pl.BlockSpec((1, m, tn), lambda j, k: (j, 0, 0))],
        out_shape=[jax.ShapeDtypeStruct((nb, m, tn), BF16)],
        scratch=[pltpu.VMEM((m, tn), F32)],
        exchanges=exchanges)
    return out, ex


ADAMW_BLOCK_BYTES = 14 * 1024 * 1024


def _sum_parts(parts_ref):
    g = parts_ref[0].astype(F32)
    for s in range(1, parts_ref.shape[0]):
        g = g + parts_ref[s].astype(F32)
    return g


def _row_tile(rows, bytes_per_row):
    if rows * bytes_per_row <= ADAMW_BLOCK_BYTES:
        return rows
    best = 8
    for cand in range(8, rows, 8):
        if rows % cand == 0 and cand * bytes_per_row <= ADAMW_BLOCK_BYTES:
            best = cand
    return best


def _pair_sum(full, theirs, core, name):
    shape = theirs.shape
    r, c = math.prod(shape[1:-1]), shape[-1]
    tr = _row_tile(r, c * 3 * full.dtype.itemsize)
    half = N_DEV // 2

    def body(core_ref, a_ref, b_ref, out_ref):
        out_ref[...] = (a_ref[...].astype(F32) + b_ref[...].astype(F32)).astype(out_ref.dtype)

    blk = pl.BlockSpec((None, tr, c), lambda q, i, core_ref: (q, i, 0))
    grid_spec = pltpu.PrefetchScalarGridSpec(
        num_scalar_prefetch=1, grid=(half, r // tr),
        in_specs=[pl.BlockSpec((None, None, tr, c), lambda q, i, core_ref: (q, core_ref[0], i, 0)), blk],
        out_specs=blk)
    return pl.pallas_call(
        body, name=name, grid_spec=grid_spec, out_shape=jax.ShapeDtypeStruct((half, r, c), full.dtype),
        compiler_params=_params(("arbitrary", "arbitrary")),
    )(core, full.reshape(half, 2, r, c), theirs.reshape(half, r, c)).reshape(shape)


def _adamw(parts, w, m, v, name):
    s, r, c = parts.shape
    tr = _row_tile(r, c * (s * parts.dtype.itemsize + 7 * 4))

    def body(parts_ref, w_ref, m_ref, v_ref, g_ref, d_ref, nm_ref, nv_ref):
        g = _sum_parts(parts_ref)
        g_ref[...] = g
        d_ref[...], nm_ref[...], nv_ref[...] = _adamw_update(g, w_ref[...], m_ref[...], v_ref[...])

    blk = pl.BlockSpec((tr, c), lambda i: (i, 0))
    return pl.pallas_call(
        body, name=name, grid=(r // tr,),
        in_specs=[pl.BlockSpec((s, tr, c), lambda i: (0, i, 0)), blk, blk, blk],
        out_specs=[blk, blk, blk, blk],
        out_shape=[jax.ShapeDtypeStruct((r, c), F32)] * 4,
        compiler_params=_params(("arbitrary",)),
    )(parts, w, m, v)


def _adamw_update(g, w, m, v):
    bc1 = 1.0 - ADAM_B1 ** ADAM_STEP
    bc2 = 1.0 - ADAM_B2 ** ADAM_STEP
    nm = ADAM_B1 * m + (1.0 - ADAM_B1) * g
    nv = ADAM_B2 * v + (1.0 - ADAM_B2) * (g * g)
    return -ADAM_LR * ((nm / bc1) / (jnp.sqrt(nv / bc2) + ADAM_EPS) + ADAM_WD * w), nm, nv


def _adamw_bucket(parts, members, name):
    n = len(members)
    shapes = [w.shape for w, _, _ in members]
    rows = [math.prod(shp) // _LANES for shp in shapes]
    starts = [sum(rows[:k]) for k in range(n)]

    def body(parts_ref, *refs):
        for k in range(n):
            w_ref, m_ref, v_ref = refs[3 * k:3 * k + 3]
            g_ref, d_ref, nm_ref, nv_ref = refs[3 * n + 4 * k:3 * n + 4 * k + 4]
            g = _sum_parts(parts_ref.at[:, starts[k]:starts[k] + rows[k], :])
            g_ref[...] = g
            d_ref[...], nm_ref[...], nv_ref[...] = _adamw_update(g, w_ref[...], m_ref[...], v_ref[...])

    flat = [a.reshape(-1, _LANES) for mem in members for a in mem]
    outs = pl.pallas_call(
        body, name=name, in_specs=[_vmem()] * (1 + 3 * n), out_specs=[_vmem()] * (4 * n),
        out_shape=[jax.ShapeDtypeStruct((rows[k], _LANES), F32) for k in range(n) for _ in range(4)],
        compiler_params=_params(),
    )(parts, *flat)
    return [[o.reshape(shapes[k]) for o in outs[4 * k:4 * k + 4]] for k in range(n)]


_LANES = 128


def _pack(arrays):
    flat = jnp.concatenate([a.reshape(-1) for a in arrays])
    pad = (-flat.shape[0]) % (8 * _LANES)
    return jnp.pad(flat, (0, pad)).reshape(-1, _LANES)


def _unpack(packed, shapes):
    flat = packed.reshape(-1)
    out, off = [], 0
    for shp in shapes:
        size = math.prod(shp)
        out.append(flat[off:off + size].reshape(shp))
        off += size
    return out


def _spatial_weights(w_s, b_s, rows):
    reps = rows // CHUNK
    tril = jnp.tril(jnp.ones((CHUNK, CHUNK), F32))
    wc = w_s * tril
    eye = jnp.eye(reps, dtype=F32)
    wc2 = jnp.einsum("ab,gts->gatbs", eye, wc).reshape(A_GROUPS, rows, rows)
    bs2 = jnp.tile(b_s, (1, reps)).reshape(A_GROUPS, rows, 1)
    return wc2.astype(BF16), jnp.swapaxes(wc2, 1, 2).astype(BF16), bs2


def _spatial_weight_grad(dwc2, dbs2):
    reps = TM // CHUNK
    tril = jnp.tril(jnp.ones((CHUNK, CHUNK), F32))
    blocks = dwc2.reshape(A_GROUPS, reps, CHUNK, reps, CHUNK)
    dws = sum(blocks[:, a, :, a, :] for a in range(reps)) * tril
    dbs = dbs2.reshape(A_GROUPS, reps, CHUNK).sum(axis=1)
    return dws, dbs


def _row2(a):
    return a.reshape(1, -1)


def kernel(x, a0_w_in, a0_v_gain, a0_v_bias, a0_w_s, a0_b_s, a0_w_out, ln0_gain, ln0_bias, b1_w_in, b1_w_grp, b1_scale, b1_w_out, ln1_gain, ln1_bias, c2_w_in, c2_conv_w, c2_w_out, ln2_gain, ln2_bias, a3_w_in, a3_v_gain, a3_v_bias, a3_w_s, a3_b_s, a3_w_out, ln3_gain, ln3_bias, loss_target, m_a0_w_in, m_a0_v_gain, m_a0_v_bias, m_a0_w_s, m_a0_b_s, m_a0_w_out, m_ln0_gain, m_ln0_bias, m_b1_w_in, m_b1_w_grp, m_b1_scale, m_b1_w_out, m_ln1_gain, m_ln1_bias, m_c2_w_in, m_c2_conv_w, m_c2_w_out, m_ln2_gain, m_ln2_bias, m_a3_w_in, m_a3_v_gain, m_a3_v_bias, m_a3_w_s, m_a3_b_s, m_a3_w_out, m_ln3_gain, m_ln3_bias, v_a0_w_in, v_a0_v_gain, v_a0_v_bias, v_a0_w_s, v_a0_b_s, v_a0_w_out, v_ln0_gain, v_ln0_bias, v_b1_w_in, v_b1_w_grp, v_b1_scale, v_b1_w_out, v_ln1_gain, v_ln1_bias, v_c2_w_in, v_c2_conv_w, v_c2_w_out, v_ln2_gain, v_ln2_bias, v_a3_w_in, v_a3_v_gain, v_a3_v_bias, v_a3_w_s, v_a3_b_s, v_a3_w_out, v_ln3_gain, v_ln3_bias):
    names = ["a0_w_in", "a0_v_gain", "a0_v_bias", "a0_w_s", "a0_b_s", "a0_w_out", "ln0_gain", "ln0_bias",
             "b1_w_in", "b1_w_grp", "b1_scale", "b1_w_out", "ln1_gain", "ln1_bias",
             "c2_w_in", "c2_conv_w", "c2_w_out", "ln2_gain", "ln2_bias",
             "a3_w_in", "a3_v_gain", "a3_v_bias", "a3_w_s", "a3_b_s", "a3_w_out", "ln3_gain", "ln3_bias"]
    env = dict(locals())
    w = {nm: env[nm] for nm in names}
    mom = {nm: env["m_" + nm] for nm in names}
    var = {nm: env["v_" + nm] for nm in names}

    x0 = x[0]
    target = loss_target[0]
    d_model = x0.shape[1]
    di = N_DEV * a0_w_out.shape[0]
    n_grp = len(POOL_WINDOWS)
    gd_b = di // n_grp

    layers = ("a0", "b1", "c2", "a3")
    big_of = {"a0": ["a0_w_in", "a0_w_out"], "b1": ["b1_w_in", "b1_w_grp", "b1_w_out"],
              "c2": ["c2_w_in", "c2_w_out"], "a3": ["a3_w_in", "a3_w_out"]}
    big = [nm for p in layers for nm in big_of[p]]
    bucket_of = {"a0": ["a0_v_gain", "a0_v_bias", "a0_w_s", "a0_b_s", "ln0_gain", "ln0_bias"],
                 "b1": ["b1_scale", "ln1_gain", "ln1_bias"], "c2": ["ln2_gain", "ln2_bias"],
                 "a3": ["a3_v_gain", "a3_v_bias", "a3_w_s", "a3_b_s", "ln3_gain", "ln3_bias"]}
    conv_shape = c2_conv_w.shape
    spatial = {p: _spatial_weights(w[p + "_w_s"], w[p + "_b_s"], TM) for p in ("a0", "a3")}

    def weight_gather(p):
        return _Gather([w[nm].astype(BF16) for nm in big_of[p]])

    (first,) = _exchange_only([_Gather([w[nm].astype(BF16) for nm in big_of["a0"]] + [_pack([c2_conv_w])])],
                              "gather_first")
    gathered = dict(zip(big_of["a0"], first))
    conv_all = jnp.stack([_unpack(first[-1][j], [conv_shape])[0] for j in range(N_DEV)], axis=1)
    conv_full = conv_all.reshape(conv_shape[0], di)
    w_in = lambda p: gathered[p + "_w_in"]
    w_out = lambda p: gathered[p + "_w_out"].reshape(di, d_model)
    saved = {}
    h = x0
    for i, p in enumerate(layers):
        lng, lnb = _row2(w[f"ln{i}_gain"]), _row2(w[f"ln{i}_bias"])
        nxt = layers[i + 1] if i + 1 < len(layers) else None
        ex = [weight_gather(nxt)] if nxt else []
        if p[0] == "a":
            wc2, _, bs2 = spatial[p]
            outs, got = _fwd_a(h, w_in(p), w_out(p), _row2(w[p + "_v_gain"]), _row2(w[p + "_v_bias"]),
                               wc2, bs2, lng, lnb, target=None if nxt else target, exchanges=ex)
        elif p[0] == "b":
            wgrp = jnp.swapaxes(gathered["b1_w_grp"], 0, 1).reshape(n_grp, gd_b, gd_b)
            outs, got = _fwd_b(h, w_in(p), wgrp, _row2(w[p + "_scale"]), w_out(p), lng, lnb, exchanges=ex)
        else:
            outs, got = _fwd_c(h, w_in(p), conv_full, w_out(p), lng, lnb, exchanges=ex)
        if nxt:
            saved[p], h = outs[:-1], outs[-1]
            gathered.update(zip(big_of[nxt], got[0]))
        else:
            saved[p], gcur, sq = outs[:-2], outs[-2], outs[-1]

    loss_share = (jnp.sum(sq) * (0.5 / d_model)).reshape(1)

    part, full, landed, small_all = {}, {}, {}, {}

    def bucket_gather(bucket):
        extra = [loss_share] if bucket == layers[-1] else []
        return _Gather([_pack([part[nm] for nm in bucket_of[bucket]] + extra)])

    core = lax.axis_index("c").astype(jnp.int32).reshape(1)

    def chip_sums(names, got):
        return [_pair_sum(full[nm], theirs, core, "pair_sum_" + nm) for nm, theirs in zip(names, got)]

    pending = None
    for i, p in reversed(list(enumerate(layers))):
        lng = _row2(w[f"ln{i}_gain"])
        ex = []
        if pending:
            ex = [_PairExchange([full[nm] for nm in big_of[pending]]), bucket_gather(pending)]
            if pending == "c2":
                ex.append(_Exchange(scatters=[full["c2_conv_w"]]))
        *factors, yt, xt, pre = saved[p]
        if p[0] == "a":
            wc2, wc2t, bs2 = spatial[p]
            (dpre, dp, dlng, dlnb, dgain, dbias, dbs2, dwc2), got = _bwd_a(
                gcur, pre, *factors, w_out(p), _row2(w[p + "_v_gain"]), wc2, wc2t, bs2, lng, exchanges=ex)
            part[p + "_v_gain"], part[p + "_v_bias"] = dgain.sum(axis=0), dbias.sum(axis=0)
            part[p + "_w_s"], part[p + "_b_s"] = _spatial_weight_grad(dwc2, dbs2)
        elif p[0] == "b":
            b1f, b2f, b3f, poolt = factors
            (dpre, dp, dmixed, dlng, dlnb, dscale), got = _bwd_b(
                gcur, pre, b1f, b2f, b3f, w_out(p), wgrp, lng, exchanges=ex)
            part[p + "_scale"] = dscale.sum(axis=0)
            dwg, _ = _wgrad(poolt, dmixed, n_grp, True, "wgrad_grp")
            full[p + "_w_grp"] = jnp.swapaxes(dwg.reshape(n_grp, N_DEV, gd_b // N_DEV, gd_b), 0, 1)
        else:
            (dpre, dp, dlng, dlnb, dcw), got = _bwd_c(gcur, pre, *factors, w_out(p), conv_full, lng, exchanges=ex)
            dconv = dcw.sum(axis=1).reshape(conv_shape[0], N_DEV, conv_shape[1])
            full["c2_conv_w"] = jnp.stack([_pack([dconv[:, j]]) for j in range(N_DEV)])
        part[f"ln{i}_gain"], part[f"ln{i}_bias"] = dlng.sum(axis=0), dlnb.sum(axis=0)
        if pending:
            small_all[pending] = got[1][0]
            if pending == "c2":
                small_all["conv"] = got[2][0]
        ex = [_ChipScatter(chip_sums(big_of[pending], got[0]))] if pending else []
        if i == 0:
            ex += [bucket_gather(p)]
        full[p + "_w_in"], got = _wgrad(xt, dp, N_DEV, False, "wgrad_in_" + p, exchanges=ex)
        if pending:
            landed.update(zip(big_of[pending], got[0]))
        if i > 0:
            dwo, _ = _wgrad(yt, dpre, 1, False, "wgrad_out_" + p)
            full[p + "_w_out"] = dwo.reshape(N_DEV, di // N_DEV, d_model)
            gcur, _ = _dx(dp, dpre, w_in(p))
            pending = p
        else:
            small_all[p] = got[-1][0]
            dwo, got = _wgrad(yt, dpre, 1, False, "wgrad_out_" + p, exchanges=[_PairExchange([full[p + "_w_in"]])])
            full[p + "_w_out"] = dwo.reshape(N_DEV, di // N_DEV, d_model)
            gcur, got = _dx(dp, dpre, w_in(p), exchanges=[
                _ChipScatter(chip_sums([p + "_w_in"], got[0])), _Exchange(scatters=[full[p + "_w_out"]])])
            landed[p + "_w_in"], landed[p + "_w_out"] = got[0][0], got[1][0]
    grad_x = gcur[None]

    grads, deltas, new_m, new_v = {}, {}, {}, {}
    for nm in big:
        shp = w[nm].shape
        r2 = (math.prod(shp[:-1]), shp[-1])
        outs = _adamw(landed[nm].reshape((-1,) + r2), w[nm].reshape(r2), mom[nm].reshape(r2), var[nm].reshape(r2),
                      "adamw_" + nm)
        grads[nm], deltas[nm], new_m[nm], new_v[nm] = (o.reshape(shp) for o in outs)
    buckets = dict(bucket_of, conv=["c2_conv_w"])
    for key, members in buckets.items():
        outs = _adamw_bucket(small_all[key], [(w[nm], mom[nm], var[nm]) for nm in members], "adamw_small_" + key)
        for nm, (g_out, d_out, m_out, v_out) in zip(members, outs):
            grads[nm], deltas[nm], new_m[nm], new_v[nm] = g_out, d_out, m_out, v_out

    loss_row = sum(math.prod(w[nm].shape) for nm in bucket_of[layers[-1]]) // _LANES
    loss = jnp.sum(small_all[layers[-1]][:, loss_row, 0])

    return (loss, grad_x, *[grads[nm] for nm in names], *[deltas[nm] for nm in names],
            *[new_m[nm] for nm in names], *[new_v[nm] for nm in names])
```
